```python
import math
import numpy as np
import jax
import jax.numpy as jnp
from jax import lax

D_MODEL = 2048
BATCH = 1
SEQ = 8192
DEPTH = 2

MLA_HEADS = 8
MLA_Q_RANK = 512
MLA_KV_RANK = 256
MLA_NOPE = 128
MLA_ROPE = 64
MLA_V = 128
ROPE_THETA = 10000.0
Q_BLOCK = 128
GLA_HEADS = 4
GLA_DK = 64
GLA_DV = 128
GLA_GATE_RANK = 16
GLA_TAU = 16.0
GLA_CHUNK = 64
SWA_HEADS = 8
SWA_KV_HEADS = 2
SWA_HD = 64
SWA_WINDOW = 128
SWA_BLOCK = 128
REL_BUCKETS = 32
REL_MAX_DIST = 128
D_FF = 5632
N_EXPERTS = 8
TOP_K = 2
PLE_DIM = 256
LN_EPS = 1e-5
RMS_EPS = 1e-6
DEEPNORM_ALPHA = (2 * DEPTH) ** 0.25
DEEPNORM_BETA = (8 * DEPTH) ** -0.25

IN_SIZES = (
    MLA_Q_RANK, MLA_KV_RANK, MLA_ROPE,
    GLA_HEADS * GLA_DK, GLA_HEADS * GLA_DK, GLA_HEADS * GLA_DV,
    GLA_GATE_RANK, GLA_HEADS * GLA_DV,
    SWA_HEADS * SWA_HD, SWA_KV_HEADS * SWA_HD, SWA_KV_HEADS * SWA_HD,
)
IN_COLS = sum(IN_SIZES)
IN_SPLITS = tuple(int(v) for v in np.cumsum(IN_SIZES)[:-1])
MIX_WIDTH = MLA_HEADS * MLA_V + GLA_HEADS * GLA_DV + SWA_HEADS * SWA_HD

kernel_name = 'hybrid_mla_gla_swa_moe_block'


def layer_norm(x, g, b):
    xf = x.astype(jnp.float32)
    mu = jnp.mean(xf, -1, keepdims=True)
    var = jnp.mean(jnp.square(xf - mu), -1, keepdims=True)
    return ((xf - mu) * lax.rsqrt(var + LN_EPS) * g + b).astype(x.dtype)


def rms_norm(x, g):
    xf = x.astype(jnp.float32)
    return (xf * lax.rsqrt(jnp.mean(jnp.square(xf), -1, keepdims=True) + RMS_EPS) * g).astype(x.dtype)


def rope(x, positions):
    half = x.shape[-1] // 2
    inv = ROPE_THETA ** (-jnp.arange(half, dtype=jnp.float32) / half)
    ang = positions.astype(jnp.float32)[..., None] * inv
    ang = ang.reshape(ang.shape[:2] + (1,) * (x.ndim - 3) + (half,))
    cos, sin = jnp.cos(ang), jnp.sin(ang)
    xf = x.astype(jnp.float32)
    x1, x2 = xf[..., :half], xf[..., half:]
    return jnp.concatenate([x1 * cos - x2 * sin, x2 * cos + x1 * sin], -1).astype(x.dtype)


def mla_attention(q_a, kv_a, k_rope, positions, q_a_gain, w_q_b, kv_a_gain, w_kv_b):
    B, S, _ = q_a.shape
    H = MLA_HEADS
    q = (rms_norm(q_a, q_a_gain) @ w_q_b).reshape(B, S, H, MLA_NOPE + MLA_ROPE)
    q_nope, q_pe = q[..., :MLA_NOPE], rope(q[..., MLA_NOPE:], positions)
    kv = (rms_norm(kv_a, kv_a_gain) @ w_kv_b).reshape(B, S, H, MLA_NOPE + MLA_V)
    k_nope, v = kv[..., :MLA_NOPE], kv[..., MLA_NOPE:]
    k_pe = rope(k_rope, positions)
    scale = (MLA_NOPE + MLA_ROPE) ** -0.5
    nb = S // Q_BLOCK
    qn_b = q_nope.reshape(B, nb, Q_BLOCK, H, MLA_NOPE).transpose(1, 0, 2, 3, 4)
    qp_b = q_pe.reshape(B, nb, Q_BLOCK, H, MLA_ROPE).transpose(1, 0, 2, 3, 4)
    key_pos = jnp.arange(S)

    def block(args):
        i, qn, qp = args
        s = (jnp.einsum('bqhd,bkhd->bhqk', qn, k_nope).astype(jnp.float32)
             + jnp.einsum('bqhd,bkd->bhqk', qp, k_pe).astype(jnp.float32)) * scale
        qpos = i * Q_BLOCK + jnp.arange(Q_BLOCK)
        s = jnp.where(key_pos[None, :] <= qpos[:, None], s, -jnp.inf)
        pr = jax.nn.softmax(s, axis=-1).astype(v.dtype)
        return jnp.einsum('bhqk,bkhd->bqhd', pr, v)

    out = lax.map(block, (jnp.arange(nb), qn_b, qp_b))
    return out.transpose(1, 0, 2, 3, 4).reshape(B, S, H * MLA_V)


def gla_attention(q, k, v, g_lr, r, w_gate_up, b_gate_up, norm_gain):
    B, S, _ = q.shape
    H, DK, DV, C = GLA_HEADS, GLA_DK, GLA_DV, GLA_CHUNK
    nc = S // C
    f32 = jnp.float32
    gk = jax.nn.log_sigmoid((g_lr @ w_gate_up + b_gate_up).astype(f32)) / GLA_TAU

    def to_chunks(t, d):
        return t.astype(f32).reshape(B, nc, C, H, d).transpose(1, 0, 3, 2, 4)

    qc = to_chunks(q, DK) * (DK ** -0.5)
    kc, vc, gc = to_chunks(k, DK), to_chunks(v, DV), to_chunks(gk, DK)
    causal = jnp.tril(jnp.ones((C, C), dtype=bool))

    def step(state, inp):
        qi, ki, vi, gi = inp
        b = jnp.cumsum(gi, axis=2)
        o_inter = jnp.einsum('bhcd,bhde->bhce', qi * jnp.exp(b), state)
        diff = b[:, :, :, None, :] - b[:, :, None, :, :]
        decay = jnp.exp(jnp.where(causal[:, :, None], diff, -jnp.inf))
        attn = jnp.einsum('bhid,bhjd,bhijd->bhij', qi, ki, decay)
        o_intra = jnp.einsum('bhij,bhje->bhie', attn, vi)
        b_last = b[:, :, -1:, :]
        new_state = (jnp.exp(b_last[:, :, 0, :])[..., None] * state
                     + jnp.einsum('bhcd,bhce->bhde', ki * jnp.exp(b_last - b), vi))
        return new_state, o_inter + o_intra

    _, o = lax.scan(step, jnp.zeros((B, H, DK, DV), f32), (qc, kc, vc, gc))
    o = o.transpose(1, 0, 3, 2, 4).reshape(B, S, H, DV)
    o = rms_norm(o, norm_gain.reshape(H, DV)).reshape(B, S, H * DV)
    return (o * jax.nn.silu(r.astype(f32))).astype(q.dtype)


def t5_bucket(dist):
    max_exact = REL_BUCKETS // 2
    d = jnp.maximum(dist, 1).astype(jnp.float32)
    large = max_exact + (jnp.log(d / max_exact) / math.log(REL_MAX_DIST / max_exact)
                         * (REL_BUCKETS - max_exact)).astype(jnp.int32)
    large = jnp.minimum(large, REL_BUCKETS - 1)
    return jnp.where(dist < max_exact, dist, large)


def swa_attention(q, k, v, sinks, rel_bias):
    B, S, _ = q.shape
    HK, G, D, L = SWA_KV_HEADS, SWA_HEADS // SWA_KV_HEADS, SWA_HD, SWA_BLOCK
    nb = S // L
    qb = q.reshape(B, nb, L, HK, G, D)

    def with_prev(t):
        t = t.reshape(B, nb, L, HK, D)
        prev = jnp.pad(t, ((0, 0), (1, 0), (0, 0), (0, 0), (0, 0)))[:, :-1]
        return jnp.concatenate([prev, t], axis=2)

    kw, vw = with_prev(k), with_prev(v)
    s = jnp.einsum('bnqhgd,bnkhd->bnhgqk', qb, kw).astype(jnp.float32) * (D ** -0.5)
    qi = jnp.arange(L)[:, None]
    kj = jnp.arange(2 * L)[None, :]
    dist = qi + L - kj
    bias = rel_bias[t5_bucket(jnp.clip(dist, 0, None))].astype(jnp.float32)
    s = s + bias.transpose(2, 0, 1).reshape(HK, G, L, 2 * L)
    in_window = (dist >= 0) & (dist < SWA_WINDOW)
    real_key = (jnp.arange(nb)[:, None, None] > 0) | (kj >= L)[None]
    mask = in_window[None] & real_key
    s = jnp.where(mask[None, :, None, None], s, -jnp.inf)
    sink = jnp.broadcast_to(sinks.astype(jnp.float32).reshape(HK, G)[None, None, :, :, None, None],
                            s.shape[:-1] + (1,))
    pr = jax.nn.softmax(jnp.concatenate([s, sink], -1), axis=-1)[..., :-1]
    o = jnp.einsum('bnhgqk,bnkhd->bnqhgd', pr.astype(v.dtype), vw)
    return o.reshape(B, S, SWA_HEADS * D)


def token_mixer(x, positions, w_in, q_a_gain, w_q_b, kv_a_gain, w_kv_b,
                gla_w_gate, gla_b_gate, gla_norm_gain, sinks, rel_bias, w_out):
    h = x @ w_in
    (q_a, kv_a, k_rope, g_q, g_k, g_v, g_lr, g_r, s_q, s_k, s_v) = jnp.split(h, IN_SPLITS, axis=-1)
    a = mla_attention(q_a, kv_a, k_rope, positions, q_a_gain, w_q_b, kv_a_gain, w_kv_b)
    b = gla_attention(g_q, g_k, g_v, g_lr, g_r, gla_w_gate, gla_b_gate, gla_norm_gain)
    c = swa_attention(s_q, s_k, s_v, sinks, rel_bias)
    return jnp.concatenate([a, b.astype(a.dtype), c], axis=-1) @ w_out


def swiglu(x, w_gate, w_up, w_down):
    return (jax.nn.silu(x @ w_gate) * (x @ w_up)) @ w_down


def moe_swiglu(x, w_router, w_gate, w_up, w_down):
    logits = (x @ w_router).astype(jnp.float32)
    top_val, top_idx = lax.top_k(logits, TOP_K)
    top_w = jax.nn.softmax(top_val, axis=-1)
    combine = jnp.sum(jax.nn.one_hot(top_idx, N_EXPERTS, dtype=jnp.float32) * top_w[..., None], axis=-2)
    out = jnp.zeros(x.shape, jnp.float32)
    for e in range(N_EXPERTS):
        out = out + combine[..., e:e + 1] * swiglu(x, w_gate[e], w_up[e], w_down[e]).astype(jnp.float32)
    return out.astype(x.dtype)


def setup_inputs(seed: int = 0) -> dict:
    key = jax.random.key(seed)
    ks = jax.random.split(key, 32)
    f32 = jnp.float32

    def nrm(i, shape, scale):
        return jax.random.normal(ks[i], shape, f32) * scale

    n_dense = (DEPTH + 1) // 2
    n_moe = DEPTH // 2
    return {
        'x': nrm(0, (BATCH, SEQ, D_MODEL), 1.0),
        'p': nrm(1, (DEPTH, BATCH, SEQ, PLE_DIM), 1.0),
        'positions': jnp.broadcast_to(jnp.arange(SEQ, dtype=jnp.int32), (BATCH, SEQ)),
        'w_in': nrm(2, (DEPTH, D_MODEL, IN_COLS), D_MODEL ** -0.5),
        'mla_q_a_gain': 1.0 + nrm(3, (DEPTH, MLA_Q_RANK), 0.02),
        'mla_w_q_b': nrm(4, (DEPTH, MLA_Q_RANK, MLA_HEADS * (MLA_NOPE + MLA_ROPE)), MLA_Q_RANK ** -0.5),
        'mla_kv_a_gain': 1.0 + nrm(5, (DEPTH, MLA_KV_RANK), 0.02),
        'mla_w_kv_b': nrm(6, (DEPTH, MLA_KV_RANK, MLA_HEADS * (MLA_NOPE + MLA_V)), MLA_KV_RANK ** -0.5),
        'gla_w_gate': nrm(7, (DEPTH, GLA_GATE_RANK, GLA_HEADS * GLA_DK), GLA_GATE_RANK ** -0.5),
        'gla_b_gate': nrm(8, (DEPTH, GLA_HEADS * GLA_DK), 0.1),
        'gla_norm_gain': 1.0 + nrm(9, (DEPTH, GLA_HEADS * GLA_DV), 0.02),
        'swa_sinks': nrm(10, (DEPTH, SWA_HEADS), 1.0),
        'rel_bias': nrm(11, (REL_BUCKETS, SWA_HEADS), 0.5),
        'w_out': nrm(12, (DEPTH, MIX_WIDTH, D_MODEL), DEEPNORM_BETA * MIX_WIDTH ** -0.5),
        'ln1_g': 1.0 + nrm(13, (DEPTH, D_MODEL), 0.02),
        'ln1_b': nrm(14, (DEPTH, D_MODEL), 0.02),
        'ffn_w_gate': nrm(15, (n_dense, D_MODEL, D_FF), D_MODEL ** -0.5),
        'ffn_w_up': nrm(16, (n_dense, D_MODEL, D_FF), D_MODEL ** -0.5),
        'ffn_w_down': nrm(17, (n_dense, D_FF, D_MODEL), DEEPNORM_BETA * D_FF ** -0.5),
        'moe_router': nrm(18, (n_moe, D_MODEL, N_EXPERTS), D_MODEL ** -0.5),
        'moe_w_gate': nrm(19, (n_moe, N_EXPERTS, D_MODEL, D_FF), D_MODEL ** -0.5),
        'moe_w_up': nrm(20, (n_moe, N_EXPERTS, D_MODEL, D_FF), D_MODEL ** -0.5),
        'moe_w_down': nrm(21, (n_moe, N_EXPERTS, D_FF, D_MODEL), DEEPNORM_BETA * D_FF ** -0.5),
        'ple_w_up': nrm(22, (DEPTH, PLE_DIM, D_MODEL), DEEPNORM_BETA * PLE_DIM ** -0.5),
        'ple_w_gate': nrm(23, (DEPTH, D_MODEL, D_MODEL), D_MODEL ** -0.5),
        'ple_b_gate': nrm(24, (DEPTH, D_MODEL), 0.02),
        'ln2_g': 1.0 + nrm(25, (DEPTH, D_MODEL), 0.02),
        'ln2_b': nrm(26, (DEPTH, D_MODEL), 0.02),
    }


def reference(x, p, positions, w_in, mla_q_a_gain, mla_w_q_b, mla_kv_a_gain, mla_w_kv_b,
              gla_w_gate, gla_b_gate, gla_norm_gain, swa_sinks, rel_bias, w_out, ln1_g, ln1_b,
              ffn_w_gate, ffn_w_up, ffn_w_down, moe_router, moe_w_gate, moe_w_up, moe_w_down,
              ple_w_up, ple_w_gate, ple_b_gate, ln2_g, ln2_b):
    for i in range(DEPTH):
        m = token_mixer(x, positions, w_in[i], mla_q_a_gain[i], mla_w_q_b[i], mla_kv_a_gain[i],
                        mla_w_kv_b[i], gla_w_gate[i], gla_b_gate[i], gla_norm_gain[i],
                        swa_sinks[i], rel_bias, w_out[i])
        x = layer_norm(DEEPNORM_ALPHA * x + m, ln1_g[i], ln1_b[i])
        if i % 2 == 0:
            j = i // 2
            f = swiglu(x, ffn_w_gate[j], ffn_w_up[j], ffn_w_down[j])
        else:
            j = i // 2
            f = moe_swiglu(x, moe_router[j], moe_w_gate[j], moe_w_up[j], moe_w_down[j])
        ple = (p[i] @ ple_w_up[i]) * jax.nn.sigmoid(x @ ple_w_gate[i] + ple_b_gate[i])
        x = layer_norm(DEEPNORM_ALPHA * x + f + ple, ln2_g[i], ln2_b[i])
    return x
```

```python
import functools
import math

import numpy as np
import jax
import jax.numpy as jnp
from jax import lax
from jax.experimental import pallas as pl
from jax.experimental.pallas import tpu as pltpu

F32 = jnp.float32
BF16 = jnp.bfloat16

D_MODEL = 2048
DEPTH = 2
MLA_HEADS = 8
MLA_Q_RANK = 512
MLA_KV_RANK = 256
MLA_NOPE = 128
MLA_ROPE = 64
MLA_V = 128
ROPE_THETA = 10000.0
GLA_HEADS = 4
GLA_DK = 64
GLA_DV = 128
GLA_GATE_RANK = 16
GLA_TAU = 16.0
SWA_HEADS = 8
SWA_KV_HEADS = 2
SWA_HD = 64
SWA_WINDOW = 128
SWA_BLOCK = 128
REL_BUCKETS = 32
REL_MAX_DIST = 128
D_FF = 5632
N_EXPERTS = 8
PLE_DIM = 256
LN_EPS = 1e-5
RMS_EPS = 1e-6
DEEPNORM_ALPHA = (2 * DEPTH) ** 0.25

LANE = 128
LOG2E = math.log2(math.e)
NEG_BIG = -1e30

H_QA, H_GV, H_GR, H_SQ = 0, 512, 1024, 1536
H_KVA, H_GQ, H_GK = 2048, 2304, 2560
H_KR, H_GLR, H_SK, H_SV = 2816, 2944, 3072, 3200
H_COLS = 3328

MLA_QK = 2 * LANE

GLA_L = 128
GLA_SUB = 32
GLA_NSUB = GLA_L // GLA_SUB


def _params(sem, vmem_mb):
    return pltpu.CompilerParams(dimension_semantics=sem, vmem_limit_bytes=vmem_mb * 2 ** 20)


def _dot(a, b):
    return jnp.dot(a, b, preferred_element_type=F32)


def _dot_nt(a, b):
    return lax.dot_general(a, b, (((1,), (1,)), ((), ())), preferred_element_type=F32)


def _dot_tn(a, b):
    return lax.dot_general(a, b, (((0,), (0,)), ((), ())), preferred_element_type=F32)


def _proj_in_kernel(x_ref, w_ref, o_ref, xb_ref):
    @pl.when(pl.program_id(1) == 0)
    def _():
        xb_ref[...] = x_ref[...].astype(BF16)

    o_ref[...] = _dot(xb_ref[...], w_ref[...]).astype(o_ref.dtype)


def proj_in(x, w_p):
    s = x.shape[0]
    tm = min(512, s)
    tn = H_COLS // 2
    return pl.pallas_call(
        _proj_in_kernel,
        out_shape=jax.ShapeDtypeStruct((s, H_COLS), BF16),
        grid=(s // tm, H_COLS // tn),
        in_specs=[pl.BlockSpec((tm, D_MODEL), lambda i, j: (i, 0)),
                  pl.BlockSpec((D_MODEL, tn), lambda i, j: (0, j))],
        out_specs=pl.BlockSpec((tm, tn), lambda i, j: (i, j)),
        scratch_shapes=[pltpu.VMEM((tm, D_MODEL), BF16)],
        compiler_params=_params(("parallel", "arbitrary"), 48),
        name="proj_in",
    )(x, w_p)


def _rope_table_kernel(pos_ref, inv_ref, cos_ref, sa_ref, sb_ref):
    ang = pos_ref[...].astype(F32) * inv_ref[...]
    lane = lax.broadcasted_iota(jnp.int32, ang.shape, 1)
    half = MLA_ROPE // 2
    c, s = jnp.cos(ang), jnp.sin(ang)
    cos_ref[...] = c
    sa_ref[...] = jnp.where((lane >= half) & (lane < 2 * half), s, 0.0)
    sb_ref[...] = jnp.where(lane < half, -s, 0.0)


def rope_tables(positions):
    s = positions.shape[-1]
    half = MLA_ROPE // 2
    inv = ROPE_THETA ** (-jnp.arange(half, dtype=F32) / half)
    inv = jnp.concatenate([inv, inv, jnp.zeros((LANE - 2 * half,), F32)]).reshape(1, LANE)
    tm = min(1024, s)
    spec = pl.BlockSpec((tm, LANE), lambda i: (i, 0))
    return pl.pallas_call(
        _rope_table_kernel,
        out_shape=[jax.ShapeDtypeStruct((s, LANE), F32)] * 3,
        grid=(s // tm,),
        in_specs=[pl.BlockSpec((tm, 1), lambda i: (i, 0)), pl.BlockSpec((1, LANE), lambda i: (0, 0))],
        out_specs=[spec, spec, spec],
        compiler_params=_params(("parallel",), 32),
        name="rope_tables",
    )(positions.reshape(s, 1), inv)


def _rope(x, cos, sa, sb):
    return x * cos + pltpu.roll(x, MLA_ROPE // 2, 1) * sa + pltpu.roll(x, LANE - MLA_ROPE // 2, 1) * sb


def _mla_prep_kernel(qa_ref, kva_ref, kr_ref, cos_ref, sa_ref, sb_ref, gq_ref, gkv_ref, wq_ref, wkv_ref,
                     q_out, k_out, v_out):
    cos, sa, sb = cos_ref[...], sa_ref[...], sb_ref[...]
    qscale = (MLA_NOPE + MLA_ROPE) ** -0.5 * LOG2E

    qa = qa_ref[...].astype(F32)
    qn = qa * lax.rsqrt(jnp.mean(qa * qa, -1, keepdims=True) + RMS_EPS) * gq_ref[...]
    q = _dot(qn.astype(BF16), wq_ref[...])
    for h in range(MLA_HEADS):
        c0 = h * MLA_QK
        q_out[h, :, 0:LANE] = (q[:, c0:c0 + LANE] * qscale).astype(BF16)
        pe = _rope(q[:, c0 + LANE:c0 + 2 * LANE], cos, sa, sb)
        q_out[h, :, LANE:2 * LANE] = (pe * qscale).astype(BF16)

    kva = kva_ref[...].astype(F32)
    kvn = kva * lax.rsqrt(jnp.mean(kva * kva, -1, keepdims=True) + RMS_EPS) * gkv_ref[...]
    kv = _dot(kvn.astype(BF16), wkv_ref[...])
    kpe = _rope(kr_ref[...].astype(F32), cos, sa, sb).astype(BF16)
    for h in range(MLA_HEADS):
        c0 = h * (MLA_NOPE + MLA_V)
        k_out[h, :, 0:LANE] = kv[:, c0:c0 + MLA_NOPE].astype(BF16)
        k_out[h, :, LANE:2 * LANE] = kpe
        v_out[h] = kv[:, c0 + MLA_NOPE:c0 + MLA_NOPE + MLA_V].astype(BF16)


def mla_prep(h, cos, sa, sb, gq, gkv, wq_p, wkv):
    s = h.shape[0]
    tm = min(512, s)
    row = lambda width, col: pl.BlockSpec((tm, width), lambda i: (i, col // width))
    full = lambda a: pl.BlockSpec(a.shape, lambda i: (0,) * a.ndim)
    return pl.pallas_call(
        _mla_prep_kernel,
        out_shape=[jax.ShapeDtypeStruct((MLA_HEADS, s, MLA_QK), BF16),
                   jax.ShapeDtypeStruct((MLA_HEADS, s, MLA_QK), BF16),
                   jax.ShapeDtypeStruct((MLA_HEADS, s, MLA_V), BF16)],
        grid=(s // tm,),
        in_specs=[row(MLA_Q_RANK, H_QA), row(MLA_KV_RANK, H_KVA), row(LANE, H_KR),
                  row(LANE, 0), row(LANE, 0), row(LANE, 0),
                  full(gq), full(gkv), full(wq_p), full(wkv)],
        out_specs=[pl.BlockSpec((MLA_HEADS, tm, MLA_QK), lambda i: (0, i, 0)),
                   pl.BlockSpec((MLA_HEADS, tm, MLA_QK), lambda i: (0, i, 0)),
                   pl.BlockSpec((MLA_HEADS, tm, MLA_V), lambda i: (0, i, 0))],
        compiler_params=_params(("parallel",), 48),
        name="mla_prep",
    )(h, h, h, cos, sa, sb, gq, gkv, wq_p, wkv)


def _mla_flash_kernel(q_ref, k_ref, v_ref, o_ref, *, tq, tk):
    qi = pl.program_id(1)
    q = q_ref[...]

    def step(j, carry, masked):
        m, l, acc = carry
        start = pl.multiple_of(j * tk, tk)
        k = k_ref[pl.ds(start, tk), :]
        v = v_ref[pl.ds(start, tk), :]
        s = _dot_nt(q, k)
        if masked:
            rows = lax.broadcasted_iota(jnp.int32, s.shape, 0) + qi * tq
            cols = lax.broadcasted_iota(jnp.int32, s.shape, 1) + j * tk
            s = jnp.where(cols <= rows, s, NEG_BIG)
        m_new = jnp.maximum(m, jnp.max(s, -1, keepdims=True))
        alpha = jnp.exp2(m - m_new)
        p = jnp.exp2(s - m_new)
        l = alpha * l + jnp.sum(p, -1, keepdims=True)
        acc = alpha * acc + _dot(p.astype(BF16), v)
        return m_new, l, acc

    init = (jnp.full((tq, 1), NEG_BIG, F32), jnp.zeros((tq, 1), F32), jnp.zeros((tq, MLA_V), F32))
    n_full = qi * (tq // tk)
    carry = lax.fori_loop(0, n_full, functools.partial(step, masked=False), init)
    for d in range(tq // tk):
        carry = step(n_full + d, carry, True)
    _, l, acc = carry
    o_ref[...] = (acc / l).astype(o_ref.dtype)


def mla_flash(q, k, v):
    _, s, _ = q.shape
    tq = min(512, s)
    tk = min(512, s)
    return pl.pallas_call(
        functools.partial(_mla_flash_kernel, tq=tq, tk=tk),
        out_shape=jax.ShapeDtypeStruct((s, MLA_HEADS * MLA_V), BF16),
        grid=(MLA_HEADS, s // tq),
        in_specs=[pl.BlockSpec((None, tq, MLA_QK), lambda h, i: (h, i, 0)),
                  pl.BlockSpec((None, s, MLA_QK), lambda h, i: (h, 0, 0)),
                  pl.BlockSpec((None, s, MLA_V), lambda h, i: (h, 0, 0))],
        out_specs=pl.BlockSpec((tq, MLA_V), lambda h, i: (i, h)),
        compiler_params=_params(("parallel", "arbitrary"), 48),
        name="mla_flash",
    )(q, k, v)


def _gla_masks():
    i = np.arange(GLA_L)[:, None]
    j = np.arange(GLA_L)[None, :]
    same = (i // GLA_SUB) == (j // GLA_SUB)
    mats = [j <= i, same & (j <= i), same & (j > i), j > i]
    for sub in range(GLA_NSUB - 1):
        mats.append((j >= (sub + 1) * GLA_SUB) & (j <= i))
    return np.concatenate(mats, 0).astype(np.float32)


def _gla_kernel(q_ref, k_ref, v_ref, lr_ref, r_ref, wg_ref, bg_ref, gain_ref, mask_ref, o_ref, state_ref):
    @pl.when(pl.program_id(0) == 0)
    def _():
        state_ref[...] = jnp.zeros_like(state_ref)

    L, hk = GLA_L, GLA_HEADS * GLA_DK
    z = _dot(lr_ref[...], wg_ref[...]) + bg_ref[...]
    g = (jnp.minimum(z, 0.0) - jnp.log(1.0 + jnp.exp(-jnp.abs(z)))) * (1.0 / GLA_TAU)
    g_hi = g.astype(BF16)
    g_lo = (g - g_hi.astype(F32)).astype(BF16)
    cums = _dot(mask_ref[...], g_hi) + _dot(mask_ref[...], g_lo)
    b_all = cums[0:L]
    b_loc = cums[L:2 * L]
    sfx_loc = cums[2 * L:3 * L]
    sfx_all = cums[3 * L:4 * L]

    q = q_ref[...].astype(F32) * (GLA_DK ** -0.5)
    k = k_ref[...].astype(F32)
    q_inter = (q * jnp.exp(b_all)).astype(BF16)
    q_diag = (q * jnp.exp(b_loc)).astype(BF16)
    k_diag = (k * jnp.exp(-b_loc)).astype(BF16)
    k_end = k * jnp.exp(sfx_loc)
    k_state = (k * jnp.exp(sfx_all)).astype(BF16)
    ones = jnp.ones((L, GLA_DV), BF16)

    row = lax.broadcasted_iota(jnp.int32, (L, L), 0)
    col = lax.broadcasted_iota(jnp.int32, (L, L), 1)
    diag_ok = ((row // GLA_SUB) == (col // GLA_SUB)) & (col <= row)
    off_ok = (row // GLA_SUB) > (col // GLA_SUB)
    sub_of_row = lax.broadcasted_iota(jnp.int32, (L, hk), 0) // GLA_SUB

    q_off, k_off = [], []
    for sub in range(GLA_NSUB - 1):
        q_off.append((q * jnp.exp(cums[(4 + sub) * L:(5 + sub) * L])).astype(BF16))
        k_off.append(jnp.where(sub_of_row == sub, k_end, 0.0).astype(BF16))

    v = v_ref[...]
    r = r_ref[...].astype(F32)
    gate = r * (1.0 / (1.0 + jnp.exp(-r)))
    for h in range(GLA_HEADS):
        ks = slice(h * GLA_DK, (h + 1) * GLA_DK)
        vs = slice(h * GLA_DV, (h + 1) * GLA_DV)
        state = state_ref[h]
        v_h = v[:, vs]
        a = jnp.where(diag_ok, _dot_nt(q_diag[:, ks], k_diag[:, ks]), 0.0)
        qo = jnp.concatenate([t[:, ks] for t in q_off], 1)
        ko = jnp.concatenate([t[:, ks] for t in k_off], 1)
        a = a + jnp.where(off_ok, _dot_nt(qo, ko), 0.0)
        o = _dot(q_inter[:, ks], state.astype(BF16)) + _dot(a.astype(BF16), v_h)
        decay = jnp.exp(_dot_tn(g_hi[:, ks], ones) + _dot_tn(g_lo[:, ks], ones))
        state_ref[h] = state * decay + _dot_tn(k_state[:, ks], v_h)
        o = o * lax.rsqrt(jnp.mean(o * o, -1, keepdims=True) + RMS_EPS) * gain_ref[:, vs]
        o_ref[:, vs] = (o * gate[:, vs]).astype(o_ref.dtype)


def gla(h, wg_p, bg, gain):
    s = h.shape[0]
    L = GLA_L
    masks = jnp.asarray(_gla_masks(), BF16)
    row = lambda width, col: pl.BlockSpec((L, width), lambda i: (i, col // width))
    full = lambda a: pl.BlockSpec(a.shape, lambda i: (0,) * a.ndim)
    hk, hv = GLA_HEADS * GLA_DK, GLA_HEADS * GLA_DV
    return pl.pallas_call(
        _gla_kernel,
        out_shape=jax.ShapeDtypeStruct((s, hv), BF16),
        grid=(s // L,),
        in_specs=[row(hk, H_GQ), row(hk, H_GK), row(hv, H_GV), row(LANE, H_GLR), row(hv, H_GR),
                  full(wg_p), full(bg), full(gain), full(masks)],
        out_specs=pl.BlockSpec((L, hv), lambda i: (i, 0)),
        scratch_shapes=[pltpu.VMEM((GLA_HEADS, GLA_DK, GLA_DV), F32)],
        compiler_params=_params(("arbitrary",), 32),
        name="gla",
    )(h, h, h, h, h, wg_p, bg, gain, masks)


def _t5_bucket_table():
    L = SWA_BLOCK
    dist = np.arange(L)[:, None] + L - np.arange(2 * L)[None, :]
    d = np.clip(dist, 0, None)
    max_exact = REL_BUCKETS // 2
    df = np.maximum(d, 1).astype(np.float32)
    large = max_exact + (np.log(df / np.float32(max_exact)) / np.float32(math.log(REL_MAX_DIST / max_exact))
                         * np.float32(REL_BUCKETS - max_exact)).astype(np.int32)
    large = np.minimum(large, REL_BUCKETS - 1)
    bucket = np.where(d < max_exact, d, large)
    in_window = (dist >= 0) & (dist < SWA_WINDOW)
    return np.where(in_window, bucket, -1).astype(np.int32)


def _swa_kernel(relb_ref, sink_ref, q_ref, kp_ref, kc_ref, vp_ref, vc_ref, bucket_ref, o_ref, bias_ref):
    i = pl.program_id(0)
    L = SWA_BLOCK

    @pl.when(i == 0)
    def _():
        bucket = bucket_ref[...]
        for h in range(SWA_HEADS):
            acc = jnp.full(bucket.shape, NEG_BIG, F32)
            for b in range(REL_BUCKETS):
                acc = jnp.where(bucket == b, relb_ref[b, h], acc)
            bias_ref[h] = acc

    kcat = jnp.concatenate([kp_ref[...], kc_ref[...]], 0)
    vcat = jnp.concatenate([vp_ref[...], vc_ref[...]], 0)
    q = q_ref[...]
    col = lax.broadcasted_iota(jnp.int32, (L, 2 * L), 1)
    real_key = (col >= L) | (i > 0)
    g = SWA_HEADS // SWA_KV_HEADS
    scale = SWA_HD ** -0.5
    for h in range(SWA_HEADS):
        kv = h // g
        hs = slice(h * SWA_HD, (h + 1) * SWA_HD)
        kvs = slice(kv * SWA_HD, (kv + 1) * SWA_HD)
        s = _dot_nt(q[:, hs], kcat[:, kvs]) * scale + bias_ref[h]
        s = jnp.where(real_key, s, NEG_BIG)
        sink = sink_ref[h]
        m = jnp.maximum(jnp.max(s, -1, keepdims=True), sink)
        p = jnp.exp(s - m)
        denom = jnp.sum(p, -1, keepdims=True) + jnp.exp(sink - m)
        o = _dot((p / denom).astype(BF16), vcat[:, kvs])
        o_ref[:, hs] = o.astype(o_ref.dtype)


def swa(h, sinks, rel_bias):
    s = h.shape[0]
    L = SWA_BLOCK
    bucket = jnp.asarray(_t5_bucket_table())
    kvw = SWA_KV_HEADS * SWA_HD
    hw = SWA_HEADS * SWA_HD
    cur = lambda width, col: pl.BlockSpec((L, width), lambda i, *_: (i, col // width))
    prev = lambda width, col: pl.BlockSpec((L, width), lambda i, *_: (jnp.maximum(i - 1, 0), col // width))
    return pl.pallas_call(
        _swa_kernel,
        out_shape=jax.ShapeDtypeStruct((s, hw), BF16),
        grid_spec=pltpu.PrefetchScalarGridSpec(
            num_scalar_prefetch=2,
            grid=(s // L,),
            in_specs=[cur(hw, H_SQ), prev(kvw, H_SK), cur(kvw, H_SK), prev(kvw, H_SV), cur(kvw, H_SV),
                      pl.BlockSpec(bucket.shape, lambda i, *_: (0, 0))],
            out_specs=pl.BlockSpec((L, hw), lambda i, *_: (i, 0)),
            scratch_shapes=[pltpu.VMEM((SWA_HEADS, L, 2 * L), F32)],
        ),
        compiler_params=_params(("arbitrary",), 32),
        name="swa",
    )(rel_bias, sinks, h, h, h, h, h, bucket)


def _layer_norm(y, g, b):
    mu = jnp.mean(y, -1, keepdims=True)
    yc = y - mu
    var = jnp.mean(yc * yc, -1, keepdims=True)
    return yc * lax.rsqrt(var + LN_EPS) * g + b


def _out_ln_kernel(a_ref, b_ref, c_ref, x_ref, w_ref, g_ref, beta_ref, o_ref):
    na, nb = a_ref.shape[1], b_ref.shape[1]
    m = _dot(a_ref[...], w_ref[0:na, :])
    m = m + _dot(b_ref[...], w_ref[na:na + nb, :])
    m = m + _dot(c_ref[...], w_ref[na + nb:, :])
    o_ref[...] = _layer_norm(DEEPNORM_ALPHA * x_ref[...] + m, g_ref[...], beta_ref[...])


def out_ln(a, b, c, x, w, g, beta):
    s = x.shape[0]
    tm = min(256, s)
    row = lambda arr: pl.BlockSpec((tm, arr.shape[1]), lambda i: (i, 0))
    full = lambda arr: pl.BlockSpec(arr.shape, lambda i: (0, 0))
    return pl.pallas_call(
        _out_ln_kernel,
        out_shape=jax.ShapeDtypeStruct((s, D_MODEL), F32),
        grid=(s // tm,),
        in_specs=[row(a), row(b), row(c), row(x), full(w), full(g), full(beta)],
        out_specs=pl.BlockSpec((tm, D_MODEL), lambda i: (i, 0)),
        compiler_params=_params(("parallel",), 56),
        name="out_ln",
    )(a, b, c, x, w, g, beta)


def _ffn_kernel(te_ref, na_ref, x_ref, wg_ref, wu_ref, wd_ref, o_ref, xb_ref):
    i, j = pl.program_id(0), pl.program_id(1)

    @pl.when(i < na_ref[0])
    def _():
        @pl.when(j == 0)
        def _():
            xb_ref[...] = x_ref[...].astype(BF16)

        xb = xb_ref[...]
        gate = _dot(xb, wg_ref[...].astype(BF16))
        up = _dot(xb, wu_ref[...].astype(BF16))
        hmid = (gate * (1.0 / (1.0 + jnp.exp(-gate))) * up).astype(BF16)
        part = _dot(hmid, wd_ref[...].astype(BF16))

        @pl.when(j == 0)
        def _():
            o_ref[...] = part

        @pl.when(j > 0)
        def _():
            o_ref[...] += part

    @pl.when((i >= na_ref[0]) & (j == 0))
    def _():
        o_ref[...] = jnp.zeros_like(o_ref)


def grouped_swiglu(xs, wg, wu, wd, tile_expert, n_active, tm, tf):
    rows = xs.shape[0]
    n_tiles, nf = rows // tm, D_FF // tf

    def tile(i, na):
        return jnp.minimum(i, na[0] - 1)

    def fcol(i, j, na):
        return jnp.where(i < na[0], j, nf - 1)

    return pl.pallas_call(
        _ffn_kernel,
        out_shape=jax.ShapeDtypeStruct((rows, D_MODEL), F32),
        grid_spec=pltpu.PrefetchScalarGridSpec(
            num_scalar_prefetch=2,
            grid=(n_tiles, nf),
            in_specs=[pl.BlockSpec((tm, D_MODEL), lambda i, j, te, na: (tile(i, na), 0)),
                      pl.BlockSpec((None, D_MODEL, tf), lambda i, j, te, na: (te[tile(i, na)], 0, fcol(i, j, na))),
                      pl.BlockSpec((None, D_MODEL, tf), lambda i, j, te, na: (te[tile(i, na)], 0, fcol(i, j, na))),
                      pl.BlockSpec((None, tf, D_MODEL), lambda i, j, te, na: (te[tile(i, na)], fcol(i, j, na), 0))],
            out_specs=pl.BlockSpec((tm, D_MODEL), lambda i, j, te, na: (i, 0)),
            scratch_shapes=[pltpu.VMEM((tm, D_MODEL), BF16)],
        ),
        compiler_params=_params(("arbitrary", "arbitrary"), 56),
        name="grouped_swiglu",
    )(tile_expert, n_active, xs, wg, wu, wd)


def _router_kernel(x_ref, w_ref, e_ref, r_ref, wt_ref, cnt_ref, run_ref):
    i = pl.program_id(0)
    tm = x_ref.shape[0]
    ne = N_EXPERTS

    @pl.when(i == 0)
    def _():
        run_ref[...] = jnp.zeros_like(run_ref)

    x = x_ref[...]
    x_hi = x.astype(BF16)
    x_lo = (x - x_hi.astype(F32)).astype(BF16)
    w = w_ref[...]
    w_hi = w.astype(BF16).astype(F32)
    w_lo = w - w_hi
    both = _dot_nt(jnp.concatenate([w_hi, w_lo], 0).astype(BF16), x_hi)
    cross = _dot_nt(jnp.concatenate([w_hi, jnp.zeros_like(w_hi)], 0).astype(BF16), x_lo)
    logits = both[0:ne] + both[ne:2 * ne] + cross[0:ne]

    eidx = lax.broadcasted_iota(jnp.int32, logits.shape, 0).astype(F32)
    v1 = jnp.max(logits, 0, keepdims=True)
    i1 = jnp.min(jnp.where(logits == v1, eidx, float(ne)), 0, keepdims=True)
    rest = jnp.where(eidx == i1, -jnp.inf, logits)
    v2 = jnp.max(rest, 0, keepdims=True)
    i2 = jnp.min(jnp.where(rest == v2, eidx, float(ne)), 0, keepdims=True)
    t = jnp.exp(v2 - v1)
    w1 = 1.0 / (1.0 + t)
    wt_ref[0:1, :] = w1
    wt_ref[1:2, :] = t * w1
    e_ref[0:1, :] = i1.astype(jnp.int32)
    e_ref[1:2, :] = i2.astype(jnp.int32)

    sel1, sel2 = eidx == i1, eidx == i2
    sel = jnp.where(sel1, 1.0, 0.0) + jnp.where(sel2, 1.0, 0.0)
    before = (lax.broadcasted_iota(jnp.int32, (tm, tm), 0) < lax.broadcasted_iota(jnp.int32, (tm, tm), 1))
    sel16 = jnp.concatenate([sel, jnp.zeros_like(sel)], 0).astype(BF16)
    prefix = _dot(sel16, jnp.where(before, 1.0, 0.0).astype(BF16))[0:ne]
    rank = prefix + run_ref[:, 0:1]
    r_ref[0:1, :] = jnp.sum(jnp.where(sel1, rank, 0.0), 0, keepdims=True).astype(jnp.int32)
    r_ref[1:2, :] = jnp.sum(jnp.where(sel2, rank, 0.0), 0, keepdims=True).astype(jnp.int32)
    run_ref[...] = run_ref[...] + jnp.sum(sel, 1, keepdims=True)
    cnt_ref[...] = run_ref[...].astype(jnp.int32)


def route_tokens(x, w_router_t):
    s = x.shape[0]
    tm = min(512, s)
    pair = pl.BlockSpec((2, tm), lambda i: (0, i))
    return pl.pallas_call(
        _router_kernel,
        out_shape=[jax.ShapeDtypeStruct((2, s), jnp.int32), jax.ShapeDtypeStruct((2, s), jnp.int32),
                   jax.ShapeDtypeStruct((2, s), F32), jax.ShapeDtypeStruct((N_EXPERTS, LANE), jnp.int32)],
        grid=(s // tm,),
        in_specs=[pl.BlockSpec((tm, D_MODEL), lambda i: (i, 0)),
                  pl.BlockSpec((N_EXPERTS, D_MODEL), lambda i: (0, 0))],
        out_specs=[pair, pair, pair, pl.BlockSpec((N_EXPERTS, LANE), lambda i: (0, 0))],
        scratch_shapes=[pltpu.VMEM((N_EXPERTS, LANE), F32)],
        compiler_params=_params(("arbitrary",), 32),
        name="moe_router",
    )(x, w_router_t)


def _dispatch_kernel(base_ref, fill_ref, e_ref, r_ref, x_ref, xs_ref, zero_ref, sem, zsem):
    i = pl.program_id(0)
    tm = x_ref.shape[0]

    def row_copy(t, slot):
        return pltpu.make_async_copy(x_ref.at[pl.ds(t, 1)], xs_ref.at[pl.ds(slot, 1)], sem)

    def issue(t, c):
        for k in range(2):
            row_copy(t, base_ref[e_ref[k, t]] + r_ref[k, t]).start()
        return c

    lax.fori_loop(0, tm, issue, 0)

    def drain(t, c):
        for k in range(2):
            row_copy(t, 0).wait()
        return c

    lax.fori_loop(0, tm, drain, 0)

    @pl.when(i == pl.num_programs(0) - 1)
    def _():
        zero_ref[...] = jnp.zeros_like(zero_ref)

        def zero_copy(slot):
            return pltpu.make_async_copy(zero_ref.at[pl.ds(0, 1)], xs_ref.at[pl.ds(slot, 1)], zsem)

        for e in range(N_EXPERTS):
            lo, hi = fill_ref[0, e], fill_ref[1, e]

            def zissue(slot, c):
                zero_copy(slot).start()
                return c

            def zdrain(slot, c):
                zero_copy(slot).wait()
                return c

            lax.fori_loop(lo, hi, zissue, 0)
            lax.fori_loop(lo, hi, zdrain, 0)

        zb = zero_ref.shape[0]

        def block_copy(b):
            return pltpu.make_async_copy(zero_ref, xs_ref.at[pl.ds(pl.multiple_of(b * zb, zb), zb)], zsem)

        def bissue(b, c):
            block_copy(b).start()
            return c

        def bdrain(b, c):
            block_copy(b).wait()
            return c

        first, last = fill_ref[2, 0] // zb, xs_ref.shape[0] // zb
        lax.fori_loop(first, last, bissue, 0)
        lax.fori_loop(first, last, bdrain, 0)


def moe_dispatch(x, e_idx, rank, base, fill, rows):
    s = x.shape[0]
    tm = min(256, s)
    zb = min(256, s)
    smem_pair = pl.BlockSpec((2, tm), lambda i, *_: (0, i), memory_space=pltpu.SMEM)
    return pl.pallas_call(
        _dispatch_kernel,
        out_shape=jax.ShapeDtypeStruct((rows, D_MODEL), F32),
        grid_spec=pltpu.PrefetchScalarGridSpec(
            num_scalar_prefetch=2,
            grid=(s // tm,),
            in_specs=[smem_pair, smem_pair, pl.BlockSpec((tm, D_MODEL), lambda i, *_: (i, 0))],
            out_specs=pl.BlockSpec(memory_space=pl.ANY),
            scratch_shapes=[pltpu.VMEM((zb, D_MODEL), F32), pltpu.SemaphoreType.DMA, pltpu.SemaphoreType.DMA],
        ),
        compiler_params=_params(("arbitrary",), 32),
        name="moe_dispatch",
    )(base, fill, e_idx, rank, x)


def _ple_ln_math(x, f, p_ref, wup_ref, wgate_ref, bgate_ref, g_ref, beta_ref):
    up = _dot(p_ref[...].astype(BF16), wup_ref[...])
    zg = _dot(x.astype(BF16), wgate_ref[...]) + bgate_ref[...]
    ple = up * (1.0 / (1.0 + jnp.exp(-zg)))
    return _layer_norm(DEEPNORM_ALPHA * x + f + ple, g_ref[...], beta_ref[...])


def _ple_ln_dense_kernel(x_ref, f_ref, p_ref, wup_ref, wgate_ref, bgate_ref, g_ref, beta_ref, o_ref):
    o_ref[...] = _ple_ln_math(x_ref[...], f_ref[...], p_ref, wup_ref, wgate_ref, bgate_ref, g_ref, beta_ref)


def _ple_ln_moe_kernel(base_ref, e_ref, r_ref, x_ref, wt_ref, p_ref, wup_ref, wgate_ref, bgate_ref, g_ref,
                       beta_ref, ys_ref, o_ref, y1_ref, y2_ref, sem):
    tm = x_ref.shape[0]
    bufs = (y1_ref, y2_ref)

    def row_copy(k, t, slot):
        return pltpu.make_async_copy(ys_ref.at[pl.ds(slot, 1)], bufs[k].at[pl.ds(t, 1)], sem)

    def issue(t, c):
        for k in range(2):
            row_copy(k, t, base_ref[e_ref[k, t]] + r_ref[k, t]).start()
        return c

    lax.fori_loop(0, tm, issue, 0)

    def drain(t, c):
        for k in range(2):
            row_copy(k, t, 0).wait()
        return c

    lax.fori_loop(0, tm, drain, 0)
    wt = wt_ref[...]
    f = wt[:, 0:1] * y1_ref[...] + wt[:, 1:2] * y2_ref[...]
    o_ref[...] = _ple_ln_math(x_ref[...], f, p_ref, wup_ref, wgate_ref, bgate_ref, g_ref, beta_ref)


def ple_ln(x, p, wup, wgate, bgate, g, beta, f=None, moe=None):
    s = x.shape[0]
    tm = min(256, s)
    row = lambda arr: pl.BlockSpec((tm, arr.shape[1]), lambda i, *_: (i, 0))
    full = lambda arr: pl.BlockSpec(arr.shape, lambda i, *_: (0, 0))
    tail = [row(p), full(wup), full(wgate), full(bgate), full(g), full(beta)]
    out_spec = pl.BlockSpec((tm, D_MODEL), lambda i, *_: (i, 0))
    out_shape = jax.ShapeDtypeStruct((s, D_MODEL), F32)
    if moe is None:
        return pl.pallas_call(
            _ple_ln_dense_kernel, out_shape=out_shape, grid=(s // tm,),
            in_specs=[row(x), row(f)] + tail, out_specs=out_spec,
            compiler_params=_params(("parallel",), 56), name="ple_ln_dense",
        )(x, f, p, wup, wgate, bgate, g, beta)
    ys, e_idx, rank, wts_t, base = moe
    smem_pair = pl.BlockSpec((2, tm), lambda i, *_: (0, i), memory_space=pltpu.SMEM)
    return pl.pallas_call(
        _ple_ln_moe_kernel,
        out_shape=out_shape,
        grid_spec=pltpu.PrefetchScalarGridSpec(
            num_scalar_prefetch=1,
            grid=(s // tm,),
            in_specs=[smem_pair, smem_pair, row(x), row(wts_t)] + tail + [pl.BlockSpec(memory_space=pl.ANY)],
            out_specs=out_spec,
            scratch_shapes=[pltpu.VMEM((tm, D_MODEL), F32), pltpu.VMEM((tm, D_MODEL), F32),
                            pltpu.SemaphoreType.DMA],
        ),
        compiler_params=_params(("arbitrary",), 56),
        name="ple_ln_moe",
    )(base, e_idx, rank, x, wts_t, p, wup, wgate, bgate, g, beta, ys)


def _pad_cols(w, width):
    return jnp.pad(w, ((0, 0), (0, width - w.shape[1])))


def _pack_w_in(w):
    sizes = (MLA_Q_RANK, MLA_KV_RANK, MLA_ROPE, GLA_HEADS * GLA_DK, GLA_HEADS * GLA_DK, GLA_HEADS * GLA_DV,
             GLA_GATE_RANK, GLA_HEADS * GLA_DV, SWA_HEADS * SWA_HD, SWA_KV_HEADS * SWA_HD, SWA_KV_HEADS * SWA_HD)
    splits = np.cumsum(sizes)[:-1]
    q_a, kv_a, k_rope, g_q, g_k, g_v, g_lr, g_r, s_q, s_k, s_v = jnp.split(w, splits, axis=1)
    pieces = [q_a, g_v, g_r, s_q, kv_a, g_q, g_k, _pad_cols(k_rope, LANE), _pad_cols(g_lr, LANE), s_k, s_v]
    return jnp.concatenate(pieces, 1).astype(BF16)


def _pack_w_q_b(w):
    w = w.reshape(MLA_Q_RANK, MLA_HEADS, MLA_NOPE + MLA_ROPE)
    w = jnp.pad(w, ((0, 0), (0, 0), (0, MLA_QK - MLA_NOPE - MLA_ROPE)))
    return w.reshape(MLA_Q_RANK, MLA_HEADS * MLA_QK).astype(BF16)


def _token_mixer_ln(x, rope, w_in, q_gain, w_q_b, kv_gain, w_kv_b, gla_w, gla_b, gla_gain, sinks, rel_bias,
                    w_out, ln_g, ln_b):
    h = proj_in(x, _pack_w_in(w_in))
    q, k, v = mla_prep(h, *rope, q_gain.reshape(1, -1), kv_gain.reshape(1, -1), _pack_w_q_b(w_q_b),
                       w_kv_b.astype(BF16))
    a = mla_flash(q, k, v)
    gla_w_p = jnp.pad(gla_w, ((0, LANE - GLA_GATE_RANK), (0, 0))).astype(BF16)
    b = gla(h, gla_w_p, gla_b.reshape(1, -1), gla_gain.reshape(1, -1))
    c = swa(h, sinks, rel_bias)
    return out_ln(a, b, c, x, w_out.astype(BF16), ln_g.reshape(1, -1), ln_b.reshape(1, -1))


def _moe_plan(counts, tm, n_tiles):
    tiles = (counts + tm - 1) // tm
    ends = jnp.cumsum(tiles)
    base = (ends - tiles) * tm
    n_active = ends[-1:].astype(jnp.int32)
    tile_expert = jnp.searchsorted(ends, jnp.arange(n_tiles, dtype=jnp.int32), side="right")
    tile_expert = jnp.minimum(tile_expert, N_EXPERTS - 1).astype(jnp.int32)
    used_rows = jnp.broadcast_to(ends[-1] * tm, counts.shape)
    fill = jnp.stack([base + counts, base + tiles * tm, used_rows]).astype(jnp.int32)
    return base.astype(jnp.int32), fill, tile_expert, n_active


def kernel(x, p, positions, w_in, mla_q_a_gain, mla_w_q_b, mla_kv_a_gain, mla_w_kv_b, gla_w_gate, gla_b_gate,
           gla_norm_gain, swa_sinks, rel_bias, w_out, ln1_g, ln1_b, ffn_w_gate, ffn_w_up, ffn_w_down,
           moe_router, moe_w_gate, moe_w_up, moe_w_down, ple_w_up, ple_w_gate, ple_b_gate, ln2_g, ln2_b):
    batch, s, _ = x.shape
    assert batch == 1
    xcur = x.reshape(s, D_MODEL)
    rope = rope_tables(positions)
    for i in range(DEPTH):
        x1 = _token_mixer_ln(xcur, rope, w_in[i], mla_q_a_gain[i], mla_w_q_b[i], mla_kv_a_gain[i],
                             mla_w_kv_b[i], gla_w_gate[i], gla_b_gate[i], gla_norm_gain[i], swa_sinks[i],
                             rel_bias, w_out[i], ln1_g[i], ln1_b[i])
        tail = (p[i, 0], ple_w_up[i].astype(BF16), ple_w_gate[i].astype(BF16), ple_b_gate[i].reshape(1, -1),
                ln2_g[i].reshape(1, -1), ln2_b[i].reshape(1, -1))
        j = i // 2
        if i % 2 == 0:
            tm = min(512, s)
            n_tiles = s // tm
            f = grouped_swiglu(x1, ffn_w_gate[j][None], ffn_w_up[j][None], ffn_w_down[j][None],
                               jnp.zeros((n_tiles,), jnp.int32), jnp.full((1,), n_tiles, jnp.int32), tm, 256)
            xcur = ple_ln(x1, *tail, f=f)
        else:
            tm = min(512, s)
            n_tiles = 2 * s // tm + N_EXPERTS
            e_idx, rank, wts, counts = route_tokens(x1, moe_router[j].T)
            base, fill, tile_expert, n_active = _moe_plan(counts[:, 0], tm, n_tiles)
            xs = moe_dispatch(x1, e_idx, rank, base, fill, n_tiles * tm)
            ys = grouped_swiglu(xs, moe_w_gate[j], moe_w_up[j], moe_w_down[j], tile_expert, n_active, tm, 256)
            xcur = ple_ln(x1, *tail, moe=(ys, e_idx, rank, wts.T, base))
    return xcur.reshape(batch, s, D_MODEL)
```

```python
import functools
import math

import numpy as np
import jax
import jax.numpy as jnp
from jax import lax
from jax.experimental import pallas as pl
from jax.experimental.pallas import tpu as pltpu

F32 = jnp.float32
BF16 = jnp.bfloat16

D_MODEL = 2048
DEPTH = 2
MLA_HEADS = 8
MLA_Q_RANK = 512
MLA_KV_RANK = 256
MLA_NOPE = 128
MLA_ROPE = 64
MLA_V = 128
ROPE_THETA = 10000.0
GLA_HEADS = 4
GLA_DK = 64
GLA_DV = 128
GLA_GATE_RANK = 16
GLA_TAU = 16.0
SWA_HEADS = 8
SWA_KV_HEADS = 2
SWA_HD = 64
SWA_WINDOW = 128
SWA_BLOCK = 128
REL_BUCKETS = 32
REL_MAX_DIST = 128
D_FF = 5632
N_EXPERTS = 8
PLE_DIM = 256
LN_EPS = 1e-5
RMS_EPS = 1e-6
DEEPNORM_ALPHA = (2 * DEPTH) ** 0.25

LANE = 128
LOG2E = math.log2(math.e)
NEG_BIG = -1e30

H_QA, H_GV, H_GR, H_SQ = 0, 512, 1024, 1536
H_KVA, H_GQ, H_GK = 2048, 2304, 2560
H_KR, H_GLR, H_SK, H_SV = 2816, 2944, 3072, 3200
H_COLS = 3328

MLA_QK = 2 * LANE

GLA_L = 128
GLA_SUB = 32
GLA_NSUB = GLA_L // GLA_SUB


def _params(sem, vmem_mb):
    return pltpu.CompilerParams(dimension_semantics=sem, vmem_limit_bytes=vmem_mb * 2 ** 20)


def _dot(a, b):
    return jnp.dot(a, b, preferred_element_type=F32)


def _dot_nt(a, b):
    return lax.dot_general(a, b, (((1,), (1,)), ((), ())), preferred_element_type=F32)


def _dot_tn(a, b):
    return lax.dot_general(a, b, (((0,), (0,)), ((), ())), preferred_element_type=F32)


def _proj_in_kernel(x_ref, w_ref, o_ref, xb_ref):
    @pl.when(pl.program_id(1) == 0)
    def _():
        xb_ref[...] = x_ref[...].astype(BF16)

    o_ref[...] = _dot(xb_ref[...], w_ref[...]).astype(o_ref.dtype)


def proj_in(x, w_p):
    s = x.shape[0]
    tm = min(512, s)
    tn = H_COLS // 2
    return pl.pallas_call(
        _proj_in_kernel,
        out_shape=jax.ShapeDtypeStruct((s, H_COLS), BF16),
        grid=(s // tm, H_COLS // tn),
        in_specs=[pl.BlockSpec((tm, D_MODEL), lambda i, j: (i, 0)),
                  pl.BlockSpec((D_MODEL, tn), lambda i, j: (0, j))],
        out_specs=pl.BlockSpec((tm, tn), lambda i, j: (i, j)),
        scratch_shapes=[pltpu.VMEM((tm, D_MODEL), BF16)],
        compiler_params=_params(("parallel", "arbitrary"), 48),
        name="proj_in",
    )(x, w_p)


def _rope_table_kernel(pos_ref, inv_ref, cos_ref, sa_ref, sb_ref):
    ang = pos_ref[...].astype(F32) * inv_ref[...]
    lane = lax.broadcasted_iota(jnp.int32, ang.shape, 1)
    half = MLA_ROPE // 2
    c, s = jnp.cos(ang), jnp.sin(ang)
    cos_ref[...] = c
    sa_ref[...] = jnp.where((lane >= half) & (lane < 2 * half), s, 0.0)
    sb_ref[...] = jnp.where(lane < half, -s, 0.0)


def rope_tables(positions):
    s = positions.shape[-1]
    half = MLA_ROPE // 2
    inv = ROPE_THETA ** (-jnp.arange(half, dtype=F32) / half)
    inv = jnp.concatenate([inv, inv, jnp.zeros((LANE - 2 * half,), F32)]).reshape(1, LANE)
    tm = min(1024, s)
    spec = pl.BlockSpec((tm, LANE), lambda i: (i, 0))
    return pl.pallas_call(
        _rope_table_kernel,
        out_shape=[jax.ShapeDtypeStruct((s, LANE), F32)] * 3,
        grid=(s // tm,),
        in_specs=[pl.BlockSpec((tm, 1), lambda i: (i, 0)), pl.BlockSpec((1, LANE), lambda i: (0, 0))],
        out_specs=[spec, spec, spec],
        compiler_params=_params(("parallel",), 32),
        name="rope_tables",
    )(positions.reshape(s, 1), inv)


def _rope(x, cos, sa, sb):
    return x * cos + pltpu.roll(x, MLA_ROPE // 2, 1) * sa + pltpu.roll(x, LANE - MLA_ROPE // 2, 1) * sb


def _mla_prep_kernel(qa_ref, kva_ref, kr_ref, cos_ref, sa_ref, sb_ref, gq_ref, gkv_ref, wq_ref, wkv_ref,
                     q_out, k_out, v_out):
    cos, sa, sb = cos_ref[...], sa_ref[...], sb_ref[...]
    qscale = (MLA_NOPE + MLA_ROPE) ** -0.5 * LOG2E

    qa = qa_ref[...].astype(F32)
    qn = qa * lax.rsqrt(jnp.mean(qa * qa, -1, keepdims=True) + RMS_EPS) * gq_ref[...]
    q = _dot(qn.astype(BF16), wq_ref[...])
    for h in range(MLA_HEADS):
        c0 = h * MLA_QK
        q_out[h, :, 0:LANE] = (q[:, c0:c0 + LANE] * qscale).astype(BF16)
        pe = _rope(q[:, c0 + LANE:c0 + 2 * LANE], cos, sa, sb)
        q_out[h, :, LANE:2 * LANE] = (pe * qscale).astype(BF16)

    kva = kva_ref[...].astype(F32)
    kvn = kva * lax.rsqrt(jnp.mean(kva * kva, -1, keepdims=True) + RMS_EPS) * gkv_ref[...]
    kv = _dot(kvn.astype(BF16), wkv_ref[...])
    kpe = _rope(kr_ref[...].astype(F32), cos, sa, sb).astype(BF16)
    for h in range(MLA_HEADS):
        c0 = h * (MLA_NOPE + MLA_V)
        k_out[h, :, 0:LANE] = kv[:, c0:c0 + MLA_NOPE].astype(BF16)
        k_out[h, :, LANE:2 * LANE] = kpe
        v_out[h] = kv[:, c0 + MLA_NOPE:c0 + MLA_NOPE + MLA_V].astype(BF16)


def mla_prep(h, cos, sa, sb, gq, gkv, wq_p, wkv):
    s = h.shape[0]
    tm = min(512, s)
    row = lambda width, col: pl.BlockSpec((tm, width), lambda i: (i, col // width))
    full = lambda a: pl.BlockSpec(a.shape, lambda i: (0,) * a.ndim)
    return pl.pallas_call(
        _mla_prep_kernel,
        out_shape=[jax.ShapeDtypeStruct((MLA_HEADS, s, MLA_QK), BF16),
                   jax.ShapeDtypeStruct((MLA_HEADS, s, MLA_QK), BF16),
                   jax.ShapeDtypeStruct((MLA_HEADS, s, MLA_V), BF16)],
        grid=(s // tm,),
        in_specs=[row(MLA_Q_RANK, H_QA), row(MLA_KV_RANK, H_KVA), row(LANE, H_KR),
                  row(LANE, 0), row(LANE, 0), row(LANE, 0),
                  full(gq), full(gkv), full(wq_p), full(wkv)],
        out_specs=[pl.BlockSpec((MLA_HEADS, tm, MLA_QK), lambda i: (0, i, 0)),
                   pl.BlockSpec((MLA_HEADS, tm, MLA_QK), lambda i: (0, i, 0)),
                   pl.BlockSpec((MLA_HEADS, tm, MLA_V), lambda i: (0, i, 0))],
        compiler_params=_params(("parallel",), 48),
        name="mla_prep",
    )(h, h, h, cos, sa, sb, gq, gkv, wq_p, wkv)


def _mla_flash_kernel(q_ref, k_ref, v_ref, o_ref, *, tq, tk, nh):
    qi = pl.program_id(1)

    def step_one(h, j, carry, masked):
        m, l, acc = carry
        start = pl.multiple_of(j * tk, tk)
        k = k_ref[h, pl.ds(start, tk), :]
        v = v_ref[h, pl.ds(start, tk), :]
        s = _dot_nt(q_ref[h], k)
        if masked:
            rows = lax.broadcasted_iota(jnp.int32, s.shape, 0) + qi * tq
            cols = lax.broadcasted_iota(jnp.int32, s.shape, 1) + j * tk
            s = jnp.where(cols <= rows, s, NEG_BIG)
        m_new = jnp.maximum(m, jnp.max(s, -1, keepdims=True))
        alpha = jnp.exp2(m - m_new)
        p = jnp.exp2(s - m_new)
        l = alpha * l + jnp.sum(p, -1, keepdims=True)
        acc = alpha * acc + _dot(p.astype(BF16), v)
        return m_new, l, acc

    def step(j, carries, masked):
        return tuple(step_one(h, j, carries[h], masked) for h in range(nh))

    init = (jnp.full((tq, 1), NEG_BIG, F32), jnp.zeros((tq, 1), F32), jnp.zeros((tq, MLA_V), F32))
    n_full = qi * (tq // tk)
    carries = lax.fori_loop(0, n_full, functools.partial(step, masked=False), (init,) * nh)
    for d in range(tq // tk):
        carries = step(n_full + d, carries, True)
    for h in range(nh):
        _, l, acc = carries[h]
        o_ref[:, h * MLA_V:(h + 1) * MLA_V] = (acc / l).astype(o_ref.dtype)


def mla_flash(q, k, v):
    _, s, _ = q.shape
    tq = min(512, s)
    tk = min(512, s)
    nh = 2
    return pl.pallas_call(
        functools.partial(_mla_flash_kernel, tq=tq, tk=tk, nh=nh),
        out_shape=jax.ShapeDtypeStruct((s, MLA_HEADS * MLA_V), BF16),
        grid=(MLA_HEADS // nh, s // tq),
        in_specs=[pl.BlockSpec((nh, tq, MLA_QK), lambda h, i: (h, i, 0)),
                  pl.BlockSpec((nh, s, MLA_QK), lambda h, i: (h, 0, 0)),
                  pl.BlockSpec((nh, s, MLA_V), lambda h, i: (h, 0, 0))],
        out_specs=pl.BlockSpec((tq, nh * MLA_V), lambda h, i: (i, h)),
        compiler_params=_params(("parallel", "arbitrary"), 48),
        name="mla_flash",
    )(q, k, v)


def _gla_masks():
    i = np.arange(GLA_L)[:, None]
    j = np.arange(GLA_L)[None, :]
    same = (i // GLA_SUB) == (j // GLA_SUB)
    mats = [j <= i, same & (j <= i), same & (j > i), j > i]
    for sub in range(GLA_NSUB - 1):
        mats.append((j >= (sub + 1) * GLA_SUB) & (j <= i))
    return np.concatenate(mats, 0).astype(np.float32)


def _gla_kernel(q_ref, k_ref, v_ref, lr_ref, r_ref, wg_ref, bg_ref, gain_ref, mask_ref, o_ref, state_ref):
    @pl.when(pl.program_id(0) == 0)
    def _():
        state_ref[...] = jnp.zeros_like(state_ref)

    L, hk = GLA_L, GLA_HEADS * GLA_DK
    z = _dot(lr_ref[...], wg_ref[...]) + bg_ref[...]
    g = (jnp.minimum(z, 0.0) - jnp.log(1.0 + jnp.exp(-jnp.abs(z)))) * (1.0 / GLA_TAU)
    g_hi = g.astype(BF16)
    g_lo = (g - g_hi.astype(F32)).astype(BF16)
    cums = _dot(mask_ref[...], g_hi) + _dot(mask_ref[...], g_lo)
    b_all = cums[0:L]
    b_loc = cums[L:2 * L]
    sfx_loc = cums[2 * L:3 * L]
    sfx_all = cums[3 * L:4 * L]

    q = q_ref[...].astype(F32) * (GLA_DK ** -0.5)
    k = k_ref[...].astype(F32)
    q_inter = (q * jnp.exp(b_all)).astype(BF16)
    q_diag = (q * jnp.exp(b_loc)).astype(BF16)
    k_diag = (k * jnp.exp(-b_loc)).astype(BF16)
    k_end = k * jnp.exp(sfx_loc)
    k_state = (k * jnp.exp(sfx_all)).astype(BF16)
    ones = jnp.ones((L, GLA_DV), BF16)

    row = lax.broadcasted_iota(jnp.int32, (L, L), 0)
    col = lax.broadcasted_iota(jnp.int32, (L, L), 1)
    diag_ok = ((row // GLA_SUB) == (col // GLA_SUB)) & (col <= row)
    off_ok = (row // GLA_SUB) > (col // GLA_SUB)
    sub_of_row = lax.broadcasted_iota(jnp.int32, (L, hk), 0) // GLA_SUB

    q_off, k_off = [], []
    for sub in range(GLA_NSUB - 1):
        q_off.append((q * jnp.exp(cums[(4 + sub) * L:(5 + sub) * L])).astype(BF16))
        k_off.append(jnp.where(sub_of_row == sub, k_end, 0.0).astype(BF16))

    v = v_ref[...]
    r = r_ref[...].astype(F32)
    gate = r * (1.0 / (1.0 + jnp.exp(-r)))
    for h in range(GLA_HEADS):
        ks = slice(h * GLA_DK, (h + 1) * GLA_DK)
        vs = slice(h * GLA_DV, (h + 1) * GLA_DV)
        state = state_ref[h]
        v_h = v[:, vs]
        a = jnp.where(diag_ok, _dot_nt(q_diag[:, ks], k_diag[:, ks]), 0.0)
        qo = jnp.concatenate([t[:, ks] for t in q_off], 1)
        ko = jnp.concatenate([t[:, ks] for t in k_off], 1)
        a = a + jnp.where(off_ok, _dot_nt(qo, ko), 0.0)
        o = _dot(q_inter[:, ks], state.astype(BF16)) + _dot(a.astype(BF16), v_h)
        decay = jnp.exp(_dot_tn(g_hi[:, ks], ones) + _dot_tn(g_lo[:, ks], ones))
        state_ref[h] = state * decay + _dot_tn(k_state[:, ks], v_h)
        o = o * lax.rsqrt(jnp.mean(o * o, -1, keepdims=True) + RMS_EPS) * gain_ref[:, vs]
        o_ref[:, vs] = (o * gate[:, vs]).astype(o_ref.dtype)


def gla(h, wg_p, bg, gain):
    s = h.shape[0]
    L = GLA_L
    masks = jnp.asarray(_gla_masks(), BF16)
    row = lambda width, col: pl.BlockSpec((L, width), lambda i: (i, col // width))
    full = lambda a: pl.BlockSpec(a.shape, lambda i: (0,) * a.ndim)
    hk, hv = GLA_HEADS * GLA_DK, GLA_HEADS * GLA_DV
    return pl.pallas_call(
        _gla_kernel,
        out_shape=jax.ShapeDtypeStruct((s, hv), BF16),
        grid=(s // L,),
        in_specs=[row(hk, H_GQ), row(hk, H_GK), row(hv, H_GV), row(LANE, H_GLR), row(hv, H_GR),
                  full(wg_p), full(bg), full(gain), full(masks)],
        out_specs=pl.BlockSpec((L, hv), lambda i: (i, 0)),
        scratch_shapes=[pltpu.VMEM((GLA_HEADS, GLA_DK, GLA_DV), F32)],
        compiler_params=_params(("arbitrary",), 32),
        name="gla",
    )(h, h, h, h, h, wg_p, bg, gain, masks)


def _t5_bucket_table():
    L = SWA_BLOCK
    dist = np.arange(L)[:, None] + L - np.arange(2 * L)[None, :]
    d = np.clip(dist, 0, None)
    max_exact = REL_BUCKETS // 2
    df = np.maximum(d, 1).astype(np.float32)
    large = max_exact + (np.log(df / np.float32(max_exact)) / np.float32(math.log(REL_MAX_DIST / max_exact))
                         * np.float32(REL_BUCKETS - max_exact)).astype(np.int32)
    large = np.minimum(large, REL_BUCKETS - 1)
    bucket = np.where(d < max_exact, d, large)
    in_window = (dist >= 0) & (dist < SWA_WINDOW)
    return np.where(in_window, bucket, -1).astype(np.int32)


def _swa_kernel(relb_ref, sink_ref, q_ref, kp_ref, kc_ref, vp_ref, vc_ref, bucket_ref, o_ref, bias_ref):
    i = pl.program_id(0)
    L = SWA_BLOCK

    @pl.when(i == 0)
    def _():
        bucket = bucket_ref[...]
        for h in range(SWA_HEADS):
            acc = jnp.full(bucket.shape, NEG_BIG, F32)
            for b in range(REL_BUCKETS):
                acc = jnp.where(bucket == b, relb_ref[b, h], acc)
            bias_ref[h] = acc

    kcat = jnp.concatenate([kp_ref[...], kc_ref[...]], 0)
    vcat = jnp.concatenate([vp_ref[...], vc_ref[...]], 0)
    q = q_ref[...]
    col = lax.broadcasted_iota(jnp.int32, (L, 2 * L), 1)
    real_key = (col >= L) | (i > 0)
    g = SWA_HEADS // SWA_KV_HEADS
    scale = SWA_HD ** -0.5
    for h in range(SWA_HEADS):
        kv = h // g
        hs = slice(h * SWA_HD, (h + 1) * SWA_HD)
        kvs = slice(kv * SWA_HD, (kv + 1) * SWA_HD)
        s = _dot_nt(q[:, hs], kcat[:, kvs]) * scale + bias_ref[h]
        s = jnp.where(real_key, s, NEG_BIG)
        sink = sink_ref[h]
        m = jnp.maximum(jnp.max(s, -1, keepdims=True), sink)
        p = jnp.exp(s - m)
        denom = jnp.sum(p, -1, keepdims=True) + jnp.exp(sink - m)
        o = _dot((p / denom).astype(BF16), vcat[:, kvs])
        o_ref[:, hs] = o.astype(o_ref.dtype)


def swa(h, sinks, rel_bias):
    s = h.shape[0]
    L = SWA_BLOCK
    bucket = jnp.asarray(_t5_bucket_table())
    kvw = SWA_KV_HEADS * SWA_HD
    hw = SWA_HEADS * SWA_HD
    cur = lambda width, col: pl.BlockSpec((L, width), lambda i, *_: (i, col // width))
    prev = lambda width, col: pl.BlockSpec((L, width), lambda i, *_: (jnp.maximum(i - 1, 0), col // width))
    return pl.pallas_call(
        _swa_kernel,
        out_shape=jax.ShapeDtypeStruct((s, hw), BF16),
        grid_spec=pltpu.PrefetchScalarGridSpec(
            num_scalar_prefetch=2,
            grid=(s // L,),
            in_specs=[cur(hw, H_SQ), prev(kvw, H_SK), cur(kvw, H_SK), prev(kvw, H_SV), cur(kvw, H_SV),
                      pl.BlockSpec(bucket.shape, lambda i, *_: (0, 0))],
            out_specs=pl.BlockSpec((L, hw), lambda i, *_: (i, 0)),
            scratch_shapes=[pltpu.VMEM((SWA_HEADS, L, 2 * L), F32)],
        ),
        compiler_params=_params(("arbitrary",), 32),
        name="swa",
    )(rel_bias, sinks, h, h, h, h, h, bucket)


def _layer_norm(y, g, b):
    mu = jnp.mean(y, -1, keepdims=True)
    yc = y - mu
    var = jnp.mean(yc * yc, -1, keepdims=True)
    return yc * lax.rsqrt(var + LN_EPS) * g + b


HALF = D_MODEL // 2
HIGH16 = 0xFFFF0000


def _pack_bf16_pairs(x):
    lo = lax.bitcast_convert_type(x[:, :HALF].astype(BF16).astype(F32), jnp.uint32) >> 16
    hi = lax.bitcast_convert_type(x[:, HALF:].astype(BF16).astype(F32), jnp.uint32) & jnp.uint32(HIGH16)
    return lo | hi


def _unpack_bf16_pairs(xp):
    lo = lax.bitcast_convert_type(xp << 16, F32).astype(BF16)
    hi = lax.bitcast_convert_type(xp & jnp.uint32(HIGH16), F32).astype(BF16)
    return lo, hi


def _out_ln_kernel(a_ref, b_ref, c_ref, x_ref, w_ref, g_ref, beta_ref, o_ref, op_ref):
    na, nb = a_ref.shape[1], b_ref.shape[1]
    m = _dot(a_ref[...], w_ref[0:na, :])
    m = m + _dot(b_ref[...], w_ref[na:na + nb, :])
    m = m + _dot(c_ref[...], w_ref[na + nb:, :])
    y = _layer_norm(DEEPNORM_ALPHA * x_ref[...] + m, g_ref[...], beta_ref[...])
    o_ref[...] = y
    op_ref[...] = _pack_bf16_pairs(y)


def out_ln(a, b, c, x, w, g, beta):
    s = x.shape[0]
    tm = min(256, s)
    row = lambda arr: pl.BlockSpec((tm, arr.shape[1]), lambda i: (i, 0))
    full = lambda arr: pl.BlockSpec(arr.shape, lambda i: (0, 0))
    return pl.pallas_call(
        _out_ln_kernel,
        out_shape=[jax.ShapeDtypeStruct((s, D_MODEL), F32), jax.ShapeDtypeStruct((s, HALF), jnp.uint32)],
        grid=(s // tm,),
        in_specs=[row(a), row(b), row(c), row(x), full(w), full(g), full(beta)],
        out_specs=[pl.BlockSpec((tm, D_MODEL), lambda i: (i, 0)), pl.BlockSpec((tm, HALF), lambda i: (i, 0))],
        compiler_params=_params(("parallel",), 56),
        name="out_ln",
    )(a, b, c, x, w, g, beta)


FFN_ROWS = 1024
FFN_COLS = 512
FFN_OUT_CHUNK = 512


def _ffn_kernel(te_ref, na_ref, xp_ref, wg_ref, wu_ref, wd_ref, o_ref, xb_ref):
    i, j = pl.program_id(0), pl.program_id(1)

    @pl.when(j == 0)
    def _():
        o_ref[...] = jnp.zeros_like(o_ref)

    @pl.when(i < na_ref[0])
    def _():
        @pl.when(j == 0)
        def _():
            lo, hi = _unpack_bf16_pairs(xp_ref[...])
            xb_ref[:, :HALF] = lo
            xb_ref[:, HALF:] = hi

        xb = xb_ref[...]
        gate = _dot(xb, wg_ref[...].astype(BF16))
        up = _dot(xb, wu_ref[...].astype(BF16))
        hmid = (gate * (1.0 / (1.0 + jnp.exp(-gate))) * up).astype(BF16)
        for c in range(0, D_MODEL, FFN_OUT_CHUNK):
            cs = slice(c, c + FFN_OUT_CHUNK)
            o_ref[:, cs] += _dot(hmid, wd_ref[:, cs].astype(BF16))


def grouped_swiglu(xp, wg, wu, wd, tile_expert, n_active, tm, tf):
    rows = xp.shape[0]
    n_tiles, nf = rows // tm, D_FF // tf

    def tile(i, na):
        return jnp.minimum(i, na[0] - 1)

    def fcol(i, j, na):
        return jnp.where(i < na[0], j, nf - 1)

    once = pl.Buffered(1)
    return pl.pallas_call(
        _ffn_kernel,
        out_shape=jax.ShapeDtypeStruct((rows, D_MODEL), F32),
        grid_spec=pltpu.PrefetchScalarGridSpec(
            num_scalar_prefetch=2,
            grid=(n_tiles, nf),
            in_specs=[pl.BlockSpec((tm, HALF), lambda i, j, te, na: (tile(i, na), 0), pipeline_mode=once),
                      pl.BlockSpec((None, D_MODEL, tf), lambda i, j, te, na: (te[tile(i, na)], 0, fcol(i, j, na))),
                      pl.BlockSpec((None, D_MODEL, tf), lambda i, j, te, na: (te[tile(i, na)], 0, fcol(i, j, na))),
                      pl.BlockSpec((None, tf, D_MODEL), lambda i, j, te, na: (te[tile(i, na)], fcol(i, j, na), 0))],
            out_specs=pl.BlockSpec((tm, D_MODEL), lambda i, j, te, na: (i, 0), pipeline_mode=once),
            scratch_shapes=[pltpu.VMEM((tm, D_MODEL), BF16)],
        ),
        compiler_params=_params(("arbitrary", "arbitrary"), 60),
        name="grouped_swiglu",
    )(tile_expert, n_active, xp, wg, wu, wd)


def _router_kernel(x_ref, w_ref, e_ref, r_ref, wt_ref, cnt_ref, run_ref):
    i = pl.program_id(0)
    tm = x_ref.shape[0]
    ne = N_EXPERTS

    @pl.when(i == 0)
    def _():
        run_ref[...] = jnp.zeros_like(run_ref)

    x = x_ref[...]
    x_hi = x.astype(BF16)
    x_lo = (x - x_hi.astype(F32)).astype(BF16)
    w = w_ref[...]
    w_hi = w.astype(BF16).astype(F32)
    w_lo = w - w_hi
    both = _dot_nt(jnp.concatenate([w_hi, w_lo], 0).astype(BF16), x_hi)
    cross = _dot_nt(jnp.concatenate([w_hi, jnp.zeros_like(w_hi)], 0).astype(BF16), x_lo)
    logits = both[0:ne] + both[ne:2 * ne] + cross[0:ne]

    eidx = lax.broadcasted_iota(jnp.int32, logits.shape, 0).astype(F32)
    v1 = jnp.max(logits, 0, keepdims=True)
    i1 = jnp.min(jnp.where(logits == v1, eidx, float(ne)), 0, keepdims=True)
    rest = jnp.where(eidx == i1, -jnp.inf, logits)
    v2 = jnp.max(rest, 0, keepdims=True)
    i2 = jnp.min(jnp.where(rest == v2, eidx, float(ne)), 0, keepdims=True)
    t = jnp.exp(v2 - v1)
    w1 = 1.0 / (1.0 + t)
    wt_ref[0:1, :] = w1
    wt_ref[1:2, :] = t * w1
    e_ref[0:1, :] = i1.astype(jnp.int32)
    e_ref[1:2, :] = i2.astype(jnp.int32)

    sel1, sel2 = eidx == i1, eidx == i2
    sel = jnp.where(sel1, 1.0, 0.0) + jnp.where(sel2, 1.0, 0.0)
    before = (lax.broadcasted_iota(jnp.int32, (tm, tm), 0) < lax.broadcasted_iota(jnp.int32, (tm, tm), 1))
    sel16 = jnp.concatenate([sel, jnp.zeros_like(sel)], 0).astype(BF16)
    prefix = _dot(sel16, jnp.where(before, 1.0, 0.0).astype(BF16))[0:ne]
    rank = prefix + run_ref[:, 0:1]
    r_ref[0:1, :] = jnp.sum(jnp.where(sel1, rank, 0.0), 0, keepdims=True).astype(jnp.int32)
    r_ref[1:2, :] = jnp.sum(jnp.where(sel2, rank, 0.0), 0, keepdims=True).astype(jnp.int32)
    run_ref[...] = run_ref[...] + jnp.sum(sel, 1, keepdims=True)
    cnt_ref[...] = run_ref[...].astype(jnp.int32)


def route_tokens(x, w_router_t):
    s = x.shape[0]
    tm = min(512, s)
    pair = pl.BlockSpec((2, tm), lambda i: (0, i))
    return pl.pallas_call(
        _router_kernel,
        out_shape=[jax.ShapeDtypeStruct((2, s), jnp.int32), jax.ShapeDtypeStruct((2, s), jnp.int32),
                   jax.ShapeDtypeStruct((2, s), F32), jax.ShapeDtypeStruct((N_EXPERTS, LANE), jnp.int32)],
        grid=(s // tm,),
        in_specs=[pl.BlockSpec((tm, D_MODEL), lambda i: (i, 0)),
                  pl.BlockSpec((N_EXPERTS, D_MODEL), lambda i: (0, 0))],
        out_specs=[pair, pair, pair, pl.BlockSpec((N_EXPERTS, LANE), lambda i: (0, 0))],
        scratch_shapes=[pltpu.VMEM((N_EXPERTS, LANE), F32)],
        compiler_params=_params(("arbitrary",), 32),
        name="moe_router",
    )(x, w_router_t)


def _dispatch_kernel(base_ref, fill_ref, e_ref, r_ref, x_ref, xs_ref, zero_ref, sem, zsem):
    i = pl.program_id(0)
    tm = x_ref.shape[0]

    def row_copy(t, slot):
        return pltpu.make_async_copy(x_ref.at[pl.ds(t, 1)], xs_ref.at[pl.ds(slot, 1)], sem)

    def issue(t, c):
        for k in range(2):
            row_copy(t, base_ref[e_ref[k, t]] + r_ref[k, t]).start()
        return c

    lax.fori_loop(0, tm, issue, 0)

    def drain(t, c):
        for k in range(2):
            row_copy(t, 0).wait()
        return c

    lax.fori_loop(0, tm, drain, 0)

    @pl.when(i == pl.num_programs(0) - 1)
    def _():
        zero_ref[...] = jnp.zeros_like(zero_ref)

        def zero_copy(slot):
            return pltpu.make_async_copy(zero_ref.at[pl.ds(0, 1)], xs_ref.at[pl.ds(slot, 1)], zsem)

        for e in range(N_EXPERTS):
            lo, hi = fill_ref[0, e], fill_ref[1, e]

            def zissue(slot, c):
                zero_copy(slot).start()
                return c

            def zdrain(slot, c):
                zero_copy(slot).wait()
                return c

            lax.fori_loop(lo, hi, zissue, 0)
            lax.fori_loop(lo, hi, zdrain, 0)

        zb = zero_ref.shape[0]

        def block_copy(b):
            return pltpu.make_async_copy(zero_ref, xs_ref.at[pl.ds(pl.multiple_of(b * zb, zb), zb)], zsem)

        def bissue(b, c):
            block_copy(b).start()
            return c

        def bdrain(b, c):
            block_copy(b).wait()
            return c

        first, last = fill_ref[2, 0] // zb, xs_ref.shape[0] // zb
        lax.fori_loop(first, last, bissue, 0)
        lax.fori_loop(first, last, bdrain, 0)


def moe_dispatch(xp, e_idx, rank, base, fill, rows):
    s, width = xp.shape
    tm = min(256, s)
    zb = min(256, s)
    smem_pair = pl.BlockSpec((2, tm), lambda i, *_: (0, i), memory_space=pltpu.SMEM)
    return pl.pallas_call(
        _dispatch_kernel,
        out_shape=jax.ShapeDtypeStruct((rows, width), xp.dtype),
        grid_spec=pltpu.PrefetchScalarGridSpec(
            num_scalar_prefetch=2,
            grid=(s // tm,),
            in_specs=[smem_pair, smem_pair, pl.BlockSpec((tm, width), lambda i, *_: (i, 0))],
            out_specs=pl.BlockSpec(memory_space=pl.ANY),
            scratch_shapes=[pltpu.VMEM((zb, width), xp.dtype), pltpu.SemaphoreType.DMA, pltpu.SemaphoreType.DMA],
        ),
        compiler_params=_params(("arbitrary",), 32),
        name="moe_dispatch",
    )(base, fill, e_idx, rank, xp)


def _ple_ln_math(x, f, p_ref, wup_ref, wgate_ref, bgate_ref, g_ref, beta_ref):
    up = _dot(p_ref[...].astype(BF16), wup_ref[...])
    zg = _dot(x.astype(BF16), wgate_ref[...]) + bgate_ref[...]
    ple = up * (1.0 / (1.0 + jnp.exp(-zg)))
    return _layer_norm(DEEPNORM_ALPHA * x + f + ple, g_ref[...], beta_ref[...])


def _ple_ln_dense_kernel(x_ref, f_ref, p_ref, wup_ref, wgate_ref, bgate_ref, g_ref, beta_ref, o_ref):
    o_ref[...] = _ple_ln_math(x_ref[...], f_ref[...], p_ref, wup_ref, wgate_ref, bgate_ref, g_ref, beta_ref)


def _ple_ln_moe_kernel(base_ref, e_ref, r_ref, x_ref, wt_ref, p_ref, wup_ref, wgate_ref, bgate_ref, g_ref,
                       beta_ref, ys_ref, o_ref, y1_ref, y2_ref, sem):
    tm = x_ref.shape[0]
    bufs = (y1_ref, y2_ref)

    def row_copy(k, t, slot):
        return pltpu.make_async_copy(ys_ref.at[pl.ds(slot, 1)], bufs[k].at[pl.ds(t, 1)], sem)

    def issue(t, c):
        for k in range(2):
            row_copy(k, t, base_ref[e_ref[k, t]] + r_ref[k, t]).start()
        return c

    lax.fori_loop(0, tm, issue, 0)

    def drain(t, c):
        for k in range(2):
            row_copy(k, t, 0).wait()
        return c

    lax.fori_loop(0, tm, drain, 0)
    wt = wt_ref[...]
    f = wt[:, 0:1] * y1_ref[...] + wt[:, 1:2] * y2_ref[...]
    o_ref[...] = _ple_ln_math(x_ref[...], f, p_ref, wup_ref, wgate_ref, bgate_ref, g_ref, beta_ref)


def ple_ln(x, p, wup, wgate, bgate, g, beta, f=None, moe=None):
    s = x.shape[0]
    tm = min(256, s)
    row = lambda arr: pl.BlockSpec((tm, arr.shape[1]), lambda i, *_: (i, 0))
    full = lambda arr: pl.BlockSpec(arr.shape, lambda i, *_: (0, 0))
    tail = [row(p), full(wup), full(wgate), full(bgate), full(g), full(beta)]
    out_spec = pl.BlockSpec((tm, D_MODEL), lambda i, *_: (i, 0))
    out_shape = jax.ShapeDtypeStruct((s, D_MODEL), F32)
    if moe is None:
        return pl.pallas_call(
            _ple_ln_dense_kernel, out_shape=out_shape, grid=(s // tm,),
            in_specs=[row(x), row(f)] + tail, out_specs=out_spec,
            compiler_params=_params(("parallel",), 56), name="ple_ln_dense",
        )(x, f, p, wup, wgate, bgate, g, beta)
    ys, e_idx, rank, wts_t, base = moe
    smem_pair = pl.BlockSpec((2, tm), lambda i, *_: (0, i), memory_space=pltpu.SMEM)
    return pl.pallas_call(
        _ple_ln_moe_kernel,
        out_shape=out_shape,
        grid_spec=pltpu.PrefetchScalarGridSpec(
            num_scalar_prefetch=1,
            grid=(s // tm,),
            in_specs=[smem_pair, smem_pair, row(x), row(wts_t)] + tail + [pl.BlockSpec(memory_space=pl.ANY)],
            out_specs=out_spec,
            scratch_shapes=[pltpu.VMEM((tm, D_MODEL), F32), pltpu.VMEM((tm, D_MODEL), F32),
                            pltpu.SemaphoreType.DMA],
        ),
        compiler_params=_params(("arbitrary",), 56),
        name="ple_ln_moe",
    )(base, e_idx, rank, x, wts_t, p, wup, wgate, bgate, g, beta, ys)


def _pad_cols(w, width):
    return jnp.pad(w, ((0, 0), (0, width - w.shape[1])))


def _pack_w_in(w):
    sizes = (MLA_Q_RANK, MLA_KV_RANK, MLA_ROPE, GLA_HEADS * GLA_DK, GLA_HEADS * GLA_DK, GLA_HEADS * GLA_DV,
             GLA_GATE_RANK, GLA_HEADS * GLA_DV, SWA_HEADS * SWA_HD, SWA_KV_HEADS * SWA_HD, SWA_KV_HEADS * SWA_HD)
    splits = np.cumsum(sizes)[:-1]
    q_a, kv_a, k_rope, g_q, g_k, g_v, g_lr, g_r, s_q, s_k, s_v = jnp.split(w, splits, axis=1)
    pieces = [q_a, g_v, g_r, s_q, kv_a, g_q, g_k, _pad_cols(k_rope, LANE), _pad_cols(g_lr, LANE), s_k, s_v]
    return jnp.concatenate(pieces, 1).astype(BF16)


def _pack_w_q_b(w):
    w = w.reshape(MLA_Q_RANK, MLA_HEADS, MLA_NOPE + MLA_ROPE)
    w = jnp.pad(w, ((0, 0), (0, 0), (0, MLA_QK - MLA_NOPE - MLA_ROPE)))
    return w.reshape(MLA_Q_RANK, MLA_HEADS * MLA_QK).astype(BF16)


def _token_mixer_ln(x, rope, w_in, q_gain, w_q_b, kv_gain, w_kv_b, gla_w, gla_b, gla_gain, sinks, rel_bias,
                    w_out, ln_g, ln_b):
    h = proj_in(x, _pack_w_in(w_in))
    q, k, v = mla_prep(h, *rope, q_gain.reshape(1, -1), kv_gain.reshape(1, -1), _pack_w_q_b(w_q_b),
                       w_kv_b.astype(BF16))
    a = mla_flash(q, k, v)
    gla_w_p = jnp.pad(gla_w, ((0, LANE - GLA_GATE_RANK), (0, 0))).astype(BF16)
    b = gla(h, gla_w_p, gla_b.reshape(1, -1), gla_gain.reshape(1, -1))
    c = swa(h, sinks, rel_bias)
    return out_ln(a, b, c, x, w_out.astype(BF16), ln_g.reshape(1, -1), ln_b.reshape(1, -1))


def _moe_plan(counts, tm, n_tiles):
    tiles = (counts + tm - 1) // tm
    ends = jnp.cumsum(tiles)
    base = (ends - tiles) * tm
    n_active = ends[-1:].astype(jnp.int32)
    tile_expert = jnp.searchsorted(ends, jnp.arange(n_tiles, dtype=jnp.int32), side="right")
    tile_expert = jnp.minimum(tile_expert, N_EXPERTS - 1).astype(jnp.int32)
    used_rows = jnp.broadcast_to(ends[-1] * tm, counts.shape)
    fill = jnp.stack([base + counts, base + tiles * tm, used_rows]).astype(jnp.int32)
    return base.astype(jnp.int32), fill, tile_expert, n_active


def kernel(x, p, positions, w_in, mla_q_a_gain, mla_w_q_b, mla_kv_a_gain, mla_w_kv_b, gla_w_gate, gla_b_gate,
           gla_norm_gain, swa_sinks, rel_bias, w_out, ln1_g, ln1_b, ffn_w_gate, ffn_w_up, ffn_w_down,
           moe_router, moe_w_gate, moe_w_up, moe_w_down, ple_w_up, ple_w_gate, ple_b_gate, ln2_g, ln2_b):
    batch, s, _ = x.shape
    assert batch == 1
    xcur = x.reshape(s, D_MODEL)
    rope = rope_tables(positions)
    tm = min(FFN_ROWS, s)
    for i in range(DEPTH):
        x1, x1p = _token_mixer_ln(xcur, rope, w_in[i], mla_q_a_gain[i], mla_w_q_b[i], mla_kv_a_gain[i],
                                  mla_w_kv_b[i], gla_w_gate[i], gla_b_gate[i], gla_norm_gain[i], swa_sinks[i],
                                  rel_bias, w_out[i], ln1_g[i], ln1_b[i])
        tail = (p[i, 0], ple_w_up[i].astype(BF16), ple_w_gate[i].astype(BF16), ple_b_gate[i].reshape(1, -1),
                ln2_g[i].reshape(1, -1), ln2_b[i].reshape(1, -1))
        j = i // 2
        if i % 2 == 0:
            n_tiles = s // tm
            f = grouped_swiglu(x1p, ffn_w_gate[j][None], ffn_w_up[j][None], ffn_w_down[j][None],
                               jnp.zeros((n_tiles,), jnp.int32), jnp.full((1,), n_tiles, jnp.int32), tm, FFN_COLS)
            xcur = ple_ln(x1, *tail, f=f)
        else:
            n_tiles = (2 * s + N_EXPERTS * (tm - 1)) // tm
            e_idx, rank, wts, counts = route_tokens(x1, moe_router[j].T)
            base, fill, tile_expert, n_active = _moe_plan(counts[:, 0], tm, n_tiles)
            xs = moe_dispatch(x1p, e_idx, rank, base, fill, n_tiles * tm)
            ys = grouped_swiglu(xs, moe_w_gate[j], moe_w_up[j], moe_w_down[j], tile_expert, n_active, tm, FFN_COLS)
            xcur = ple_ln(x1, *tail, moe=(ys, e_idx, rank, wts.T, base))
    return xcur.reshape(batch, s, D_MODEL)
```

```python
import functools
import math

import numpy as np
import jax
import jax.numpy as jnp
from jax import lax
from jax.experimental import pallas as pl
from jax.experimental.pallas import tpu as pltpu

F32 = jnp.float32
BF16 = jnp.bfloat16

D_MODEL = 2048
DEPTH = 2
MLA_HEADS = 8
MLA_Q_RANK = 512
MLA_KV_RANK = 256
MLA_NOPE = 128
MLA_ROPE = 64
MLA_V = 128
ROPE_THETA = 10000.0
GLA_HEADS = 4
GLA_DK = 64
GLA_DV = 128
GLA_GATE_RANK = 16
GLA_TAU = 16.0
SWA_HEADS = 8
SWA_KV_HEADS = 2
SWA_HD = 64
SWA_WINDOW = 128
SWA_BLOCK = 128
REL_BUCKETS = 32
REL_MAX_DIST = 128
D_FF = 5632
N_EXPERTS = 8
PLE_DIM = 256
LN_EPS = 1e-5
RMS_EPS = 1e-6
DEEPNORM_ALPHA = (2 * DEPTH) ** 0.25

LANE = 128
LOG2E = math.log2(math.e)
NEG_BIG = -1e30

H_QA, H_GV, H_GR, H_SQ = 0, 512, 1024, 1536
H_KVA, H_GQ, H_GK = 2048, 2304, 2560
H_KR, H_GLR, H_SK, H_SV = 2816, 2944, 3072, 3200
H_COLS = 3328

MLA_QK = 2 * LANE

GLA_L = 128
GLA_SUB = 32
GLA_NSUB = GLA_L // GLA_SUB


def _params(sem, vmem_mb):
    return pltpu.CompilerParams(dimension_semantics=sem, vmem_limit_bytes=vmem_mb * 2 ** 20)


def _dot(a, b):
    return jnp.dot(a, b, preferred_element_type=F32)


def _dot_nt(a, b):
    return lax.dot_general(a, b, (((1,), (1,)), ((), ())), preferred_element_type=F32)


def _dot_tn(a, b):
    return lax.dot_general(a, b, (((0,), (0,)), ((), ())), preferred_element_type=F32)


def _proj_in_kernel(x_ref, w_ref, o_ref, xb_ref):
    @pl.when(pl.program_id(1) == 0)
    def _():
        xb_ref[...] = x_ref[...].astype(BF16)

    o_ref[...] = _dot(xb_ref[...], w_ref[...]).astype(o_ref.dtype)


def proj_in(x, w_p):
    s = x.shape[0]
    tm = min(512, s)
    tn = H_COLS // 2
    return pl.pallas_call(
        _proj_in_kernel,
        out_shape=jax.ShapeDtypeStruct((s, H_COLS), BF16),
        grid=(s // tm, H_COLS // tn),
        in_specs=[pl.BlockSpec((tm, D_MODEL), lambda i, j: (i, 0)),
                  pl.BlockSpec((D_MODEL, tn), lambda i, j: (0, j))],
        out_specs=pl.BlockSpec((tm, tn), lambda i, j: (i, j)),
        scratch_shapes=[pltpu.VMEM((tm, D_MODEL), BF16)],
        compiler_params=_params(("parallel", "arbitrary"), 48),
        name="proj_in",
    )(x, w_p)


def _rope_table_kernel(pos_ref, inv_ref, cos_ref, sa_ref, sb_ref):
    ang = pos_ref[...].astype(F32) * inv_ref[...]
    lane = lax.broadcasted_iota(jnp.int32, ang.shape, 1)
    half = MLA_ROPE // 2
    c, s = jnp.cos(ang), jnp.sin(ang)
    cos_ref[...] = c
    sa_ref[...] = jnp.where((lane >= half) & (lane < 2 * half), s, 0.0)
    sb_ref[...] = jnp.where(lane < half, -s, 0.0)


def rope_tables(positions):
    s = positions.shape[-1]
    half = MLA_ROPE // 2
    inv = ROPE_THETA ** (-jnp.arange(half, dtype=F32) / half)
    inv = jnp.concatenate([inv, inv, jnp.zeros((LANE - 2 * half,), F32)]).reshape(1, LANE)
    tm = min(1024, s)
    spec = pl.BlockSpec((tm, LANE), lambda i: (i, 0))
    return pl.pallas_call(
        _rope_table_kernel,
        out_shape=[jax.ShapeDtypeStruct((s, LANE), F32)] * 3,
        grid=(s // tm,),
        in_specs=[pl.BlockSpec((tm, 1), lambda i: (i, 0)), pl.BlockSpec((1, LANE), lambda i: (0, 0))],
        out_specs=[spec, spec, spec],
        compiler_params=_params(("parallel",), 32),
        name="rope_tables",
    )(positions.reshape(s, 1), inv)


def _rope(x, cos, sa, sb):
    return x * cos + pltpu.roll(x, MLA_ROPE // 2, 1) * sa + pltpu.roll(x, LANE - MLA_ROPE // 2, 1) * sb


def _mla_prep_kernel(qa_ref, kva_ref, kr_ref, cos_ref, sa_ref, sb_ref, gq_ref, gkv_ref, wq_ref, wkv_ref,
                     q_out, k_out, v_out):
    cos, sa, sb = cos_ref[...], sa_ref[...], sb_ref[...]
    qscale = (MLA_NOPE + MLA_ROPE) ** -0.5 * LOG2E

    qa = qa_ref[...].astype(F32)
    qn = qa * lax.rsqrt(jnp.mean(qa * qa, -1, keepdims=True) + RMS_EPS) * gq_ref[...]
    q = _dot(qn.astype(BF16), wq_ref[...])
    for h in range(MLA_HEADS):
        c0 = h * MLA_QK
        q_out[h, :, 0:LANE] = (q[:, c0:c0 + LANE] * qscale).astype(BF16)
        pe = _rope(q[:, c0 + LANE:c0 + 2 * LANE], cos, sa, sb)
        q_out[h, :, LANE:2 * LANE] = (pe * qscale).astype(BF16)

    kva = kva_ref[...].astype(F32)
    kvn = kva * lax.rsqrt(jnp.mean(kva * kva, -1, keepdims=True) + RMS_EPS) * gkv_ref[...]
    kv = _dot(kvn.astype(BF16), wkv_ref[...])
    kpe = _rope(kr_ref[...].astype(F32), cos, sa, sb).astype(BF16)
    for h in range(MLA_HEADS):
        c0 = h * (MLA_NOPE + MLA_V)
        k_out[h, :, 0:LANE] = kv[:, c0:c0 + MLA_NOPE].astype(BF16)
        k_out[h, :, LANE:2 * LANE] = kpe
        v_out[h] = kv[:, c0 + MLA_NOPE:c0 + MLA_NOPE + MLA_V].astype(BF16)


def mla_prep(h, cos, sa, sb, gq, gkv, wq_p, wkv):
    s = h.shape[0]
    tm = min(512, s)
    row = lambda width, col: pl.BlockSpec((tm, width), lambda i: (i, col // width))
    full = lambda a: pl.BlockSpec(a.shape, lambda i: (0,) * a.ndim)
    return pl.pallas_call(
        _mla_prep_kernel,
        out_shape=[jax.ShapeDtypeStruct((MLA_HEADS, s, MLA_QK), BF16),
                   jax.ShapeDtypeStruct((MLA_HEADS, s, MLA_QK), BF16),
                   jax.ShapeDtypeStruct((MLA_HEADS, s, MLA_V), BF16)],
        grid=(s // tm,),
        in_specs=[row(MLA_Q_RANK, H_QA), row(MLA_KV_RANK, H_KVA), row(LANE, H_KR),
                  row(LANE, 0), row(LANE, 0), row(LANE, 0),
                  full(gq), full(gkv), full(wq_p), full(wkv)],
        out_specs=[pl.BlockSpec((MLA_HEADS, tm, MLA_QK), lambda i: (0, i, 0)),
                   pl.BlockSpec((MLA_HEADS, tm, MLA_QK), lambda i: (0, i, 0)),
                   pl.BlockSpec((MLA_HEADS, tm, MLA_V), lambda i: (0, i, 0))],
        compiler_params=_params(("parallel",), 48),
        name="mla_prep",
    )(h, h, h, cos, sa, sb, gq, gkv, wq_p, wkv)


def _mla_flash_kernel(q_ref, k_ref, v_ref, o_ref, *, tq, tk, nh):
    qi = pl.program_id(1)

    def step_one(h, j, carry, masked):
        m, l, acc = carry
        start = pl.multiple_of(j * tk, tk)
        k = k_ref[h, pl.ds(start, tk), :]
        v = v_ref[h, pl.ds(start, tk), :]
        s = _dot_nt(q_ref[h], k)
        if masked:
            rows = lax.broadcasted_iota(jnp.int32, s.shape, 0) + qi * tq
            cols = lax.broadcasted_iota(jnp.int32, s.shape, 1) + j * tk
            s = jnp.where(cols <= rows, s, NEG_BIG)
        m_new = jnp.maximum(m, jnp.max(s, -1, keepdims=True))
        alpha = jnp.exp2(m - m_new)
        p = jnp.exp2(s - m_new)
        l = alpha * l + jnp.sum(p, -1, keepdims=True)
        acc = alpha * acc + _dot(p.astype(BF16), v)
        return m_new, l, acc

    def step(j, carries, masked):
        return tuple(step_one(h, j, carries[h], masked) for h in range(nh))

    init = (jnp.full((tq, 1), NEG_BIG, F32), jnp.zeros((tq, 1), F32), jnp.zeros((tq, MLA_V), F32))
    n_full = qi * (tq // tk)
    carries = lax.fori_loop(0, n_full, functools.partial(step, masked=False), (init,) * nh)
    for d in range(tq // tk):
        carries = step(n_full + d, carries, True)
    for h in range(nh):
        _, l, acc = carries[h]
        o_ref[:, h * MLA_V:(h + 1) * MLA_V] = (acc / l).astype(o_ref.dtype)


def mla_flash(q, k, v):
    _, s, _ = q.shape
    tq = min(512, s)
    tk = min(512, s)
    nh = 2
    return pl.pallas_call(
        functools.partial(_mla_flash_kernel, tq=tq, tk=tk, nh=nh),
        out_shape=jax.ShapeDtypeStruct((s, MLA_HEADS * MLA_V), BF16),
        grid=(MLA_HEADS // nh, s // tq),
        in_specs=[pl.BlockSpec((nh, tq, MLA_QK), lambda h, i: (h, i, 0)),
                  pl.BlockSpec((nh, s, MLA_QK), lambda h, i: (h, 0, 0)),
                  pl.BlockSpec((nh, s, MLA_V), lambda h, i: (h, 0, 0))],
        out_specs=pl.BlockSpec((tq, nh * MLA_V), lambda h, i: (i, h)),
        compiler_params=_params(("parallel", "arbitrary"), 48),
        name="mla_flash",
    )(q, k, v)


def _gla_masks():
    i = np.arange(GLA_L)[:, None]
    j = np.arange(GLA_L)[None, :]
    same = (i // GLA_SUB) == (j // GLA_SUB)
    mats = [j <= i, same & (j <= i), same & (j > i), j > i]
    for sub in range(GLA_NSUB - 1):
        mats.append((j >= (sub + 1) * GLA_SUB) & (j <= i))
    return np.concatenate(mats, 0).astype(np.float32)


def _gla_kernel(q_ref, k_ref, v_ref, lr_ref, r_ref, wg_ref, bg_ref, gain_ref, mask_ref, o_ref, state_ref):
    @pl.when(pl.program_id(0) == 0)
    def _():
        state_ref[...] = jnp.zeros_like(state_ref)

    L, hk = GLA_L, GLA_HEADS * GLA_DK
    z = _dot(lr_ref[...], wg_ref[...]) + bg_ref[...]
    g = (jnp.minimum(z, 0.0) - jnp.log(1.0 + jnp.exp(-jnp.abs(z)))) * (1.0 / GLA_TAU)
    g_hi = g.astype(BF16)
    g_lo = (g - g_hi.astype(F32)).astype(BF16)
    cums = _dot(mask_ref[...], g_hi) + _dot(mask_ref[...], g_lo)
    b_all = cums[0:L]
    b_loc = cums[L:2 * L]
    sfx_loc = cums[2 * L:3 * L]
    sfx_all = cums[3 * L:4 * L]

    q = q_ref[...].astype(F32) * (GLA_DK ** -0.5)
    k = k_ref[...].astype(F32)
    q_inter = (q * jnp.exp(b_all)).astype(BF16)
    q_diag = (q * jnp.exp(b_loc)).astype(BF16)
    k_diag = (k * jnp.exp(-b_loc)).astype(BF16)
    k_end = k * jnp.exp(sfx_loc)
    k_state = (k * jnp.exp(sfx_all)).astype(BF16)
    ones = jnp.ones((L, GLA_DV), BF16)

    row = lax.broadcasted_iota(jnp.int32, (L, L), 0)
    col = lax.broadcasted_iota(jnp.int32, (L, L), 1)
    diag_ok = ((row // GLA_SUB) == (col // GLA_SUB)) & (col <= row)
    off_ok = (row // GLA_SUB) > (col // GLA_SUB)
    sub_of_row = lax.broadcasted_iota(jnp.int32, (L, hk), 0) // GLA_SUB

    q_off, k_off = [], []
    for sub in range(GLA_NSUB - 1):
        q_off.append((q * jnp.exp(cums[(4 + sub) * L:(5 + sub) * L])).astype(BF16))
        k_off.append(jnp.where(sub_of_row == sub, k_end, 0.0).astype(BF16))

    v = v_ref[...]
    r = r_ref[...].astype(F32)
    gate = r * (1.0 / (1.0 + jnp.exp(-r)))
    for h in range(GLA_HEADS):
        ks = slice(h * GLA_DK, (h + 1) * GLA_DK)
        vs = slice(h * GLA_DV, (h + 1) * GLA_DV)
        state = state_ref[h]
        v_h = v[:, vs]
        a = jnp.where(diag_ok, _dot_nt(q_diag[:, ks], k_diag[:, ks]), 0.0)
        qo = jnp.concatenate([t[:, ks] for t in q_off], 1)
        ko = jnp.concatenate([t[:, ks] for t in k_off], 1)
        a = a + jnp.where(off_ok, _dot_nt(qo, ko), 0.0)
        o = _dot(q_inter[:, ks], state.astype(BF16)) + _dot(a.astype(BF16), v_h)
        decay = jnp.exp(_dot_tn(g_hi[:, ks], ones) + _dot_tn(g_lo[:, ks], ones))
        state_ref[h] = state * decay + _dot_tn(k_state[:, ks], v_h)
        o = o * lax.rsqrt(jnp.mean(o * o, -1, keepdims=True) + RMS_EPS) * gain_ref[:, vs]
        o_ref[:, vs] = (o * gate[:, vs]).astype(o_ref.dtype)


def gla(h, wg_p, bg, gain):
    s = h.shape[0]
    L = GLA_L
    masks = jnp.asarray(_gla_masks(), BF16)
    row = lambda width, col: pl.BlockSpec((L, width), lambda i: (i, col // width))
    full = lambda a: pl.BlockSpec(a.shape, lambda i: (0,) * a.ndim)
    hk, hv = GLA_HEADS * GLA_DK, GLA_HEADS * GLA_DV
    return pl.pallas_call(
        _gla_kernel,
        out_shape=jax.ShapeDtypeStruct((s, hv), BF16),
        grid=(s // L,),
        in_specs=[row(hk, H_GQ), row(hk, H_GK), row(hv, H_GV), row(LANE, H_GLR), row(hv, H_GR),
                  full(wg_p), full(bg), full(gain), full(masks)],
        out_specs=pl.BlockSpec((L, hv), lambda i: (i, 0)),
        scratch_shapes=[pltpu.VMEM((GLA_HEADS, GLA_DK, GLA_DV), F32)],
        compiler_params=_params(("arbitrary",), 32),
        name="gla",
    )(h, h, h, h, h, wg_p, bg, gain, masks)


def _t5_bucket_table():
    L = SWA_BLOCK
    dist = np.arange(L)[:, None] + L - np.arange(2 * L)[None, :]
    d = np.clip(dist, 0, None)
    max_exact = REL_BUCKETS // 2
    df = np.maximum(d, 1).astype(np.float32)
    large = max_exact + (np.log(df / np.float32(max_exact)) / np.float32(math.log(REL_MAX_DIST / max_exact))
                         * np.float32(REL_BUCKETS - max_exact)).astype(np.int32)
    large = np.minimum(large, REL_BUCKETS - 1)
    bucket = np.where(d < max_exact, d, large)
    in_window = (dist >= 0) & (dist < SWA_WINDOW)
    return np.where(in_window, bucket, -1).astype(np.int32)


def _swa_kernel(relb_ref, sink_ref, q_ref, kp_ref, kc_ref, vp_ref, vc_ref, bucket_ref, o_ref, bias_ref):
    i = pl.program_id(0)
    L = SWA_BLOCK

    @pl.when(i == 0)
    def _():
        bucket = bucket_ref[...]
        for h in range(SWA_HEADS):
            acc = jnp.full(bucket.shape, NEG_BIG, F32)
            for b in range(REL_BUCKETS):
                acc = jnp.where(bucket == b, relb_ref[b, h], acc)
            bias_ref[h] = acc

    kcat = jnp.concatenate([kp_ref[...], kc_ref[...]], 0)
    vcat = jnp.concatenate([vp_ref[...], vc_ref[...]], 0)
    q = q_ref[...]
    col = lax.broadcasted_iota(jnp.int32, (L, 2 * L), 1)
    real_key = (col >= L) | (i > 0)
    g = SWA_HEADS // SWA_KV_HEADS
    scale = SWA_HD ** -0.5
    for h in range(SWA_HEADS):
        kv = h // g
        hs = slice(h * SWA_HD, (h + 1) * SWA_HD)
        kvs = slice(kv * SWA_HD, (kv + 1) * SWA_HD)
        s = _dot_nt(q[:, hs], kcat[:, kvs]) * scale + bias_ref[h]
        s = jnp.where(real_key, s, NEG_BIG)
        sink = sink_ref[h]
        m = jnp.maximum(jnp.max(s, -1, keepdims=True), sink)
        p = jnp.exp(s - m)
        denom = jnp.sum(p, -1, keepdims=True) + jnp.exp(sink - m)
        o = _dot((p / denom).astype(BF16), vcat[:, kvs])
        o_ref[:, hs] = o.astype(o_ref.dtype)


def swa(h, sinks, rel_bias):
    s = h.shape[0]
    L = SWA_BLOCK
    bucket = jnp.asarray(_t5_bucket_table())
    kvw = SWA_KV_HEADS * SWA_HD
    hw = SWA_HEADS * SWA_HD
    cur = lambda width, col: pl.BlockSpec((L, width), lambda i, *_: (i, col // width))
    prev = lambda width, col: pl.BlockSpec((L, width), lambda i, *_: (jnp.maximum(i - 1, 0), col // width))
    return pl.pallas_call(
        _swa_kernel,
        out_shape=jax.ShapeDtypeStruct((s, hw), BF16),
        grid_spec=pltpu.PrefetchScalarGridSpec(
            num_scalar_prefetch=2,
            grid=(s // L,),
            in_specs=[cur(hw, H_SQ), prev(kvw, H_SK), cur(kvw, H_SK), prev(kvw, H_SV), cur(kvw, H_SV),
                      pl.BlockSpec(bucket.shape, lambda i, *_: (0, 0))],
            out_specs=pl.BlockSpec((L, hw), lambda i, *_: (i, 0)),
            scratch_shapes=[pltpu.VMEM((SWA_HEADS, L, 2 * L), F32)],
        ),
        compiler_params=_params(("arbitrary",), 32),
        name="swa",
    )(rel_bias, sinks, h, h, h, h, h, bucket)


def _layer_norm(y, g, b):
    mu = jnp.mean(y, -1, keepdims=True)
    yc = y - mu
    var = jnp.mean(yc * yc, -1, keepdims=True)
    return yc * lax.rsqrt(var + LN_EPS) * g + b


HALF = D_MODEL // 2
HIGH16 = 0xFFFF0000


SUBLANE = 8
TOK = HALF // LANE


def _store_token_tiles(ref, x):
    tm = x.shape[0]
    lo = lax.bitcast_convert_type(x[:, :HALF].astype(BF16).astype(F32), jnp.uint32) >> 16
    hi = lax.bitcast_convert_type(x[:, HALF:].astype(BF16).astype(F32), jnp.uint32) & jnp.uint32(HIGH16)
    packed = lo | hi
    for s in range(TOK):
        ref[pl.ds(s, tm, stride=TOK), :] = packed[:, s * LANE:(s + 1) * LANE]


def _load_token_tiles(ref, rows=None):
    first, tm = rows if rows is not None else (0, ref.shape[0] // TOK)
    lo, hi = [], []
    for s in range(TOK):
        w = ref[pl.ds(first * TOK + s, tm, stride=TOK), :]
        lo.append(lax.bitcast_convert_type(w << 16, F32))
        hi.append(lax.bitcast_convert_type(w & jnp.uint32(HIGH16), F32))
    return jnp.concatenate(lo, 1), jnp.concatenate(hi, 1)


def _out_ln_kernel(a_ref, b_ref, c_ref, x_ref, w_ref, g_ref, beta_ref, o_ref, op_ref):
    na, nb = a_ref.shape[1], b_ref.shape[1]
    m = _dot(a_ref[...], w_ref[0:na, :])
    m = m + _dot(b_ref[...], w_ref[na:na + nb, :])
    m = m + _dot(c_ref[...], w_ref[na + nb:, :])
    y = _layer_norm(DEEPNORM_ALPHA * x_ref[...] + m, g_ref[...], beta_ref[...])
    o_ref[...] = y
    _store_token_tiles(op_ref, y)


def out_ln(a, b, c, x, w, g, beta):
    s = x.shape[0]
    tm = min(256, s)
    row = lambda arr: pl.BlockSpec((tm, arr.shape[1]), lambda i: (i, 0))
    full = lambda arr: pl.BlockSpec(arr.shape, lambda i: (0, 0))
    return pl.pallas_call(
        _out_ln_kernel,
        out_shape=[jax.ShapeDtypeStruct((s, D_MODEL), F32), jax.ShapeDtypeStruct((s * TOK, LANE), jnp.uint32)],
        grid=(s // tm,),
        in_specs=[row(a), row(b), row(c), row(x), full(w), full(g), full(beta)],
        out_specs=[pl.BlockSpec((tm, D_MODEL), lambda i: (i, 0)), pl.BlockSpec((tm * TOK, LANE), lambda i: (i, 0))],
        compiler_params=_params(("parallel",), 56),
        name="out_ln",
    )(a, b, c, x, w, g, beta)


FFN_ROWS = 1024
FFN_COLS = 512
FFN_OUT_CHUNK = 512
FFN_SUB = 256


def _ffn_kernel(te_ref, tv_ref, na_ref, xp_ref, wg_ref, wu_ref, wd_ref, o_ref, xb_ref, acc_ref):
    i, j = pl.program_id(0), pl.program_id(1)
    tm = xb_ref.shape[0]
    valid = tv_ref[i]

    @pl.when(j == 0)
    def _():
        acc_ref[...] = jnp.zeros_like(acc_ref)

    @pl.when((j == 0) & (valid > 0))
    def _():
        lo, hi = _load_token_tiles(xp_ref)
        xb_ref[:, :HALF] = lo.astype(BF16)
        xb_ref[:, HALF:] = hi.astype(BF16)

    def rows_step(r0, nr):
        xb = xb_ref[r0:r0 + nr, :]
        gate = _dot(xb, wg_ref[...].astype(BF16))
        up = _dot(xb, wu_ref[...].astype(BF16))
        hmid = (gate * (1.0 / (1.0 + jnp.exp(-gate))) * up).astype(BF16)
        for c in range(0, D_MODEL, FFN_OUT_CHUNK):
            cs = slice(c, c + FFN_OUT_CHUNK)
            acc_ref[r0:r0 + nr, cs] += _dot(hmid, wd_ref[:, cs].astype(BF16))

    nearly_full = valid > tm - FFN_SUB

    @pl.when(nearly_full)
    def _():
        rows_step(0, tm)

    for r0 in range(0, tm - FFN_SUB, FFN_SUB):
        @pl.when(jnp.logical_not(nearly_full) & (valid > r0))
        def _():
            rows_step(r0, FFN_SUB)

    @pl.when(j == pl.num_programs(1) - 1)
    def _():
        _store_token_tiles(o_ref, acc_ref[...])


def grouped_swiglu(xp, wg, wu, wd, tile_expert, tile_valid, n_active, tm, tf):
    n_tiles, nf = xp.shape[0] // (tm * TOK), D_FF // tf

    def tile(i, na):
        return jnp.minimum(i, na[0] - 1)

    def fcol(i, j, na):
        return jnp.where(i < na[0], j, nf - 1)

    return pl.pallas_call(
        _ffn_kernel,
        out_shape=jax.ShapeDtypeStruct(xp.shape, jnp.uint32),
        grid_spec=pltpu.PrefetchScalarGridSpec(
            num_scalar_prefetch=3,
            grid=(n_tiles, nf),
            in_specs=[pl.BlockSpec((tm * TOK, LANE), lambda i, j, te, tv, na: (tile(i, na), 0)),
                      pl.BlockSpec((None, D_MODEL, tf),
                                   lambda i, j, te, tv, na: (te[tile(i, na)], 0, fcol(i, j, na))),
                      pl.BlockSpec((None, D_MODEL, tf),
                                   lambda i, j, te, tv, na: (te[tile(i, na)], 0, fcol(i, j, na))),
                      pl.BlockSpec((None, tf, D_MODEL),
                                   lambda i, j, te, tv, na: (te[tile(i, na)], fcol(i, j, na), 0))],
            out_specs=pl.BlockSpec((tm * TOK, LANE), lambda i, j, te, tv, na: (i, 0)),
            scratch_shapes=[pltpu.VMEM((tm, D_MODEL), BF16), pltpu.VMEM((tm, D_MODEL), F32)],
        ),
        compiler_params=_params(("arbitrary", "arbitrary"), 60),
        name="grouped_swiglu",
    )(tile_expert, tile_valid, n_active, xp, wg, wu, wd)


def _router_kernel(x_ref, w_ref, e_ref, r_ref, wt_ref, cnt_ref, run_ref):
    i = pl.program_id(0)
    tm = x_ref.shape[0]
    ne = N_EXPERTS

    @pl.when(i == 0)
    def _():
        run_ref[...] = jnp.zeros_like(run_ref)

    x = x_ref[...]
    x_hi = x.astype(BF16)
    x_lo = (x - x_hi.astype(F32)).astype(BF16)
    w = w_ref[...]
    w_hi = w.astype(BF16).astype(F32)
    w_lo = w - w_hi
    both = _dot_nt(jnp.concatenate([w_hi, w_lo], 0).astype(BF16), x_hi)
    cross = _dot_nt(jnp.concatenate([w_hi, jnp.zeros_like(w_hi)], 0).astype(BF16), x_lo)
    logits = both[0:ne] + both[ne:2 * ne] + cross[0:ne]

    eidx = lax.broadcasted_iota(jnp.int32, logits.shape, 0).astype(F32)
    v1 = jnp.max(logits, 0, keepdims=True)
    i1 = jnp.min(jnp.where(logits == v1, eidx, float(ne)), 0, keepdims=True)
    rest = jnp.where(eidx == i1, -jnp.inf, logits)
    v2 = jnp.max(rest, 0, keepdims=True)
    i2 = jnp.min(jnp.where(rest == v2, eidx, float(ne)), 0, keepdims=True)
    t = jnp.exp(v2 - v1)
    w1 = 1.0 / (1.0 + t)
    wt_ref[0:1, :] = w1
    wt_ref[1:2, :] = t * w1
    e_ref[0:1, :] = i1.astype(jnp.int32)
    e_ref[1:2, :] = i2.astype(jnp.int32)

    sel1, sel2 = eidx == i1, eidx == i2
    sel = jnp.where(sel1, 1.0, 0.0) + jnp.where(sel2, 1.0, 0.0)
    before = (lax.broadcasted_iota(jnp.int32, (tm, tm), 0) < lax.broadcasted_iota(jnp.int32, (tm, tm), 1))
    sel16 = jnp.concatenate([sel, jnp.zeros_like(sel)], 0).astype(BF16)
    prefix = _dot(sel16, jnp.where(before, 1.0, 0.0).astype(BF16))[0:ne]
    rank = prefix + run_ref[:, 0:1]
    r_ref[0:1, :] = jnp.sum(jnp.where(sel1, rank, 0.0), 0, keepdims=True).astype(jnp.int32)
    r_ref[1:2, :] = jnp.sum(jnp.where(sel2, rank, 0.0), 0, keepdims=True).astype(jnp.int32)
    run_ref[...] = run_ref[...] + jnp.sum(sel, 1, keepdims=True)
    cnt_ref[...] = run_ref[...].astype(jnp.int32)


def route_tokens(x, w_router_t):
    s = x.shape[0]
    tm = min(512, s)
    pair = pl.BlockSpec((2, tm), lambda i: (0, i))
    return pl.pallas_call(
        _router_kernel,
        out_shape=[jax.ShapeDtypeStruct((2, s), jnp.int32), jax.ShapeDtypeStruct((2, s), jnp.int32),
                   jax.ShapeDtypeStruct((2, s), F32), jax.ShapeDtypeStruct((N_EXPERTS, LANE), jnp.int32)],
        grid=(s // tm,),
        in_specs=[pl.BlockSpec((tm, D_MODEL), lambda i: (i, 0)),
                  pl.BlockSpec((N_EXPERTS, D_MODEL), lambda i: (0, 0))],
        out_specs=[pair, pair, pair, pl.BlockSpec((N_EXPERTS, LANE), lambda i: (0, 0))],
        scratch_shapes=[pltpu.VMEM((N_EXPERTS, LANE), F32)],
        compiler_params=_params(("arbitrary",), 32),
        name="moe_router",
    )(x, w_router_t)


def _slot_kernel(base_ref, e_ref, r_ref, s_ref):
    e = e_ref[...]
    slot = r_ref[...]
    for k in range(N_EXPERTS):
        slot = slot + jnp.where(e == k, base_ref[k], 0)
    s_ref[...] = slot


def token_slots(e_idx, rank, base):
    whole = pl.BlockSpec(e_idx.shape, lambda i, *_: (0, 0))
    return pl.pallas_call(
        _slot_kernel,
        out_shape=jax.ShapeDtypeStruct(e_idx.shape, jnp.int32),
        grid_spec=pltpu.PrefetchScalarGridSpec(num_scalar_prefetch=1, grid=(1,), in_specs=[whole, whole],
                                               out_specs=whole),
        compiler_params=_params(("arbitrary",), 32),
        name="token_slots",
    )(base, e_idx, rank)


DMA_UNROLL = 8


def _tile_rows(index):
    return pl.ds(pl.multiple_of(index * TOK, TOK), TOK)


def _dispatch_kernel(fill_ref, slot_ref, x_ref, xs_ref, zero_ref, sem, zsem):
    i = pl.program_id(0)
    tm = x_ref.shape[0] // TOK

    def issue(b, c):
        for u in range(DMA_UNROLL):
            t = b * DMA_UNROLL + u
            for k in range(2):
                pltpu.make_async_copy(x_ref.at[_tile_rows(t)], xs_ref.at[_tile_rows(slot_ref[k, t])], sem).start()
        return c

    lax.fori_loop(0, tm // DMA_UNROLL, issue, 0)
    for k in range(2):
        pltpu.make_async_copy(x_ref, xs_ref.at[pl.ds(0, tm * TOK)], sem).wait()

    @pl.when(i == pl.num_programs(0) - 1)
    def _():
        zero_ref[...] = jnp.zeros_like(zero_ref)

        def zero_copy(slot):
            return pltpu.make_async_copy(zero_ref.at[pl.ds(0, TOK)], xs_ref.at[_tile_rows(slot)], zsem)

        for e in range(N_EXPERTS):
            lo, hi = fill_ref[0, e], fill_ref[1, e]

            def zissue(slot, c):
                zero_copy(slot).start()
                return c

            def zdrain(slot, c):
                zero_copy(slot).wait()
                return c

            lax.fori_loop(lo, hi, zissue, 0)
            lax.fori_loop(lo, hi, zdrain, 0)

        zrows = zero_ref.shape[0]
        zb = zrows // TOK

        def block_copy(b):
            return pltpu.make_async_copy(zero_ref, xs_ref.at[pl.ds(pl.multiple_of(b * zrows, zrows), zrows)], zsem)

        def bissue(b, c):
            block_copy(b).start()
            return c

        def bdrain(b, c):
            block_copy(b).wait()
            return c

        first, last = fill_ref[2, 0] // zb, xs_ref.shape[0] // zrows
        lax.fori_loop(first, last, bissue, 0)
        lax.fori_loop(first, last, bdrain, 0)


def moe_dispatch(xp, slots, fill, rows):
    s = xp.shape[0] // TOK
    tm = min(256, s)
    zb = min(256, s)
    return pl.pallas_call(
        _dispatch_kernel,
        out_shape=jax.ShapeDtypeStruct((rows * TOK, LANE), xp.dtype),
        grid_spec=pltpu.PrefetchScalarGridSpec(
            num_scalar_prefetch=1,
            grid=(s // tm,),
            in_specs=[pl.BlockSpec((2, tm), lambda i, *_: (0, i), memory_space=pltpu.SMEM),
                      pl.BlockSpec((tm * TOK, LANE), lambda i, *_: (i, 0))],
            out_specs=pl.BlockSpec(memory_space=pl.ANY),
            scratch_shapes=[pltpu.VMEM((zb * TOK, LANE), xp.dtype), pltpu.SemaphoreType.DMA,
                            pltpu.SemaphoreType.DMA],
        ),
        compiler_params=_params(("arbitrary",), 32),
        name="moe_dispatch",
    )(fill, slots, xp)


def _ple(x, p_ref, wup_ref, wgate_ref, bgate_ref):
    up = _dot(p_ref[...].astype(BF16), wup_ref[...])
    zg = _dot(x.astype(BF16), wgate_ref[...]) + bgate_ref[...]
    return up * (1.0 / (1.0 + jnp.exp(-zg)))


def _ple_ln_dense_kernel(x_ref, f_ref, p_ref, wup_ref, wgate_ref, bgate_ref, g_ref, beta_ref, o_ref):
    x = x_ref[...]
    f = jnp.concatenate(_load_token_tiles(f_ref), 1)
    ple = _ple(x, p_ref, wup_ref, wgate_ref, bgate_ref)
    o_ref[...] = _layer_norm(DEEPNORM_ALPHA * x + f + ple, g_ref[...], beta_ref[...])


def _ple_ln_moe_kernel(slot_ref, x_ref, wt_ref, p_ref, wup_ref, wgate_ref, bgate_ref, g_ref, beta_ref, ys_ref,
                       o_ref, y1_ref, y2_ref, sem):
    tm = x_ref.shape[0]
    bufs = (y1_ref, y2_ref)

    def issue(b, c):
        for u in range(DMA_UNROLL):
            t = b * DMA_UNROLL + u
            for k in range(2):
                pltpu.make_async_copy(ys_ref.at[_tile_rows(slot_ref[k, t])], bufs[k].at[_tile_rows(t)], sem).start()
        return c

    lax.fori_loop(0, tm // DMA_UNROLL, issue, 0)
    x = x_ref[...]
    ple = _ple(x, p_ref, wup_ref, wgate_ref, bgate_ref)
    for k in range(2):
        pltpu.make_async_copy(ys_ref.at[pl.ds(0, tm * TOK)], bufs[k], sem).wait()
    wt = wt_ref[...]
    f = (wt[:, 0:1] * jnp.concatenate(_load_token_tiles(y1_ref), 1)
         + wt[:, 1:2] * jnp.concatenate(_load_token_tiles(y2_ref), 1))
    o_ref[...] = _layer_norm(DEEPNORM_ALPHA * x + f + ple, g_ref[...], beta_ref[...])


def ple_ln(x, p, wup, wgate, bgate, g, beta, f=None, moe=None):
    s = x.shape[0]
    tm = min(256, s)
    row = lambda arr: pl.BlockSpec((tm, arr.shape[1]), lambda i, *_: (i, 0))
    full = lambda arr: pl.BlockSpec(arr.shape, lambda i, *_: (0, 0))
    tail = [row(p), full(wup), full(wgate), full(bgate), full(g), full(beta)]
    out_spec = pl.BlockSpec((tm, D_MODEL), lambda i, *_: (i, 0))
    out_shape = jax.ShapeDtypeStruct((s, D_MODEL), F32)
    if moe is None:
        return pl.pallas_call(
            _ple_ln_dense_kernel, out_shape=out_shape, grid=(s // tm,),
            in_specs=[row(x), pl.BlockSpec((tm * TOK, LANE), lambda i: (i, 0))] + tail, out_specs=out_spec,
            compiler_params=_params(("parallel",), 56), name="ple_ln_dense",
        )(x, f, p, wup, wgate, bgate, g, beta)
    ys, slots, wts_t = moe
    return pl.pallas_call(
        _ple_ln_moe_kernel,
        out_shape=out_shape,
        grid_spec=pltpu.PrefetchScalarGridSpec(
            num_scalar_prefetch=0,
            grid=(s // tm,),
            in_specs=[pl.BlockSpec((2, tm), lambda i: (0, i), memory_space=pltpu.SMEM), row(x), row(wts_t)] + tail
            + [pl.BlockSpec(memory_space=pl.ANY)],
            out_specs=out_spec,
            scratch_shapes=[pltpu.VMEM((tm * TOK, LANE), jnp.uint32), pltpu.VMEM((tm * TOK, LANE), jnp.uint32),
                            pltpu.SemaphoreType.DMA],
        ),
        compiler_params=_params(("arbitrary",), 56),
        name="ple_ln_moe",
    )(slots, x, wts_t, p, wup, wgate, bgate, g, beta, ys)


def _pad_cols(w, width):
    return jnp.pad(w, ((0, 0), (0, width - w.shape[1])))


def _pack_w_in(w):
    sizes = (MLA_Q_RANK, MLA_KV_RANK, MLA_ROPE, GLA_HEADS * GLA_DK, GLA_HEADS * GLA_DK, GLA_HEADS * GLA_DV,
             GLA_GATE_RANK, GLA_HEADS * GLA_DV, SWA_HEADS * SWA_HD, SWA_KV_HEADS * SWA_HD, SWA_KV_HEADS * SWA_HD)
    splits = np.cumsum(sizes)[:-1]
    q_a, kv_a, k_rope, g_q, g_k, g_v, g_lr, g_r, s_q, s_k, s_v = jnp.split(w, splits, axis=1)
    pieces = [q_a, g_v, g_r, s_q, kv_a, g_q, g_k, _pad_cols(k_rope, LANE), _pad_cols(g_lr, LANE), s_k, s_v]
    return jnp.concatenate(pieces, 1).astype(BF16)


def _pack_w_q_b(w):
    w = w.reshape(MLA_Q_RANK, MLA_HEADS, MLA_NOPE + MLA_ROPE)
    w = jnp.pad(w, ((0, 0), (0, 0), (0, MLA_QK - MLA_NOPE - MLA_ROPE)))
    return w.reshape(MLA_Q_RANK, MLA_HEADS * MLA_QK).astype(BF16)


def _token_mixer_ln(x, rope, w_in, q_gain, w_q_b, kv_gain, w_kv_b, gla_w, gla_b, gla_gain, sinks, rel_bias,
                    w_out, ln_g, ln_b):
    h = proj_in(x, _pack_w_in(w_in))
    q, k, v = mla_prep(h, *rope, q_gain.reshape(1, -1), kv_gain.reshape(1, -1), _pack_w_q_b(w_q_b),
                       w_kv_b.astype(BF16))
    a = mla_flash(q, k, v)
    gla_w_p = jnp.pad(gla_w, ((0, LANE - GLA_GATE_RANK), (0, 0))).astype(BF16)
    b = gla(h, gla_w_p, gla_b.reshape(1, -1), gla_gain.reshape(1, -1))
    c = swa(h, sinks, rel_bias)
    return out_ln(a, b, c, x, w_out.astype(BF16), ln_g.reshape(1, -1), ln_b.reshape(1, -1))


def _moe_plan(counts, tm, n_tiles):
    tiles = (counts + tm - 1) // tm
    ends = jnp.cumsum(tiles)
    base = (ends - tiles) * tm
    n_active = ends[-1:].astype(jnp.int32)
    tile_expert = jnp.searchsorted(ends, jnp.arange(n_tiles, dtype=jnp.int32), side="right")
    tile_expert = jnp.minimum(tile_expert, N_EXPERTS - 1).astype(jnp.int32)
    tile_start = jnp.arange(n_tiles, dtype=jnp.int32) * tm
    tile_valid = jnp.clip((base + counts)[tile_expert] - tile_start, 0, tm).astype(jnp.int32)
    used_rows = jnp.broadcast_to(ends[-1] * tm, counts.shape)
    fill = jnp.stack([base + counts, base + tiles * tm, used_rows]).astype(jnp.int32)
    return base.astype(jnp.int32), fill, tile_expert, tile_valid, n_active


def kernel(x, p, positions, w_in, mla_q_a_gain, mla_w_q_b, mla_kv_a_gain, mla_w_kv_b, gla_w_gate, gla_b_gate,
           gla_norm_gain, swa_sinks, rel_bias, w_out, ln1_g, ln1_b, ffn_w_gate, ffn_w_up, ffn_w_down,
           moe_router, moe_w_gate, moe_w_up, moe_w_down, ple_w_up, ple_w_gate, ple_b_gate, ln2_g, ln2_b):
    batch, s, _ = x.shape
    assert batch == 1
    xcur = x.reshape(s, D_MODEL)
    rope = rope_tables(positions)
    tm = min(FFN_ROWS, s)
    for i in range(DEPTH):
        x1, x1p = _token_mixer_ln(xcur, rope, w_in[i], mla_q_a_gain[i], mla_w_q_b[i], mla_kv_a_gain[i],
                                  mla_w_kv_b[i], gla_w_gate[i], gla_b_gate[i], gla_norm_gain[i], swa_sinks[i],
                                  rel_bias, w_out[i], ln1_g[i], ln1_b[i])
        tail = (p[i, 0], ple_w_up[i].astype(BF16), ple_w_gate[i].astype(BF16), ple_b_gate[i].reshape(1, -1),
                ln2_g[i].reshape(1, -1), ln2_b[i].reshape(1, -1))
        j = i // 2
        if i % 2 == 0:
            n_tiles = s // tm
            f = grouped_swiglu(x1p, ffn_w_gate[j][None], ffn_w_up[j][None], ffn_w_down[j][None],
                               jnp.zeros((n_tiles,), jnp.int32), jnp.full((n_tiles,), tm, jnp.int32),
                               jnp.full((1,), n_tiles, jnp.int32), tm, FFN_COLS)
            xcur = ple_ln(x1, *tail, f=f)
        else:
            n_tiles = (2 * s + N_EXPERTS * (tm - 1)) // tm
            e_idx, rank, wts, counts = route_tokens(x1, moe_router[j].T)
            base, fill, tile_expert, tile_valid, n_active = _moe_plan(counts[:, 0], tm, n_tiles)
            slots = token_slots(e_idx, rank, base)
            xs = moe_dispatch(x1p, slots, fill, n_tiles * tm)
            ys = grouped_swiglu(xs, moe_w_gate[j], moe_w_up[j], moe_w_down[j], tile_expert, tile_valid, n_active,
                                tm, FFN_COLS)
            xcur = ple_ln(x1, *tail, moe=(ys, slots, wts.T))
    return xcur.reshape(batch, s, D_MODEL)
```

```python
import functools
import math

import numpy as np
import jax
import jax.numpy as jnp
from jax import lax
from jax.experimental import pallas as pl
from jax.experimental.pallas import tpu as pltpu

F32 = jnp.float32
BF16 = jnp.bfloat16

D_MODEL = 2048
DEPTH = 2
MLA_HEADS = 8
MLA_Q_RANK = 512
MLA_KV_RANK = 256
MLA_NOPE = 128
MLA_ROPE = 64
MLA_V = 128
ROPE_THETA = 10000.0
GLA_HEADS = 4
GLA_DK = 64
GLA_DV = 128
GLA_GATE_RANK = 16
GLA_TAU = 16.0
SWA_HEADS = 8
SWA_KV_HEADS = 2
SWA_HD = 64
SWA_WINDOW = 128
SWA_BLOCK = 128
REL_BUCKETS = 32
REL_MAX_DIST = 128
D_FF = 5632
N_EXPERTS = 8
PLE_DIM = 256
LN_EPS = 1e-5
RMS_EPS = 1e-6
DEEPNORM_ALPHA = (2 * DEPTH) ** 0.25

LANE = 128
LOG2E = math.log2(math.e)
NEG_BIG = -1e30

H_QA, H_GV, H_GR, H_SQ = 0, 512, 1024, 1536
H_KVA, H_GQ, H_GK = 2048, 2304, 2560
H_KR, H_GLR, H_SK, H_SV = 2816, 2944, 3072, 3200
H_COLS = 3328

MLA_QK = 2 * LANE

GLA_L = 128
GLA_SUB = 32
GLA_NSUB = GLA_L // GLA_SUB


def _params(sem, vmem_mb):
    return pltpu.CompilerParams(dimension_semantics=sem, vmem_limit_bytes=vmem_mb * 2 ** 20)


def _dot(a, b):
    return jnp.dot(a, b, preferred_element_type=F32)


def _dot_nt(a, b):
    return lax.dot_general(a, b, (((1,), (1,)), ((), ())), preferred_element_type=F32)


def _dot_tn(a, b):
    return lax.dot_general(a, b, (((0,), (0,)), ((), ())), preferred_element_type=F32)


def _proj_in_kernel(x_ref, w_ref, o_ref, xb_ref):
    @pl.when(pl.program_id(1) == 0)
    def _():
        xb_ref[...] = x_ref[...].astype(BF16)

    o_ref[...] = _dot(xb_ref[...], w_ref[...]).astype(o_ref.dtype)


def proj_in(x, w_p):
    s = x.shape[0]
    tm = min(1024, s)
    tn = H_COLS // 2
    return pl.pallas_call(
        _proj_in_kernel,
        out_shape=jax.ShapeDtypeStruct((s, H_COLS), BF16),
        grid=(s // tm, H_COLS // tn),
        in_specs=[pl.BlockSpec((tm, D_MODEL), lambda i, j: (i, 0)),
                  pl.BlockSpec((D_MODEL, tn), lambda i, j: (0, j))],
        out_specs=pl.BlockSpec((tm, tn), lambda i, j: (i, j)),
        scratch_shapes=[pltpu.VMEM((tm, D_MODEL), BF16)],
        compiler_params=_params(("parallel", "arbitrary"), 56),
        name="proj_in",
    )(x, w_p)


def _rope_table_kernel(pos_ref, inv_ref, cos_ref, sa_ref, sb_ref):
    ang = pos_ref[...].astype(F32) * inv_ref[...]
    lane = lax.broadcasted_iota(jnp.int32, ang.shape, 1)
    half = MLA_ROPE // 2
    c, s = jnp.cos(ang), jnp.sin(ang)
    cos_ref[...] = c
    sa_ref[...] = jnp.where((lane >= half) & (lane < 2 * half), s, 0.0)
    sb_ref[...] = jnp.where(lane < half, -s, 0.0)


def rope_tables(positions):
    s = positions.shape[-1]
    half = MLA_ROPE // 2
    inv = ROPE_THETA ** (-jnp.arange(half, dtype=F32) / half)
    inv = jnp.concatenate([inv, inv, jnp.zeros((LANE - 2 * half,), F32)]).reshape(1, LANE)
    tm = min(1024, s)
    spec = pl.BlockSpec((tm, LANE), lambda i: (i, 0))
    return pl.pallas_call(
        _rope_table_kernel,
        out_shape=[jax.ShapeDtypeStruct((s, LANE), F32)] * 3,
        grid=(s // tm,),
        in_specs=[pl.BlockSpec((tm, 1), lambda i: (i, 0)), pl.BlockSpec((1, LANE), lambda i: (0, 0))],
        out_specs=[spec, spec, spec],
        compiler_params=_params(("parallel",), 32),
        name="rope_tables",
    )(positions.reshape(s, 1), inv)


def _rope(x, cos, sa, sb):
    return x * cos + pltpu.roll(x, MLA_ROPE // 2, 1) * sa + pltpu.roll(x, LANE - MLA_ROPE // 2, 1) * sb


def _mla_prep_kernel(qa_ref, kva_ref, kr_ref, cos_ref, sa_ref, sb_ref, gq_ref, gkv_ref, wq_ref, wkv_ref,
                     q_out, k_out, v_out):
    cos, sa, sb = cos_ref[...], sa_ref[...], sb_ref[...]
    qscale = (MLA_NOPE + MLA_ROPE) ** -0.5 * LOG2E

    qa = qa_ref[...].astype(F32)
    qn = qa * lax.rsqrt(jnp.mean(qa * qa, -1, keepdims=True) + RMS_EPS) * gq_ref[...]
    q = _dot(qn.astype(BF16), wq_ref[...])
    for h in range(MLA_HEADS):
        c0 = h * MLA_QK
        q_out[h, :, 0:LANE] = (q[:, c0:c0 + LANE] * qscale).astype(BF16)
        pe = _rope(q[:, c0 + LANE:c0 + 2 * LANE], cos, sa, sb)
        q_out[h, :, LANE:2 * LANE] = (pe * qscale).astype(BF16)

    kva = kva_ref[...].astype(F32)
    kvn = kva * lax.rsqrt(jnp.mean(kva * kva, -1, keepdims=True) + RMS_EPS) * gkv_ref[...]
    kv = _dot(kvn.astype(BF16), wkv_ref[...])
    kpe = _rope(kr_ref[...].astype(F32), cos, sa, sb).astype(BF16)
    for h in range(MLA_HEADS):
        c0 = h * (MLA_NOPE + MLA_V)
        k_out[h, :, 0:LANE] = kv[:, c0:c0 + MLA_NOPE].astype(BF16)
        k_out[h, :, LANE:2 * LANE] = kpe
        v_out[h] = kv[:, c0 + MLA_NOPE:c0 + MLA_NOPE + MLA_V].astype(BF16)


def mla_prep(h, cos, sa, sb, gq, gkv, wq_p, wkv):
    s = h.shape[0]
    tm = min(512, s)
    row = lambda width, col: pl.BlockSpec((tm, width), lambda i: (i, col // width))
    full = lambda a: pl.BlockSpec(a.shape, lambda i: (0,) * a.ndim)
    return pl.pallas_call(
        _mla_prep_kernel,
        out_shape=[jax.ShapeDtypeStruct((MLA_HEADS, s, MLA_QK), BF16),
                   jax.ShapeDtypeStruct((MLA_HEADS, s, MLA_QK), BF16),
                   jax.ShapeDtypeStruct((MLA_HEADS, s, MLA_V), BF16)],
        grid=(s // tm,),
        in_specs=[row(MLA_Q_RANK, H_QA), row(MLA_KV_RANK, H_KVA), row(LANE, H_KR),
                  row(LANE, 0), row(LANE, 0), row(LANE, 0),
                  full(gq), full(gkv), full(wq_p), full(wkv)],
        out_specs=[pl.BlockSpec((MLA_HEADS, tm, MLA_QK), lambda i: (0, i, 0)),
                   pl.BlockSpec((MLA_HEADS, tm, MLA_QK), lambda i: (0, i, 0)),
                   pl.BlockSpec((MLA_HEADS, tm, MLA_V), lambda i: (0, i, 0))],
        compiler_params=_params(("parallel",), 48),
        name="mla_prep",
    )(h, h, h, cos, sa, sb, gq, gkv, wq_p, wkv)


def _mla_flash_kernel(q_ref, k_ref, v_ref, o_ref, sa_ref, sb_ref, mxa_ref, mxb_ref, m_ref, l_ref, acc_ref,
                      *, t, nh):
    qi = pl.program_id(1)
    heads = range(nh)
    bufs = ((sa_ref, mxa_ref), (sb_ref, mxb_ref))

    def produce(b, parity, masked):
        s_ref, mx_ref = bufs[parity]
        start = pl.multiple_of(b * t, t)
        for h in heads:
            s = _dot_nt(q_ref[h], k_ref[h, pl.ds(start, t), :])
            if masked:
                rows = lax.broadcasted_iota(jnp.int32, s.shape, 0) + qi * t
                cols = lax.broadcasted_iota(jnp.int32, s.shape, 1) + b * t
                s = jnp.where(cols <= rows, s, NEG_BIG)
            s_ref[h] = s
            mx_ref[h] = jnp.broadcast_to(jnp.max(s, -1, keepdims=True), (t, LANE))

    def absorb(b, parity):
        s_ref, mx_ref = bufs[parity]
        start = pl.multiple_of(b * t, t)
        for h in heads:
            m_new = jnp.maximum(m_ref[h], mx_ref[h])
            alpha = jnp.exp2(m_ref[h] - m_new)
            p = jnp.exp2(s_ref[h] - jnp.concatenate([m_new] * (t // LANE), 1))
            l_ref[h] = alpha * l_ref[h] + jnp.broadcast_to(jnp.sum(p, -1, keepdims=True), (t, LANE))
            acc_ref[h] = alpha * acc_ref[h] + _dot(p.astype(BF16), v_ref[h, pl.ds(start, t), :])
            m_ref[h] = m_new

    m_ref[...] = jnp.full(m_ref.shape, NEG_BIG, F32)
    l_ref[...] = jnp.zeros_like(l_ref)
    acc_ref[...] = jnp.zeros_like(acc_ref)
    produce(0, 0, True)

    def pair(i, c):
        b = 2 * i
        produce(b + 1, 1, False)
        absorb(b, 0)
        produce(b + 2, 0, False)
        absorb(b + 1, 1)
        return c

    n_pairs = jnp.maximum(qi - 1, 0) // 2
    lax.fori_loop(0, n_pairs, pair, 0)
    done = 2 * n_pairs

    @pl.when(qi == 0)
    def _():
        absorb(0, 0)

    @pl.when((qi > 0) & (qi - done == 1))
    def _():
        produce(qi, 1, True)
        absorb(qi - 1, 0)
        absorb(qi, 1)

    @pl.when((qi > 0) & (qi - done == 2))
    def _():
        produce(qi - 1, 1, False)
        absorb(qi - 2, 0)
        produce(qi, 0, True)
        absorb(qi - 1, 1)
        absorb(qi, 0)

    for h in heads:
        o_ref[:, h * MLA_V:(h + 1) * MLA_V] = (acc_ref[h] / l_ref[h]).astype(o_ref.dtype)


def mla_flash(q, k, v):
    _, s, _ = q.shape
    t = min(512, s)
    nh = 2
    return pl.pallas_call(
        functools.partial(_mla_flash_kernel, t=t, nh=nh),
        out_shape=jax.ShapeDtypeStruct((s, MLA_HEADS * MLA_V), BF16),
        grid=(MLA_HEADS // nh, s // t),
        in_specs=[pl.BlockSpec((nh, t, MLA_QK), lambda h, i: (h, i, 0)),
                  pl.BlockSpec((nh, s, MLA_QK), lambda h, i: (h, 0, 0)),
                  pl.BlockSpec((nh, s, MLA_V), lambda h, i: (h, 0, 0))],
        out_specs=pl.BlockSpec((t, nh * MLA_V), lambda h, i: (i, h)),
        scratch_shapes=[pltpu.VMEM((nh, t, t), F32), pltpu.VMEM((nh, t, t), F32),
                        pltpu.VMEM((nh, t, LANE), F32), pltpu.VMEM((nh, t, LANE), F32),
                        pltpu.VMEM((nh, t, LANE), F32), pltpu.VMEM((nh, t, LANE), F32),
                        pltpu.VMEM((nh, t, MLA_V), F32)],
        compiler_params=_params(("parallel", "arbitrary"), 48),
        name="mla_flash",
    )(q, k, v)


def _gla_masks():
    i = np.arange(GLA_L)[:, None]
    j = np.arange(GLA_L)[None, :]
    same = (i // GLA_SUB) == (j // GLA_SUB)
    mats = [j <= i, same & (j <= i), same & (j > i), j > i]
    for sub in range(GLA_NSUB - 1):
        mats.append((j >= (sub + 1) * GLA_SUB) & (j <= i))
    return np.concatenate(mats, 0).astype(np.float32)


def _gla_kernel(q_ref, k_ref, v_ref, lr_ref, r_ref, wg_ref, bg_ref, gain_ref, mask_ref, o_ref, state_ref):
    @pl.when(pl.program_id(0) == 0)
    def _():
        state_ref[...] = jnp.zeros_like(state_ref)

    L, hk = GLA_L, GLA_HEADS * GLA_DK
    z = _dot(lr_ref[...], wg_ref[...]) + bg_ref[...]
    g = (jnp.minimum(z, 0.0) - jnp.log(1.0 + jnp.exp(-jnp.abs(z)))) * (1.0 / GLA_TAU)
    g_hi = g.astype(BF16)
    g_lo = (g - g_hi.astype(F32)).astype(BF16)
    cums = _dot(mask_ref[...], g_hi) + _dot(mask_ref[...], g_lo)
    b_all = cums[0:L]
    b_loc = cums[L:2 * L]
    sfx_loc = cums[2 * L:3 * L]
    sfx_all = cums[3 * L:4 * L]

    q = q_ref[...].astype(F32) * (GLA_DK ** -0.5)
    k = k_ref[...].astype(F32)
    q_inter = (q * jnp.exp(b_all)).astype(BF16)
    q_diag = (q * jnp.exp(b_loc)).astype(BF16)
    k_diag = (k * jnp.exp(-b_loc)).astype(BF16)
    k_end = k * jnp.exp(sfx_loc)
    k_state = (k * jnp.exp(sfx_all)).astype(BF16)
    ones = jnp.ones((L, GLA_DV), BF16)

    row = lax.broadcasted_iota(jnp.int32, (L, L), 0)
    col = lax.broadcasted_iota(jnp.int32, (L, L), 1)
    diag_ok = ((row // GLA_SUB) == (col // GLA_SUB)) & (col <= row)
    off_ok = (row // GLA_SUB) > (col // GLA_SUB)
    sub_of_row = lax.broadcasted_iota(jnp.int32, (L, hk), 0) // GLA_SUB

    q_off, k_off = [], []
    for sub in range(GLA_NSUB - 1):
        q_off.append((q * jnp.exp(cums[(4 + sub) * L:(5 + sub) * L])).astype(BF16))
        k_off.append(jnp.where(sub_of_row == sub, k_end, 0.0).astype(BF16))

    v = v_ref[...]
    r = r_ref[...].astype(F32)
    gate = r * (1.0 / (1.0 + jnp.exp(-r)))
    for h in range(GLA_HEADS):
        ks = slice(h * GLA_DK, (h + 1) * GLA_DK)
        vs = slice(h * GLA_DV, (h + 1) * GLA_DV)
        state = state_ref[h]
        v_h = v[:, vs]
        a = jnp.where(diag_ok, _dot_nt(q_diag[:, ks], k_diag[:, ks]), 0.0)
        qo = jnp.concatenate([t[:, ks] for t in q_off], 1)
        ko = jnp.concatenate([t[:, ks] for t in k_off], 1)
        a = a + jnp.where(off_ok, _dot_nt(qo, ko), 0.0)
        o = _dot(q_inter[:, ks], state.astype(BF16)) + _dot(a.astype(BF16), v_h)
        decay = jnp.exp(_dot_tn(g_hi[:, ks], ones) + _dot_tn(g_lo[:, ks], ones))
        state_ref[h] = state * decay + _dot_tn(k_state[:, ks], v_h)
        o = o * lax.rsqrt(jnp.mean(o * o, -1, keepdims=True) + RMS_EPS) * gain_ref[:, vs]
        o_ref[:, vs] = (o * gate[:, vs]).astype(o_ref.dtype)


def gla(h, wg_p, bg, gain):
    s = h.shape[0]
    L = GLA_L
    masks = jnp.asarray(_gla_masks(), BF16)
    row = lambda width, col: pl.BlockSpec((L, width), lambda i: (i, col // width))
    full = lambda a: pl.BlockSpec(a.shape, lambda i: (0,) * a.ndim)
    hk, hv = GLA_HEADS * GLA_DK, GLA_HEADS * GLA_DV
    return pl.pallas_call(
        _gla_kernel,
        out_shape=jax.ShapeDtypeStruct((s, hv), BF16),
        grid=(s // L,),
        in_specs=[row(hk, H_GQ), row(hk, H_GK), row(hv, H_GV), row(LANE, H_GLR), row(hv, H_GR),
                  full(wg_p), full(bg), full(gain), full(masks)],
        out_specs=pl.BlockSpec((L, hv), lambda i: (i, 0)),
        scratch_shapes=[pltpu.VMEM((GLA_HEADS, GLA_DK, GLA_DV), F32)],
        compiler_params=_params(("arbitrary",), 32),
        name="gla",
    )(h, h, h, h, h, wg_p, bg, gain, masks)


def _t5_bucket_table():
    L = SWA_BLOCK
    dist = np.arange(L)[:, None] + L - np.arange(2 * L)[None, :]
    d = np.clip(dist, 0, None)
    max_exact = REL_BUCKETS // 2
    df = np.maximum(d, 1).astype(np.float32)
    large = max_exact + (np.log(df / np.float32(max_exact)) / np.float32(math.log(REL_MAX_DIST / max_exact))
                         * np.float32(REL_BUCKETS - max_exact)).astype(np.int32)
    large = np.minimum(large, REL_BUCKETS - 1)
    bucket = np.where(d < max_exact, d, large)
    in_window = (dist >= 0) & (dist < SWA_WINDOW)
    return np.where(in_window, bucket, -1).astype(np.int32)


def _swa_kernel(relb_ref, sink_ref, q_ref, kp_ref, kc_ref, vp_ref, vc_ref, bucket_ref, o_ref, bias_ref):
    i = pl.program_id(0)
    L = SWA_BLOCK

    @pl.when(i == 0)
    def _():
        bucket = bucket_ref[...]
        for h in range(SWA_HEADS):
            acc = jnp.full(bucket.shape, NEG_BIG, F32)
            for b in range(REL_BUCKETS):
                acc = jnp.where(bucket == b, relb_ref[b, h], acc)
            bias_ref[h] = acc

    kcat = jnp.concatenate([kp_ref[...], kc_ref[...]], 0)
    vcat = jnp.concatenate([vp_ref[...], vc_ref[...]], 0)
    q = q_ref[...]
    col = lax.broadcasted_iota(jnp.int32, (L, 2 * L), 1)
    real_key = (col >= L) | (i > 0)
    g = SWA_HEADS // SWA_KV_HEADS
    scale = SWA_HD ** -0.5
    for h in range(SWA_HEADS):
        kv = h // g
        hs = slice(h * SWA_HD, (h + 1) * SWA_HD)
        kvs = slice(kv * SWA_HD, (kv + 1) * SWA_HD)
        s = _dot_nt(q[:, hs], kcat[:, kvs]) * scale + bias_ref[h]
        s = jnp.where(real_key, s, NEG_BIG)
        sink = sink_ref[h]
        m = jnp.maximum(jnp.max(s, -1, keepdims=True), sink)
        p = jnp.exp(s - m)
        denom = jnp.sum(p, -1, keepdims=True) + jnp.exp(sink - m)
        o = _dot((p / denom).astype(BF16), vcat[:, kvs])
        o_ref[:, hs] = o.astype(o_ref.dtype)


def swa(h, sinks, rel_bias):
    s = h.shape[0]
    L = SWA_BLOCK
    bucket = jnp.asarray(_t5_bucket_table())
    kvw = SWA_KV_HEADS * SWA_HD
    hw = SWA_HEADS * SWA_HD
    cur = lambda width, col: pl.BlockSpec((L, width), lambda i, *_: (i, col // width))
    prev = lambda width, col: pl.BlockSpec((L, width), lambda i, *_: (jnp.maximum(i - 1, 0), col // width))
    return pl.pallas_call(
        _swa_kernel,
        out_shape=jax.ShapeDtypeStruct((s, hw), BF16),
        grid_spec=pltpu.PrefetchScalarGridSpec(
            num_scalar_prefetch=2,
            grid=(s // L,),
            in_specs=[cur(hw, H_SQ), prev(kvw, H_SK), cur(kvw, H_SK), prev(kvw, H_SV), cur(kvw, H_SV),
                      pl.BlockSpec(bucket.shape, lambda i, *_: (0, 0))],
            out_specs=pl.BlockSpec((L, hw), lambda i, *_: (i, 0)),
            scratch_shapes=[pltpu.VMEM((SWA_HEADS, L, 2 * L), F32)],
        ),
        compiler_params=_params(("arbitrary",), 32),
        name="swa",
    )(rel_bias, sinks, h, h, h, h, h, bucket)


def _layer_norm(y, g, b):
    mu = jnp.mean(y, -1, keepdims=True)
    yc = y - mu
    var = jnp.mean(yc * yc, -1, keepdims=True)
    return yc * lax.rsqrt(var + LN_EPS) * g + b


HALF = D_MODEL // 2
HIGH16 = 0xFFFF0000


SUBLANE = 8
TOK = HALF // LANE


def _store_token_tiles(ref, x):
    tm = x.shape[0]
    lo = lax.bitcast_convert_type(x[:, :HALF].astype(BF16).astype(F32), jnp.uint32) >> 16
    hi = lax.bitcast_convert_type(x[:, HALF:].astype(BF16).astype(F32), jnp.uint32) & jnp.uint32(HIGH16)
    packed = lo | hi
    for s in range(TOK):
        ref[pl.ds(s, tm, stride=TOK), :] = packed[:, s * LANE:(s + 1) * LANE]


def _load_token_tiles(ref, rows=None):
    first, tm = rows if rows is not None else (0, ref.shape[0] // TOK)
    lo, hi = [], []
    for s in range(TOK):
        w = ref[pl.ds(first * TOK + s, tm, stride=TOK), :]
        lo.append(lax.bitcast_convert_type(w << 16, F32))
        hi.append(lax.bitcast_convert_type(w & jnp.uint32(HIGH16), F32))
    return jnp.concatenate(lo, 1), jnp.concatenate(hi, 1)


def _out_ln_kernel(a_ref, b_ref, c_ref, x_ref, w_ref, g_ref, beta_ref, o_ref, op_ref):
    na, nb = a_ref.shape[1], b_ref.shape[1]
    m = _dot(a_ref[...], w_ref[0:na, :])
    m = m + _dot(b_ref[...], w_ref[na:na + nb, :])
    m = m + _dot(c_ref[...], w_ref[na + nb:, :])
    y = _layer_norm(DEEPNORM_ALPHA * x_ref[...] + m, g_ref[...], beta_ref[...])
    o_ref[...] = y
    _store_token_tiles(op_ref, y)


def out_ln(a, b, c, x, w, g, beta):
    s = x.shape[0]
    tm = min(512, s)
    row = lambda arr: pl.BlockSpec((tm, arr.shape[1]), lambda i: (i, 0))
    full = lambda arr: pl.BlockSpec(arr.shape, lambda i: (0, 0), pipeline_mode=pl.Buffered(1))
    return pl.pallas_call(
        _out_ln_kernel,
        out_shape=[jax.ShapeDtypeStruct((s, D_MODEL), F32), jax.ShapeDtypeStruct((s * TOK, LANE), jnp.uint32)],
        grid=(s // tm,),
        in_specs=[row(a), row(b), row(c), row(x), full(w), full(g), full(beta)],
        out_specs=[pl.BlockSpec((tm, D_MODEL), lambda i: (i, 0)), pl.BlockSpec((tm * TOK, LANE), lambda i: (i, 0))],
        compiler_params=_params(("parallel",), 56),
        name="out_ln",
    )(a, b, c, x, w, g, beta)


FFN_ROWS = 1024
FFN_COLS = 512
FFN_OUT_CHUNK = 512
FFN_SUB = 256


def _ffn_kernel(te_ref, tv_ref, na_ref, xp_ref, wg_ref, wu_ref, wd_ref, o_ref, xb_ref, acc_ref):
    i, j = pl.program_id(0), pl.program_id(1)
    tm = xb_ref.shape[0]
    valid = tv_ref[i]

    @pl.when(j == 0)
    def _():
        acc_ref[...] = jnp.zeros_like(acc_ref)

    @pl.when((j == 0) & (valid > 0))
    def _():
        lo, hi = _load_token_tiles(xp_ref)
        xb_ref[:, :HALF] = lo.astype(BF16)
        xb_ref[:, HALF:] = hi.astype(BF16)

    def rows_step(r0, nr):
        xb = xb_ref[r0:r0 + nr, :]
        gate = _dot(xb, wg_ref[...].astype(BF16))
        up = _dot(xb, wu_ref[...].astype(BF16))
        hmid = (gate * (1.0 / (1.0 + jnp.exp(-gate))) * up).astype(BF16)
        for c in range(0, D_MODEL, FFN_OUT_CHUNK):
            cs = slice(c, c + FFN_OUT_CHUNK)
            acc_ref[r0:r0 + nr, cs] += _dot(hmid, wd_ref[:, cs].astype(BF16))

    nearly_full = valid > tm - FFN_SUB

    @pl.when(nearly_full)
    def _():
        rows_step(0, tm)

    for r0 in range(0, tm - FFN_SUB, FFN_SUB):
        @pl.when(jnp.logical_not(nearly_full) & (valid > r0))
        def _():
            rows_step(r0, FFN_SUB)

    @pl.when(j == pl.num_programs(1) - 1)
    def _():
        _store_token_tiles(o_ref, acc_ref[...])


def grouped_swiglu(xp, wg, wu, wd, tile_expert, tile_valid, n_active, tm, tf):
    n_tiles, nf = xp.shape[0] // (tm * TOK), D_FF // tf

    def tile(i, na):
        return jnp.minimum(i, na[0] - 1)

    def fcol(i, j, na):
        return jnp.where(i < na[0], j, nf - 1)

    return pl.pallas_call(
        _ffn_kernel,
        out_shape=jax.ShapeDtypeStruct(xp.shape, jnp.uint32),
        grid_spec=pltpu.PrefetchScalarGridSpec(
            num_scalar_prefetch=3,
            grid=(n_tiles, nf),
            in_specs=[pl.BlockSpec((tm * TOK, LANE), lambda i, j, te, tv, na: (tile(i, na), 0)),
                      pl.BlockSpec((None, D_MODEL, tf),
                                   lambda i, j, te, tv, na: (te[tile(i, na)], 0, fcol(i, j, na))),
                      pl.BlockSpec((None, D_MODEL, tf),
                                   lambda i, j, te, tv, na: (te[tile(i, na)], 0, fcol(i, j, na))),
                      pl.BlockSpec((None, tf, D_MODEL),
                                   lambda i, j, te, tv, na: (te[tile(i, na)], fcol(i, j, na), 0))],
            out_specs=pl.BlockSpec((tm * TOK, LANE), lambda i, j, te, tv, na: (i, 0)),
            scratch_shapes=[pltpu.VMEM((tm, D_MODEL), BF16), pltpu.VMEM((tm, D_MODEL), F32)],
        ),
        compiler_params=_params(("arbitrary", "arbitrary"), 60),
        name="grouped_swiglu",
    )(tile_expert, tile_valid, n_active, xp, wg, wu, wd)


def _router_kernel(x_ref, w_ref, e_ref, r_ref, wt_ref, cnt_ref, run_ref):
    i = pl.program_id(0)
    tm = x_ref.shape[0]
    ne = N_EXPERTS

    @pl.when(i == 0)
    def _():
        run_ref[...] = jnp.zeros_like(run_ref)

    x = x_ref[...]
    x_hi = x.astype(BF16)
    x_lo = (x - x_hi.astype(F32)).astype(BF16)
    w = w_ref[...]
    w_hi = w.astype(BF16).astype(F32)
    w_lo = w - w_hi
    both = _dot_nt(jnp.concatenate([w_hi, w_lo], 0).astype(BF16), x_hi)
    cross = _dot_nt(jnp.concatenate([w_hi, jnp.zeros_like(w_hi)], 0).astype(BF16), x_lo)
    logits = both[0:ne] + both[ne:2 * ne] + cross[0:ne]

    eidx = lax.broadcasted_iota(jnp.int32, logits.shape, 0).astype(F32)
    v1 = jnp.max(logits, 0, keepdims=True)
    i1 = jnp.min(jnp.where(logits == v1, eidx, float(ne)), 0, keepdims=True)
    rest = jnp.where(eidx == i1, -jnp.inf, logits)
    v2 = jnp.max(rest, 0, keepdims=True)
    i2 = jnp.min(jnp.where(rest == v2, eidx, float(ne)), 0, keepdims=True)
    t = jnp.exp(v2 - v1)
    w1 = 1.0 / (1.0 + t)
    wt_ref[0:1, :] = w1
    wt_ref[1:2, :] = t * w1
    e_ref[0:1, :] = i1.astype(jnp.int32)
    e_ref[1:2, :] = i2.astype(jnp.int32)

    sel1, sel2 = eidx == i1, eidx == i2
    sel = jnp.where(sel1, 1.0, 0.0) + jnp.where(sel2, 1.0, 0.0)
    before = (lax.broadcasted_iota(jnp.int32, (tm, tm), 0) < lax.broadcasted_iota(jnp.int32, (tm, tm), 1))
    sel16 = jnp.concatenate([sel, jnp.zeros_like(sel)], 0).astype(BF16)
    prefix = _dot(sel16, jnp.where(before, 1.0, 0.0).astype(BF16))[0:ne]
    rank = prefix + run_ref[:, 0:1]
    r_ref[0:1, :] = jnp.sum(jnp.where(sel1, rank, 0.0), 0, keepdims=True).astype(jnp.int32)
    r_ref[1:2, :] = jnp.sum(jnp.where(sel2, rank, 0.0), 0, keepdims=True).astype(jnp.int32)
    run_ref[...] = run_ref[...] + jnp.sum(sel, 1, keepdims=True)
    cnt_ref[...] = run_ref[...].astype(jnp.int32)


def route_tokens(x, w_router_t):
    s = x.shape[0]
    tm = min(512, s)
    pair = pl.BlockSpec((2, tm), lambda i: (0, i))
    return pl.pallas_call(
        _router_kernel,
        out_shape=[jax.ShapeDtypeStruct((2, s), jnp.int32), jax.ShapeDtypeStruct((2, s), jnp.int32),
                   jax.ShapeDtypeStruct((2, s), F32), jax.ShapeDtypeStruct((N_EXPERTS, LANE), jnp.int32)],
        grid=(s // tm,),
        in_specs=[pl.BlockSpec((tm, D_MODEL), lambda i: (i, 0)),
                  pl.BlockSpec((N_EXPERTS, D_MODEL), lambda i: (0, 0))],
        out_specs=[pair, pair, pair, pl.BlockSpec((N_EXPERTS, LANE), lambda i: (0, 0))],
        scratch_shapes=[pltpu.VMEM((N_EXPERTS, LANE), F32)],
        compiler_params=_params(("arbitrary",), 32),
        name="moe_router",
    )(x, w_router_t)


def _slot_kernel(base_ref, e_ref, r_ref, s_ref):
    e = e_ref[...]
    slot = r_ref[...]
    for k in range(N_EXPERTS):
        slot = slot + jnp.where(e == k, base_ref[k], 0)
    s_ref[...] = slot


def token_slots(e_idx, rank, base):
    whole = pl.BlockSpec(e_idx.shape, lambda i, *_: (0, 0))
    return pl.pallas_call(
        _slot_kernel,
        out_shape=jax.ShapeDtypeStruct(e_idx.shape, jnp.int32),
        grid_spec=pltpu.PrefetchScalarGridSpec(num_scalar_prefetch=1, grid=(1,), in_specs=[whole, whole],
                                               out_specs=whole),
        compiler_params=_params(("arbitrary",), 32),
        name="token_slots",
    )(base, e_idx, rank)


DMA_UNROLL = 8


def _tile_rows(index):
    return pl.ds(pl.multiple_of(index * TOK, TOK), TOK)


def _dispatch_kernel(fill_ref, slot_ref, x_ref, xs_ref, zero_ref, sem, zsem):
    i = pl.program_id(0)
    tm = x_ref.shape[0] // TOK

    def issue(b, c):
        for u in range(DMA_UNROLL):
            t = b * DMA_UNROLL + u
            for k in range(2):
                pltpu.make_async_copy(x_ref.at[_tile_rows(t)], xs_ref.at[_tile_rows(slot_ref[k, t])], sem).start()
        return c

    lax.fori_loop(0, tm // DMA_UNROLL, issue, 0)
    for k in range(2):
        pltpu.make_async_copy(x_ref, xs_ref.at[pl.ds(0, tm * TOK)], sem).wait()

    @pl.when(i == pl.num_programs(0) - 1)
    def _():
        zero_ref[...] = jnp.zeros_like(zero_ref)

        def zero_copy(slot):
            return pltpu.make_async_copy(zero_ref.at[pl.ds(0, TOK)], xs_ref.at[_tile_rows(slot)], zsem)

        for e in range(N_EXPERTS):
            lo, hi = fill_ref[0, e], fill_ref[1, e]

            def zissue(slot, c):
                zero_copy(slot).start()
                return c

            def zdrain(slot, c):
                zero_copy(slot).wait()
                return c

            lax.fori_loop(lo, hi, zissue, 0)
            lax.fori_loop(lo, hi, zdrain, 0)

        zrows = zero_ref.shape[0]
        zb = zrows // TOK

        def block_copy(b):
            return pltpu.make_async_copy(zero_ref, xs_ref.at[pl.ds(pl.multiple_of(b * zrows, zrows), zrows)], zsem)

        def bissue(b, c):
            block_copy(b).start()
            return c

        def bdrain(b, c):
            block_copy(b).wait()
            return c

        first, last = fill_ref[2, 0] // zb, xs_ref.shape[0] // zrows
        lax.fori_loop(first, last, bissue, 0)
        lax.fori_loop(first, last, bdrain, 0)


def moe_dispatch(xp, slots, fill, rows):
    s = xp.shape[0] // TOK
    tm = min(256, s)
    zb = min(256, s)
    return pl.pallas_call(
        _dispatch_kernel,
        out_shape=jax.ShapeDtypeStruct((rows * TOK, LANE), xp.dtype),
        grid_spec=pltpu.PrefetchScalarGridSpec(
            num_scalar_prefetch=1,
            grid=(s // tm,),
            in_specs=[pl.BlockSpec((2, tm), lambda i, *_: (0, i), memory_space=pltpu.SMEM),
                      pl.BlockSpec((tm * TOK, LANE), lambda i, *_: (i, 0))],
            out_specs=pl.BlockSpec(memory_space=pl.ANY),
            scratch_shapes=[pltpu.VMEM((zb * TOK, LANE), xp.dtype), pltpu.SemaphoreType.DMA,
                            pltpu.SemaphoreType.DMA],
        ),
        compiler_params=_params(("arbitrary",), 32),
        name="moe_dispatch",
    )(fill, slots, xp)


def _ple(x, p_ref, wup_ref, wgate_ref, bgate_ref):
    up = _dot(p_ref[...].astype(BF16), wup_ref[...])
    zg = _dot(x.astype(BF16), wgate_ref[...]) + bgate_ref[...]
    return up * (1.0 / (1.0 + jnp.exp(-zg)))


def _ple_ln_dense_kernel(x_ref, f_ref, p_ref, wup_ref, wgate_ref, bgate_ref, g_ref, beta_ref, o_ref):
    x = x_ref[...]
    f = jnp.concatenate(_load_token_tiles(f_ref), 1)
    ple = _ple(x, p_ref, wup_ref, wgate_ref, bgate_ref)
    o_ref[...] = _layer_norm(DEEPNORM_ALPHA * x + f + ple, g_ref[...], beta_ref[...])


def _ple_ln_moe_kernel(slot_ref, x_ref, wt_ref, p_ref, wup_ref, wgate_ref, bgate_ref, g_ref, beta_ref, ys_ref,
                       o_ref, y1_ref, y2_ref, sem):
    tm = x_ref.shape[0]
    bufs = (y1_ref, y2_ref)

    def issue(b, c):
        for u in range(DMA_UNROLL):
            t = b * DMA_UNROLL + u
            for k in range(2):
                pltpu.make_async_copy(ys_ref.at[_tile_rows(slot_ref[k, t])], bufs[k].at[_tile_rows(t)], sem).start()
        return c

    lax.fori_loop(0, tm // DMA_UNROLL, issue, 0)
    x = x_ref[...]
    ple = _ple(x, p_ref, wup_ref, wgate_ref, bgate_ref)
    for k in range(2):
        pltpu.make_async_copy(ys_ref.at[pl.ds(0, tm * TOK)], bufs[k], sem).wait()
    wt = wt_ref[...]
    f = (wt[:, 0:1] * jnp.concatenate(_load_token_tiles(y1_ref), 1)
         + wt[:, 1:2] * jnp.concatenate(_load_token_tiles(y2_ref), 1))
    o_ref[...] = _layer_norm(DEEPNORM_ALPHA * x + f + ple, g_ref[...], beta_ref[...])


def ple_ln(x, p, wup, wgate, bgate, g, beta, f=None, moe=None):
    s = x.shape[0]
    tm = min(512, s)
    row = lambda arr: pl.BlockSpec((tm, arr.shape[1]), lambda i, *_: (i, 0))
    full = lambda arr: pl.BlockSpec(arr.shape, lambda i, *_: (0, 0), pipeline_mode=pl.Buffered(1))
    tail = [row(p), full(wup), full(wgate), full(bgate), full(g), full(beta)]
    out_spec = pl.BlockSpec((tm, D_MODEL), lambda i, *_: (i, 0))
    out_shape = jax.ShapeDtypeStruct((s, D_MODEL), F32)
    if moe is None:
        return pl.pallas_call(
            _ple_ln_dense_kernel, out_shape=out_shape, grid=(s // tm,),
            in_specs=[row(x), pl.BlockSpec((tm * TOK, LANE), lambda i: (i, 0))] + tail, out_specs=out_spec,
            compiler_params=_params(("parallel",), 56), name="ple_ln_dense",
        )(x, f, p, wup, wgate, bgate, g, beta)
    ys, slots, wts_t = moe
    return pl.pallas_call(
        _ple_ln_moe_kernel,
        out_shape=out_shape,
        grid_spec=pltpu.PrefetchScalarGridSpec(
            num_scalar_prefetch=0,
            grid=(s // tm,),
            in_specs=[pl.BlockSpec((2, tm), lambda i: (0, i), memory_space=pltpu.SMEM), row(x), row(wts_t)] + tail
            + [pl.BlockSpec(memory_space=pl.ANY)],
            out_specs=out_spec,
            scratch_shapes=[pltpu.VMEM((tm * TOK, LANE), jnp.uint32), pltpu.VMEM((tm * TOK, LANE), jnp.uint32),
                            pltpu.SemaphoreType.DMA],
        ),
        compiler_params=_params(("arbitrary",), 56),
        name="ple_ln_moe",
    )(slots, x, wts_t, p, wup, wgate, bgate, g, beta, ys)


def _pad_cols(w, width):
    return jnp.pad(w, ((0, 0), (0, width - w.shape[1])))


def _pack_w_in(w):
    sizes = (MLA_Q_RANK, MLA_KV_RANK, MLA_ROPE, GLA_HEADS * GLA_DK, GLA_HEADS * GLA_DK, GLA_HEADS * GLA_DV,
             GLA_GATE_RANK, GLA_HEADS * GLA_DV, SWA_HEADS * SWA_HD, SWA_KV_HEADS * SWA_HD, SWA_KV_HEADS * SWA_HD)
    splits = np.cumsum(sizes)[:-1]
    q_a, kv_a, k_rope, g_q, g_k, g_v, g_lr, g_r, s_q, s_k, s_v = jnp.split(w, splits, axis=1)
    pieces = [q_a, g_v, g_r, s_q, kv_a, g_q, g_k, _pad_cols(k_rope, LANE), _pad_cols(g_lr, LANE), s_k, s_v]
    return jnp.concatenate(pieces, 1).astype(BF16)


def _pack_w_q_b(w):
    w = w.reshape(MLA_Q_RANK, MLA_HEADS, MLA_NOPE + MLA_ROPE)
    w = jnp.pad(w, ((0, 0), (0, 0), (0, MLA_QK - MLA_NOPE - MLA_ROPE)))
    return w.reshape(MLA_Q_RANK, MLA_HEADS * MLA_QK).astype(BF16)


def _token_mixer_ln(x, rope, w_in, q_gain, w_q_b, kv_gain, w_kv_b, gla_w, gla_b, gla_gain, sinks, rel_bias,
                    w_out, ln_g, ln_b):
    h = proj_in(x, _pack_w_in(w_in))
    q, k, v = mla_prep(h, *rope, q_gain.reshape(1, -1), kv_gain.reshape(1, -1), _pack_w_q_b(w_q_b),
                       w_kv_b.astype(BF16))
    a = mla_flash(q, k, v)
    gla_w_p = jnp.pad(gla_w, ((0, LANE - GLA_GATE_RANK), (0, 0))).astype(BF16)
    b = gla(h, gla_w_p, gla_b.reshape(1, -1), gla_gain.reshape(1, -1))
    c = swa(h, sinks, rel_bias)
    return out_ln(a, b, c, x, w_out.astype(BF16), ln_g.reshape(1, -1), ln_b.reshape(1, -1))


def _moe_plan(counts, tm, n_tiles):
    tiles = (counts + tm - 1) // tm
    ends = jnp.cumsum(tiles)
    base = (ends - tiles) * tm
    n_active = ends[-1:].astype(jnp.int32)
    tile_expert = jnp.searchsorted(ends, jnp.arange(n_tiles, dtype=jnp.int32), side="right")
    tile_expert = jnp.minimum(tile_expert, N_EXPERTS - 1).astype(jnp.int32)
    tile_start = jnp.arange(n_tiles, dtype=jnp.int32) * tm
    tile_valid = jnp.clip((base + counts)[tile_expert] - tile_start, 0, tm).astype(jnp.int32)
    used_rows = jnp.broadcast_to(ends[-1] * tm, counts.shape)
    fill = jnp.stack([base + counts, base + tiles * tm, used_rows]).astype(jnp.int32)
    return base.astype(jnp.int32), fill, tile_expert, tile_valid, n_active


def kernel(x, p, positions, w_in, mla_q_a_gain, mla_w_q_b, mla_kv_a_gain, mla_w_kv_b, gla_w_gate, gla_b_gate,
           gla_norm_gain, swa_sinks, rel_bias, w_out, ln1_g, ln1_b, ffn_w_gate, ffn_w_up, ffn_w_down,
           moe_router, moe_w_gate, moe_w_up, moe_w_down, ple_w_up, ple_w_gate, ple_b_gate, ln2_g, ln2_b):
    batch, s, _ = x.shape
    assert batch == 1
    xcur = x.reshape(s, D_MODEL)
    rope = rope_tables(positions)
    tm = min(FFN_ROWS, s)
    for i in range(DEPTH):
        x1, x1p = _token_mixer_ln(xcur, rope, w_in[i], mla_q_a_gain[i], mla_w_q_b[i], mla_kv_a_gain[i],
                                  mla_w_kv_b[i], gla_w_gate[i], gla_b_gate[i], gla_norm_gain[i], swa_sinks[i],
                                  rel_bias, w_out[i], ln1_g[i], ln1_b[i])
        tail = (p[i, 0], ple_w_up[i].astype(BF16), ple_w_gate[i].astype(BF16), ple_b_gate[i].reshape(1, -1),
                ln2_g[i].reshape(1, -1), ln2_b[i].reshape(1, -1))
        j = i // 2
        if i % 2 == 0:
            n_tiles = s // tm
            f = grouped_swiglu(x1p, ffn_w_gate[j][None], ffn_w_up[j][None], ffn_w_down[j][None],
                               jnp.zeros((n_tiles,), jnp.int32), jnp.full((n_tiles,), tm, jnp.int32),
                               jnp.full((1,), n_tiles, jnp.int32), tm, FFN_COLS)
            xcur = ple_ln(x1, *tail, f=f)
        else:
            n_tiles = (2 * s + N_EXPERTS * (tm - 1)) // tm
            e_idx, rank, wts, counts = route_tokens(x1, moe_router[j].T)
            base, fill, tile_expert, tile_valid, n_active = _moe_plan(counts[:, 0], tm, n_tiles)
            slots = token_slots(e_idx, rank, base)
            xs = moe_dispatch(x1p, slots, fill, n_tiles * tm)
            ys = grouped_swiglu(xs, moe_w_gate[j], moe_w_up[j], moe_w_down[j], tile_expert, tile_valid, n_active,
                                tm, FFN_COLS)
            xcur = ple_ln(x1, *tail, moe=(ys, slots, wts.T))
    return xcur.reshape(batch, s, D_MODEL)
```

```python
import functools
import math

import numpy as np
import jax
import jax.numpy as jnp
from jax import lax
from jax.experimental import pallas as pl
from jax.experimental.pallas import tpu as pltpu

F32 = jnp.float32
BF16 = jnp.bfloat16

D_MODEL = 2048
DEPTH = 2
MLA_HEADS = 8
MLA_Q_RANK = 512
MLA_KV_RANK = 256
MLA_NOPE = 128
MLA_ROPE = 64
MLA_V = 128
ROPE_THETA = 10000.0
GLA_HEADS = 4
GLA_DK = 64
GLA_DV = 128
GLA_GATE_RANK = 16
GLA_TAU = 16.0
SWA_HEADS = 8
SWA_KV_HEADS = 2
SWA_HD = 64
SWA_WINDOW = 128
SWA_BLOCK = 128
REL_BUCKETS = 32
REL_MAX_DIST = 128
D_FF = 5632
N_EXPERTS = 8
PLE_DIM = 256
LN_EPS = 1e-5
RMS_EPS = 1e-6
DEEPNORM_ALPHA = (2 * DEPTH) ** 0.25

LANE = 128
LOG2E = math.log2(math.e)
NEG_BIG = -1e30

H_QA, H_GV, H_GR, H_SQ = 0, 512, 1024, 1536
H_KVA, H_GQ, H_GK = 2048, 2304, 2560
H_KR, H_GLR, H_SK, H_SV = 2816, 2944, 3072, 3200
H_COLS = 3328

MLA_QK = 2 * LANE

GLA_L = 128
GLA_SUB = 32
GLA_NSUB = GLA_L // GLA_SUB


def _params(sem, vmem_mb):
    return pltpu.CompilerParams(dimension_semantics=sem, vmem_limit_bytes=vmem_mb * 2 ** 20)


def _dot(a, b):
    return jnp.dot(a, b, preferred_element_type=F32)


def _dot_nt(a, b):
    return lax.dot_general(a, b, (((1,), (1,)), ((), ())), preferred_element_type=F32)


def _dot_tn(a, b):
    return lax.dot_general(a, b, (((0,), (0,)), ((), ())), preferred_element_type=F32)


def _proj_in_kernel(x_ref, w_ref, o_ref, xb_ref):
    @pl.when(pl.program_id(1) == 0)
    def _():
        xb_ref[...] = x_ref[...].astype(BF16)

    o_ref[...] = _dot(xb_ref[...], w_ref[...]).astype(o_ref.dtype)


def proj_in(x, w_p):
    s = x.shape[0]
    tm = min(1024, s)
    tn = H_COLS // 2
    return pl.pallas_call(
        _proj_in_kernel,
        out_shape=jax.ShapeDtypeStruct((s, H_COLS), BF16),
        grid=(s // tm, H_COLS // tn),
        in_specs=[pl.BlockSpec((tm, D_MODEL), lambda i, j: (i, 0)),
                  pl.BlockSpec((D_MODEL, tn), lambda i, j: (0, j))],
        out_specs=pl.BlockSpec((tm, tn), lambda i, j: (i, j)),
        scratch_shapes=[pltpu.VMEM((tm, D_MODEL), BF16)],
        compiler_params=_params(("parallel", "arbitrary"), 56),
        name="proj_in",
    )(x, w_p)


def _rope_table_kernel(pos_ref, inv_ref, cos_ref, sa_ref, sb_ref):
    ang = pos_ref[...].astype(F32) * inv_ref[...]
    lane = lax.broadcasted_iota(jnp.int32, ang.shape, 1)
    half = MLA_ROPE // 2
    c, s = jnp.cos(ang), jnp.sin(ang)
    cos_ref[...] = c
    sa_ref[...] = jnp.where((lane >= half) & (lane < 2 * half), s, 0.0)
    sb_ref[...] = jnp.where(lane < half, -s, 0.0)


def rope_tables(positions):
    s = positions.shape[-1]
    half = MLA_ROPE // 2
    inv = ROPE_THETA ** (-jnp.arange(half, dtype=F32) / half)
    inv = jnp.concatenate([inv, inv, jnp.zeros((LANE - 2 * half,), F32)]).reshape(1, LANE)
    tm = min(1024, s)
    spec = pl.BlockSpec((tm, LANE), lambda i: (i, 0))
    return pl.pallas_call(
        _rope_table_kernel,
        out_shape=[jax.ShapeDtypeStruct((s, LANE), F32)] * 3,
        grid=(s // tm,),
        in_specs=[pl.BlockSpec((tm, 1), lambda i: (i, 0)), pl.BlockSpec((1, LANE), lambda i: (0, 0))],
        out_specs=[spec, spec, spec],
        compiler_params=_params(("parallel",), 32),
        name="rope_tables",
    )(positions.reshape(s, 1), inv)


def _rope(x, cos, sa, sb):
    return x * cos + pltpu.roll(x, MLA_ROPE // 2, 1) * sa + pltpu.roll(x, LANE - MLA_ROPE // 2, 1) * sb


def _mla_prep_kernel(qa_ref, kva_ref, kr_ref, cos_ref, sa_ref, sb_ref, gq_ref, gkv_ref, wq_ref, wkv_ref,
                     q_out, k_out, v_out):
    cos, sa, sb = cos_ref[...], sa_ref[...], sb_ref[...]
    qscale = (MLA_NOPE + MLA_ROPE) ** -0.5 * LOG2E

    qa = qa_ref[...].astype(F32)
    qn = qa * lax.rsqrt(jnp.mean(qa * qa, -1, keepdims=True) + RMS_EPS) * gq_ref[...]
    q = _dot(qn.astype(BF16), wq_ref[...])
    for h in range(MLA_HEADS):
        c0 = h * MLA_QK
        q_out[h, :, 0:LANE] = (q[:, c0:c0 + LANE] * qscale).astype(BF16)
        pe = _rope(q[:, c0 + LANE:c0 + 2 * LANE], cos, sa, sb)
        q_out[h, :, LANE:2 * LANE] = (pe * qscale).astype(BF16)

    kva = kva_ref[...].astype(F32)
    kvn = kva * lax.rsqrt(jnp.mean(kva * kva, -1, keepdims=True) + RMS_EPS) * gkv_ref[...]
    kv = _dot(kvn.astype(BF16), wkv_ref[...])
    kpe = _rope(kr_ref[...].astype(F32), cos, sa, sb).astype(BF16)
    for h in range(MLA_HEADS):
        c0 = h * (MLA_NOPE + MLA_V)
        k_out[h, :, 0:LANE] = kv[:, c0:c0 + MLA_NOPE].astype(BF16)
        k_out[h, :, LANE:2 * LANE] = kpe
        v_out[h] = kv[:, c0 + MLA_NOPE:c0 + MLA_NOPE + MLA_V].astype(BF16)


def mla_prep(h, cos, sa, sb, gq, gkv, wq_p, wkv):
    s = h.shape[0]
    tm = min(512, s)
    row = lambda width, col: pl.BlockSpec((tm, width), lambda i: (i, col // width))
    full = lambda a: pl.BlockSpec(a.shape, lambda i: (0,) * a.ndim)
    return pl.pallas_call(
        _mla_prep_kernel,
        out_shape=[jax.ShapeDtypeStruct((MLA_HEADS, s, MLA_QK), BF16),
                   jax.ShapeDtypeStruct((MLA_HEADS, s, MLA_QK), BF16),
                   jax.ShapeDtypeStruct((MLA_HEADS, s, MLA_V), BF16)],
        grid=(s // tm,),
        in_specs=[row(MLA_Q_RANK, H_QA), row(MLA_KV_RANK, H_KVA), row(LANE, H_KR),
                  row(LANE, 0), row(LANE, 0), row(LANE, 0),
                  full(gq), full(gkv), full(wq_p), full(wkv)],
        out_specs=[pl.BlockSpec((MLA_HEADS, tm, MLA_QK), lambda i: (0, i, 0)),
                   pl.BlockSpec((MLA_HEADS, tm, MLA_QK), lambda i: (0, i, 0)),
                   pl.BlockSpec((MLA_HEADS, tm, MLA_V), lambda i: (0, i, 0))],
        compiler_params=_params(("parallel",), 48),
        name="mla_prep",
    )(h, h, h, cos, sa, sb, gq, gkv, wq_p, wkv)


def _mla_flash_kernel(q_ref, k_ref, v_ref, o_ref, sa_ref, sb_ref, mxa_ref, mxb_ref, m_ref, l_ref, acc_ref,
                      *, t, nh):
    qi = pl.program_id(1)
    heads = range(nh)
    bufs = ((sa_ref, mxa_ref), (sb_ref, mxb_ref))

    def produce(b, parity, masked):
        s_ref, mx_ref = bufs[parity]
        start = pl.multiple_of(b * t, t)
        for h in heads:
            s = _dot_nt(q_ref[h], k_ref[h, pl.ds(start, t), :])
            if masked:
                rows = lax.broadcasted_iota(jnp.int32, s.shape, 0) + qi * t
                cols = lax.broadcasted_iota(jnp.int32, s.shape, 1) + b * t
                s = jnp.where(cols <= rows, s, NEG_BIG)
            s_ref[h] = s
            mx_ref[h] = jnp.broadcast_to(jnp.max(s, -1, keepdims=True), (t, LANE))

    def absorb(b, parity):
        s_ref, mx_ref = bufs[parity]
        start = pl.multiple_of(b * t, t)
        for h in heads:
            m_new = jnp.maximum(m_ref[h], mx_ref[h])
            alpha = jnp.exp2(m_ref[h] - m_new)
            p = jnp.exp2(s_ref[h] - jnp.concatenate([m_new] * (t // LANE), 1))
            l_ref[h] = alpha * l_ref[h] + jnp.broadcast_to(jnp.sum(p, -1, keepdims=True), (t, LANE))
            acc_ref[h] = alpha * acc_ref[h] + _dot(p.astype(BF16), v_ref[h, pl.ds(start, t), :])
            m_ref[h] = m_new

    m_ref[...] = jnp.full(m_ref.shape, NEG_BIG, F32)
    l_ref[...] = jnp.zeros_like(l_ref)
    acc_ref[...] = jnp.zeros_like(acc_ref)
    produce(0, 0, True)

    def pair(i, c):
        b = 2 * i
        produce(b + 1, 1, False)
        absorb(b, 0)
        produce(b + 2, 0, False)
        absorb(b + 1, 1)
        return c

    n_pairs = jnp.maximum(qi - 1, 0) // 2
    lax.fori_loop(0, n_pairs, pair, 0)
    done = 2 * n_pairs

    @pl.when(qi == 0)
    def _():
        absorb(0, 0)

    @pl.when((qi > 0) & (qi - done == 1))
    def _():
        produce(qi, 1, True)
        absorb(qi - 1, 0)
        absorb(qi, 1)

    @pl.when((qi > 0) & (qi - done == 2))
    def _():
        produce(qi - 1, 1, False)
        absorb(qi - 2, 0)
        produce(qi, 0, True)
        absorb(qi - 1, 1)
        absorb(qi, 0)

    for h in heads:
        o_ref[:, h * MLA_V:(h + 1) * MLA_V] = (acc_ref[h] / l_ref[h]).astype(o_ref.dtype)


def mla_flash(q, k, v):
    _, s, _ = q.shape
    t = min(512, s)
    nh = 2
    return pl.pallas_call(
        functools.partial(_mla_flash_kernel, t=t, nh=nh),
        out_shape=jax.ShapeDtypeStruct((s, MLA_HEADS * MLA_V), BF16),
        grid=(MLA_HEADS // nh, s // t),
        in_specs=[pl.BlockSpec((nh, t, MLA_QK), lambda h, i: (h, i, 0)),
                  pl.BlockSpec((nh, s, MLA_QK), lambda h, i: (h, 0, 0)),
                  pl.BlockSpec((nh, s, MLA_V), lambda h, i: (h, 0, 0))],
        out_specs=pl.BlockSpec((t, nh * MLA_V), lambda h, i: (i, h)),
        scratch_shapes=[pltpu.VMEM((nh, t, t), F32), pltpu.VMEM((nh, t, t), F32),
                        pltpu.VMEM((nh, t, LANE), F32), pltpu.VMEM((nh, t, LANE), F32),
                        pltpu.VMEM((nh, t, LANE), F32), pltpu.VMEM((nh, t, LANE), F32),
                        pltpu.VMEM((nh, t, MLA_V), F32)],
        compiler_params=_params(("parallel", "arbitrary"), 48),
        name="mla_flash",
    )(q, k, v)


def _gla_masks():
    i = np.arange(GLA_L)[:, None]
    j = np.arange(GLA_L)[None, :]
    same = (i // GLA_SUB) == (j // GLA_SUB)
    mats = [j <= i, same & (j <= i), same & (j > i), j > i]
    for sub in range(GLA_NSUB - 1):
        mats.append((j >= (sub + 1) * GLA_SUB) & (j <= i))
    return np.concatenate(mats, 0).astype(np.float32)


def _gla_chunk(rows, states, q_ref, k_ref, v_ref, lr_ref, r_ref, wg_ref, bg_ref, gain_ref, mask_ref, o_ref):
    L, hk = GLA_L, GLA_HEADS * GLA_DK
    z = _dot(lr_ref[rows, :], wg_ref[...]) + bg_ref[...]
    g = (jnp.minimum(z, 0.0) - jnp.log(1.0 + jnp.exp(-jnp.abs(z)))) * (1.0 / GLA_TAU)
    g_hi = g.astype(BF16)
    g_lo = (g - g_hi.astype(F32)).astype(BF16)
    cums = _dot(mask_ref[...], g_hi) + _dot(mask_ref[...], g_lo)
    b_all = cums[0:L]
    b_loc = cums[L:2 * L]
    sfx_loc = cums[2 * L:3 * L]
    sfx_all = cums[3 * L:4 * L]

    q = q_ref[rows, :].astype(F32) * (GLA_DK ** -0.5)
    k = k_ref[rows, :].astype(F32)
    q_inter = (q * jnp.exp(b_all)).astype(BF16)
    q_diag = (q * jnp.exp(b_loc)).astype(BF16)
    k_diag = (k * jnp.exp(-b_loc)).astype(BF16)
    k_end = k * jnp.exp(sfx_loc)
    k_state = (k * jnp.exp(sfx_all)).astype(BF16)
    ones = jnp.ones((L, GLA_DV), BF16)

    row = lax.broadcasted_iota(jnp.int32, (L, L), 0)
    col = lax.broadcasted_iota(jnp.int32, (L, L), 1)
    diag_ok = ((row // GLA_SUB) == (col // GLA_SUB)) & (col <= row)
    off_ok = (row // GLA_SUB) > (col // GLA_SUB)
    sub_of_row = lax.broadcasted_iota(jnp.int32, (L, hk), 0) // GLA_SUB

    q_off, k_off = [], []
    for sub in range(GLA_NSUB - 1):
        q_off.append((q * jnp.exp(cums[(4 + sub) * L:(5 + sub) * L])).astype(BF16))
        k_off.append(jnp.where(sub_of_row == sub, k_end, 0.0).astype(BF16))

    v = v_ref[rows, :]
    r = r_ref[rows, :].astype(F32)
    gate = r * (1.0 / (1.0 + jnp.exp(-r)))
    new_states = []
    for h in range(GLA_HEADS):
        ks = slice(h * GLA_DK, (h + 1) * GLA_DK)
        vs = slice(h * GLA_DV, (h + 1) * GLA_DV)
        state = states[h]
        v_h = v[:, vs]
        a = jnp.where(diag_ok, _dot_nt(q_diag[:, ks], k_diag[:, ks]), 0.0)
        qo = jnp.concatenate([t[:, ks] for t in q_off], 1)
        ko = jnp.concatenate([t[:, ks] for t in k_off], 1)
        a = a + jnp.where(off_ok, _dot_nt(qo, ko), 0.0)
        o = _dot(q_inter[:, ks], state.astype(BF16)) + _dot(a.astype(BF16), v_h)
        decay = jnp.exp(_dot_tn(g_hi[:, ks], ones) + _dot_tn(g_lo[:, ks], ones))
        new_states.append(state * decay + _dot_tn(k_state[:, ks], v_h))
        o = o * lax.rsqrt(jnp.mean(o * o, -1, keepdims=True) + RMS_EPS) * gain_ref[:, vs]
        o_ref[rows, vs] = (o * gate[:, vs]).astype(o_ref.dtype)
    return new_states


GLA_CHUNKS_PER_STEP = 4


def _gla_kernel(q_ref, k_ref, v_ref, lr_ref, r_ref, wg_ref, bg_ref, gain_ref, mask_ref, o_ref, state_ref):
    @pl.when(pl.program_id(0) == 0)
    def _():
        state_ref[...] = jnp.zeros_like(state_ref)

    states = [state_ref[h] for h in range(GLA_HEADS)]
    for c in range(q_ref.shape[0] // GLA_L):
        states = _gla_chunk(slice(c * GLA_L, (c + 1) * GLA_L), states, q_ref, k_ref, v_ref, lr_ref, r_ref, wg_ref,
                            bg_ref, gain_ref, mask_ref, o_ref)
    for h in range(GLA_HEADS):
        state_ref[h] = states[h]


def gla(h, wg_p, bg, gain):
    s = h.shape[0]
    rows = min(GLA_CHUNKS_PER_STEP * GLA_L, s)
    masks = jnp.asarray(_gla_masks(), BF16)
    row = lambda width, col: pl.BlockSpec((rows, width), lambda i: (i, col // width))
    full = lambda a: pl.BlockSpec(a.shape, lambda i: (0,) * a.ndim)
    hk, hv = GLA_HEADS * GLA_DK, GLA_HEADS * GLA_DV
    return pl.pallas_call(
        _gla_kernel,
        out_shape=jax.ShapeDtypeStruct((s, hv), BF16),
        grid=(s // rows,),
        in_specs=[row(hk, H_GQ), row(hk, H_GK), row(hv, H_GV), row(LANE, H_GLR), row(hv, H_GR),
                  full(wg_p), full(bg), full(gain), full(masks)],
        out_specs=pl.BlockSpec((rows, hv), lambda i: (i, 0)),
        scratch_shapes=[pltpu.VMEM((GLA_HEADS, GLA_DK, GLA_DV), F32)],
        compiler_params=_params(("arbitrary",), 32),
        name="gla",
    )(h, h, h, h, h, wg_p, bg, gain, masks)


def _t5_bucket_table():
    L = SWA_BLOCK
    dist = np.arange(L)[:, None] + L - np.arange(2 * L)[None, :]
    d = np.clip(dist, 0, None)
    max_exact = REL_BUCKETS // 2
    df = np.maximum(d, 1).astype(np.float32)
    large = max_exact + (np.log(df / np.float32(max_exact)) / np.float32(math.log(REL_MAX_DIST / max_exact))
                         * np.float32(REL_BUCKETS - max_exact)).astype(np.int32)
    large = np.minimum(large, REL_BUCKETS - 1)
    bucket = np.where(d < max_exact, d, large)
    in_window = (dist >= 0) & (dist < SWA_WINDOW)
    return np.where(in_window, bucket, -1).astype(np.int32)


def _swa_kernel(relb_ref, sink_ref, q_ref, kp_ref, kc_ref, vp_ref, vc_ref, bucket_ref, o_ref, bias_ref):
    i = pl.program_id(0)
    L = SWA_BLOCK

    @pl.when(i == 0)
    def _():
        bucket = bucket_ref[...]
        for h in range(SWA_HEADS):
            acc = jnp.full(bucket.shape, NEG_BIG, F32)
            for b in range(REL_BUCKETS):
                acc = jnp.where(bucket == b, relb_ref[b, h], acc)
            bias_ref[h] = acc

    kcat = jnp.concatenate([kp_ref[...], kc_ref[...]], 0)
    vcat = jnp.concatenate([vp_ref[...], vc_ref[...]], 0)
    q = q_ref[...]
    col = lax.broadcasted_iota(jnp.int32, (L, 2 * L), 1)
    real_key = (col >= L) | (i > 0)
    g = SWA_HEADS // SWA_KV_HEADS
    scale = SWA_HD ** -0.5
    for h in range(SWA_HEADS):
        kv = h // g
        hs = slice(h * SWA_HD, (h + 1) * SWA_HD)
        kvs = slice(kv * SWA_HD, (kv + 1) * SWA_HD)
        s = _dot_nt(q[:, hs], kcat[:, kvs]) * scale + bias_ref[h]
        s = jnp.where(real_key, s, NEG_BIG)
        sink = sink_ref[h]
        m = jnp.maximum(jnp.max(s, -1, keepdims=True), sink)
        p = jnp.exp(s - m)
        denom = jnp.sum(p, -1, keepdims=True) + jnp.exp(sink - m)
        o = _dot((p / denom).astype(BF16), vcat[:, kvs])
        o_ref[:, hs] = o.astype(o_ref.dtype)


def swa(h, sinks, rel_bias):
    s = h.shape[0]
    L = SWA_BLOCK
    bucket = jnp.asarray(_t5_bucket_table())
    kvw = SWA_KV_HEADS * SWA_HD
    hw = SWA_HEADS * SWA_HD
    cur = lambda width, col: pl.BlockSpec((L, width), lambda i, *_: (i, col // width))
    prev = lambda width, col: pl.BlockSpec((L, width), lambda i, *_: (jnp.maximum(i - 1, 0), col // width))
    return pl.pallas_call(
        _swa_kernel,
        out_shape=jax.ShapeDtypeStruct((s, hw), BF16),
        grid_spec=pltpu.PrefetchScalarGridSpec(
            num_scalar_prefetch=2,
            grid=(s // L,),
            in_specs=[cur(hw, H_SQ), prev(kvw, H_SK), cur(kvw, H_SK), prev(kvw, H_SV), cur(kvw, H_SV),
                      pl.BlockSpec(bucket.shape, lambda i, *_: (0, 0))],
            out_specs=pl.BlockSpec((L, hw), lambda i, *_: (i, 0)),
            scratch_shapes=[pltpu.VMEM((SWA_HEADS, L, 2 * L), F32)],
        ),
        compiler_params=_params(("arbitrary",), 32),
        name="swa",
    )(rel_bias, sinks, h, h, h, h, h, bucket)


def _layer_norm(y, g, b):
    mu = jnp.mean(y, -1, keepdims=True)
    yc = y - mu
    var = jnp.mean(yc * yc, -1, keepdims=True)
    return yc * lax.rsqrt(var + LN_EPS) * g + b


HALF = D_MODEL // 2
HIGH16 = 0xFFFF0000


SUBLANE = 8
TOK = HALF // LANE


def _store_token_tiles(ref, x):
    tm = x.shape[0]
    lo = lax.bitcast_convert_type(x[:, :HALF].astype(BF16).astype(F32), jnp.uint32) >> 16
    hi = lax.bitcast_convert_type(x[:, HALF:].astype(BF16).astype(F32), jnp.uint32) & jnp.uint32(HIGH16)
    packed = lo | hi
    for s in range(TOK):
        ref[pl.ds(s, tm, stride=TOK), :] = packed[:, s * LANE:(s + 1) * LANE]


def _load_token_tiles(ref, rows=None):
    first, tm = rows if rows is not None else (0, ref.shape[0] // TOK)
    lo, hi = [], []
    for s in range(TOK):
        w = ref[pl.ds(first * TOK + s, tm, stride=TOK), :]
        lo.append(lax.bitcast_convert_type(w << 16, F32))
        hi.append(lax.bitcast_convert_type(w & jnp.uint32(HIGH16), F32))
    return jnp.concatenate(lo, 1), jnp.concatenate(hi, 1)


ROW_SPLIT = 2


def _out_ln_kernel(a_ref, b_ref, c_ref, x_ref, w_ref, g_ref, beta_ref, o_ref, op_ref):
    na, nb = a_ref.shape[1], b_ref.shape[1]
    nr = x_ref.shape[0] // ROW_SPLIT
    for r0 in range(0, x_ref.shape[0], nr):
        rows = slice(r0, r0 + nr)
        m = _dot(a_ref[rows, :], w_ref[0:na, :])
        m = m + _dot(b_ref[rows, :], w_ref[na:na + nb, :])
        m = m + _dot(c_ref[rows, :], w_ref[na + nb:, :])
        y = _layer_norm(DEEPNORM_ALPHA * x_ref[rows, :] + m, g_ref[...], beta_ref[...])
        o_ref[rows, :] = y
        _store_token_tiles(op_ref.at[pl.ds(r0 * TOK, nr * TOK)], y)


def out_ln(a, b, c, x, w, g, beta):
    s = x.shape[0]
    tm = min(512, s)
    row = lambda arr: pl.BlockSpec((tm, arr.shape[1]), lambda i: (i, 0))
    full = lambda arr: pl.BlockSpec(arr.shape, lambda i: (0, 0), pipeline_mode=pl.Buffered(1))
    return pl.pallas_call(
        _out_ln_kernel,
        out_shape=[jax.ShapeDtypeStruct((s, D_MODEL), F32), jax.ShapeDtypeStruct((s * TOK, LANE), jnp.uint32)],
        grid=(s // tm,),
        in_specs=[row(a), row(b), row(c), row(x), full(w), full(g), full(beta)],
        out_specs=[pl.BlockSpec((tm, D_MODEL), lambda i: (i, 0)), pl.BlockSpec((tm * TOK, LANE), lambda i: (i, 0))],
        compiler_params=_params(("parallel",), 56),
        name="out_ln",
    )(a, b, c, x, w, g, beta)


FFN_ROWS = 1024
FFN_COLS = 512
FFN_OUT_CHUNK = 512
FFN_SUB = 256


def _ffn_kernel(te_ref, tv_ref, na_ref, xp_ref, wg_ref, wu_ref, wd_ref, o_ref, xb_ref, acc_ref):
    i, j = pl.program_id(0), pl.program_id(1)
    tm = xb_ref.shape[0]
    valid = tv_ref[i]

    @pl.when(j == 0)
    def _():
        acc_ref[...] = jnp.zeros_like(acc_ref)

    @pl.when((j == 0) & (valid > 0))
    def _():
        lo, hi = _load_token_tiles(xp_ref)
        xb_ref[:, :HALF] = lo.astype(BF16)
        xb_ref[:, HALF:] = hi.astype(BF16)

    def rows_step(r0, nr):
        xb = xb_ref[r0:r0 + nr, :]
        gate = _dot(xb, wg_ref[...].astype(BF16))
        up = _dot(xb, wu_ref[...].astype(BF16))
        hmid = (gate * (1.0 / (1.0 + jnp.exp(-gate))) * up).astype(BF16)
        for c in range(0, D_MODEL, FFN_OUT_CHUNK):
            cs = slice(c, c + FFN_OUT_CHUNK)
            acc_ref[r0:r0 + nr, cs] += _dot(hmid, wd_ref[:, cs].astype(BF16))

    nearly_full = valid > tm - FFN_SUB

    @pl.when(nearly_full)
    def _():
        rows_step(0, tm)

    for r0 in range(0, tm - FFN_SUB, FFN_SUB):
        @pl.when(jnp.logical_not(nearly_full) & (valid > r0))
        def _():
            rows_step(r0, FFN_SUB)

    @pl.when(j == pl.num_programs(1) - 1)
    def _():
        _store_token_tiles(o_ref, acc_ref[...])


def grouped_swiglu(xp, wg, wu, wd, tile_expert, tile_valid, n_active, tm, tf):
    n_tiles, nf = xp.shape[0] // (tm * TOK), D_FF // tf

    def tile(i, na):
        return jnp.minimum(i, na[0] - 1)

    def fcol(i, j, na):
        return jnp.where(i < na[0], j, nf - 1)

    return pl.pallas_call(
        _ffn_kernel,
        out_shape=jax.ShapeDtypeStruct(xp.shape, jnp.uint32),
        grid_spec=pltpu.PrefetchScalarGridSpec(
            num_scalar_prefetch=3,
            grid=(n_tiles, nf),
            in_specs=[pl.BlockSpec((tm * TOK, LANE), lambda i, j, te, tv, na: (tile(i, na), 0)),
                      pl.BlockSpec((None, D_MODEL, tf),
                                   lambda i, j, te, tv, na: (te[tile(i, na)], 0, fcol(i, j, na))),
                      pl.BlockSpec((None, D_MODEL, tf),
                                   lambda i, j, te, tv, na: (te[tile(i, na)], 0, fcol(i, j, na))),
                      pl.BlockSpec((None, tf, D_MODEL),
                                   lambda i, j, te, tv, na: (te[tile(i, na)], fcol(i, j, na), 0))],
            out_specs=pl.BlockSpec((tm * TOK, LANE), lambda i, j, te, tv, na: (i, 0)),
            scratch_shapes=[pltpu.VMEM((tm, D_MODEL), BF16), pltpu.VMEM((tm, D_MODEL), F32)],
        ),
        compiler_params=_params(("arbitrary", "arbitrary"), 60),
        name="grouped_swiglu",
    )(tile_expert, tile_valid, n_active, xp, wg, wu, wd)


def _router_kernel(x_ref, w_ref, e_ref, r_ref, wt_ref, cnt_ref, run_ref):
    i = pl.program_id(0)
    tm = x_ref.shape[0]
    ne = N_EXPERTS

    @pl.when(i == 0)
    def _():
        run_ref[...] = jnp.zeros_like(run_ref)

    x = x_ref[...]
    x_hi = x.astype(BF16)
    x_lo = (x - x_hi.astype(F32)).astype(BF16)
    w = w_ref[...]
    w_hi = w.astype(BF16).astype(F32)
    w_lo = w - w_hi
    both = _dot_nt(jnp.concatenate([w_hi, w_lo], 0).astype(BF16), x_hi)
    cross = _dot_nt(jnp.concatenate([w_hi, jnp.zeros_like(w_hi)], 0).astype(BF16), x_lo)
    logits = both[0:ne] + both[ne:2 * ne] + cross[0:ne]

    eidx = lax.broadcasted_iota(jnp.int32, logits.shape, 0).astype(F32)
    v1 = jnp.max(logits, 0, keepdims=True)
    i1 = jnp.min(jnp.where(logits == v1, eidx, float(ne)), 0, keepdims=True)
    rest = jnp.where(eidx == i1, -jnp.inf, logits)
    v2 = jnp.max(rest, 0, keepdims=True)
    i2 = jnp.min(jnp.where(rest == v2, eidx, float(ne)), 0, keepdims=True)
    t = jnp.exp(v2 - v1)
    w1 = 1.0 / (1.0 + t)
    wt_ref[0:1, :] = w1
    wt_ref[1:2, :] = t * w1
    e_ref[0:1, :] = i1.astype(jnp.int32)
    e_ref[1:2, :] = i2.astype(jnp.int32)

    sel1, sel2 = eidx == i1, eidx == i2
    sel = jnp.where(sel1, 1.0, 0.0) + jnp.where(sel2, 1.0, 0.0)
    before = (lax.broadcasted_iota(jnp.int32, (tm, tm), 0) < lax.broadcasted_iota(jnp.int32, (tm, tm), 1))
    sel16 = jnp.concatenate([sel, jnp.zeros_like(sel)], 0).astype(BF16)
    prefix = _dot(sel16, jnp.where(before, 1.0, 0.0).astype(BF16))[0:ne]
    rank = prefix + run_ref[:, 0:1]
    r_ref[0:1, :] = jnp.sum(jnp.where(sel1, rank, 0.0), 0, keepdims=True).astype(jnp.int32)
    r_ref[1:2, :] = jnp.sum(jnp.where(sel2, rank, 0.0), 0, keepdims=True).astype(jnp.int32)
    run_ref[...] = run_ref[...] + jnp.sum(sel, 1, keepdims=True)
    cnt_ref[...] = run_ref[...].astype(jnp.int32)


def route_tokens(x, w_router_t):
    s = x.shape[0]
    tm = min(512, s)
    pair = pl.BlockSpec((2, tm), lambda i: (0, i))
    return pl.pallas_call(
        _router_kernel,
        out_shape=[jax.ShapeDtypeStruct((2, s), jnp.int32), jax.ShapeDtypeStruct((2, s), jnp.int32),
                   jax.ShapeDtypeStruct((2, s), F32), jax.ShapeDtypeStruct((N_EXPERTS, LANE), jnp.int32)],
        grid=(s // tm,),
        in_specs=[pl.BlockSpec((tm, D_MODEL), lambda i: (i, 0)),
                  pl.BlockSpec((N_EXPERTS, D_MODEL), lambda i: (0, 0))],
        out_specs=[pair, pair, pair, pl.BlockSpec((N_EXPERTS, LANE), lambda i: (0, 0))],
        scratch_shapes=[pltpu.VMEM((N_EXPERTS, LANE), F32)],
        compiler_params=_params(("arbitrary",), 32),
        name="moe_router",
    )(x, w_router_t)


def _slot_kernel(base_ref, e_ref, r_ref, s_ref):
    e = e_ref[...]
    slot = r_ref[...]
    for k in range(N_EXPERTS):
        slot = slot + jnp.where(e == k, base_ref[k], 0)
    s_ref[...] = slot


def token_slots(e_idx, rank, base):
    whole = pl.BlockSpec(e_idx.shape, lambda i, *_: (0, 0))
    return pl.pallas_call(
        _slot_kernel,
        out_shape=jax.ShapeDtypeStruct(e_idx.shape, jnp.int32),
        grid_spec=pltpu.PrefetchScalarGridSpec(num_scalar_prefetch=1, grid=(1,), in_specs=[whole, whole],
                                               out_specs=whole),
        compiler_params=_params(("arbitrary",), 32),
        name="token_slots",
    )(base, e_idx, rank)


DMA_UNROLL = 8


def _tile_rows(index):
    return pl.ds(pl.multiple_of(index * TOK, TOK), TOK)


def _dispatch_kernel(fill_ref, slot_ref, x_ref, xs_ref, zero_ref, sem, zsem):
    i = pl.program_id(0)
    tm = x_ref.shape[0] // TOK

    def issue(b, c):
        for u in range(DMA_UNROLL):
            t = b * DMA_UNROLL + u
            for k in range(2):
                pltpu.make_async_copy(x_ref.at[_tile_rows(t)], xs_ref.at[_tile_rows(slot_ref[k, t])],
                                      sem).start(priority=k)
        return c

    lax.fori_loop(0, tm // DMA_UNROLL, issue, 0)
    for k in range(2):
        pltpu.make_async_copy(x_ref, xs_ref.at[pl.ds(0, tm * TOK)], sem).wait()

    @pl.when(i == pl.num_programs(0) - 1)
    def _():
        zero_ref[...] = jnp.zeros_like(zero_ref)

        def zero_copy(slot):
            return pltpu.make_async_copy(zero_ref.at[pl.ds(0, TOK)], xs_ref.at[_tile_rows(slot)], zsem)

        for e in range(N_EXPERTS):
            lo, hi = fill_ref[0, e], fill_ref[1, e]

            def zissue(slot, c):
                zero_copy(slot).start()
                return c

            def zdrain(slot, c):
                zero_copy(slot).wait()
                return c

            lax.fori_loop(lo, hi, zissue, 0)
            lax.fori_loop(lo, hi, zdrain, 0)

        zrows = zero_ref.shape[0]
        zb = zrows // TOK

        def block_copy(b):
            return pltpu.make_async_copy(zero_ref, xs_ref.at[pl.ds(pl.multiple_of(b * zrows, zrows), zrows)], zsem)

        def bissue(b, c):
            block_copy(b).start()
            return c

        def bdrain(b, c):
            block_copy(b).wait()
            return c

        first, last = fill_ref[2, 0] // zb, xs_ref.shape[0] // zrows
        lax.fori_loop(first, last, bissue, 0)
        lax.fori_loop(first, last, bdrain, 0)


def moe_dispatch(xp, slots, fill, rows):
    s = xp.shape[0] // TOK
    tm = min(512, s)
    zb = min(256, s)
    return pl.pallas_call(
        _dispatch_kernel,
        out_shape=jax.ShapeDtypeStruct((rows * TOK, LANE), xp.dtype),
        grid_spec=pltpu.PrefetchScalarGridSpec(
            num_scalar_prefetch=1,
            grid=(s // tm,),
            in_specs=[pl.BlockSpec((2, tm), lambda i, *_: (0, i), memory_space=pltpu.SMEM),
                      pl.BlockSpec((tm * TOK, LANE), lambda i, *_: (i, 0))],
            out_specs=pl.BlockSpec(memory_space=pl.ANY),
            scratch_shapes=[pltpu.VMEM((zb * TOK, LANE), xp.dtype), pltpu.SemaphoreType.DMA,
                            pltpu.SemaphoreType.DMA],
        ),
        compiler_params=_params(("arbitrary",), 32),
        name="moe_dispatch",
    )(fill, slots, xp)


def _ple(x, p, wup_ref, wgate_ref, bgate_ref):
    up = _dot(p.astype(BF16), wup_ref[...])
    zg = _dot(x.astype(BF16), wgate_ref[...]) + bgate_ref[...]
    return up * (1.0 / (1.0 + jnp.exp(-zg)))


def _ple_ln_dense_kernel(x_ref, f_ref, p_ref, wup_ref, wgate_ref, bgate_ref, g_ref, beta_ref, o_ref):
    nr = x_ref.shape[0] // ROW_SPLIT
    for r0 in range(0, x_ref.shape[0], nr):
        rows = slice(r0, r0 + nr)
        x = x_ref[rows, :]
        f = jnp.concatenate(_load_token_tiles(f_ref, (r0, nr)), 1)
        ple = _ple(x, p_ref[rows, :], wup_ref, wgate_ref, bgate_ref)
        o_ref[rows, :] = _layer_norm(DEEPNORM_ALPHA * x + f + ple, g_ref[...], beta_ref[...])


def _ple_ln_moe_kernel(slot_ref, x_ref, wt_ref, p_ref, wup_ref, wgate_ref, bgate_ref, g_ref, beta_ref, ys_ref,
                       o_ref, y1_ref, y2_ref, sem):
    tm = x_ref.shape[0]
    bufs = (y1_ref, y2_ref)

    def issue(b, c):
        for u in range(DMA_UNROLL):
            t = b * DMA_UNROLL + u
            for k in range(2):
                pltpu.make_async_copy(ys_ref.at[_tile_rows(slot_ref[k, t])], bufs[k].at[_tile_rows(t)],
                                      sem).start(priority=k)
        return c

    lax.fori_loop(0, tm // DMA_UNROLL, issue, 0)
    nr = tm // ROW_SPLIT
    ples = [_ple(x_ref[r0:r0 + nr, :], p_ref[r0:r0 + nr, :], wup_ref, wgate_ref, bgate_ref)
            for r0 in range(0, tm, nr)]
    for k in range(2):
        pltpu.make_async_copy(ys_ref.at[pl.ds(0, tm * TOK)], bufs[k], sem).wait()
    for i, r0 in enumerate(range(0, tm, nr)):
        rows = slice(r0, r0 + nr)
        wt = wt_ref[rows, :]
        f = (wt[:, 0:1] * jnp.concatenate(_load_token_tiles(y1_ref, (r0, nr)), 1)
             + wt[:, 1:2] * jnp.concatenate(_load_token_tiles(y2_ref, (r0, nr)), 1))
        o_ref[rows, :] = _layer_norm(DEEPNORM_ALPHA * x_ref[rows, :] + f + ples[i], g_ref[...], beta_ref[...])


def ple_ln(x, p, wup, wgate, bgate, g, beta, f=None, moe=None):
    s = x.shape[0]
    tm = min(512, s)
    row = lambda arr: pl.BlockSpec((tm, arr.shape[1]), lambda i, *_: (i, 0))
    full = lambda arr: pl.BlockSpec(arr.shape, lambda i, *_: (0, 0), pipeline_mode=pl.Buffered(1))
    tail = [row(p), full(wup), full(wgate), full(bgate), full(g), full(beta)]
    out_spec = pl.BlockSpec((tm, D_MODEL), lambda i, *_: (i, 0))
    out_shape = jax.ShapeDtypeStruct((s, D_MODEL), F32)
    if moe is None:
        return pl.pallas_call(
            _ple_ln_dense_kernel, out_shape=out_shape, grid=(s // tm,),
            in_specs=[row(x), pl.BlockSpec((tm * TOK, LANE), lambda i: (i, 0))] + tail, out_specs=out_spec,
            compiler_params=_params(("parallel",), 56), name="ple_ln_dense",
        )(x, f, p, wup, wgate, bgate, g, beta)
    ys, slots, wts_t = moe
    return pl.pallas_call(
        _ple_ln_moe_kernel,
        out_shape=out_shape,
        grid_spec=pltpu.PrefetchScalarGridSpec(
            num_scalar_prefetch=0,
            grid=(s // tm,),
            in_specs=[pl.BlockSpec((2, tm), lambda i: (0, i), memory_space=pltpu.SMEM), row(x), row(wts_t)] + tail
            + [pl.BlockSpec(memory_space=pl.ANY)],
            out_specs=out_spec,
            scratch_shapes=[pltpu.VMEM((tm * TOK, LANE), jnp.uint32), pltpu.VMEM((tm * TOK, LANE), jnp.uint32),
                            pltpu.SemaphoreType.DMA],
        ),
        compiler_params=_params(("arbitrary",), 56),
        name="ple_ln_moe",
    )(slots, x, wts_t, p, wup, wgate, bgate, g, beta, ys)


def _pad_cols(w, width):
    return jnp.pad(w, ((0, 0), (0, width - w.shape[1])))


def _pack_w_in(w):
    sizes = (MLA_Q_RANK, MLA_KV_RANK, MLA_ROPE, GLA_HEADS * GLA_DK, GLA_HEADS * GLA_DK, GLA_HEADS * GLA_DV,
             GLA_GATE_RANK, GLA_HEADS * GLA_DV, SWA_HEADS * SWA_HD, SWA_KV_HEADS * SWA_HD, SWA_KV_HEADS * SWA_HD)
    splits = np.cumsum(sizes)[:-1]
    q_a, kv_a, k_rope, g_q, g_k, g_v, g_lr, g_r, s_q, s_k, s_v = jnp.split(w, splits, axis=1)
    pieces = [q_a, g_v, g_r, s_q, kv_a, g_q, g_k, _pad_cols(k_rope, LANE), _pad_cols(g_lr, LANE), s_k, s_v]
    return jnp.concatenate(pieces, 1).astype(BF16)


def _pack_w_q_b(w):
    w = w.reshape(MLA_Q_RANK, MLA_HEADS, MLA_NOPE + MLA_ROPE)
    w = jnp.pad(w, ((0, 0), (0, 0), (0, MLA_QK - MLA_NOPE - MLA_ROPE)))
    return w.reshape(MLA_Q_RANK, MLA_HEADS * MLA_QK).astype(BF16)


def _token_mixer_ln(x, rope, w_in, q_gain, w_q_b, kv_gain, w_kv_b, gla_w, gla_b, gla_gain, sinks, rel_bias,
                    w_out, ln_g, ln_b):
    h = proj_in(x, _pack_w_in(w_in))
    q, k, v = mla_prep(h, *rope, q_gain.reshape(1, -1), kv_gain.reshape(1, -1), _pack_w_q_b(w_q_b),
                       w_kv_b.astype(BF16))
    a = mla_flash(q, k, v)
    gla_w_p = jnp.pad(gla_w, ((0, LANE - GLA_GATE_RANK), (0, 0))).astype(BF16)
    b = gla(h, gla_w_p, gla_b.reshape(1, -1), gla_gain.reshape(1, -1))
    c = swa(h, sinks, rel_bias)
    return out_ln(a, b, c, x, w_out.astype(BF16), ln_g.reshape(1, -1), ln_b.reshape(1, -1))


def _moe_plan(counts, tm, n_tiles):
    tiles = (counts + tm - 1) // tm
    ends = jnp.cumsum(tiles)
    base = (ends - tiles) * tm
    n_active = ends[-1:].astype(jnp.int32)
    tile_expert = jnp.searchsorted(ends, jnp.arange(n_tiles, dtype=jnp.int32), side="right")
    tile_expert = jnp.minimum(tile_expert, N_EXPERTS - 1).astype(jnp.int32)
    tile_start = jnp.arange(n_tiles, dtype=jnp.int32) * tm
    tile_valid = jnp.clip((base + counts)[tile_expert] - tile_start, 0, tm).astype(jnp.int32)
    used_rows = jnp.broadcast_to(ends[-1] * tm, counts.shape)
    fill = jnp.stack([base + counts, base + tiles * tm, used_rows]).astype(jnp.int32)
    return base.astype(jnp.int32), fill, tile_expert, tile_valid, n_active


def kernel(x, p, positions, w_in, mla_q_a_gain, mla_w_q_b, mla_kv_a_gain, mla_w_kv_b, gla_w_gate, gla_b_gate,
           gla_norm_gain, swa_sinks, rel_bias, w_out, ln1_g, ln1_b, ffn_w_gate, ffn_w_up, ffn_w_down,
           moe_router, moe_w_gate, moe_w_up, moe_w_down, ple_w_up, ple_w_gate, ple_b_gate, ln2_g, ln2_b):
    batch, s, _ = x.shape
    assert batch == 1
    xcur = x.reshape(s, D_MODEL)
    rope = rope_tables(positions)
    tm = min(FFN_ROWS, s)
    for i in range(DEPTH):
        x1, x1p = _token_mixer_ln(xcur, rope, w_in[i], mla_q_a_gain[i], mla_w_q_b[i], mla_kv_a_gain[i],
                                  mla_w_kv_b[i], gla_w_gate[i], gla_b_gate[i], gla_norm_gain[i], swa_sinks[i],
                                  rel_bias, w_out[i], ln1_g[i], ln1_b[i])
        tail = (p[i, 0], ple_w_up[i].astype(BF16), ple_w_gate[i].astype(BF16), ple_b_gate[i].reshape(1, -1),
                ln2_g[i].reshape(1, -1), ln2_b[i].reshape(1, -1))
        j = i // 2
        if i % 2 == 0:
            n_tiles = s // tm
            f = grouped_swiglu(x1p, ffn_w_gate[j][None], ffn_w_up[j][None], ffn_w_down[j][None],
                               jnp.zeros((n_tiles,), jnp.int32), jnp.full((n_tiles,), tm, jnp.int32),
                               jnp.full((1,), n_tiles, jnp.int32), tm, FFN_COLS)
            xcur = ple_ln(x1, *tail, f=f)
        else:
            n_tiles = (2 * s + N_EXPERTS * (tm - 1)) // tm
            e_idx, rank, wts, counts = route_tokens(x1, moe_router[j].T)
            base, fill, tile_expert, tile_valid, n_active = _moe_plan(counts[:, 0], tm, n_tiles)
            slots = token_slots(e_idx, rank, base)
            xs = moe_dispatch(x1p, slots, fill, n_tiles * tm)
            ys = grouped_swiglu(xs, moe_w_gate[j], moe_w_up[j], moe_w_down[j], tile_expert, tile_valid, n_active,
                                tm, FFN_COLS)
            xcur = ple_ln(x1, *tail, moe=(ys, slots, wts.T))
    return xcur.reshape(batch, s, D_MODEL)
```

```python
import functools
import math

import numpy as np
import jax
import jax.numpy as jnp
from jax import lax
from jax.experimental import pallas as pl
from jax.experimental.pallas import tpu as pltpu

F32 = jnp.float32
BF16 = jnp.bfloat16

D_MODEL = 2048
DEPTH = 2
MLA_HEADS = 8
MLA_Q_RANK = 512
MLA_KV_RANK = 256
MLA_NOPE = 128
MLA_ROPE = 64
MLA_V = 128
ROPE_THETA = 10000.0
GLA_HEADS = 4
GLA_DK = 64
GLA_DV = 128
GLA_GATE_RANK = 16
GLA_TAU = 16.0
SWA_HEADS = 8
SWA_KV_HEADS = 2
SWA_HD = 64
SWA_WINDOW = 128
SWA_BLOCK = 128
REL_BUCKETS = 32
REL_MAX_DIST = 128
D_FF = 5632
N_EXPERTS = 8
PLE_DIM = 256
LN_EPS = 1e-5
RMS_EPS = 1e-6
DEEPNORM_ALPHA = (2 * DEPTH) ** 0.25

LANE = 128
LOG2E = math.log2(math.e)
NEG_BIG = -1e30

H_QA, H_GV, H_GR, H_SQ = 0, 512, 1024, 1536
H_KVA, H_GQ, H_GK = 2048, 2304, 2560
H_KR, H_GLR, H_SK, H_SV = 2816, 2944, 3072, 3200
H_COLS = 3328

MLA_QK = 2 * LANE

GLA_L = 128
GLA_SUB = 32
GLA_NSUB = GLA_L // GLA_SUB


def _params(sem, vmem_mb):
    return pltpu.CompilerParams(dimension_semantics=sem, vmem_limit_bytes=vmem_mb * 2 ** 20)


def _dot(a, b):
    return jnp.dot(a, b, preferred_element_type=F32)


def _dot_nt(a, b):
    return lax.dot_general(a, b, (((1,), (1,)), ((), ())), preferred_element_type=F32)


def _dot_tn(a, b):
    return lax.dot_general(a, b, (((0,), (0,)), ((), ())), preferred_element_type=F32)


def _proj_in_kernel(x_ref, w_ref, o_ref, xb_ref):
    @pl.when(pl.program_id(1) == 0)
    def _():
        xb_ref[...] = x_ref[...].astype(BF16)

    o_ref[...] = _dot(xb_ref[...], w_ref[...]).astype(o_ref.dtype)


def proj_in(x, w_p):
    s = x.shape[0]
    tm = min(1024, s)
    tn = H_COLS // 2
    return pl.pallas_call(
        _proj_in_kernel,
        out_shape=jax.ShapeDtypeStruct((s, H_COLS), BF16),
        grid=(s // tm, H_COLS // tn),
        in_specs=[pl.BlockSpec((tm, D_MODEL), lambda i, j: (i, 0)),
                  pl.BlockSpec((D_MODEL, tn), lambda i, j: (0, j))],
        out_specs=pl.BlockSpec((tm, tn), lambda i, j: (i, j)),
        scratch_shapes=[pltpu.VMEM((tm, D_MODEL), BF16)],
        compiler_params=_params(("parallel", "arbitrary"), 56),
        name="proj_in",
    )(x, w_p)


def _rope_table_kernel(pos_ref, inv_ref, cos_ref, sa_ref, sb_ref):
    ang = pos_ref[...].astype(F32) * inv_ref[...]
    lane = lax.broadcasted_iota(jnp.int32, ang.shape, 1)
    half = MLA_ROPE // 2
    c, s = jnp.cos(ang), jnp.sin(ang)
    cos_ref[...] = c
    sa_ref[...] = jnp.where((lane >= half) & (lane < 2 * half), s, 0.0)
    sb_ref[...] = jnp.where(lane < half, -s, 0.0)


def rope_tables(positions):
    s = positions.shape[-1]
    half = MLA_ROPE // 2
    inv = ROPE_THETA ** (-jnp.arange(half, dtype=F32) / half)
    inv = jnp.concatenate([inv, inv, jnp.zeros((LANE - 2 * half,), F32)]).reshape(1, LANE)
    tm = min(1024, s)
    spec = pl.BlockSpec((tm, LANE), lambda i: (i, 0))
    return pl.pallas_call(
        _rope_table_kernel,
        out_shape=[jax.ShapeDtypeStruct((s, LANE), F32)] * 3,
        grid=(s // tm,),
        in_specs=[pl.BlockSpec((tm, 1), lambda i: (i, 0)), pl.BlockSpec((1, LANE), lambda i: (0, 0))],
        out_specs=[spec, spec, spec],
        compiler_params=_params(("parallel",), 32),
        name="rope_tables",
    )(positions.reshape(s, 1), inv)


def _rope(x, cos, sa, sb):
    return x * cos + pltpu.roll(x, MLA_ROPE // 2, 1) * sa + pltpu.roll(x, LANE - MLA_ROPE // 2, 1) * sb


def _mla_prep_kernel(qa_ref, kva_ref, kr_ref, cos_ref, sa_ref, sb_ref, gq_ref, gkv_ref, wq_ref, wkv_ref,
                     q_out, k_out, v_out):
    cos, sa, sb = cos_ref[...], sa_ref[...], sb_ref[...]
    qscale = (MLA_NOPE + MLA_ROPE) ** -0.5 * LOG2E

    qa = qa_ref[...].astype(F32)
    qn = qa * lax.rsqrt(jnp.mean(qa * qa, -1, keepdims=True) + RMS_EPS) * gq_ref[...]
    q = _dot(qn.astype(BF16), wq_ref[...])
    for h in range(MLA_HEADS):
        c0 = h * MLA_QK
        q_out[h, :, 0:LANE] = (q[:, c0:c0 + LANE] * qscale).astype(BF16)
        pe = _rope(q[:, c0 + LANE:c0 + 2 * LANE], cos, sa, sb)
        q_out[h, :, LANE:2 * LANE] = (pe * qscale).astype(BF16)

    kva = kva_ref[...].astype(F32)
    kvn = kva * lax.rsqrt(jnp.mean(kva * kva, -1, keepdims=True) + RMS_EPS) * gkv_ref[...]
    kv = _dot(kvn.astype(BF16), wkv_ref[...])
    kpe = _rope(kr_ref[...].astype(F32), cos, sa, sb).astype(BF16)
    for h in range(MLA_HEADS):
        c0 = h * (MLA_NOPE + MLA_V)
        k_out[h, :, 0:LANE] = kv[:, c0:c0 + MLA_NOPE].astype(BF16)
        k_out[h, :, LANE:2 * LANE] = kpe
        v_out[h, :, 0:MLA_V] = kv[:, c0 + MLA_NOPE:c0 + MLA_NOPE + MLA_V].astype(BF16)
        v_out[h, :, MLA_V:2 * MLA_V] = jnp.ones((kv.shape[0], MLA_V), BF16)


def mla_prep(h, cos, sa, sb, gq, gkv, wq_p, wkv):
    s = h.shape[0]
    tm = min(512, s)
    row = lambda width, col: pl.BlockSpec((tm, width), lambda i: (i, col // width))
    full = lambda a: pl.BlockSpec(a.shape, lambda i: (0,) * a.ndim)
    return pl.pallas_call(
        _mla_prep_kernel,
        out_shape=[jax.ShapeDtypeStruct((MLA_HEADS, s, MLA_QK), BF16),
                   jax.ShapeDtypeStruct((MLA_HEADS, s, MLA_QK), BF16),
                   jax.ShapeDtypeStruct((MLA_HEADS, s, 2 * MLA_V), BF16)],
        grid=(s // tm,),
        in_specs=[row(MLA_Q_RANK, H_QA), row(MLA_KV_RANK, H_KVA), row(LANE, H_KR),
                  row(LANE, 0), row(LANE, 0), row(LANE, 0),
                  full(gq), full(gkv), full(wq_p), full(wkv)],
        out_specs=[pl.BlockSpec((MLA_HEADS, tm, MLA_QK), lambda i: (0, i, 0)),
                   pl.BlockSpec((MLA_HEADS, tm, MLA_QK), lambda i: (0, i, 0)),
                   pl.BlockSpec((MLA_HEADS, tm, 2 * MLA_V), lambda i: (0, i, 0))],
        compiler_params=_params(("parallel",), 48),
        name="mla_prep",
    )(h, h, h, cos, sa, sb, gq, gkv, wq_p, wkv)


def _mla_flash_kernel(q_ref, k_ref, v_ref, o_ref, sa_ref, sb_ref, mxa_ref, mxb_ref, m_ref, acc_ref, *, t, nh):
    qi = pl.program_id(1)
    heads = range(nh)
    bufs = ((sa_ref, mxa_ref), (sb_ref, mxb_ref))

    def produce(b, parity, masked):
        s_ref, mx_ref = bufs[parity]
        start = pl.multiple_of(b * t, t)
        for h in heads:
            s = _dot_nt(q_ref[h], k_ref[h, pl.ds(start, t), :])
            if masked:
                rows = lax.broadcasted_iota(jnp.int32, s.shape, 0) + qi * t
                cols = lax.broadcasted_iota(jnp.int32, s.shape, 1) + b * t
                s = jnp.where(cols <= rows, s, NEG_BIG)
            s_ref[h] = s
            mx_ref[h] = jnp.broadcast_to(jnp.max(s, -1, keepdims=True), (t, LANE))

    def absorb(b, parity):
        s_ref, mx_ref = bufs[parity]
        start = pl.multiple_of(b * t, t)
        for h in heads:
            m_new = jnp.maximum(m_ref[h], mx_ref[h])
            alpha = jnp.exp2(m_ref[h] - m_new)
            p = jnp.exp2(s_ref[h] - jnp.concatenate([m_new] * (t // LANE), 1))
            acc_ref[h] = (jnp.concatenate([alpha, alpha], 1) * acc_ref[h]
                          + _dot(p.astype(BF16), v_ref[h, pl.ds(start, t), :]))
            m_ref[h] = m_new

    m_ref[...] = jnp.full(m_ref.shape, NEG_BIG, F32)
    acc_ref[...] = jnp.zeros_like(acc_ref)
    produce(0, 0, True)

    def pair(i, c):
        b = 2 * i
        produce(b + 1, 1, False)
        absorb(b, 0)
        produce(b + 2, 0, False)
        absorb(b + 1, 1)
        return c

    n_pairs = jnp.maximum(qi - 1, 0) // 2
    lax.fori_loop(0, n_pairs, pair, 0)
    done = 2 * n_pairs

    @pl.when(qi == 0)
    def _():
        absorb(0, 0)

    @pl.when((qi > 0) & (qi - done == 1))
    def _():
        produce(qi, 1, True)
        absorb(qi - 1, 0)
        absorb(qi, 1)

    @pl.when((qi > 0) & (qi - done == 2))
    def _():
        produce(qi - 1, 1, False)
        absorb(qi - 2, 0)
        produce(qi, 0, True)
        absorb(qi - 1, 1)
        absorb(qi, 0)

    for h in heads:
        acc = acc_ref[h]
        o_ref[:, h * MLA_V:(h + 1) * MLA_V] = (acc[:, :MLA_V] / acc[:, MLA_V:]).astype(o_ref.dtype)


def mla_flash(q, k, v1):
    _, s, _ = q.shape
    t = min(512, s)
    nh = 2
    return pl.pallas_call(
        functools.partial(_mla_flash_kernel, t=t, nh=nh),
        out_shape=jax.ShapeDtypeStruct((s, MLA_HEADS * MLA_V), BF16),
        grid=(MLA_HEADS // nh, s // t),
        in_specs=[pl.BlockSpec((nh, t, MLA_QK), lambda h, i: (h, i, 0)),
                  pl.BlockSpec((nh, s, MLA_QK), lambda h, i: (h, 0, 0)),
                  pl.BlockSpec((nh, s, 2 * MLA_V), lambda h, i: (h, 0, 0))],
        out_specs=pl.BlockSpec((t, nh * MLA_V), lambda h, i: (i, h)),
        scratch_shapes=[pltpu.VMEM((nh, t, t), F32), pltpu.VMEM((nh, t, t), F32),
                        pltpu.VMEM((nh, t, LANE), F32), pltpu.VMEM((nh, t, LANE), F32),
                        pltpu.VMEM((nh, t, LANE), F32), pltpu.VMEM((nh, t, 2 * MLA_V), F32)],
        compiler_params=_params(("parallel", "arbitrary"), 56),
        name="mla_flash",
    )(q, k, v1)


def _gla_masks():
    i = np.arange(GLA_L)[:, None]
    j = np.arange(GLA_L)[None, :]
    same = (i // GLA_SUB) == (j // GLA_SUB)
    mats = [j <= i, same & (j <= i), same & (j > i), j > i]
    for sub in range(GLA_NSUB - 1):
        mats.append((j >= (sub + 1) * GLA_SUB) & (j <= i))
    return np.concatenate(mats, 0).astype(np.float32)


GLA_CHUNKS_PER_STEP = 4


def _gla_kernel(q_ref, k_ref, v_ref, lr_ref, r_ref, wg_ref, bg_ref, gain_ref, mask_ref, o_ref, state_ref):
    @pl.when(pl.program_id(0) == 0)
    def _():
        state_ref[...] = jnp.zeros_like(state_ref)

    L, hk = GLA_L, GLA_HEADS * GLA_DK
    chunks = [slice(c * L, (c + 1) * L) for c in range(q_ref.shape[0] // L)]
    heads = [(slice(h * GLA_DK, (h + 1) * GLA_DK), slice(h * GLA_DV, (h + 1) * GLA_DV)) for h in range(GLA_HEADS)]

    z = _dot(lr_ref[...], wg_ref[...]) + bg_ref[...]
    g = (jnp.minimum(z, 0.0) - jnp.log(1.0 + jnp.exp(-jnp.abs(z)))) * (1.0 / GLA_TAU)
    g_hi = g.astype(BF16)
    g_lo = (g - g_hi.astype(F32)).astype(BF16)

    cums = [_dot(mask_ref[...], jnp.concatenate([g_hi[rs], g_lo[rs]], 0)) for rs in chunks]

    sub_of_row = lax.broadcasted_iota(jnp.int32, (L, hk), 0) // GLA_SUB
    prep = []
    for rs, cum in zip(chunks, cums):
        b_all, b_loc, sfx_loc, sfx_all = cum[0:L], cum[L:2 * L], cum[2 * L:3 * L], cum[3 * L:4 * L]
        q = q_ref[rs, :].astype(F32) * (GLA_DK ** -0.5)
        k = k_ref[rs, :].astype(F32)
        k_end = k * jnp.exp(sfx_loc)
        prep.append(dict(
            decay=jnp.exp(b_all[L - 1:L, :]),
            q_inter=(q * jnp.exp(b_all)).astype(BF16),
            q_diag=(q * jnp.exp(b_loc)).astype(BF16),
            k_diag=(k * jnp.exp(-b_loc)).astype(BF16),
            k_state=(k * jnp.exp(sfx_all)).astype(BF16),
            q_off=[(q * jnp.exp(cum[(4 + sub) * L:(5 + sub) * L])).astype(BF16) for sub in range(GLA_NSUB - 1)],
            k_off=[jnp.where(sub_of_row == sub, k_end, 0.0).astype(BF16) for sub in range(GLA_NSUB - 1)]))

    row = lax.broadcasted_iota(jnp.int32, (L, L), 0)
    col = lax.broadcasted_iota(jnp.int32, (L, L), 1)
    diag_ok = ((row // GLA_SUB) == (col // GLA_SUB)) & (col <= row)
    off_ok = (row // GLA_SUB) > (col // GLA_SUB)
    local = []
    for rs, pr in zip(chunks, prep):
        per_head = []
        for ks, vs in heads:
            v_h = v_ref[rs, vs]
            a = jnp.where(diag_ok, _dot_nt(pr["q_diag"][:, ks], pr["k_diag"][:, ks]), 0.0)
            qo = jnp.concatenate([t[:, ks] for t in pr["q_off"]], 1)
            ko = jnp.concatenate([t[:, ks] for t in pr["k_off"]], 1)
            a = a + jnp.where(off_ok, _dot_nt(qo, ko), 0.0)
            per_head.append((_dot(a.astype(BF16), v_h), _dot_tn(v_h, pr["k_state"][:, ks])))
        local.append(per_head)

    states = [state_ref[h] for h in range(GLA_HEADS)]
    for rs, pr, per_head in zip(chunks, prep, local):
        r = r_ref[rs, :].astype(F32)
        gate = r * (1.0 / (1.0 + jnp.exp(-r)))
        for h, (ks, vs) in enumerate(heads):
            o_intra, vk = per_head[h]
            o = _dot_nt(pr["q_inter"][:, ks], states[h].astype(BF16)) + o_intra
            states[h] = states[h] * pr["decay"][:, ks] + vk
            o = o * lax.rsqrt(jnp.mean(o * o, -1, keepdims=True) + RMS_EPS) * gain_ref[:, vs]
            o_ref[rs, vs] = (o * gate[:, vs]).astype(o_ref.dtype)
    for h in range(GLA_HEADS):
        state_ref[h] = states[h]


def gla(h, wg_p, bg, gain):
    s = h.shape[0]
    rows = min(GLA_CHUNKS_PER_STEP * GLA_L, s)
    masks = jnp.asarray(np.tile(_gla_masks(), (1, 2)), BF16)
    row = lambda width, col: pl.BlockSpec((rows, width), lambda i: (i, col // width))
    full = lambda a: pl.BlockSpec(a.shape, lambda i: (0,) * a.ndim)
    hk, hv = GLA_HEADS * GLA_DK, GLA_HEADS * GLA_DV
    return pl.pallas_call(
        _gla_kernel,
        out_shape=jax.ShapeDtypeStruct((s, hv), BF16),
        grid=(s // rows,),
        in_specs=[row(hk, H_GQ), row(hk, H_GK), row(hv, H_GV), row(LANE, H_GLR), row(hv, H_GR),
                  full(wg_p), full(bg), full(gain), full(masks)],
        out_specs=pl.BlockSpec((rows, hv), lambda i: (i, 0)),
        scratch_shapes=[pltpu.VMEM((GLA_HEADS, GLA_DV, GLA_DK), F32)],
        compiler_params=_params(("arbitrary",), 32),
        name="gla",
    )(h, h, h, h, h, wg_p, bg, gain, masks)


def _t5_bucket_table():
    L = SWA_BLOCK
    dist = np.arange(L)[:, None] + L - np.arange(2 * L)[None, :]
    d = np.clip(dist, 0, None)
    max_exact = REL_BUCKETS // 2
    df = np.maximum(d, 1).astype(np.float32)
    large = max_exact + (np.log(df / np.float32(max_exact)) / np.float32(math.log(REL_MAX_DIST / max_exact))
                         * np.float32(REL_BUCKETS - max_exact)).astype(np.int32)
    large = np.minimum(large, REL_BUCKETS - 1)
    bucket = np.where(d < max_exact, d, large)
    in_window = (dist >= 0) & (dist < SWA_WINDOW)
    return np.where(in_window, bucket, -1).astype(np.int32)


def _swa_kernel(relb_ref, sink_ref, q_ref, kp_ref, kc_ref, vp_ref, vc_ref, bucket_ref, o_ref, bias_ref):
    i = pl.program_id(0)
    L = SWA_BLOCK

    @pl.when(i == 0)
    def _():
        bucket = bucket_ref[...]
        for h in range(SWA_HEADS):
            acc = jnp.full(bucket.shape, NEG_BIG, F32)
            for b in range(REL_BUCKETS):
                acc = jnp.where(bucket == b, relb_ref[b, h], acc)
            bias_ref[h] = acc

    kcat = jnp.concatenate([kp_ref[...], kc_ref[...]], 0)
    vcat = jnp.concatenate([vp_ref[...], vc_ref[...]], 0)
    q = q_ref[...]
    col = lax.broadcasted_iota(jnp.int32, (L, 2 * L), 1)
    real_key = (col >= L) | (i > 0)
    g = SWA_HEADS // SWA_KV_HEADS
    scale = SWA_HD ** -0.5
    for h in range(SWA_HEADS):
        kv = h // g
        hs = slice(h * SWA_HD, (h + 1) * SWA_HD)
        kvs = slice(kv * SWA_HD, (kv + 1) * SWA_HD)
        s = _dot_nt(q[:, hs], kcat[:, kvs]) * scale + bias_ref[h]
        s = jnp.where(real_key, s, NEG_BIG)
        sink = sink_ref[h]
        m = jnp.maximum(jnp.max(s, -1, keepdims=True), sink)
        p = jnp.exp(s - m)
        denom = jnp.sum(p, -1, keepdims=True) + jnp.exp(sink - m)
        o = _dot((p / denom).astype(BF16), vcat[:, kvs])
        o_ref[:, hs] = o.astype(o_ref.dtype)


def swa(h, sinks, rel_bias):
    s = h.shape[0]
    L = SWA_BLOCK
    bucket = jnp.asarray(_t5_bucket_table())
    kvw = SWA_KV_HEADS * SWA_HD
    hw = SWA_HEADS * SWA_HD
    cur = lambda width, col: pl.BlockSpec((L, width), lambda i, *_: (i, col // width))
    prev = lambda width, col: pl.BlockSpec((L, width), lambda i, *_: (jnp.maximum(i - 1, 0), col // width))
    return pl.pallas_call(
        _swa_kernel,
        out_shape=jax.ShapeDtypeStruct((s, hw), BF16),
        grid_spec=pltpu.PrefetchScalarGridSpec(
            num_scalar_prefetch=2,
            grid=(s // L,),
            in_specs=[cur(hw, H_SQ), prev(kvw, H_SK), cur(kvw, H_SK), prev(kvw, H_SV), cur(kvw, H_SV),
                      pl.BlockSpec(bucket.shape, lambda i, *_: (0, 0))],
            out_specs=pl.BlockSpec((L, hw), lambda i, *_: (i, 0)),
            scratch_shapes=[pltpu.VMEM((SWA_HEADS, L, 2 * L), F32)],
        ),
        compiler_params=_params(("arbitrary",), 32),
        name="swa",
    )(rel_bias, sinks, h, h, h, h, h, bucket)


def _layer_norm(y, g, b):
    mu = jnp.mean(y, -1, keepdims=True)
    yc = y - mu
    var = jnp.mean(yc * yc, -1, keepdims=True)
    return yc * lax.rsqrt(var + LN_EPS) * g + b


HALF = D_MODEL // 2
HIGH16 = 0xFFFF0000


SUBLANE = 8
TOK = HALF // LANE


def _store_token_tiles(ref, x):
    tm = x.shape[0]
    lo = lax.bitcast_convert_type(x[:, :HALF].astype(BF16).astype(F32), jnp.uint32) >> 16
    hi = lax.bitcast_convert_type(x[:, HALF:].astype(BF16).astype(F32), jnp.uint32) & jnp.uint32(HIGH16)
    packed = lo | hi
    for s in range(TOK):
        ref[pl.ds(s, tm, stride=TOK), :] = packed[:, s * LANE:(s + 1) * LANE]


def _load_token_tiles(ref, rows=None):
    first, tm = rows if rows is not None else (0, ref.shape[0] // TOK)
    lo, hi = [], []
    for s in range(TOK):
        w = ref[pl.ds(first * TOK + s, tm, stride=TOK), :]
        lo.append(lax.bitcast_convert_type(w << 16, F32))
        hi.append(lax.bitcast_convert_type(w & jnp.uint32(HIGH16), F32))
    return jnp.concatenate(lo, 1), jnp.concatenate(hi, 1)


ROW_SPLIT = 2


def _out_ln_kernel(a_ref, b_ref, c_ref, x_ref, w_ref, g_ref, beta_ref, o_ref, op_ref):
    na, nb = a_ref.shape[1], b_ref.shape[1]
    nr = x_ref.shape[0] // ROW_SPLIT
    for r0 in range(0, x_ref.shape[0], nr):
        rows = slice(r0, r0 + nr)
        m = _dot(a_ref[rows, :], w_ref[0:na, :])
        m = m + _dot(b_ref[rows, :], w_ref[na:na + nb, :])
        m = m + _dot(c_ref[rows, :], w_ref[na + nb:, :])
        y = _layer_norm(DEEPNORM_ALPHA * x_ref[rows, :] + m, g_ref[...], beta_ref[...])
        o_ref[rows, :] = y
        _store_token_tiles(op_ref.at[pl.ds(r0 * TOK, nr * TOK)], y)


def out_ln(a, b, c, x, w, g, beta):
    s = x.shape[0]
    tm = min(512, s)
    row = lambda arr: pl.BlockSpec((tm, arr.shape[1]), lambda i: (i, 0))
    full = lambda arr: pl.BlockSpec(arr.shape, lambda i: (0, 0), pipeline_mode=pl.Buffered(1))
    return pl.pallas_call(
        _out_ln_kernel,
        out_shape=[jax.ShapeDtypeStruct((s, D_MODEL), F32), jax.ShapeDtypeStruct((s * TOK, LANE), jnp.uint32)],
        grid=(s // tm,),
        in_specs=[row(a), row(b), row(c), row(x), full(w), full(g), full(beta)],
        out_specs=[pl.BlockSpec((tm, D_MODEL), lambda i: (i, 0)), pl.BlockSpec((tm * TOK, LANE), lambda i: (i, 0))],
        compiler_params=_params(("parallel",), 56),
        name="out_ln",
    )(a, b, c, x, w, g, beta)


FFN_ROWS = 1024
FFN_COLS = 512
FFN_OUT_CHUNK = 512
FFN_SUB = 256


def _ffn_kernel(te_ref, tv_ref, na_ref, xp_ref, wg_ref, wu_ref, wd_ref, o_ref, xb_ref, acc_ref):
    i, j = pl.program_id(0), pl.program_id(1)
    tm = xb_ref.shape[0]
    valid = tv_ref[i]

    @pl.when(j == 0)
    def _():
        acc_ref[...] = jnp.zeros_like(acc_ref)

    @pl.when((j == 0) & (valid > 0))
    def _():
        lo, hi = _load_token_tiles(xp_ref)
        xb_ref[:, :HALF] = lo.astype(BF16)
        xb_ref[:, HALF:] = hi.astype(BF16)

    def rows_step(r0, nr):
        xb = xb_ref[r0:r0 + nr, :]
        gate = _dot(xb, wg_ref[...].astype(BF16))
        up = _dot(xb, wu_ref[...].astype(BF16))
        hmid = (gate * (1.0 / (1.0 + jnp.exp(-gate))) * up).astype(BF16)
        for c in range(0, D_MODEL, FFN_OUT_CHUNK):
            cs = slice(c, c + FFN_OUT_CHUNK)
            acc_ref[r0:r0 + nr, cs] += _dot(hmid, wd_ref[:, cs].astype(BF16))

    nearly_full = valid > tm - FFN_SUB

    @pl.when(nearly_full)
    def _():
        rows_step(0, tm)

    for r0 in range(0, tm - FFN_SUB, FFN_SUB):
        @pl.when(jnp.logical_not(nearly_full) & (valid > r0))
        def _():
            rows_step(r0, FFN_SUB)

    @pl.when(j == pl.num_programs(1) - 1)
    def _():
        _store_token_tiles(o_ref, acc_ref[...])


def grouped_swiglu(xp, wg, wu, wd, tile_expert, tile_valid, n_active, tm, tf):
    n_tiles, nf = xp.shape[0] // (tm * TOK), D_FF // tf

    def tile(i, na):
        return jnp.minimum(i, na[0] - 1)

    def fcol(i, j, na):
        return jnp.where(i < na[0], j, nf - 1)

    return pl.pallas_call(
        _ffn_kernel,
        out_shape=jax.ShapeDtypeStruct(xp.shape, jnp.uint32),
        grid_spec=pltpu.PrefetchScalarGridSpec(
            num_scalar_prefetch=3,
            grid=(n_tiles, nf),
            in_specs=[pl.BlockSpec((tm * TOK, LANE), lambda i, j, te, tv, na: (tile(i, na), 0)),
                      pl.BlockSpec((None, D_MODEL, tf),
                                   lambda i, j, te, tv, na: (te[tile(i, na)], 0, fcol(i, j, na))),
                      pl.BlockSpec((None, D_MODEL, tf),
                                   lambda i, j, te, tv, na: (te[tile(i, na)], 0, fcol(i, j, na))),
                      pl.BlockSpec((None, tf, D_MODEL),
                                   lambda i, j, te, tv, na: (te[tile(i, na)], fcol(i, j, na), 0))],
            out_specs=pl.BlockSpec((tm * TOK, LANE), lambda i, j, te, tv, na: (i, 0)),
            scratch_shapes=[pltpu.VMEM((tm, D_MODEL), BF16), pltpu.VMEM((tm, D_MODEL), F32)],
        ),
        compiler_params=_params(("arbitrary", "arbitrary"), 60),
        name="grouped_swiglu",
    )(tile_expert, tile_valid, n_active, xp, wg, wu, wd)


def _router_kernel(x_ref, w_ref, e_ref, r_ref, wt_ref, cnt_ref, run_ref):
    i = pl.program_id(0)
    tm = x_ref.shape[0]
    ne = N_EXPERTS

    @pl.when(i == 0)
    def _():
        run_ref[...] = jnp.zeros_like(run_ref)

    x = x_ref[...]
    x_hi = x.astype(BF16)
    x_lo = (x - x_hi.astype(F32)).astype(BF16)
    w = w_ref[...]
    w_hi = w.astype(BF16).astype(F32)
    w_lo = w - w_hi
    both = _dot_nt(jnp.concatenate([w_hi, w_lo], 0).astype(BF16), x_hi)
    cross = _dot_nt(jnp.concatenate([w_hi, jnp.zeros_like(w_hi)], 0).astype(BF16), x_lo)
    logits = both[0:ne] + both[ne:2 * ne] + cross[0:ne]

    eidx = lax.broadcasted_iota(jnp.int32, logits.shape, 0).astype(F32)
    v1 = jnp.max(logits, 0, keepdims=True)
    i1 = jnp.min(jnp.where(logits == v1, eidx, float(ne)), 0, keepdims=True)
    rest = jnp.where(eidx == i1, -jnp.inf, logits)
    v2 = jnp.max(rest, 0, keepdims=True)
    i2 = jnp.min(jnp.where(rest == v2, eidx, float(ne)), 0, keepdims=True)
    t = jnp.exp(v2 - v1)
    w1 = 1.0 / (1.0 + t)
    wt_ref[0:1, :] = w1
    wt_ref[1:2, :] = t * w1
    e_ref[0:1, :] = i1.astype(jnp.int32)
    e_ref[1:2, :] = i2.astype(jnp.int32)

    sel1, sel2 = eidx == i1, eidx == i2
    sel = jnp.where(sel1, 1.0, 0.0) + jnp.where(sel2, 1.0, 0.0)
    before = (lax.broadcasted_iota(jnp.int32, (tm, tm), 0) < lax.broadcasted_iota(jnp.int32, (tm, tm), 1))
    sel16 = jnp.concatenate([sel, jnp.zeros_like(sel)], 0).astype(BF16)
    prefix = _dot(sel16, jnp.where(before, 1.0, 0.0).astype(BF16))[0:ne]
    rank = prefix + run_ref[:, 0:1]
    r_ref[0:1, :] = jnp.sum(jnp.where(sel1, rank, 0.0), 0, keepdims=True).astype(jnp.int32)
    r_ref[1:2, :] = jnp.sum(jnp.where(sel2, rank, 0.0), 0, keepdims=True).astype(jnp.int32)
    run_ref[...] = run_ref[...] + jnp.sum(sel, 1, keepdims=True)
    cnt_ref[...] = run_ref[...].astype(jnp.int32)


def route_tokens(x, w_router_t):
    s = x.shape[0]
    tm = min(512, s)
    pair = pl.BlockSpec((2, tm), lambda i: (0, i))
    return pl.pallas_call(
        _router_kernel,
        out_shape=[jax.ShapeDtypeStruct((2, s), jnp.int32), jax.ShapeDtypeStruct((2, s), jnp.int32),
                   jax.ShapeDtypeStruct((2, s), F32), jax.ShapeDtypeStruct((N_EXPERTS, LANE), jnp.int32)],
        grid=(s // tm,),
        in_specs=[pl.BlockSpec((tm, D_MODEL), lambda i: (i, 0)),
                  pl.BlockSpec((N_EXPERTS, D_MODEL), lambda i: (0, 0))],
        out_specs=[pair, pair, pair, pl.BlockSpec((N_EXPERTS, LANE), lambda i: (0, 0))],
        scratch_shapes=[pltpu.VMEM((N_EXPERTS, LANE), F32)],
        compiler_params=_params(("arbitrary",), 32),
        name="moe_router",
    )(x, w_router_t)


def _slot_kernel(base_ref, e_ref, r_ref, s_ref):
    e = e_ref[...]
    slot = r_ref[...]
    for k in range(N_EXPERTS):
        slot = slot + jnp.where(e == k, base_ref[k], 0)
    s_ref[...] = slot


def token_slots(e_idx, rank, base):
    whole = pl.BlockSpec(e_idx.shape, lambda i, *_: (0, 0))
    return pl.pallas_call(
        _slot_kernel,
        out_shape=jax.ShapeDtypeStruct(e_idx.shape, jnp.int32),
        grid_spec=pltpu.PrefetchScalarGridSpec(num_scalar_prefetch=1, grid=(1,), in_specs=[whole, whole],
                                               out_specs=whole),
        compiler_params=_params(("arbitrary",), 32),
        name="token_slots",
    )(base, e_idx, rank)


DMA_UNROLL = 8


def _tile_rows(index):
    return pl.ds(pl.multiple_of(index * TOK, TOK), TOK)


def _dispatch_kernel(fill_ref, slot_ref, x_ref, xs_ref, zero_ref, sem, zsem):
    i = pl.program_id(0)
    tm = x_ref.shape[0] // TOK

    def issue(b, c):
        for u in range(DMA_UNROLL):
            t = b * DMA_UNROLL + u
            for k in range(2):
                pltpu.make_async_copy(x_ref.at[_tile_rows(t)], xs_ref.at[_tile_rows(slot_ref[k, t])],
                                      sem).start(priority=k)
        return c

    lax.fori_loop(0, tm // DMA_UNROLL, issue, 0)
    for k in range(2):
        pltpu.make_async_copy(x_ref, xs_ref.at[pl.ds(0, tm * TOK)], sem).wait()

    @pl.when(i == pl.num_programs(0) - 1)
    def _():
        zero_ref[...] = jnp.zeros_like(zero_ref)

        def zero_copy(slot):
            return pltpu.make_async_copy(zero_ref.at[pl.ds(0, TOK)], xs_ref.at[_tile_rows(slot)], zsem)

        for e in range(N_EXPERTS):
            lo, hi = fill_ref[0, e], fill_ref[1, e]

            def zissue(slot, c):
                zero_copy(slot).start()
                return c

            def zdrain(slot, c):
                zero_copy(slot).wait()
                return c

            lax.fori_loop(lo, hi, zissue, 0)
            lax.fori_loop(lo, hi, zdrain, 0)

        zrows = zero_ref.shape[0]
        zb = zrows // TOK

        def block_copy(b):
            return pltpu.make_async_copy(zero_ref, xs_ref.at[pl.ds(pl.multiple_of(b * zrows, zrows), zrows)], zsem)

        def bissue(b, c):
            block_copy(b).start()
            return c

        def bdrain(b, c):
            block_copy(b).wait()
            return c

        first, last = fill_ref[2, 0] // zb, xs_ref.shape[0] // zrows
        lax.fori_loop(first, last, bissue, 0)
        lax.fori_loop(first, last, bdrain, 0)


def moe_dispatch(xp, slots, fill, rows):
    s = xp.shape[0] // TOK
    tm = min(512, s)
    zb = min(256, s)
    return pl.pallas_call(
        _dispatch_kernel,
        out_shape=jax.ShapeDtypeStruct((rows * TOK, LANE), xp.dtype),
        grid_spec=pltpu.PrefetchScalarGridSpec(
            num_scalar_prefetch=1,
            grid=(s // tm,),
            in_specs=[pl.BlockSpec((2, tm), lambda i, *_: (0, i), memory_space=pltpu.SMEM),
                      pl.BlockSpec((tm * TOK, LANE), lambda i, *_: (i, 0))],
            out_specs=pl.BlockSpec(memory_space=pl.ANY),
            scratch_shapes=[pltpu.VMEM((zb * TOK, LANE), xp.dtype), pltpu.SemaphoreType.DMA,
                            pltpu.SemaphoreType.DMA],
        ),
        compiler_params=_params(("arbitrary",), 32),
        name="moe_dispatch",
    )(fill, slots, xp)


def _ple(x, p, wup_ref, wgate_ref, bgate_ref):
    up = _dot(p.astype(BF16), wup_ref[...])
    zg = _dot(x.astype(BF16), wgate_ref[...]) + bgate_ref[...]
    return up * (1.0 / (1.0 + jnp.exp(-zg)))


def _ple_ln_dense_kernel(x_ref, f_ref, p_ref, wup_ref, wgate_ref, bgate_ref, g_ref, beta_ref, o_ref):
    nr = x_ref.shape[0] // ROW_SPLIT
    for r0 in range(0, x_ref.shape[0], nr):
        rows = slice(r0, r0 + nr)
        x = x_ref[rows, :]
        f = jnp.concatenate(_load_token_tiles(f_ref, (r0, nr)), 1)
        ple = _ple(x, p_ref[rows, :], wup_ref, wgate_ref, bgate_ref)
        o_ref[rows, :] = _layer_norm(DEEPNORM_ALPHA * x + f + ple, g_ref[...], beta_ref[...])


def _ple_ln_moe_kernel(slot_ref, x_ref, wt_ref, p_ref, wup_ref, wgate_ref, bgate_ref, g_ref, beta_ref, ys_ref,
                       o_ref, y1_ref, y2_ref, sem):
    tm = x_ref.shape[0]
    bufs = (y1_ref, y2_ref)

    def issue(b, c):
        for u in range(DMA_UNROLL):
            t = b * DMA_UNROLL + u
            for k in range(2):
                pltpu.make_async_copy(ys_ref.at[_tile_rows(slot_ref[k, t])], bufs[k].at[_tile_rows(t)],
                                      sem).start(priority=k)
        return c

    lax.fori_loop(0, tm // DMA_UNROLL, issue, 0)
    nr = tm // ROW_SPLIT
    ples = [_ple(x_ref[r0:r0 + nr, :], p_ref[r0:r0 + nr, :], wup_ref, wgate_ref, bgate_ref)
            for r0 in range(0, tm, nr)]
    for k in range(2):
        pltpu.make_async_copy(ys_ref.at[pl.ds(0, tm * TOK)], bufs[k], sem).wait()
    for i, r0 in enumerate(range(0, tm, nr)):
        rows = slice(r0, r0 + nr)
        wt = wt_ref[rows, :]
        f = (wt[:, 0:1] * jnp.concatenate(_load_token_tiles(y1_ref, (r0, nr)), 1)
             + wt[:, 1:2] * jnp.concatenate(_load_token_tiles(y2_ref, (r0, nr)), 1))
        o_ref[rows, :] = _layer_norm(DEEPNORM_ALPHA * x_ref[rows, :] + f + ples[i], g_ref[...], beta_ref[...])


def ple_ln(x, p, wup, wgate, bgate, g, beta, f=None, moe=None):
    s = x.shape[0]
    tm = min(512, s)
    row = lambda arr: pl.BlockSpec((tm, arr.shape[1]), lambda i, *_: (i, 0))
    full = lambda arr: pl.BlockSpec(arr.shape, lambda i, *_: (0, 0), pipeline_mode=pl.Buffered(1))
    tail = [row(p), full(wup), full(wgate), full(bgate), full(g), full(beta)]
    out_spec = pl.BlockSpec((tm, D_MODEL), lambda i, *_: (i, 0))
    out_shape = jax.ShapeDtypeStruct((s, D_MODEL), F32)
    if moe is None:
        return pl.pallas_call(
            _ple_ln_dense_kernel, out_shape=out_shape, grid=(s // tm,),
            in_specs=[row(x), pl.BlockSpec((tm * TOK, LANE), lambda i: (i, 0))] + tail, out_specs=out_spec,
            compiler_params=_params(("parallel",), 56), name="ple_ln_dense",
        )(x, f, p, wup, wgate, bgate, g, beta)
    ys, slots, wts_t = moe
    return pl.pallas_call(
        _ple_ln_moe_kernel,
        out_shape=out_shape,
        grid_spec=pltpu.PrefetchScalarGridSpec(
            num_scalar_prefetch=0,
            grid=(s // tm,),
            in_specs=[pl.BlockSpec((2, tm), lambda i: (0, i), memory_space=pltpu.SMEM), row(x), row(wts_t)] + tail
            + [pl.BlockSpec(memory_space=pl.ANY)],
            out_specs=out_spec,
            scratch_shapes=[pltpu.VMEM((tm * TOK, LANE), jnp.uint32), pltpu.VMEM((tm * TOK, LANE), jnp.uint32),
                            pltpu.SemaphoreType.DMA],
        ),
        compiler_params=_params(("arbitrary",), 56),
        name="ple_ln_moe",
    )(slots, x, wts_t, p, wup, wgate, bgate, g, beta, ys)


def _pad_cols(w, width):
    return jnp.pad(w, ((0, 0), (0, width - w.shape[1])))


def _pack_w_in(w):
    sizes = (MLA_Q_RANK, MLA_KV_RANK, MLA_ROPE, GLA_HEADS * GLA_DK, GLA_HEADS * GLA_DK, GLA_HEADS * GLA_DV,
             GLA_GATE_RANK, GLA_HEADS * GLA_DV, SWA_HEADS * SWA_HD, SWA_KV_HEADS * SWA_HD, SWA_KV_HEADS * SWA_HD)
    splits = np.cumsum(sizes)[:-1]
    q_a, kv_a, k_rope, g_q, g_k, g_v, g_lr, g_r, s_q, s_k, s_v = jnp.split(w, splits, axis=1)
    pieces = [q_a, g_v, g_r, s_q, kv_a, g_q, g_k, _pad_cols(k_rope, LANE), _pad_cols(g_lr, LANE), s_k, s_v]
    return jnp.concatenate(pieces, 1).astype(BF16)


def _pack_w_q_b(w):
    w = w.reshape(MLA_Q_RANK, MLA_HEADS, MLA_NOPE + MLA_ROPE)
    w = jnp.pad(w, ((0, 0), (0, 0), (0, MLA_QK - MLA_NOPE - MLA_ROPE)))
    return w.reshape(MLA_Q_RANK, MLA_HEADS * MLA_QK).astype(BF16)


def _token_mixer_ln(x, rope, w_in, q_gain, w_q_b, kv_gain, w_kv_b, gla_w, gla_b, gla_gain, sinks, rel_bias,
                    w_out, ln_g, ln_b):
    h = proj_in(x, _pack_w_in(w_in))
    q, k, v = mla_prep(h, *rope, q_gain.reshape(1, -1), kv_gain.reshape(1, -1), _pack_w_q_b(w_q_b),
                       w_kv_b.astype(BF16))
    a = mla_flash(q, k, v)
    gla_w_p = jnp.pad(gla_w, ((0, LANE - GLA_GATE_RANK), (0, 0))).astype(BF16)
    b = gla(h, gla_w_p, gla_b.reshape(1, -1), gla_gain.reshape(1, -1))
    c = swa(h, sinks, rel_bias)
    return out_ln(a, b, c, x, w_out.astype(BF16), ln_g.reshape(1, -1), ln_b.reshape(1, -1))


def _moe_plan(counts, tm, n_tiles):
    tiles = (counts + tm - 1) // tm
    ends = jnp.cumsum(tiles)
    base = (ends - tiles) * tm
    n_active = ends[-1:].astype(jnp.int32)
    tile_expert = jnp.searchsorted(ends, jnp.arange(n_tiles, dtype=jnp.int32), side="right")
    tile_expert = jnp.minimum(tile_expert, N_EXPERTS - 1).astype(jnp.int32)
    tile_start = jnp.arange(n_tiles, dtype=jnp.int32) * tm
    tile_valid = jnp.clip((base + counts)[tile_expert] - tile_start, 0, tm).astype(jnp.int32)
    used_rows = jnp.broadcast_to(ends[-1] * tm, counts.shape)
    fill = jnp.stack([base + counts, base + tiles * tm, used_rows]).astype(jnp.int32)
    return base.astype(jnp.int32), fill, tile_expert, tile_valid, n_active


def kernel(x, p, positions, w_in, mla_q_a_gain, mla_w_q_b, mla_kv_a_gain, mla_w_kv_b, gla_w_gate, gla_b_gate,
           gla_norm_gain, swa_sinks, rel_bias, w_out, ln1_g, ln1_b, ffn_w_gate, ffn_w_up, ffn_w_down,
           moe_router, moe_w_gate, moe_w_up, moe_w_down, ple_w_up, ple_w_gate, ple_b_gate, ln2_g, ln2_b):
    batch, s, _ = x.shape
    assert batch == 1
    xcur = x.reshape(s, D_MODEL)
    rope = rope_tables(positions)
    tm = min(FFN_ROWS, s)
    for i in range(DEPTH):
        x1, x1p = _token_mixer_ln(xcur, rope, w_in[i], mla_q_a_gain[i], mla_w_q_b[i], mla_kv_a_gain[i],
                                  mla_w_kv_b[i], gla_w_gate[i], gla_b_gate[i], gla_norm_gain[i], swa_sinks[i],
                                  rel_bias, w_out[i], ln1_g[i], ln1_b[i])
        tail = (p[i, 0], ple_w_up[i].astype(BF16), ple_w_gate[i].astype(BF16), ple_b_gate[i].reshape(1, -1),
                ln2_g[i].reshape(1, -1), ln2_b[i].reshape(1, -1))
        j = i // 2
        if i % 2 == 0:
            n_tiles = s // tm
            f = grouped_swiglu(x1p, ffn_w_gate[j][None], ffn_w_up[j][None], ffn_w_down[j][None],
                               jnp.zeros((n_tiles,), jnp.int32), jnp.full((n_tiles,), tm, jnp.int32),
                               jnp.full((1,), n_tiles, jnp.int32), tm, FFN_COLS)
            xcur = ple_ln(x1, *tail, f=f)
        else:
            n_tiles = (2 * s + N_EXPERTS * (tm - 1)) // tm
            e_idx, rank, wts, counts = route_tokens(x1, moe_router[j].T)
            base, fill, tile_expert, tile_valid, n_active = _moe_plan(counts[:, 0], tm, n_tiles)
            slots = token_slots(e_idx, rank, base)
            xs = moe_dispatch(x1p, slots, fill, n_tiles * tm)
            ys = grouped_swiglu(xs, moe_w_gate[j], moe_w_up[j], moe_w_down[j], tile_expert, tile_valid, n_active,
                                tm, FFN_COLS)
            xcur = ple_ln(x1, *tail, moe=(ys, slots, wts.T))
    return xcur.reshape(batch, s, D_MODEL)
```

```python
import functools
import math

import numpy as np
import jax
import jax.numpy as jnp
from jax import lax
from jax.experimental import pallas as pl
from jax.experimental.pallas import tpu as pltpu

F32 = jnp.float32
BF16 = jnp.bfloat16

D_MODEL = 2048
DEPTH = 2
MLA_HEADS = 8
MLA_Q_RANK = 512
MLA_KV_RANK = 256
MLA_NOPE = 128
MLA_ROPE = 64
MLA_V = 128
ROPE_THETA = 10000.0
GLA_HEADS = 4
GLA_DK = 64
GLA_DV = 128
GLA_GATE_RANK = 16
GLA_TAU = 16.0
SWA_HEADS = 8
SWA_KV_HEADS = 2
SWA_HD = 64
SWA_WINDOW = 128
SWA_BLOCK = 128
REL_BUCKETS = 32
REL_MAX_DIST = 128
D_FF = 5632
N_EXPERTS = 8
PLE_DIM = 256
LN_EPS = 1e-5
RMS_EPS = 1e-6
DEEPNORM_ALPHA = (2 * DEPTH) ** 0.25

LANE = 128
LOG2E = math.log2(math.e)
NEG_BIG = -1e30

H_QA, H_GV, H_GR, H_SQ = 0, 512, 1024, 1536
H_KVA, H_GQ, H_GK = 2048, 2304, 2560
H_KR, H_GLR, H_SK, H_SV = 2816, 2944, 3072, 3200
H_COLS = 3328

MLA_QK = 2 * LANE

GLA_L = 128
GLA_SUB = 32
GLA_NSUB = GLA_L // GLA_SUB


def _params(sem, vmem_mb):
    return pltpu.CompilerParams(dimension_semantics=sem, vmem_limit_bytes=vmem_mb * 2 ** 20)


def _dot(a, b):
    return jnp.dot(a, b, preferred_element_type=F32)


def _dot_nt(a, b):
    return lax.dot_general(a, b, (((1,), (1,)), ((), ())), preferred_element_type=F32)


def _dot_tn(a, b):
    return lax.dot_general(a, b, (((0,), (0,)), ((), ())), preferred_element_type=F32)


def _proj_in_kernel(x_ref, w_ref, o_ref, xb_ref):
    @pl.when(pl.program_id(1) == 0)
    def _():
        xb_ref[...] = x_ref[...].astype(BF16)

    o_ref[...] = _dot(xb_ref[...], w_ref[...]).astype(o_ref.dtype)


def proj_in(x, w_p):
    s = x.shape[0]
    tm = min(1024, s)
    tn = H_COLS // 2
    return pl.pallas_call(
        _proj_in_kernel,
        out_shape=jax.ShapeDtypeStruct((s, H_COLS), BF16),
        grid=(s // tm, H_COLS // tn),
        in_specs=[pl.BlockSpec((tm, D_MODEL), lambda i, j: (i, 0)),
                  pl.BlockSpec((D_MODEL, tn), lambda i, j: (0, j))],
        out_specs=pl.BlockSpec((tm, tn), lambda i, j: (i, j)),
        scratch_shapes=[pltpu.VMEM((tm, D_MODEL), BF16)],
        compiler_params=_params(("parallel", "arbitrary"), 56),
        name="proj_in",
    )(x, w_p)


def _rope_table_kernel(pos_ref, inv_ref, cos_ref, sa_ref, sb_ref):
    ang = pos_ref[...].astype(F32) * inv_ref[...]
    lane = lax.broadcasted_iota(jnp.int32, ang.shape, 1)
    half = MLA_ROPE // 2
    c, s = jnp.cos(ang), jnp.sin(ang)
    cos_ref[...] = c
    sa_ref[...] = jnp.where((lane >= half) & (lane < 2 * half), s, 0.0)
    sb_ref[...] = jnp.where(lane < half, -s, 0.0)


def rope_tables(positions):
    s = positions.shape[-1]
    half = MLA_ROPE // 2
    inv = ROPE_THETA ** (-jnp.arange(half, dtype=F32) / half)
    inv = jnp.concatenate([inv, inv, jnp.zeros((LANE - 2 * half,), F32)]).reshape(1, LANE)
    tm = min(1024, s)
    spec = pl.BlockSpec((tm, LANE), lambda i: (i, 0))
    return pl.pallas_call(
        _rope_table_kernel,
        out_shape=[jax.ShapeDtypeStruct((s, LANE), F32)] * 3,
        grid=(s // tm,),
        in_specs=[pl.BlockSpec((tm, 1), lambda i: (i, 0)), pl.BlockSpec((1, LANE), lambda i: (0, 0))],
        out_specs=[spec, spec, spec],
        compiler_params=_params(("parallel",), 32),
        name="rope_tables",
    )(positions.reshape(s, 1), inv)


def _rope(x, cos, sa, sb):
    return x * cos + pltpu.roll(x, MLA_ROPE // 2, 1) * sa + pltpu.roll(x, LANE - MLA_ROPE // 2, 1) * sb


def _mla_prep_kernel(qa_ref, kva_ref, kr_ref, cos_ref, sa_ref, sb_ref, gq_ref, gkv_ref, wq_ref, wkv_ref,
                     q_out, k_out, v_out):
    cos, sa, sb = cos_ref[...], sa_ref[...], sb_ref[...]
    qscale = (MLA_NOPE + MLA_ROPE) ** -0.5 * LOG2E

    qa = qa_ref[...].astype(F32)
    qn = qa * lax.rsqrt(jnp.mean(qa * qa, -1, keepdims=True) + RMS_EPS) * gq_ref[...]
    q = _dot(qn.astype(BF16), wq_ref[...])
    for h in range(MLA_HEADS):
        c0 = h * MLA_QK
        q_out[h, :, 0:LANE] = (q[:, c0:c0 + LANE] * qscale).astype(BF16)
        pe = _rope(q[:, c0 + LANE:c0 + 2 * LANE], cos, sa, sb)
        q_out[h, :, LANE:2 * LANE] = (pe * qscale).astype(BF16)

    kva = kva_ref[...].astype(F32)
    kvn = kva * lax.rsqrt(jnp.mean(kva * kva, -1, keepdims=True) + RMS_EPS) * gkv_ref[...]
    kv = _dot(kvn.astype(BF16), wkv_ref[...])
    kpe = _rope(kr_ref[...].astype(F32), cos, sa, sb).astype(BF16)
    for h in range(MLA_HEADS):
        c0 = h * (MLA_NOPE + MLA_V)
        k_out[h, :, 0:LANE] = kv[:, c0:c0 + MLA_NOPE].astype(BF16)
        k_out[h, :, LANE:2 * LANE] = kpe
        v_out[h, :, 0:MLA_V] = kv[:, c0 + MLA_NOPE:c0 + MLA_NOPE + MLA_V].astype(BF16)
        v_out[h, :, MLA_V:2 * MLA_V] = jnp.ones((kv.shape[0], MLA_V), BF16)


def mla_prep(h, cos, sa, sb, gq, gkv, wq_p, wkv):
    s = h.shape[0]
    tm = min(512, s)
    row = lambda width, col: pl.BlockSpec((tm, width), lambda i: (i, col // width))
    full = lambda a: pl.BlockSpec(a.shape, lambda i: (0,) * a.ndim)
    return pl.pallas_call(
        _mla_prep_kernel,
        out_shape=[jax.ShapeDtypeStruct((MLA_HEADS, s, MLA_QK), BF16),
                   jax.ShapeDtypeStruct((MLA_HEADS, s, MLA_QK), BF16),
                   jax.ShapeDtypeStruct((MLA_HEADS, s, 2 * MLA_V), BF16)],
        grid=(s // tm,),
        in_specs=[row(MLA_Q_RANK, H_QA), row(MLA_KV_RANK, H_KVA), row(LANE, H_KR),
                  row(LANE, 0), row(LANE, 0), row(LANE, 0),
                  full(gq), full(gkv), full(wq_p), full(wkv)],
        out_specs=[pl.BlockSpec((MLA_HEADS, tm, MLA_QK), lambda i: (0, i, 0)),
                   pl.BlockSpec((MLA_HEADS, tm, MLA_QK), lambda i: (0, i, 0)),
                   pl.BlockSpec((MLA_HEADS, tm, 2 * MLA_V), lambda i: (0, i, 0))],
        compiler_params=_params(("parallel",), 48),
        name="mla_prep",
    )(h, h, h, cos, sa, sb, gq, gkv, wq_p, wkv)


def _mla_flash_kernel(q_ref, k_ref, v_ref, o_ref, sa_ref, sb_ref, mxa_ref, mxb_ref, m_ref, acc_ref, *, t, nh):
    qi = pl.program_id(1)
    heads = range(nh)
    bufs = ((sa_ref, mxa_ref), (sb_ref, mxb_ref))

    def produce(b, parity, masked):
        s_ref, mx_ref = bufs[parity]
        start = pl.multiple_of(b * t, t)
        for h in heads:
            s = _dot_nt(q_ref[h], k_ref[h, pl.ds(start, t), :])
            if masked:
                rows = lax.broadcasted_iota(jnp.int32, s.shape, 0) + qi * t
                cols = lax.broadcasted_iota(jnp.int32, s.shape, 1) + b * t
                s = jnp.where(cols <= rows, s, NEG_BIG)
            s_ref[h] = s
            mx_ref[h] = jnp.broadcast_to(jnp.max(s, -1, keepdims=True), (t, LANE))

    def absorb(b, parity):
        s_ref, mx_ref = bufs[parity]
        start = pl.multiple_of(b * t, t)
        for h in heads:
            m_new = jnp.maximum(m_ref[h], mx_ref[h])
            alpha = jnp.exp2(m_ref[h] - m_new)
            p = jnp.exp2(s_ref[h] - jnp.concatenate([m_new] * (t // LANE), 1))
            acc_ref[h] = (jnp.concatenate([alpha, alpha], 1) * acc_ref[h]
                          + _dot(p.astype(BF16), v_ref[h, pl.ds(start, t), :]))
            m_ref[h] = m_new

    m_ref[...] = jnp.full(m_ref.shape, NEG_BIG, F32)
    acc_ref[...] = jnp.zeros_like(acc_ref)
    produce(0, 0, True)

    def pair(i, c):
        b = 2 * i
        produce(b + 1, 1, False)
        absorb(b, 0)
        produce(b + 2, 0, False)
        absorb(b + 1, 1)
        return c

    n_pairs = jnp.maximum(qi - 1, 0) // 2
    lax.fori_loop(0, n_pairs, pair, 0)
    done = 2 * n_pairs

    @pl.when(qi == 0)
    def _():
        absorb(0, 0)

    @pl.when((qi > 0) & (qi - done == 1))
    def _():
        produce(qi, 1, True)
        absorb(qi - 1, 0)
        absorb(qi, 1)

    @pl.when((qi > 0) & (qi - done == 2))
    def _():
        produce(qi - 1, 1, False)
        absorb(qi - 2, 0)
        produce(qi, 0, True)
        absorb(qi - 1, 1)
        absorb(qi, 0)

    for h in heads:
        acc = acc_ref[h]
        o_ref[:, h * MLA_V:(h + 1) * MLA_V] = (acc[:, :MLA_V] / acc[:, MLA_V:]).astype(o_ref.dtype)


def mla_flash(q, k, v1):
    _, s, _ = q.shape
    t = min(512, s)
    nh = 2
    return pl.pallas_call(
        functools.partial(_mla_flash_kernel, t=t, nh=nh),
        out_shape=jax.ShapeDtypeStruct((s, MLA_HEADS * MLA_V), BF16),
        grid=(MLA_HEADS // nh, s // t),
        in_specs=[pl.BlockSpec((nh, t, MLA_QK), lambda h, i: (h, i, 0)),
                  pl.BlockSpec((nh, s, MLA_QK), lambda h, i: (h, 0, 0)),
                  pl.BlockSpec((nh, s, 2 * MLA_V), lambda h, i: (h, 0, 0))],
        out_specs=pl.BlockSpec((t, nh * MLA_V), lambda h, i: (i, h)),
        scratch_shapes=[pltpu.VMEM((nh, t, t), F32), pltpu.VMEM((nh, t, t), F32),
                        pltpu.VMEM((nh, t, LANE), F32), pltpu.VMEM((nh, t, LANE), F32),
                        pltpu.VMEM((nh, t, LANE), F32), pltpu.VMEM((nh, t, 2 * MLA_V), F32)],
        compiler_params=_params(("parallel", "arbitrary"), 56),
        name="mla_flash",
    )(q, k, v1)


def _gla_masks():
    i = np.arange(GLA_L)[:, None]
    j = np.arange(GLA_L)[None, :]
    same = (i // GLA_SUB) == (j // GLA_SUB)
    mats = [j <= i, same & (j <= i), same & (j > i), j > i]
    for sub in range(GLA_NSUB - 1):
        mats.append((j >= (sub + 1) * GLA_SUB) & (j <= i))
    return np.concatenate(mats, 0).astype(np.float32)


GLA_CHUNKS_PER_STEP = 4


def _gla_kernel(q_ref, k_ref, v_ref, lr_ref, r_ref, wg_ref, bg_ref, gain_ref, mask_ref, o_ref, state_ref):
    @pl.when(pl.program_id(0) == 0)
    def _():
        state_ref[...] = jnp.zeros_like(state_ref)

    L, hk = GLA_L, GLA_HEADS * GLA_DK
    chunks = [slice(c * L, (c + 1) * L) for c in range(q_ref.shape[0] // L)]
    heads = [(slice(h * GLA_DK, (h + 1) * GLA_DK), slice(h * GLA_DV, (h + 1) * GLA_DV)) for h in range(GLA_HEADS)]

    z = _dot(lr_ref[...], wg_ref[...]) + bg_ref[...]
    g = (jnp.minimum(z, 0.0) - jnp.log(1.0 + jnp.exp(-jnp.abs(z)))) * (1.0 / GLA_TAU)
    g_hi = g.astype(BF16)
    g_lo = (g - g_hi.astype(F32)).astype(BF16)

    cums = [_dot(mask_ref[...], jnp.concatenate([g_hi[rs], g_lo[rs]], 0)) for rs in chunks]

    sub_of_row = lax.broadcasted_iota(jnp.int32, (L, hk), 0) // GLA_SUB
    prep = []
    for rs, cum in zip(chunks, cums):
        b_all, b_loc, sfx_loc, sfx_all = cum[0:L], cum[L:2 * L], cum[2 * L:3 * L], cum[3 * L:4 * L]
        q = q_ref[rs, :].astype(F32) * (GLA_DK ** -0.5)
        k = k_ref[rs, :].astype(F32)
        k_end = k * jnp.exp(sfx_loc)
        prep.append(dict(
            decay=jnp.exp(b_all[L - 1:L, :]),
            q_inter=(q * jnp.exp(b_all)).astype(BF16),
            q_diag=(q * jnp.exp(b_loc)).astype(BF16),
            k_diag=(k * jnp.exp(-b_loc)).astype(BF16),
            k_state=(k * jnp.exp(sfx_all)).astype(BF16),
            q_off=[(q * jnp.exp(cum[(4 + sub) * L:(5 + sub) * L])).astype(BF16) for sub in range(GLA_NSUB - 1)],
            k_off=[jnp.where(sub_of_row == sub, k_end, 0.0).astype(BF16) for sub in range(GLA_NSUB - 1)]))

    row = lax.broadcasted_iota(jnp.int32, (L, L), 0)
    col = lax.broadcasted_iota(jnp.int32, (L, L), 1)
    diag_ok = ((row // GLA_SUB) == (col // GLA_SUB)) & (col <= row)
    off_ok = (row // GLA_SUB) > (col // GLA_SUB)
    local = []
    for rs, pr in zip(chunks, prep):
        per_head = []
        for ks, vs in heads:
            v_h = v_ref[rs, vs]
            a = jnp.where(diag_ok, _dot_nt(pr["q_diag"][:, ks], pr["k_diag"][:, ks]), 0.0)
            qo = jnp.concatenate([t[:, ks] for t in pr["q_off"]], 1)
            ko = jnp.concatenate([t[:, ks] for t in pr["k_off"]], 1)
            a = a + jnp.where(off_ok, _dot_nt(qo, ko), 0.0)
            per_head.append((_dot(a.astype(BF16), v_h), _dot_tn(v_h, pr["k_state"][:, ks])))
        local.append(per_head)

    states = [state_ref[h] for h in range(GLA_HEADS)]
    for rs, pr, per_head in zip(chunks, prep, local):
        r = r_ref[rs, :].astype(F32)
        gate = r * (1.0 / (1.0 + jnp.exp(-r)))
        for h, (ks, vs) in enumerate(heads):
            o_intra, vk = per_head[h]
            o = _dot_nt(pr["q_inter"][:, ks], states[h].astype(BF16)) + o_intra
            states[h] = states[h] * pr["decay"][:, ks] + vk
            o = o * lax.rsqrt(jnp.mean(o * o, -1, keepdims=True) + RMS_EPS) * gain_ref[:, vs]
            o_ref[rs, vs] = (o * gate[:, vs]).astype(o_ref.dtype)
    for h in range(GLA_HEADS):
        state_ref[h] = states[h]


def gla(h, wg_p, bg, gain):
    s = h.shape[0]
    rows = min(GLA_CHUNKS_PER_STEP * GLA_L, s)
    masks = jnp.asarray(np.tile(_gla_masks(), (1, 2)), BF16)
    row = lambda width, col: pl.BlockSpec((rows, width), lambda i: (i, col // width))
    full = lambda a: pl.BlockSpec(a.shape, lambda i: (0,) * a.ndim)
    hk, hv = GLA_HEADS * GLA_DK, GLA_HEADS * GLA_DV
    return pl.pallas_call(
        _gla_kernel,
        out_shape=jax.ShapeDtypeStruct((s, hv), BF16),
        grid=(s // rows,),
        in_specs=[row(hk, H_GQ), row(hk, H_GK), row(hv, H_GV), row(LANE, H_GLR), row(hv, H_GR),
                  full(wg_p), full(bg), full(gain), full(masks)],
        out_specs=pl.BlockSpec((rows, hv), lambda i: (i, 0)),
        scratch_shapes=[pltpu.VMEM((GLA_HEADS, GLA_DV, GLA_DK), F32)],
        compiler_params=_params(("arbitrary",), 32),
        name="gla",
    )(h, h, h, h, h, wg_p, bg, gain, masks)


def _t5_bucket_table():
    L = SWA_BLOCK
    dist = np.arange(L)[:, None] + L - np.arange(2 * L)[None, :]
    d = np.clip(dist, 0, None)
    max_exact = REL_BUCKETS // 2
    df = np.maximum(d, 1).astype(np.float32)
    large = max_exact + (np.log(df / np.float32(max_exact)) / np.float32(math.log(REL_MAX_DIST / max_exact))
                         * np.float32(REL_BUCKETS - max_exact)).astype(np.int32)
    large = np.minimum(large, REL_BUCKETS - 1)
    bucket = np.where(d < max_exact, d, large)
    in_window = (dist >= 0) & (dist < SWA_WINDOW)
    return np.where(in_window, bucket, -1).astype(np.int32)


def _swa_kernel(relb_ref, sink_ref, q_ref, kp_ref, kc_ref, vp_ref, vc_ref, bucket_ref, o_ref, bias_ref):
    i = pl.program_id(0)
    L = SWA_BLOCK

    @pl.when(i == 0)
    def _():
        bucket = bucket_ref[...]
        for h in range(SWA_HEADS):
            acc = jnp.full(bucket.shape, NEG_BIG, F32)
            for b in range(REL_BUCKETS):
                acc = jnp.where(bucket == b, relb_ref[b, h], acc)
            bias_ref[h] = acc

    kcat = jnp.concatenate([kp_ref[...], kc_ref[...]], 0)
    vcat = jnp.concatenate([vp_ref[...], vc_ref[...]], 0)
    q = q_ref[...]
    col = lax.broadcasted_iota(jnp.int32, (L, 2 * L), 1)
    real_key = (col >= L) | (i > 0)
    g = SWA_HEADS // SWA_KV_HEADS
    scale = SWA_HD ** -0.5
    for h in range(SWA_HEADS):
        kv = h // g
        hs = slice(h * SWA_HD, (h + 1) * SWA_HD)
        kvs = slice(kv * SWA_HD, (kv + 1) * SWA_HD)
        s = _dot_nt(q[:, hs], kcat[:, kvs]) * scale + bias_ref[h]
        s = jnp.where(real_key, s, NEG_BIG)
        sink = sink_ref[h]
        m = jnp.maximum(jnp.max(s, -1, keepdims=True), sink)
        p = jnp.exp(s - m)
        denom = jnp.sum(p, -1, keepdims=True) + jnp.exp(sink - m)
        o = _dot((p / denom).astype(BF16), vcat[:, kvs])
        o_ref[:, hs] = o.astype(o_ref.dtype)


def swa(h, sinks, rel_bias):
    s = h.shape[0]
    L = SWA_BLOCK
    bucket = jnp.asarray(_t5_bucket_table())
    kvw = SWA_KV_HEADS * SWA_HD
    hw = SWA_HEADS * SWA_HD
    cur = lambda width, col: pl.BlockSpec((L, width), lambda i, *_: (i, col // width))
    prev = lambda width, col: pl.BlockSpec((L, width), lambda i, *_: (jnp.maximum(i - 1, 0), col // width))
    return pl.pallas_call(
        _swa_kernel,
        out_shape=jax.ShapeDtypeStruct((s, hw), BF16),
        grid_spec=pltpu.PrefetchScalarGridSpec(
            num_scalar_prefetch=2,
            grid=(s // L,),
            in_specs=[cur(hw, H_SQ), prev(kvw, H_SK), cur(kvw, H_SK), prev(kvw, H_SV), cur(kvw, H_SV),
                      pl.BlockSpec(bucket.shape, lambda i, *_: (0, 0))],
            out_specs=pl.BlockSpec((L, hw), lambda i, *_: (i, 0)),
            scratch_shapes=[pltpu.VMEM((SWA_HEADS, L, 2 * L), F32)],
        ),
        compiler_params=_params(("arbitrary",), 32),
        name="swa",
    )(rel_bias, sinks, h, h, h, h, h, bucket)


def _layer_norm(y, g, b):
    mu = jnp.mean(y, -1, keepdims=True)
    yc = y - mu
    var = jnp.mean(yc * yc, -1, keepdims=True)
    return yc * lax.rsqrt(var + LN_EPS) * g + b


HALF = D_MODEL // 2
HIGH16 = 0xFFFF0000


SUBLANE = 8
TOK = HALF // LANE


def _store_token_tiles(ref, x):
    tm = x.shape[0]
    lo = lax.bitcast_convert_type(x[:, :HALF].astype(BF16).astype(F32), jnp.uint32) >> 16
    hi = lax.bitcast_convert_type(x[:, HALF:].astype(BF16).astype(F32), jnp.uint32) & jnp.uint32(HIGH16)
    packed = lo | hi
    for s in range(TOK):
        ref[pl.ds(s, tm, stride=TOK), :] = packed[:, s * LANE:(s + 1) * LANE]


def _load_token_tiles(ref, rows=None):
    first, tm = rows if rows is not None else (0, ref.shape[0] // TOK)
    lo, hi = [], []
    for s in range(TOK):
        w = ref[pl.ds(first * TOK + s, tm, stride=TOK), :]
        lo.append(lax.bitcast_convert_type(w << 16, F32))
        hi.append(lax.bitcast_convert_type(w & jnp.uint32(HIGH16), F32))
    return jnp.concatenate(lo, 1), jnp.concatenate(hi, 1)


ROW_SPLIT = 2


def _out_ln_kernel(a_ref, b_ref, c_ref, x_ref, w_ref, g_ref, beta_ref, o_ref, op_ref):
    na, nb = a_ref.shape[1], b_ref.shape[1]
    nr = x_ref.shape[0] // ROW_SPLIT
    for r0 in range(0, x_ref.shape[0], nr):
        rows = slice(r0, r0 + nr)
        m = _dot(a_ref[rows, :], w_ref[0:na, :])
        m = m + _dot(b_ref[rows, :], w_ref[na:na + nb, :])
        m = m + _dot(c_ref[rows, :], w_ref[na + nb:, :])
        y = _layer_norm(DEEPNORM_ALPHA * x_ref[rows, :] + m, g_ref[...], beta_ref[...])
        o_ref[rows, :] = y
        _store_token_tiles(op_ref.at[pl.ds(r0 * TOK, nr * TOK)], y)


def out_ln(a, b, c, x, w, g, beta):
    s = x.shape[0]
    tm = min(512, s)
    row = lambda arr: pl.BlockSpec((tm, arr.shape[1]), lambda i: (i, 0))
    full = lambda arr: pl.BlockSpec(arr.shape, lambda i: (0, 0), pipeline_mode=pl.Buffered(1))
    return pl.pallas_call(
        _out_ln_kernel,
        out_shape=[jax.ShapeDtypeStruct((s, D_MODEL), F32), jax.ShapeDtypeStruct((s * TOK, LANE), jnp.uint32)],
        grid=(s // tm,),
        in_specs=[row(a), row(b), row(c), row(x), full(w), full(g), full(beta)],
        out_specs=[pl.BlockSpec((tm, D_MODEL), lambda i: (i, 0)), pl.BlockSpec((tm * TOK, LANE), lambda i: (i, 0))],
        compiler_params=_params(("parallel",), 56),
        name="out_ln",
    )(a, b, c, x, w, g, beta)


FFN_ROWS = 1024
FFN_COLS = 512
FFN_OUT_CHUNK = 512
FFN_SUB = 256


def _ffn_kernel(te_ref, tv_ref, na_ref, xp_ref, wg_ref, wu_ref, wd_ref, o_ref, xb_ref, acc_ref):
    i, j = pl.program_id(0), pl.program_id(1)
    tm = xb_ref.shape[0]
    valid = tv_ref[i]

    @pl.when(j == 0)
    def _():
        acc_ref[...] = jnp.zeros_like(acc_ref)

    @pl.when((j == 0) & (valid > 0))
    def _():
        lo, hi = _load_token_tiles(xp_ref)
        xb_ref[:, :HALF] = lo.astype(BF16)
        xb_ref[:, HALF:] = hi.astype(BF16)

    def rows_step(r0, nr):
        xb = xb_ref[r0:r0 + nr, :]
        gate = _dot(xb, wg_ref[...].astype(BF16))
        up = _dot(xb, wu_ref[...].astype(BF16))
        hmid = (gate * (1.0 / (1.0 + jnp.exp(-gate))) * up).astype(BF16)
        for c in range(0, D_MODEL, FFN_OUT_CHUNK):
            cs = slice(c, c + FFN_OUT_CHUNK)
            acc_ref[r0:r0 + nr, cs] += _dot(hmid, wd_ref[:, cs].astype(BF16))

    nearly_full = valid > tm - FFN_SUB

    @pl.when(nearly_full)
    def _():
        rows_step(0, tm)

    for r0 in range(0, tm - FFN_SUB, FFN_SUB):
        @pl.when(jnp.logical_not(nearly_full) & (valid > r0))
        def _():
            rows_step(r0, FFN_SUB)

    @pl.when(j == pl.num_programs(1) - 1)
    def _():
        _store_token_tiles(o_ref, acc_ref[...])


def grouped_swiglu(xp, wg, wu, wd, tile_expert, tile_valid, n_active, tm, tf):
    n_tiles, nf = xp.shape[0] // (tm * TOK), D_FF // tf

    def tile(i, na):
        return jnp.minimum(i, na[0] - 1)

    def fcol(i, j, na):
        return jnp.where(i < na[0], j, nf - 1)

    return pl.pallas_call(
        _ffn_kernel,
        out_shape=jax.ShapeDtypeStruct(xp.shape, jnp.uint32),
        grid_spec=pltpu.PrefetchScalarGridSpec(
            num_scalar_prefetch=3,
            grid=(n_tiles, nf),
            in_specs=[pl.BlockSpec((tm * TOK, LANE), lambda i, j, te, tv, na: (tile(i, na), 0)),
                      pl.BlockSpec((None, D_MODEL, tf),
                                   lambda i, j, te, tv, na: (te[tile(i, na)], 0, fcol(i, j, na))),
                      pl.BlockSpec((None, D_MODEL, tf),
                                   lambda i, j, te, tv, na: (te[tile(i, na)], 0, fcol(i, j, na))),
                      pl.BlockSpec((None, tf, D_MODEL),
                                   lambda i, j, te, tv, na: (te[tile(i, na)], fcol(i, j, na), 0))],
            out_specs=pl.BlockSpec((tm * TOK, LANE), lambda i, j, te, tv, na: (i, 0)),
            scratch_shapes=[pltpu.VMEM((tm, D_MODEL), BF16), pltpu.VMEM((tm, D_MODEL), F32)],
        ),
        compiler_params=_params(("arbitrary", "arbitrary"), 60),
        name="grouped_swiglu",
    )(tile_expert, tile_valid, n_active, xp, wg, wu, wd)


def _router_kernel(x_ref, w_ref, e_ref, r_ref, wt_ref, cnt_ref, run_ref):
    i = pl.program_id(0)
    tm = x_ref.shape[0]
    ne = N_EXPERTS

    @pl.when(i == 0)
    def _():
        run_ref[...] = jnp.zeros_like(run_ref)

    x = x_ref[...]
    x_hi = x.astype(BF16)
    x_lo = (x - x_hi.astype(F32)).astype(BF16)
    w = w_ref[...]
    w_hi = w.astype(BF16).astype(F32)
    w_lo = w - w_hi
    both = _dot_nt(jnp.concatenate([w_hi, w_lo], 0).astype(BF16), x_hi)
    cross = _dot_nt(jnp.concatenate([w_hi, jnp.zeros_like(w_hi)], 0).astype(BF16), x_lo)
    logits = both[0:ne] + both[ne:2 * ne] + cross[0:ne]

    eidx = lax.broadcasted_iota(jnp.int32, logits.shape, 0).astype(F32)
    v1 = jnp.max(logits, 0, keepdims=True)
    i1 = jnp.min(jnp.where(logits == v1, eidx, float(ne)), 0, keepdims=True)
    rest = jnp.where(eidx == i1, -jnp.inf, logits)
    v2 = jnp.max(rest, 0, keepdims=True)
    i2 = jnp.min(jnp.where(rest == v2, eidx, float(ne)), 0, keepdims=True)
    t = jnp.exp(v2 - v1)
    w1 = 1.0 / (1.0 + t)
    wt_ref[0:1, :] = w1
    wt_ref[1:2, :] = t * w1
    e_ref[0:1, :] = i1.astype(jnp.int32)
    e_ref[1:2, :] = i2.astype(jnp.int32)

    sel1, sel2 = eidx == i1, eidx == i2
    sel = jnp.where(sel1, 1.0, 0.0) + jnp.where(sel2, 1.0, 0.0)
    before = (lax.broadcasted_iota(jnp.int32, (tm, tm), 0) < lax.broadcasted_iota(jnp.int32, (tm, tm), 1))
    sel16 = jnp.concatenate([sel, jnp.zeros_like(sel)], 0).astype(BF16)
    prefix = _dot(sel16, jnp.where(before, 1.0, 0.0).astype(BF16))[0:ne]
    rank = prefix + run_ref[:, 0:1]
    r_ref[0:1, :] = jnp.sum(jnp.where(sel1, rank, 0.0), 0, keepdims=True).astype(jnp.int32)
    r_ref[1:2, :] = jnp.sum(jnp.where(sel2, rank, 0.0), 0, keepdims=True).astype(jnp.int32)
    run_ref[...] = run_ref[...] + jnp.sum(sel, 1, keepdims=True)
    cnt_ref[...] = run_ref[...].astype(jnp.int32)


def route_tokens(x, w_router_t):
    s = x.shape[0]
    tm = min(512, s)
    pair = pl.BlockSpec((2, tm), lambda i: (0, i))
    return pl.pallas_call(
        _router_kernel,
        out_shape=[jax.ShapeDtypeStruct((2, s), jnp.int32), jax.ShapeDtypeStruct((2, s), jnp.int32),
                   jax.ShapeDtypeStruct((2, s), F32), jax.ShapeDtypeStruct((N_EXPERTS, LANE), jnp.int32)],
        grid=(s // tm,),
        in_specs=[pl.BlockSpec((tm, D_MODEL), lambda i: (i, 0)),
                  pl.BlockSpec((N_EXPERTS, D_MODEL), lambda i: (0, 0))],
        out_specs=[pair, pair, pair, pl.BlockSpec((N_EXPERTS, LANE), lambda i: (0, 0))],
        scratch_shapes=[pltpu.VMEM((N_EXPERTS, LANE), F32)],
        compiler_params=_params(("arbitrary",), 32),
        name="moe_router",
    )(x, w_router_t)


def _slot_kernel(base_ref, e_ref, r_ref, s_ref):
    e = e_ref[...]
    slot = r_ref[...]
    for k in range(N_EXPERTS):
        slot = slot + jnp.where(e == k, base_ref[k], 0)
    s_ref[...] = slot


def token_slots(e_idx, rank, base):
    whole = pl.BlockSpec(e_idx.shape, lambda i, *_: (0, 0))
    return pl.pallas_call(
        _slot_kernel,
        out_shape=jax.ShapeDtypeStruct(e_idx.shape, jnp.int32),
        grid_spec=pltpu.PrefetchScalarGridSpec(num_scalar_prefetch=1, grid=(1,), in_specs=[whole, whole],
                                               out_specs=whole),
        compiler_params=_params(("arbitrary",), 32),
        name="token_slots",
    )(base, e_idx, rank)


DMA_UNROLL = 8


def _tile_rows(index):
    return pl.ds(pl.multiple_of(index * TOK, TOK), TOK)


def _dispatch_kernel(fill_ref, slot_ref, x_ref, xs_ref, zero_ref, sem, zsem):
    i = pl.program_id(0)
    tm = slot_ref.shape[1]

    def issue(b, c):
        for u in range(DMA_UNROLL):
            t = b * DMA_UNROLL + u
            for k in range(2):
                pltpu.make_async_copy(x_ref.at[_tile_rows(i * tm + t)], xs_ref.at[_tile_rows(slot_ref[k, t])],
                                      sem).start()
        return c

    lax.fori_loop(0, tm // DMA_UNROLL, issue, 0)

    @pl.when(i == pl.num_programs(0) - 1)
    def _():
        for k in range(2):
            pltpu.make_async_copy(x_ref, xs_ref.at[pl.ds(0, x_ref.shape[0])], sem).wait()

    @pl.when(i == pl.num_programs(0) - 1)
    def _():
        zero_ref[...] = jnp.zeros_like(zero_ref)

        def zero_copy(slot):
            return pltpu.make_async_copy(zero_ref.at[pl.ds(0, TOK)], xs_ref.at[_tile_rows(slot)], zsem)

        for e in range(N_EXPERTS):
            lo, hi = fill_ref[0, e], fill_ref[1, e]

            def zissue(slot, c):
                zero_copy(slot).start()
                return c

            def zdrain(slot, c):
                zero_copy(slot).wait()
                return c

            lax.fori_loop(lo, hi, zissue, 0)
            lax.fori_loop(lo, hi, zdrain, 0)

        zrows = zero_ref.shape[0]
        zb = zrows // TOK

        def block_copy(b):
            return pltpu.make_async_copy(zero_ref, xs_ref.at[pl.ds(pl.multiple_of(b * zrows, zrows), zrows)], zsem)

        def bissue(b, c):
            block_copy(b).start()
            return c

        def bdrain(b, c):
            block_copy(b).wait()
            return c

        first, last = fill_ref[2, 0] // zb, xs_ref.shape[0] // zrows
        lax.fori_loop(first, last, bissue, 0)
        lax.fori_loop(first, last, bdrain, 0)


def moe_dispatch(xp, slots, fill, rows):
    s = xp.shape[0] // TOK
    tm = min(512, s)
    zb = min(256, s)
    return pl.pallas_call(
        _dispatch_kernel,
        out_shape=jax.ShapeDtypeStruct((rows * TOK, LANE), xp.dtype),
        grid_spec=pltpu.PrefetchScalarGridSpec(
            num_scalar_prefetch=1,
            grid=(s // tm,),
            in_specs=[pl.BlockSpec((2, tm), lambda i, *_: (0, i), memory_space=pltpu.SMEM),
                      pl.BlockSpec(memory_space=pl.ANY)],
            out_specs=pl.BlockSpec(memory_space=pl.ANY),
            scratch_shapes=[pltpu.VMEM((zb * TOK, LANE), xp.dtype), pltpu.SemaphoreType.DMA,
                            pltpu.SemaphoreType.DMA],
        ),
        compiler_params=_params(("arbitrary",), 32),
        name="moe_dispatch",
    )(fill, slots, xp)


def _ple(x, p, wup_ref, wgate_ref, bgate_ref):
    up = _dot(p.astype(BF16), wup_ref[...])
    zg = _dot(x.astype(BF16), wgate_ref[...]) + bgate_ref[...]
    return up * (1.0 / (1.0 + jnp.exp(-zg)))


def _ple_ln_dense_kernel(x_ref, f_ref, p_ref, wup_ref, wgate_ref, bgate_ref, g_ref, beta_ref, o_ref):
    nr = x_ref.shape[0] // ROW_SPLIT
    for r0 in range(0, x_ref.shape[0], nr):
        rows = slice(r0, r0 + nr)
        x = x_ref[rows, :]
        f = jnp.concatenate(_load_token_tiles(f_ref, (r0, nr)), 1)
        ple = _ple(x, p_ref[rows, :], wup_ref, wgate_ref, bgate_ref)
        o_ref[rows, :] = _layer_norm(DEEPNORM_ALPHA * x + f + ple, g_ref[...], beta_ref[...])


def _ple_ln_moe_kernel(slot_ref, x_ref, wt_ref, p_ref, wup_ref, wgate_ref, bgate_ref, g_ref, beta_ref, ys_ref,
                       o_ref, y1_ref, y2_ref, sem):
    tm = x_ref.shape[0]
    bufs = (y1_ref, y2_ref)

    def issue(b, c):
        for u in range(DMA_UNROLL):
            t = b * DMA_UNROLL + u
            for k in range(2):
                pltpu.make_async_copy(ys_ref.at[_tile_rows(slot_ref[k, t])], bufs[k].at[_tile_rows(t)],
                                      sem).start(priority=k)
        return c

    lax.fori_loop(0, tm // DMA_UNROLL, issue, 0)
    nr = tm // ROW_SPLIT
    ples = [_ple(x_ref[r0:r0 + nr, :], p_ref[r0:r0 + nr, :], wup_ref, wgate_ref, bgate_ref)
            for r0 in range(0, tm, nr)]
    for k in range(2):
        pltpu.make_async_copy(ys_ref.at[pl.ds(0, tm * TOK)], bufs[k], sem).wait()
    for i, r0 in enumerate(range(0, tm, nr)):
        rows = slice(r0, r0 + nr)
        wt = wt_ref[rows, :]
        f = (wt[:, 0:1] * jnp.concatenate(_load_token_tiles(y1_ref, (r0, nr)), 1)
             + wt[:, 1:2] * jnp.concatenate(_load_token_tiles(y2_ref, (r0, nr)), 1))
        o_ref[rows, :] = _layer_norm(DEEPNORM_ALPHA * x_ref[rows, :] + f + ples[i], g_ref[...], beta_ref[...])


def ple_ln(x, p, wup, wgate, bgate, g, beta, f=None, moe=None):
    s = x.shape[0]
    tm = min(512, s)
    row = lambda arr: pl.BlockSpec((tm, arr.shape[1]), lambda i, *_: (i, 0))
    full = lambda arr: pl.BlockSpec(arr.shape, lambda i, *_: (0, 0), pipeline_mode=pl.Buffered(1))
    tail = [row(p), full(wup), full(wgate), full(bgate), full(g), full(beta)]
    out_spec = pl.BlockSpec((tm, D_MODEL), lambda i, *_: (i, 0))
    out_shape = jax.ShapeDtypeStruct((s, D_MODEL), F32)
    if moe is None:
        return pl.pallas_call(
            _ple_ln_dense_kernel, out_shape=out_shape, grid=(s // tm,),
            in_specs=[row(x), pl.BlockSpec((tm * TOK, LANE), lambda i: (i, 0))] + tail, out_specs=out_spec,
            compiler_params=_params(("parallel",), 56), name="ple_ln_dense",
        )(x, f, p, wup, wgate, bgate, g, beta)
    ys, slots, wts_t = moe
    return pl.pallas_call(
        _ple_ln_moe_kernel,
        out_shape=out_shape,
        grid_spec=pltpu.PrefetchScalarGridSpec(
            num_scalar_prefetch=0,
            grid=(s // tm,),
            in_specs=[pl.BlockSpec((2, tm), lambda i: (0, i), memory_space=pltpu.SMEM), row(x), row(wts_t)] + tail
            + [pl.BlockSpec(memory_space=pl.ANY)],
            out_specs=out_spec,
            scratch_shapes=[pltpu.VMEM((tm * TOK, LANE), jnp.uint32), pltpu.VMEM((tm * TOK, LANE), jnp.uint32),
                            pltpu.SemaphoreType.DMA],
        ),
        compiler_params=_params(("arbitrary",), 56),
        name="ple_ln_moe",
    )(slots, x, wts_t, p, wup, wgate, bgate, g, beta, ys)


def _pad_cols(w, width):
    return jnp.pad(w, ((0, 0), (0, width - w.shape[1])))


def _pack_w_in(w):
    sizes = (MLA_Q_RANK, MLA_KV_RANK, MLA_ROPE, GLA_HEADS * GLA_DK, GLA_HEADS * GLA_DK, GLA_HEADS * GLA_DV,
             GLA_GATE_RANK, GLA_HEADS * GLA_DV, SWA_HEADS * SWA_HD, SWA_KV_HEADS * SWA_HD, SWA_KV_HEADS * SWA_HD)
    splits = np.cumsum(sizes)[:-1]
    q_a, kv_a, k_rope, g_q, g_k, g_v, g_lr, g_r, s_q, s_k, s_v = jnp.split(w, splits, axis=1)
    pieces = [q_a, g_v, g_r, s_q, kv_a, g_q, g_k, _pad_cols(k_rope, LANE), _pad_cols(g_lr, LANE), s_k, s_v]
    return jnp.concatenate(pieces, 1).astype(BF16)


def _pack_w_q_b(w):
    w = w.reshape(MLA_Q_RANK, MLA_HEADS, MLA_NOPE + MLA_ROPE)
    w = jnp.pad(w, ((0, 0), (0, 0), (0, MLA_QK - MLA_NOPE - MLA_ROPE)))
    return w.reshape(MLA_Q_RANK, MLA_HEADS * MLA_QK).astype(BF16)


def _token_mixer_ln(x, rope, w_in, q_gain, w_q_b, kv_gain, w_kv_b, gla_w, gla_b, gla_gain, sinks, rel_bias,
                    w_out, ln_g, ln_b):
    h = proj_in(x, _pack_w_in(w_in))
    q, k, v = mla_prep(h, *rope, q_gain.reshape(1, -1), kv_gain.reshape(1, -1), _pack_w_q_b(w_q_b),
                       w_kv_b.astype(BF16))
    a = mla_flash(q, k, v)
    gla_w_p = jnp.pad(gla_w, ((0, LANE - GLA_GATE_RANK), (0, 0))).astype(BF16)
    b = gla(h, gla_w_p, gla_b.reshape(1, -1), gla_gain.reshape(1, -1))
    c = swa(h, sinks, rel_bias)
    return out_ln(a, b, c, x, w_out.astype(BF16), ln_g.reshape(1, -1), ln_b.reshape(1, -1))


def _moe_plan(counts, tm, n_tiles):
    tiles = (counts + tm - 1) // tm
    ends = jnp.cumsum(tiles)
    base = (ends - tiles) * tm
    n_active = ends[-1:].astype(jnp.int32)
    tile_expert = jnp.searchsorted(ends, jnp.arange(n_tiles, dtype=jnp.int32), side="right")
    tile_expert = jnp.minimum(tile_expert, N_EXPERTS - 1).astype(jnp.int32)
    tile_start = jnp.arange(n_tiles, dtype=jnp.int32) * tm
    tile_valid = jnp.clip((base + counts)[tile_expert] - tile_start, 0, tm).astype(jnp.int32)
    used_rows = jnp.broadcast_to(ends[-1] * tm, counts.shape)
    fill = jnp.stack([base + counts, base + tiles * tm, used_rows]).astype(jnp.int32)
    return base.astype(jnp.int32), fill, tile_expert, tile_valid, n_active


def kernel(x, p, positions, w_in, mla_q_a_gain, mla_w_q_b, mla_kv_a_gain, mla_w_kv_b, gla_w_gate, gla_b_gate,
           gla_norm_gain, swa_sinks, rel_bias, w_out, ln1_g, ln1_b, ffn_w_gate, ffn_w_up, ffn_w_down,
           moe_router, moe_w_gate, moe_w_up, moe_w_down, ple_w_up, ple_w_gate, ple_b_gate, ln2_g, ln2_b):
    batch, s, _ = x.shape
    assert batch == 1
    xcur = x.reshape(s, D_MODEL)
    rope = rope_tables(positions)
    tm = min(FFN_ROWS, s)
    for i in range(DEPTH):
        x1, x1p = _token_mixer_ln(xcur, rope, w_in[i], mla_q_a_gain[i], mla_w_q_b[i], mla_kv_a_gain[i],
                                  mla_w_kv_b[i], gla_w_gate[i], gla_b_gate[i], gla_norm_gain[i], swa_sinks[i],
                                  rel_bias, w_out[i], ln1_g[i], ln1_b[i])
        tail = (p[i, 0], ple_w_up[i].astype(BF16), ple_w_gate[i].astype(BF16), ple_b_gate[i].reshape(1, -1),
                ln2_g[i].reshape(1, -1), ln2_b[i].reshape(1, -1))
        j = i // 2
        if i % 2 == 0:
            n_tiles = s // tm
            f = grouped_swiglu(x1p, ffn_w_gate[j][None], ffn_w_up[j][None], ffn_w_down[j][None],
                               jnp.zeros((n_tiles,), jnp.int32), jnp.full((n_tiles,), tm, jnp.int32),
                               jnp.full((1,), n_tiles, jnp.int32), tm, FFN_COLS)
            xcur = ple_ln(x1, *tail, f=f)
        else:
            n_tiles = (2 * s + N_EXPERTS * (tm - 1)) // tm
            e_idx, rank, wts, counts = route_tokens(x1, moe_router[j].T)
            base, fill, tile_expert, tile_valid, n_active = _moe_plan(counts[:, 0], tm, n_tiles)
            slots = token_slots(e_idx, rank, base)
            xs = moe_dispatch(x1p, slots, fill, n_tiles * tm)
            ys = grouped_swiglu(xs, moe_w_gate[j], moe_w_up[j], moe_w_down[j], tile_expert, tile_valid, n_active,
                                tm, FFN_COLS)
            xcur = ple_ln(x1, *tail, moe=(ys, slots, wts.T))
    return xcur.reshape(batch, s, D_MODEL)
```

```python
import functools
import math

import numpy as np
import jax
import jax.numpy as jnp
from jax import lax
from jax.experimental import pallas as pl
from jax.experimental.pallas import tpu as pltpu

F32 = jnp.float32
BF16 = jnp.bfloat16

D_MODEL = 2048
DEPTH = 2
MLA_HEADS = 8
MLA_Q_RANK = 512
MLA_KV_RANK = 256
MLA_NOPE = 128
MLA_ROPE = 64
MLA_V = 128
ROPE_THETA = 10000.0
GLA_HEADS = 4
GLA_DK = 64
GLA_DV = 128
GLA_GATE_RANK = 16
GLA_TAU = 16.0
SWA_HEADS = 8
SWA_KV_HEADS = 2
SWA_HD = 64
SWA_WINDOW = 128
SWA_BLOCK = 128
REL_BUCKETS = 32
REL_MAX_DIST = 128
D_FF = 5632
N_EXPERTS = 8
PLE_DIM = 256
LN_EPS = 1e-5
RMS_EPS = 1e-6
DEEPNORM_ALPHA = (2 * DEPTH) ** 0.25

LANE = 128
LOG2E = math.log2(math.e)
NEG_BIG = -1e30

H_QA, H_GV, H_GR, H_SQ = 0, 512, 1024, 1536
H_KVA, H_GQ, H_GK = 2048, 2304, 2560
H_KR, H_GLR, H_SK, H_SV = 2816, 2944, 3072, 3200
H_COLS = 3328

MLA_QK = 2 * LANE

GLA_L = 128
GLA_SUB = 32
GLA_NSUB = GLA_L // GLA_SUB


def _params(sem, vmem_mb):
    return pltpu.CompilerParams(dimension_semantics=sem, vmem_limit_bytes=vmem_mb * 2 ** 20)


def _dot(a, b):
    return jnp.dot(a, b, preferred_element_type=F32)


def _dot_nt(a, b):
    return lax.dot_general(a, b, (((1,), (1,)), ((), ())), preferred_element_type=F32)


def _dot_tn(a, b):
    return lax.dot_general(a, b, (((0,), (0,)), ((), ())), preferred_element_type=F32)


def _proj_in_kernel(x_ref, w_ref, o_ref, xb_ref):
    @pl.when(pl.program_id(1) == 0)
    def _():
        xb_ref[...] = x_ref[...].astype(BF16)

    o_ref[...] = _dot(xb_ref[...], w_ref[...]).astype(o_ref.dtype)


def _layer_block(arr, layer):
    zeros = (0,) * (arr.ndim - 1)
    return pl.BlockSpec((None,) + arr.shape[1:], lambda i, *_: (layer,) + zeros, pipeline_mode=pl.Buffered(1))


def proj_in(x, w_p, layer):
    s = x.shape[0]
    tm = min(1024, s)
    tn = H_COLS // 2
    return pl.pallas_call(
        _proj_in_kernel,
        out_shape=jax.ShapeDtypeStruct((s, H_COLS), BF16),
        grid=(s // tm, H_COLS // tn),
        in_specs=[pl.BlockSpec((tm, D_MODEL), lambda i, j: (i, 0)),
                  pl.BlockSpec((None, D_MODEL, tn), lambda i, j: (layer, 0, j))],
        out_specs=pl.BlockSpec((tm, tn), lambda i, j: (i, j)),
        scratch_shapes=[pltpu.VMEM((tm, D_MODEL), BF16)],
        compiler_params=_params(("parallel", "arbitrary"), 56),
        name="proj_in",
    )(x, w_p)


def _rope_table_kernel(pos_ref, inv_ref, cos_ref, sa_ref, sb_ref):
    ang = pos_ref[...].astype(F32) * inv_ref[...]
    lane = lax.broadcasted_iota(jnp.int32, ang.shape, 1)
    half = MLA_ROPE // 2
    c, s = jnp.cos(ang), jnp.sin(ang)
    cos_ref[...] = c
    sa_ref[...] = jnp.where((lane >= half) & (lane < 2 * half), s, 0.0)
    sb_ref[...] = jnp.where(lane < half, -s, 0.0)


def rope_tables(positions):
    s = positions.shape[-1]
    half = MLA_ROPE // 2
    inv = ROPE_THETA ** (-jnp.arange(half, dtype=F32) / half)
    inv = jnp.concatenate([inv, inv, jnp.zeros((LANE - 2 * half,), F32)]).reshape(1, LANE)
    tm = min(1024, s)
    spec = pl.BlockSpec((tm, LANE), lambda i: (i, 0))
    return pl.pallas_call(
        _rope_table_kernel,
        out_shape=[jax.ShapeDtypeStruct((s, LANE), F32)] * 3,
        grid=(s // tm,),
        in_specs=[pl.BlockSpec((tm, 1), lambda i: (i, 0)), pl.BlockSpec((1, LANE), lambda i: (0, 0))],
        out_specs=[spec, spec, spec],
        compiler_params=_params(("parallel",), 32),
        name="rope_tables",
    )(positions.reshape(s, 1), inv)


def _rope(x, cos, sa, sb):
    return x * cos + pltpu.roll(x, MLA_ROPE // 2, 1) * sa + pltpu.roll(x, LANE - MLA_ROPE // 2, 1) * sb


def _mla_prep_kernel(qa_ref, kva_ref, kr_ref, cos_ref, sa_ref, sb_ref, gq_ref, gkv_ref, wq_ref, wkv_ref,
                     q_out, k_out, v_out):
    cos, sa, sb = cos_ref[...], sa_ref[...], sb_ref[...]
    qscale = (MLA_NOPE + MLA_ROPE) ** -0.5 * LOG2E

    qa = qa_ref[...].astype(F32)
    qn = qa * lax.rsqrt(jnp.mean(qa * qa, -1, keepdims=True) + RMS_EPS) * gq_ref[...]
    q = _dot(qn.astype(BF16), wq_ref[...])
    for h in range(MLA_HEADS):
        c0 = h * MLA_QK
        q_out[h, :, 0:LANE] = (q[:, c0:c0 + LANE] * qscale).astype(BF16)
        pe = _rope(q[:, c0 + LANE:c0 + 2 * LANE], cos, sa, sb)
        q_out[h, :, LANE:2 * LANE] = (pe * qscale).astype(BF16)

    kva = kva_ref[...].astype(F32)
    kvn = kva * lax.rsqrt(jnp.mean(kva * kva, -1, keepdims=True) + RMS_EPS) * gkv_ref[...]
    kv = _dot(kvn.astype(BF16), wkv_ref[...])
    kpe = _rope(kr_ref[...].astype(F32), cos, sa, sb).astype(BF16)
    for h in range(MLA_HEADS):
        c0 = h * (MLA_NOPE + MLA_V)
        k_out[h, :, 0:LANE] = kv[:, c0:c0 + MLA_NOPE].astype(BF16)
        k_out[h, :, LANE:2 * LANE] = kpe
        v_out[h, :, 0:MLA_V] = kv[:, c0 + MLA_NOPE:c0 + MLA_NOPE + MLA_V].astype(BF16)
        v_out[h, :, MLA_V:2 * MLA_V] = jnp.ones((kv.shape[0], MLA_V), BF16)


def mla_prep(h, cos, sa, sb, gq, gkv, wq_p, wkv):
    s = h.shape[0]
    tm = min(512, s)
    row = lambda width, col: pl.BlockSpec((tm, width), lambda i: (i, col // width))
    full = lambda a: pl.BlockSpec(a.shape, lambda i: (0,) * a.ndim)
    return pl.pallas_call(
        _mla_prep_kernel,
        out_shape=[jax.ShapeDtypeStruct((MLA_HEADS, s, MLA_QK), BF16),
                   jax.ShapeDtypeStruct((MLA_HEADS, s, MLA_QK), BF16),
                   jax.ShapeDtypeStruct((MLA_HEADS, s, 2 * MLA_V), BF16)],
        grid=(s // tm,),
        in_specs=[row(MLA_Q_RANK, H_QA), row(MLA_KV_RANK, H_KVA), row(LANE, H_KR),
                  row(LANE, 0), row(LANE, 0), row(LANE, 0),
                  full(gq), full(gkv), full(wq_p), full(wkv)],
        out_specs=[pl.BlockSpec((MLA_HEADS, tm, MLA_QK), lambda i: (0, i, 0)),
                   pl.BlockSpec((MLA_HEADS, tm, MLA_QK), lambda i: (0, i, 0)),
                   pl.BlockSpec((MLA_HEADS, tm, 2 * MLA_V), lambda i: (0, i, 0))],
        compiler_params=_params(("parallel",), 48),
        name="mla_prep",
    )(h, h, h, cos, sa, sb, gq, gkv, wq_p, wkv)


def _mla_flash_kernel(q_ref, k_ref, v_ref, o_ref, sa_ref, sb_ref, mxa_ref, mxb_ref, m_ref, acc_ref, *, t, nh):
    qi = pl.program_id(1)
    heads = range(nh)
    bufs = ((sa_ref, mxa_ref), (sb_ref, mxb_ref))

    def produce(b, parity, masked):
        s_ref, mx_ref = bufs[parity]
        start = pl.multiple_of(b * t, t)
        for h in heads:
            s = _dot_nt(q_ref[h], k_ref[h, pl.ds(start, t), :])
            if masked:
                rows = lax.broadcasted_iota(jnp.int32, s.shape, 0) + qi * t
                cols = lax.broadcasted_iota(jnp.int32, s.shape, 1) + b * t
                s = jnp.where(cols <= rows, s, NEG_BIG)
            s_ref[h] = s
            mx_ref[h] = jnp.broadcast_to(jnp.max(s, -1, keepdims=True), (t, LANE))

    def absorb(b, parity):
        s_ref, mx_ref = bufs[parity]
        start = pl.multiple_of(b * t, t)
        for h in heads:
            m_new = jnp.maximum(m_ref[h], mx_ref[h])
            alpha = jnp.exp2(m_ref[h] - m_new)
            p = jnp.exp2(s_ref[h] - jnp.concatenate([m_new] * (t // LANE), 1))
            acc_ref[h] = (jnp.concatenate([alpha, alpha], 1) * acc_ref[h]
                          + _dot(p.astype(BF16), v_ref[h, pl.ds(start, t), :]))
            m_ref[h] = m_new

    m_ref[...] = jnp.full(m_ref.shape, NEG_BIG, F32)
    acc_ref[...] = jnp.zeros_like(acc_ref)
    produce(0, 0, True)

    def pair(i, c):
        b = 2 * i
        produce(b + 1, 1, False)
        absorb(b, 0)
        produce(b + 2, 0, False)
        absorb(b + 1, 1)
        return c

    n_pairs = jnp.maximum(qi - 1, 0) // 2
    lax.fori_loop(0, n_pairs, pair, 0)
    done = 2 * n_pairs

    @pl.when(qi == 0)
    def _():
        absorb(0, 0)

    @pl.when((qi > 0) & (qi - done == 1))
    def _():
        produce(qi, 1, True)
        absorb(qi - 1, 0)
        absorb(qi, 1)

    @pl.when((qi > 0) & (qi - done == 2))
    def _():
        produce(qi - 1, 1, False)
        absorb(qi - 2, 0)
        produce(qi, 0, True)
        absorb(qi - 1, 1)
        absorb(qi, 0)

    for h in heads:
        acc = acc_ref[h]
        o_ref[:, h * MLA_V:(h + 1) * MLA_V] = (acc[:, :MLA_V] / acc[:, MLA_V:]).astype(o_ref.dtype)


def mla_flash(q, k, v1):
    _, s, _ = q.shape
    t = min(512, s)
    nh = 2
    return pl.pallas_call(
        functools.partial(_mla_flash_kernel, t=t, nh=nh),
        out_shape=jax.ShapeDtypeStruct((s, MLA_HEADS * MLA_V), BF16),
        grid=(MLA_HEADS // nh, s // t),
        in_specs=[pl.BlockSpec((nh, t, MLA_QK), lambda h, i: (h, i, 0)),
                  pl.BlockSpec((nh, s, MLA_QK), lambda h, i: (h, 0, 0)),
                  pl.BlockSpec((nh, s, 2 * MLA_V), lambda h, i: (h, 0, 0))],
        out_specs=pl.BlockSpec((t, nh * MLA_V), lambda h, i: (i, h)),
        scratch_shapes=[pltpu.VMEM((nh, t, t), F32), pltpu.VMEM((nh, t, t), F32),
                        pltpu.VMEM((nh, t, LANE), F32), pltpu.VMEM((nh, t, LANE), F32),
                        pltpu.VMEM((nh, t, LANE), F32), pltpu.VMEM((nh, t, 2 * MLA_V), F32)],
        compiler_params=_params(("parallel", "arbitrary"), 56),
        name="mla_flash",
    )(q, k, v1)


def _gla_masks():
    i = np.arange(GLA_L)[:, None]
    j = np.arange(GLA_L)[None, :]
    same = (i // GLA_SUB) == (j // GLA_SUB)
    mats = [j <= i, same & (j <= i), same & (j > i), j > i]
    for sub in range(GLA_NSUB - 1):
        mats.append((j >= (sub + 1) * GLA_SUB) & (j <= i))
    return np.concatenate(mats, 0).astype(np.float32)


GLA_CHUNKS_PER_STEP = 4


def _gla_kernel(q_ref, k_ref, v_ref, lr_ref, r_ref, wg_ref, bg_ref, gain_ref, mask_ref, o_ref, state_ref):
    @pl.when(pl.program_id(0) == 0)
    def _():
        state_ref[...] = jnp.zeros_like(state_ref)

    L, hk = GLA_L, GLA_HEADS * GLA_DK
    chunks = [slice(c * L, (c + 1) * L) for c in range(q_ref.shape[0] // L)]
    heads = [(slice(h * GLA_DK, (h + 1) * GLA_DK), slice(h * GLA_DV, (h + 1) * GLA_DV)) for h in range(GLA_HEADS)]

    z = _dot(lr_ref[...], wg_ref[...]) + bg_ref[...]
    g = (jnp.minimum(z, 0.0) - jnp.log(1.0 + jnp.exp(-jnp.abs(z)))) * (1.0 / GLA_TAU)
    g_hi = g.astype(BF16)
    g_lo = (g - g_hi.astype(F32)).astype(BF16)

    cums = [_dot(mask_ref[...], jnp.concatenate([g_hi[rs], g_lo[rs]], 0)) for rs in chunks]

    sub_of_row = lax.broadcasted_iota(jnp.int32, (L, hk), 0) // GLA_SUB
    prep = []
    for rs, cum in zip(chunks, cums):
        b_all, b_loc, sfx_loc, sfx_all = cum[0:L], cum[L:2 * L], cum[2 * L:3 * L], cum[3 * L:4 * L]
        q = q_ref[rs, :].astype(F32) * (GLA_DK ** -0.5)
        k = k_ref[rs, :].astype(F32)
        k_end = k * jnp.exp(sfx_loc)
        prep.append(dict(
            decay=jnp.exp(b_all[L - 1:L, :]),
            q_inter=(q * jnp.exp(b_all)).astype(BF16),
            q_diag=(q * jnp.exp(b_loc)).astype(BF16),
            k_diag=(k * jnp.exp(-b_loc)).astype(BF16),
            k_state=(k * jnp.exp(sfx_all)).astype(BF16),
            q_off=[(q * jnp.exp(cum[(4 + sub) * L:(5 + sub) * L])).astype(BF16) for sub in range(GLA_NSUB - 1)],
            k_off=[jnp.where(sub_of_row == sub, k_end, 0.0).astype(BF16) for sub in range(GLA_NSUB - 1)]))

    row = lax.broadcasted_iota(jnp.int32, (L, L), 0)
    col = lax.broadcasted_iota(jnp.int32, (L, L), 1)
    diag_ok = ((row // GLA_SUB) == (col // GLA_SUB)) & (col <= row)
    off_ok = (row // GLA_SUB) > (col // GLA_SUB)
    local = []
    for rs, pr in zip(chunks, prep):
        per_head = []
        for ks, vs in heads:
            v_h = v_ref[rs, vs]
            a = jnp.where(diag_ok, _dot_nt(pr["q_diag"][:, ks], pr["k_diag"][:, ks]), 0.0)
            qo = jnp.concatenate([t[:, ks] for t in pr["q_off"]], 1)
            ko = jnp.concatenate([t[:, ks] for t in pr["k_off"]], 1)
            a = a + jnp.where(off_ok, _dot_nt(qo, ko), 0.0)
            per_head.append((_dot(a.astype(BF16), v_h), _dot_tn(v_h, pr["k_state"][:, ks])))
        local.append(per_head)

    states = [state_ref[h] for h in range(GLA_HEADS)]
    for rs, pr, per_head in zip(chunks, prep, local):
        r = r_ref[rs, :].astype(F32)
        gate = r * (1.0 / (1.0 + jnp.exp(-r)))
        for h, (ks, vs) in enumerate(heads):
            o_intra, vk = per_head[h]
            o = _dot_nt(pr["q_inter"][:, ks], states[h].astype(BF16)) + o_intra
            states[h] = states[h] * pr["decay"][:, ks] + vk
            o = o * lax.rsqrt(jnp.mean(o * o, -1, keepdims=True) + RMS_EPS) * gain_ref[:, vs]
            o_ref[rs, vs] = (o * gate[:, vs]).astype(o_ref.dtype)
    for h in range(GLA_HEADS):
        state_ref[h] = states[h]


def gla(h, wg_p, bg, gain):
    s = h.shape[0]
    rows = min(GLA_CHUNKS_PER_STEP * GLA_L, s)
    masks = jnp.asarray(np.tile(_gla_masks(), (1, 2)), BF16)
    row = lambda width, col: pl.BlockSpec((rows, width), lambda i: (i, col // width))
    full = lambda a: pl.BlockSpec(a.shape, lambda i: (0,) * a.ndim)
    hk, hv = GLA_HEADS * GLA_DK, GLA_HEADS * GLA_DV
    return pl.pallas_call(
        _gla_kernel,
        out_shape=jax.ShapeDtypeStruct((s, hv), BF16),
        grid=(s // rows,),
        in_specs=[row(hk, H_GQ), row(hk, H_GK), row(hv, H_GV), row(LANE, H_GLR), row(hv, H_GR),
                  full(wg_p), full(bg), full(gain), full(masks)],
        out_specs=pl.BlockSpec((rows, hv), lambda i: (i, 0)),
        scratch_shapes=[pltpu.VMEM((GLA_HEADS, GLA_DV, GLA_DK), F32)],
        compiler_params=_params(("arbitrary",), 32),
        name="gla",
    )(h, h, h, h, h, wg_p, bg, gain, masks)


def _t5_bucket_table():
    L = SWA_BLOCK
    dist = np.arange(L)[:, None] + L - np.arange(2 * L)[None, :]
    d = np.clip(dist, 0, None)
    max_exact = REL_BUCKETS // 2
    df = np.maximum(d, 1).astype(np.float32)
    large = max_exact + (np.log(df / np.float32(max_exact)) / np.float32(math.log(REL_MAX_DIST / max_exact))
                         * np.float32(REL_BUCKETS - max_exact)).astype(np.int32)
    large = np.minimum(large, REL_BUCKETS - 1)
    bucket = np.where(d < max_exact, d, large)
    in_window = (dist >= 0) & (dist < SWA_WINDOW)
    return np.where(in_window, bucket, -1).astype(np.int32)


def _swa_kernel(relb_ref, sink_ref, q_ref, kp_ref, kc_ref, vp_ref, vc_ref, bucket_ref, o_ref, bias_ref):
    i = pl.program_id(0)
    L = SWA_BLOCK

    @pl.when(i == 0)
    def _():
        bucket = bucket_ref[...]
        for h in range(SWA_HEADS):
            acc = jnp.full(bucket.shape, NEG_BIG, F32)
            for b in range(REL_BUCKETS):
                acc = jnp.where(bucket == b, relb_ref[b, h], acc)
            bias_ref[h] = acc

    kcat = jnp.concatenate([kp_ref[...], kc_ref[...]], 0)
    vcat = jnp.concatenate([vp_ref[...], vc_ref[...]], 0)
    q = q_ref[...]
    col = lax.broadcasted_iota(jnp.int32, (L, 2 * L), 1)
    real_key = (col >= L) | (i > 0)
    g = SWA_HEADS // SWA_KV_HEADS
    scale = SWA_HD ** -0.5
    for h in range(SWA_HEADS):
        kv = h // g
        hs = slice(h * SWA_HD, (h + 1) * SWA_HD)
        kvs = slice(kv * SWA_HD, (kv + 1) * SWA_HD)
        s = _dot_nt(q[:, hs], kcat[:, kvs]) * scale + bias_ref[h]
        s = jnp.where(real_key, s, NEG_BIG)
        sink = sink_ref[h]
        m = jnp.maximum(jnp.max(s, -1, keepdims=True), sink)
        p = jnp.exp(s - m)
        denom = jnp.sum(p, -1, keepdims=True) + jnp.exp(sink - m)
        o = _dot((p / denom).astype(BF16), vcat[:, kvs])
        o_ref[:, hs] = o.astype(o_ref.dtype)


def swa(h, sinks, rel_bias):
    s = h.shape[0]
    L = SWA_BLOCK
    bucket = jnp.asarray(_t5_bucket_table())
    kvw = SWA_KV_HEADS * SWA_HD
    hw = SWA_HEADS * SWA_HD
    cur = lambda width, col: pl.BlockSpec((L, width), lambda i, *_: (i, col // width))
    prev = lambda width, col: pl.BlockSpec((L, width), lambda i, *_: (jnp.maximum(i - 1, 0), col // width))
    return pl.pallas_call(
        _swa_kernel,
        out_shape=jax.ShapeDtypeStruct((s, hw), BF16),
        grid_spec=pltpu.PrefetchScalarGridSpec(
            num_scalar_prefetch=2,
            grid=(s // L,),
            in_specs=[cur(hw, H_SQ), prev(kvw, H_SK), cur(kvw, H_SK), prev(kvw, H_SV), cur(kvw, H_SV),
                      pl.BlockSpec(bucket.shape, lambda i, *_: (0, 0))],
            out_specs=pl.BlockSpec((L, hw), lambda i, *_: (i, 0)),
            scratch_shapes=[pltpu.VMEM((SWA_HEADS, L, 2 * L), F32)],
        ),
        compiler_params=_params(("arbitrary",), 32),
        name="swa",
    )(rel_bias, sinks, h, h, h, h, h, bucket)


def _layer_norm(y, g, b):
    mu = jnp.mean(y, -1, keepdims=True)
    yc = y - mu
    var = jnp.mean(yc * yc, -1, keepdims=True)
    return yc * lax.rsqrt(var + LN_EPS) * g + b


HALF = D_MODEL // 2
HIGH16 = 0xFFFF0000


SUBLANE = 8
TOK = HALF // LANE


def _store_token_tiles(ref, x):
    tm = x.shape[0]
    lo = lax.bitcast_convert_type(x[:, :HALF].astype(BF16).astype(F32), jnp.uint32) >> 16
    hi = lax.bitcast_convert_type(x[:, HALF:].astype(BF16).astype(F32), jnp.uint32) & jnp.uint32(HIGH16)
    packed = lo | hi
    for s in range(TOK):
        ref[pl.ds(s, tm, stride=TOK), :] = packed[:, s * LANE:(s + 1) * LANE]


def _load_token_tiles(ref, rows=None):
    first, tm = rows if rows is not None else (0, ref.shape[0] // TOK)
    lo, hi = [], []
    for s in range(TOK):
        w = ref[pl.ds(first * TOK + s, tm, stride=TOK), :]
        lo.append(lax.bitcast_convert_type(w << 16, F32))
        hi.append(lax.bitcast_convert_type(w & jnp.uint32(HIGH16), F32))
    return jnp.concatenate(lo, 1), jnp.concatenate(hi, 1)


ROW_SPLIT = 2


def _out_ln_kernel(a_ref, b_ref, c_ref, x_ref, w_ref, g_ref, beta_ref, o_ref, op_ref):
    na, nb = a_ref.shape[1], b_ref.shape[1]
    nr = x_ref.shape[0] // ROW_SPLIT
    for r0 in range(0, x_ref.shape[0], nr):
        rows = slice(r0, r0 + nr)
        m = _dot(a_ref[rows, :], w_ref[0:na, :])
        m = m + _dot(b_ref[rows, :], w_ref[na:na + nb, :])
        m = m + _dot(c_ref[rows, :], w_ref[na + nb:, :])
        y = _layer_norm(DEEPNORM_ALPHA * x_ref[rows, :] + m, g_ref[...], beta_ref[...])
        o_ref[rows, :] = y
        _store_token_tiles(op_ref.at[pl.ds(r0 * TOK, nr * TOK)], y)


def out_ln(a, b, c, x, w, g, beta, layer):
    s = x.shape[0]
    tm = min(512, s)
    row = lambda arr: pl.BlockSpec((tm, arr.shape[1]), lambda i: (i, 0))
    full = lambda arr: _layer_block(arr, layer)
    return pl.pallas_call(
        _out_ln_kernel,
        out_shape=[jax.ShapeDtypeStruct((s, D_MODEL), F32), jax.ShapeDtypeStruct((s * TOK, LANE), jnp.uint32)],
        grid=(s // tm,),
        in_specs=[row(a), row(b), row(c), row(x), full(w), full(g), full(beta)],
        out_specs=[pl.BlockSpec((tm, D_MODEL), lambda i: (i, 0)), pl.BlockSpec((tm * TOK, LANE), lambda i: (i, 0))],
        compiler_params=_params(("parallel",), 56),
        name="out_ln",
    )(a, b, c, x, w, g, beta)


FFN_ROWS = 1024
FFN_COLS = 512
FFN_OUT_CHUNK = 512
FFN_SUB = 256


def _ffn_kernel(te_ref, tv_ref, na_ref, xp_ref, wg_ref, wu_ref, wd_ref, o_ref, xb_ref, acc_ref):
    i, j = pl.program_id(0), pl.program_id(1)
    tm = xb_ref.shape[0]
    valid = tv_ref[i]

    @pl.when(j == 0)
    def _():
        acc_ref[...] = jnp.zeros_like(acc_ref)

    @pl.when((j == 0) & (valid > 0))
    def _():
        lo, hi = _load_token_tiles(xp_ref)
        xb_ref[:, :HALF] = lo.astype(BF16)
        xb_ref[:, HALF:] = hi.astype(BF16)

    def rows_step(r0, nr):
        xb = xb_ref[r0:r0 + nr, :]
        gate = _dot(xb, wg_ref[...].astype(BF16))
        up = _dot(xb, wu_ref[...].astype(BF16))
        hmid = (gate * (1.0 / (1.0 + jnp.exp(-gate))) * up).astype(BF16)
        for c in range(0, D_MODEL, FFN_OUT_CHUNK):
            cs = slice(c, c + FFN_OUT_CHUNK)
            acc_ref[r0:r0 + nr, cs] += _dot(hmid, wd_ref[:, cs].astype(BF16))

    nearly_full = valid > tm - FFN_SUB

    @pl.when(nearly_full)
    def _():
        rows_step(0, tm)

    for r0 in range(0, tm - FFN_SUB, FFN_SUB):
        @pl.when(jnp.logical_not(nearly_full) & (valid > r0))
        def _():
            rows_step(r0, FFN_SUB)

    @pl.when(j == pl.num_programs(1) - 1)
    def _():
        _store_token_tiles(o_ref, acc_ref[...])


def grouped_swiglu(xp, wg, wu, wd, tile_expert, tile_valid, n_active, tm, tf):
    n_tiles, nf = xp.shape[0] // (tm * TOK), D_FF // tf

    def tile(i, na):
        return jnp.minimum(i, na[0] - 1)

    def fcol(i, j, na):
        return jnp.where(i < na[0], j, nf - 1)

    return pl.pallas_call(
        _ffn_kernel,
        out_shape=jax.ShapeDtypeStruct(xp.shape, jnp.uint32),
        grid_spec=pltpu.PrefetchScalarGridSpec(
            num_scalar_prefetch=3,
            grid=(n_tiles, nf),
            in_specs=[pl.BlockSpec((tm * TOK, LANE), lambda i, j, te, tv, na: (tile(i, na), 0)),
                      pl.BlockSpec((None, D_MODEL, tf),
                                   lambda i, j, te, tv, na: (te[tile(i, na)], 0, fcol(i, j, na))),
                      pl.BlockSpec((None, D_MODEL, tf),
                                   lambda i, j, te, tv, na: (te[tile(i, na)], 0, fcol(i, j, na))),
                      pl.BlockSpec((None, tf, D_MODEL),
                                   lambda i, j, te, tv, na: (te[tile(i, na)], fcol(i, j, na), 0))],
            out_specs=pl.BlockSpec((tm * TOK, LANE), lambda i, j, te, tv, na: (i, 0)),
            scratch_shapes=[pltpu.VMEM((tm, D_MODEL), BF16), pltpu.VMEM((tm, D_MODEL), F32)],
        ),
        compiler_params=_params(("arbitrary", "arbitrary"), 60),
        name="grouped_swiglu",
    )(tile_expert, tile_valid, n_active, xp, wg, wu, wd)


def _router_kernel(x_ref, w_ref, e_ref, r_ref, wt_ref, cnt_ref, run_ref):
    i = pl.program_id(0)
    tm = x_ref.shape[0]
    ne = N_EXPERTS

    @pl.when(i == 0)
    def _():
        run_ref[...] = jnp.zeros_like(run_ref)

    x = x_ref[...]
    x_hi = x.astype(BF16)
    x_lo = (x - x_hi.astype(F32)).astype(BF16)
    w = w_ref[...]
    w_hi = w.astype(BF16).astype(F32)
    w_lo = w - w_hi
    both = _dot_nt(jnp.concatenate([w_hi, w_lo], 0).astype(BF16), x_hi)
    cross = _dot_nt(jnp.concatenate([w_hi, jnp.zeros_like(w_hi)], 0).astype(BF16), x_lo)
    logits = both[0:ne] + both[ne:2 * ne] + cross[0:ne]

    eidx = lax.broadcasted_iota(jnp.int32, logits.shape, 0).astype(F32)
    v1 = jnp.max(logits, 0, keepdims=True)
    i1 = jnp.min(jnp.where(logits == v1, eidx, float(ne)), 0, keepdims=True)
    rest = jnp.where(eidx == i1, -jnp.inf, logits)
    v2 = jnp.max(rest, 0, keepdims=True)
    i2 = jnp.min(jnp.where(rest == v2, eidx, float(ne)), 0, keepdims=True)
    t = jnp.exp(v2 - v1)
    w1 = 1.0 / (1.0 + t)
    wt_ref[0:1, :] = w1
    wt_ref[1:2, :] = t * w1
    e_ref[0:1, :] = i1.astype(jnp.int32)
    e_ref[1:2, :] = i2.astype(jnp.int32)

    sel1, sel2 = eidx == i1, eidx == i2
    sel = jnp.where(sel1, 1.0, 0.0) + jnp.where(sel2, 1.0, 0.0)
    before = (lax.broadcasted_iota(jnp.int32, (tm, tm), 0) < lax.broadcasted_iota(jnp.int32, (tm, tm), 1))
    sel16 = jnp.concatenate([sel, jnp.zeros_like(sel)], 0).astype(BF16)
    prefix = _dot(sel16, jnp.where(before, 1.0, 0.0).astype(BF16))[0:ne]
    rank = prefix + run_ref[:, 0:1]
    r_ref[0:1, :] = jnp.sum(jnp.where(sel1, rank, 0.0), 0, keepdims=True).astype(jnp.int32)
    r_ref[1:2, :] = jnp.sum(jnp.where(sel2, rank, 0.0), 0, keepdims=True).astype(jnp.int32)
    run_ref[...] = run_ref[...] + jnp.sum(sel, 1, keepdims=True)
    cnt_ref[...] = run_ref[...].astype(jnp.int32)


def route_tokens(x, w_router_t):
    s = x.shape[0]
    tm = min(512, s)
    pair = pl.BlockSpec((2, tm), lambda i: (0, i))
    return pl.pallas_call(
        _router_kernel,
        out_shape=[jax.ShapeDtypeStruct((2, s), jnp.int32), jax.ShapeDtypeStruct((2, s), jnp.int32),
                   jax.ShapeDtypeStruct((2, s), F32), jax.ShapeDtypeStruct((N_EXPERTS, LANE), jnp.int32)],
        grid=(s // tm,),
        in_specs=[pl.BlockSpec((tm, D_MODEL), lambda i: (i, 0)),
                  pl.BlockSpec((N_EXPERTS, D_MODEL), lambda i: (0, 0))],
        out_specs=[pair, pair, pair, pl.BlockSpec((N_EXPERTS, LANE), lambda i: (0, 0))],
        scratch_shapes=[pltpu.VMEM((N_EXPERTS, LANE), F32)],
        compiler_params=_params(("arbitrary",), 32),
        name="moe_router",
    )(x, w_router_t)


def _slot_kernel(base_ref, e_ref, r_ref, s_ref):
    e = e_ref[...]
    slot = r_ref[...]
    for k in range(N_EXPERTS):
        slot = slot + jnp.where(e == k, base_ref[k], 0)
    s_ref[...] = slot


def token_slots(e_idx, rank, base):
    whole = pl.BlockSpec(e_idx.shape, lambda i, *_: (0, 0))
    return pl.pallas_call(
        _slot_kernel,
        out_shape=jax.ShapeDtypeStruct(e_idx.shape, jnp.int32),
        grid_spec=pltpu.PrefetchScalarGridSpec(num_scalar_prefetch=1, grid=(1,), in_specs=[whole, whole],
                                               out_specs=whole),
        compiler_params=_params(("arbitrary",), 32),
        name="token_slots",
    )(base, e_idx, rank)


DMA_UNROLL = 8


def _tile_rows(index):
    return pl.ds(pl.multiple_of(index * TOK, TOK), TOK)


def _dispatch_kernel(fill_ref, slot_ref, x_ref, xs_ref, zero_ref, sem, zsem):
    i = pl.program_id(0)
    tm = x_ref.shape[0] // TOK

    def issue(b, c):
        for u in range(DMA_UNROLL):
            t = b * DMA_UNROLL + u
            for k in range(2):
                pltpu.make_async_copy(x_ref.at[_tile_rows(t)], xs_ref.at[_tile_rows(slot_ref[k, t])],
                                      sem).start(priority=k)
        return c

    lax.fori_loop(0, tm // DMA_UNROLL, issue, 0)
    for k in range(2):
        pltpu.make_async_copy(x_ref, xs_ref.at[pl.ds(0, tm * TOK)], sem).wait()

    @pl.when(i == pl.num_programs(0) - 1)
    def _():
        zero_ref[...] = jnp.zeros_like(zero_ref)

        def zero_copy(slot):
            return pltpu.make_async_copy(zero_ref.at[pl.ds(0, TOK)], xs_ref.at[_tile_rows(slot)], zsem)

        for e in range(N_EXPERTS):
            lo, hi = fill_ref[0, e], fill_ref[1, e]

            def zissue(slot, c):
                zero_copy(slot).start()
                return c

            def zdrain(slot, c):
                zero_copy(slot).wait()
                return c

            lax.fori_loop(lo, hi, zissue, 0)
            lax.fori_loop(lo, hi, zdrain, 0)

        zrows = zero_ref.shape[0]
        zb = zrows // TOK

        def block_copy(b):
            return pltpu.make_async_copy(zero_ref, xs_ref.at[pl.ds(pl.multiple_of(b * zrows, zrows), zrows)], zsem)

        def bissue(b, c):
            block_copy(b).start()
            return c

        def bdrain(b, c):
            block_copy(b).wait()
            return c

        first, last = fill_ref[2, 0] // zb, xs_ref.shape[0] // zrows
        lax.fori_loop(first, last, bissue, 0)
        lax.fori_loop(first, last, bdrain, 0)


def moe_dispatch(xp, slots, fill, rows):
    s = xp.shape[0] // TOK
    tm = min(512, s)
    zb = min(256, s)
    return pl.pallas_call(
        _dispatch_kernel,
        out_shape=jax.ShapeDtypeStruct((rows * TOK, LANE), xp.dtype),
        grid_spec=pltpu.PrefetchScalarGridSpec(
            num_scalar_prefetch=1,
            grid=(s // tm,),
            in_specs=[pl.BlockSpec((2, tm), lambda i, *_: (0, i), memory_space=pltpu.SMEM),
                      pl.BlockSpec((tm * TOK, LANE), lambda i, *_: (i, 0))],
            out_specs=pl.BlockSpec(memory_space=pl.ANY),
            scratch_shapes=[pltpu.VMEM((zb * TOK, LANE), xp.dtype), pltpu.SemaphoreType.DMA,
                            pltpu.SemaphoreType.DMA],
        ),
        compiler_params=_params(("arbitrary",), 32),
        name="moe_dispatch",
    )(fill, slots, xp)


def _ple(x, p, wup_ref, wgate_ref, bgate_ref):
    up = _dot(p.astype(BF16), wup_ref[...])
    zg = _dot(x.astype(BF16), wgate_ref[...]) + bgate_ref[...]
    return up * (1.0 / (1.0 + jnp.exp(-zg)))


def _ple_ln_dense_kernel(x_ref, f_ref, p_ref, wup_ref, wgate_ref, bgate_ref, g_ref, beta_ref, o_ref):
    nr = x_ref.shape[0] // ROW_SPLIT
    for r0 in range(0, x_ref.shape[0], nr):
        rows = slice(r0, r0 + nr)
        x = x_ref[rows, :]
        f = jnp.concatenate(_load_token_tiles(f_ref, (r0, nr)), 1)
        ple = _ple(x, p_ref[rows, :], wup_ref, wgate_ref, bgate_ref)
        o_ref[rows, :] = _layer_norm(DEEPNORM_ALPHA * x + f + ple, g_ref[...], beta_ref[...])


def _ple_ln_moe_kernel(slot_ref, x_ref, wt_ref, p_ref, wup_ref, wgate_ref, bgate_ref, g_ref, beta_ref, ys_ref,
                       o_ref, y1_ref, y2_ref, sem):
    tm = x_ref.shape[0]
    bufs = (y1_ref, y2_ref)

    def issue(b, c):
        for u in range(DMA_UNROLL):
            t = b * DMA_UNROLL + u
            for k in range(2):
                pltpu.make_async_copy(ys_ref.at[_tile_rows(slot_ref[k, t])], bufs[k].at[_tile_rows(t)],
                                      sem).start(priority=k)
        return c

    lax.fori_loop(0, tm // DMA_UNROLL, issue, 0)
    nr = tm // ROW_SPLIT
    ples = [_ple(x_ref[r0:r0 + nr, :], p_ref[r0:r0 + nr, :], wup_ref, wgate_ref, bgate_ref)
            for r0 in range(0, tm, nr)]
    for k in range(2):
        pltpu.make_async_copy(ys_ref.at[pl.ds(0, tm * TOK)], bufs[k], sem).wait()
    for i, r0 in enumerate(range(0, tm, nr)):
        rows = slice(r0, r0 + nr)
        wt = wt_ref[rows, :]
        f = (wt[:, 0:1] * jnp.concatenate(_load_token_tiles(y1_ref, (r0, nr)), 1)
             + wt[:, 1:2] * jnp.concatenate(_load_token_tiles(y2_ref, (r0, nr)), 1))
        o_ref[rows, :] = _layer_norm(DEEPNORM_ALPHA * x_ref[rows, :] + f + ples[i], g_ref[...], beta_ref[...])


def ple_ln(x, p, wup, wgate, bgate, g, beta, layer, f=None, moe=None):
    s = x.shape[0]
    tm = min(512, s)
    row = lambda arr: pl.BlockSpec((tm, arr.shape[1]), lambda i, *_: (i, 0))
    full = lambda arr: _layer_block(arr, layer)
    p_spec = pl.BlockSpec((None, None, tm, p.shape[-1]), lambda i, *_: (layer, 0, i, 0))
    tail = [p_spec, full(wup), full(wgate), full(bgate), full(g), full(beta)]
    out_spec = pl.BlockSpec((tm, D_MODEL), lambda i, *_: (i, 0))
    out_shape = jax.ShapeDtypeStruct((s, D_MODEL), F32)
    if moe is None:
        return pl.pallas_call(
            _ple_ln_dense_kernel, out_shape=out_shape, grid=(s // tm,),
            in_specs=[row(x), pl.BlockSpec((tm * TOK, LANE), lambda i: (i, 0))] + tail, out_specs=out_spec,
            compiler_params=_params(("parallel",), 56), name="ple_ln_dense",
        )(x, f, p, wup, wgate, bgate, g, beta)
    ys, slots, wts_t = moe
    return pl.pallas_call(
        _ple_ln_moe_kernel,
        out_shape=out_shape,
        grid_spec=pltpu.PrefetchScalarGridSpec(
            num_scalar_prefetch=0,
            grid=(s // tm,),
            in_specs=[pl.BlockSpec((2, tm), lambda i: (0, i), memory_space=pltpu.SMEM), row(x), row(wts_t)] + tail
            + [pl.BlockSpec(memory_space=pl.ANY)],
            out_specs=out_spec,
            scratch_shapes=[pltpu.VMEM((tm * TOK, LANE), jnp.uint32), pltpu.VMEM((tm * TOK, LANE), jnp.uint32),
                            pltpu.SemaphoreType.DMA],
        ),
        compiler_params=_params(("arbitrary",), 56),
        name="ple_ln_moe",
    )(slots, x, wts_t, p, wup, wgate, bgate, g, beta, ys)


def _pad_cols(w, width):
    return jnp.pad(w, ((0, 0),) * (w.ndim - 1) + ((0, width - w.shape[-1]),))


def _pack_w_in(w):
    sizes = (MLA_Q_RANK, MLA_KV_RANK, MLA_ROPE, GLA_HEADS * GLA_DK, GLA_HEADS * GLA_DK, GLA_HEADS * GLA_DV,
             GLA_GATE_RANK, GLA_HEADS * GLA_DV, SWA_HEADS * SWA_HD, SWA_KV_HEADS * SWA_HD, SWA_KV_HEADS * SWA_HD)
    splits = np.cumsum(sizes)[:-1]
    q_a, kv_a, k_rope, g_q, g_k, g_v, g_lr, g_r, s_q, s_k, s_v = jnp.split(w, splits, axis=-1)
    pieces = [q_a, g_v, g_r, s_q, kv_a, g_q, g_k, _pad_cols(k_rope, LANE), _pad_cols(g_lr, LANE), s_k, s_v]
    return jnp.concatenate(pieces, -1).astype(BF16)


def _pack_w_q_b(w):
    w = w.reshape(MLA_Q_RANK, MLA_HEADS, MLA_NOPE + MLA_ROPE)
    w = jnp.pad(w, ((0, 0), (0, 0), (0, MLA_QK - MLA_NOPE - MLA_ROPE)))
    return w.reshape(MLA_Q_RANK, MLA_HEADS * MLA_QK).astype(BF16)


def _token_mixer_ln(x, rope, layer, w_in_p, q_gain, w_q_b, kv_gain, w_kv_b, gla_w, gla_b, gla_gain, sinks,
                    rel_bias, w_out_b, ln_g, ln_b):
    h = proj_in(x, w_in_p, layer)
    q, k, v = mla_prep(h, *rope, q_gain.reshape(1, -1), kv_gain.reshape(1, -1), _pack_w_q_b(w_q_b),
                       w_kv_b.astype(BF16))
    a = mla_flash(q, k, v)
    gla_w_p = jnp.pad(gla_w, ((0, LANE - GLA_GATE_RANK), (0, 0))).astype(BF16)
    b = gla(h, gla_w_p, gla_b.reshape(1, -1), gla_gain.reshape(1, -1))
    c = swa(h, sinks, rel_bias)
    return out_ln(a, b, c, x, w_out_b, ln_g, ln_b, layer)


def _moe_plan(counts, tm, n_tiles):
    tiles = (counts + tm - 1) // tm
    ends = jnp.cumsum(tiles)
    base = (ends - tiles) * tm
    n_active = ends[-1:].astype(jnp.int32)
    tile_expert = jnp.searchsorted(ends, jnp.arange(n_tiles, dtype=jnp.int32), side="right")
    tile_expert = jnp.minimum(tile_expert, N_EXPERTS - 1).astype(jnp.int32)
    tile_start = jnp.arange(n_tiles, dtype=jnp.int32) * tm
    tile_valid = jnp.clip((base + counts)[tile_expert] - tile_start, 0, tm).astype(jnp.int32)
    used_rows = jnp.broadcast_to(ends[-1] * tm, counts.shape)
    fill = jnp.stack([base + counts, base + tiles * tm, used_rows]).astype(jnp.int32)
    return base.astype(jnp.int32), fill, tile_expert, tile_valid, n_active


def kernel(x, p, positions, w_in, mla_q_a_gain, mla_w_q_b, mla_kv_a_gain, mla_w_kv_b, gla_w_gate, gla_b_gate,
           gla_norm_gain, swa_sinks, rel_bias, w_out, ln1_g, ln1_b, ffn_w_gate, ffn_w_up, ffn_w_down,
           moe_router, moe_w_gate, moe_w_up, moe_w_down, ple_w_up, ple_w_gate, ple_b_gate, ln2_g, ln2_b):
    batch, s, _ = x.shape
    assert batch == 1
    xcur = x.reshape(s, D_MODEL)
    rope = rope_tables(positions)
    tm = min(FFN_ROWS, s)
    row_stack = lambda v: v.reshape(DEPTH, 1, -1)
    w_in_p = _pack_w_in(w_in)
    w_out_b = w_out.astype(BF16)
    tail = (p, ple_w_up.astype(BF16), ple_w_gate.astype(BF16), row_stack(ple_b_gate), row_stack(ln2_g),
            row_stack(ln2_b))
    for i in range(DEPTH):
        x1, x1p = _token_mixer_ln(xcur, rope, i, w_in_p, mla_q_a_gain[i], mla_w_q_b[i], mla_kv_a_gain[i],
                                  mla_w_kv_b[i], gla_w_gate[i], gla_b_gate[i], gla_norm_gain[i], swa_sinks[i],
                                  rel_bias, w_out_b, row_stack(ln1_g), row_stack(ln1_b))
        j = i // 2
        if i % 2 == 0:
            n_tiles = s // tm
            f = grouped_swiglu(x1p, ffn_w_gate[j][None], ffn_w_up[j][None], ffn_w_down[j][None],
                               jnp.zeros((n_tiles,), jnp.int32), jnp.full((n_tiles,), tm, jnp.int32),
                               jnp.full((1,), n_tiles, jnp.int32), tm, FFN_COLS)
            xcur = ple_ln(x1, *tail, i, f=f)
        else:
            n_tiles = (2 * s + N_EXPERTS * (tm - 1)) // tm
            e_idx, rank, wts, counts = route_tokens(x1, moe_router[j].T)
            base, fill, tile_expert, tile_valid, n_active = _moe_plan(counts[:, 0], tm, n_tiles)
            slots = token_slots(e_idx, rank, base)
            xs = moe_dispatch(x1p, slots, fill, n_tiles * tm)
            ys = grouped_swiglu(xs, moe_w_gate[j], moe_w_up[j], moe_w_down[j], tile_expert, tile_valid, n_active,
                                tm, FFN_COLS)
            xcur = ple_ln(x1, *tail, i, moe=(ys, slots, wts.T))
    return xcur.reshape(batch, s, D_MODEL)
```

```python
import functools
import math

import numpy as np
import jax
import jax.numpy as jnp
from jax import lax
from jax.experimental import pallas as pl
from jax.experimental.pallas import tpu as pltpu

F32 = jnp.float32
BF16 = jnp.bfloat16

D_MODEL = 2048
DEPTH = 2
MLA_HEADS = 8
MLA_Q_RANK = 512
MLA_KV_RANK = 256
MLA_NOPE = 128
MLA_ROPE = 64
MLA_V = 128
ROPE_THETA = 10000.0
GLA_HEADS = 4
GLA_DK = 64
GLA_DV = 128
GLA_GATE_RANK = 16
GLA_TAU = 16.0
SWA_HEADS = 8
SWA_KV_HEADS = 2
SWA_HD = 64
SWA_WINDOW = 128
SWA_BLOCK = 128
REL_BUCKETS = 32
REL_MAX_DIST = 128
D_FF = 5632
N_EXPERTS = 8
PLE_DIM = 256
LN_EPS = 1e-5
RMS_EPS = 1e-6
DEEPNORM_ALPHA = (2 * DEPTH) ** 0.25

LANE = 128
LOG2E = math.log2(math.e)
NEG_BIG = -1e30

H_QA, H_GV, H_GR, H_SQ = 0, 512, 1024, 1536
H_KVA, H_GQ, H_GK = 2048, 2304, 2560
H_KR, H_GLR, H_SK, H_SV = 2816, 2944, 3072, 3200
H_COLS = 3328

MLA_QK = 2 * LANE

GLA_L = 128
GLA_SUB = 32
GLA_NSUB = GLA_L // GLA_SUB


def _params(sem, vmem_mb):
    return pltpu.CompilerParams(dimension_semantics=sem, vmem_limit_bytes=vmem_mb * 2 ** 20)


def _dot(a, b):
    return jnp.dot(a, b, preferred_element_type=F32)


def _dot_nt(a, b):
    return lax.dot_general(a, b, (((1,), (1,)), ((), ())), preferred_element_type=F32)


def _dot_tn(a, b):
    return lax.dot_general(a, b, (((0,), (0,)), ((), ())), preferred_element_type=F32)


def _proj_in_kernel(x_ref, w_ref, o_ref, xb_ref):
    @pl.when(pl.program_id(1) == 0)
    def _():
        xb_ref[...] = x_ref[...].astype(BF16)

    o_ref[...] = _dot(xb_ref[...], w_ref[...]).astype(o_ref.dtype)


def _layer_block(arr, layer):
    zeros = (0,) * (arr.ndim - 1)
    return pl.BlockSpec((None,) + arr.shape[1:], lambda i, *_: (layer,) + zeros, pipeline_mode=pl.Buffered(1))


def proj_in(x, w_p, layer):
    s = x.shape[0]
    tm = min(1024, s)
    tn = H_COLS // 2
    return pl.pallas_call(
        _proj_in_kernel,
        out_shape=jax.ShapeDtypeStruct((s, H_COLS), BF16),
        grid=(s // tm, H_COLS // tn),
        in_specs=[pl.BlockSpec((tm, D_MODEL), lambda i, j: (i, 0)),
                  pl.BlockSpec((None, D_MODEL, tn), lambda i, j: (layer, 0, j))],
        out_specs=pl.BlockSpec((tm, tn), lambda i, j: (i, j)),
        scratch_shapes=[pltpu.VMEM((tm, D_MODEL), BF16)],
        compiler_params=_params(("parallel", "arbitrary"), 56),
        name="proj_in",
    )(x, w_p)


def _rope_table_kernel(pos_ref, inv_ref, cos_ref, sa_ref, sb_ref):
    ang = pos_ref[...].astype(F32) * inv_ref[...]
    lane = lax.broadcasted_iota(jnp.int32, ang.shape, 1)
    half = MLA_ROPE // 2
    c, s = jnp.cos(ang), jnp.sin(ang)
    cos_ref[...] = c
    sa_ref[...] = jnp.where((lane >= half) & (lane < 2 * half), s, 0.0)
    sb_ref[...] = jnp.where(lane < half, -s, 0.0)


def rope_tables(positions):
    s = positions.shape[-1]
    half = MLA_ROPE // 2
    inv = ROPE_THETA ** (-jnp.arange(half, dtype=F32) / half)
    inv = jnp.concatenate([inv, inv, jnp.zeros((LANE - 2 * half,), F32)]).reshape(1, LANE)
    tm = min(1024, s)
    spec = pl.BlockSpec((tm, LANE), lambda i: (i, 0))
    return pl.pallas_call(
        _rope_table_kernel,
        out_shape=[jax.ShapeDtypeStruct((s, LANE), F32)] * 3,
        grid=(s // tm,),
        in_specs=[pl.BlockSpec((tm, 1), lambda i: (i, 0)), pl.BlockSpec((1, LANE), lambda i: (0, 0))],
        out_specs=[spec, spec, spec],
        compiler_params=_params(("parallel",), 32),
        name="rope_tables",
    )(positions.reshape(s, 1), inv)


def _rope(x, cos, sa, sb):
    return x * cos + pltpu.roll(x, MLA_ROPE // 2, 1) * sa + pltpu.roll(x, LANE - MLA_ROPE // 2, 1) * sb


def _mla_prep_kernel(qa_ref, kva_ref, kr_ref, cos_ref, sa_ref, sb_ref, gq_ref, gkv_ref, wq_ref, wkv_ref,
                     q_out, k_out, v_out):
    cos, sa, sb = cos_ref[...], sa_ref[...], sb_ref[...]
    qscale = (MLA_NOPE + MLA_ROPE) ** -0.5 * LOG2E

    qa = qa_ref[...].astype(F32)
    qn = qa * lax.rsqrt(jnp.mean(qa * qa, -1, keepdims=True) + RMS_EPS) * gq_ref[...]
    q = _dot(qn.astype(BF16), wq_ref[...])
    for h in range(MLA_HEADS):
        c0 = h * MLA_QK
        q_out[h, :, 0:LANE] = (q[:, c0:c0 + LANE] * qscale).astype(BF16)
        pe = _rope(q[:, c0 + LANE:c0 + 2 * LANE], cos, sa, sb)
        q_out[h, :, LANE:2 * LANE] = (pe * qscale).astype(BF16)

    kva = kva_ref[...].astype(F32)
    kvn = kva * lax.rsqrt(jnp.mean(kva * kva, -1, keepdims=True) + RMS_EPS) * gkv_ref[...]
    kv = _dot(kvn.astype(BF16), wkv_ref[...])
    kpe = _rope(kr_ref[...].astype(F32), cos, sa, sb).astype(BF16)
    for h in range(MLA_HEADS):
        c0 = h * (MLA_NOPE + MLA_V)
        k_out[h, :, 0:LANE] = kv[:, c0:c0 + MLA_NOPE].astype(BF16)
        k_out[h, :, LANE:2 * LANE] = kpe
        v_out[h, :, 0:MLA_V] = kv[:, c0 + MLA_NOPE:c0 + MLA_NOPE + MLA_V].astype(BF16)
        v_out[h, :, MLA_V:2 * MLA_V] = jnp.ones((kv.shape[0], MLA_V), BF16)


def mla_prep(h, cos, sa, sb, gq, gkv, wq_p, wkv):
    s = h.shape[0]
    tm = min(512, s)
    row = lambda width, col: pl.BlockSpec((tm, width), lambda i: (i, col // width))
    full = lambda a: pl.BlockSpec(a.shape, lambda i: (0,) * a.ndim)
    return pl.pallas_call(
        _mla_prep_kernel,
        out_shape=[jax.ShapeDtypeStruct((MLA_HEADS, s, MLA_QK), BF16),
                   jax.ShapeDtypeStruct((MLA_HEADS, s, MLA_QK), BF16),
                   jax.ShapeDtypeStruct((MLA_HEADS, s, 2 * MLA_V), BF16)],
        grid=(s // tm,),
        in_specs=[row(MLA_Q_RANK, H_QA), row(MLA_KV_RANK, H_KVA), row(LANE, H_KR),
                  row(LANE, 0), row(LANE, 0), row(LANE, 0),
                  full(gq), full(gkv), full(wq_p), full(wkv)],
        out_specs=[pl.BlockSpec((MLA_HEADS, tm, MLA_QK), lambda i: (0, i, 0)),
                   pl.BlockSpec((MLA_HEADS, tm, MLA_QK), lambda i: (0, i, 0)),
                   pl.BlockSpec((MLA_HEADS, tm, 2 * MLA_V), lambda i: (0, i, 0))],
        compiler_params=_params(("parallel",), 48),
        name="mla_prep",
    )(h, h, h, cos, sa, sb, gq, gkv, wq_p, wkv)


def _mla_flash_kernel(q_ref, k_ref, v_ref, o_ref, sa_ref, sb_ref, mxa_ref, mxb_ref, m_ref, acc_ref, *, t, nh):
    qi = pl.program_id(1)
    heads = range(nh)
    bufs = ((sa_ref, mxa_ref), (sb_ref, mxb_ref))

    def produce(b, parity, masked):
        s_ref, mx_ref = bufs[parity]
        start = pl.multiple_of(b * t, t)
        for h in heads:
            s = _dot_nt(q_ref[h], k_ref[h, pl.ds(start, t), :])
            if masked:
                rows = lax.broadcasted_iota(jnp.int32, s.shape, 0) + qi * t
                cols = lax.broadcasted_iota(jnp.int32, s.shape, 1) + b * t
                s = jnp.where(cols <= rows, s, NEG_BIG)
            s_ref[h] = s
            mx_ref[h] = jnp.broadcast_to(jnp.max(s, -1, keepdims=True), (t, LANE))

    def absorb(b, parity):
        s_ref, mx_ref = bufs[parity]
        start = pl.multiple_of(b * t, t)
        for h in heads:
            m_new = jnp.maximum(m_ref[h], mx_ref[h])
            alpha = jnp.exp2(m_ref[h] - m_new)
            p = jnp.exp2(s_ref[h] - jnp.concatenate([m_new] * (t // LANE), 1))
            acc_ref[h] = (jnp.concatenate([alpha, alpha], 1) * acc_ref[h]
                          + _dot(p.astype(BF16), v_ref[h, pl.ds(start, t), :]))
            m_ref[h] = m_new

    m_ref[...] = jnp.full(m_ref.shape, NEG_BIG, F32)
    acc_ref[...] = jnp.zeros_like(acc_ref)
    produce(0, 0, True)

    def pair(i, c):
        b = 2 * i
        produce(b + 1, 1, False)
        absorb(b, 0)
        produce(b + 2, 0, False)
        absorb(b + 1, 1)
        return c

    n_pairs = jnp.maximum(qi - 1, 0) // 2
    lax.fori_loop(0, n_pairs, pair, 0)
    done = 2 * n_pairs

    @pl.when(qi == 0)
    def _():
        absorb(0, 0)

    @pl.when((qi > 0) & (qi - done == 1))
    def _():
        produce(qi, 1, True)
        absorb(qi - 1, 0)
        absorb(qi, 1)

    @pl.when((qi > 0) & (qi - done == 2))
    def _():
        produce(qi - 1, 1, False)
        absorb(qi - 2, 0)
        produce(qi, 0, True)
        absorb(qi - 1, 1)
        absorb(qi, 0)

    for h in heads:
        acc = acc_ref[h]
        o_ref[:, h * MLA_V:(h + 1) * MLA_V] = (acc[:, :MLA_V] / acc[:, MLA_V:]).astype(o_ref.dtype)


def mla_flash(q, k, v1):
    _, s, _ = q.shape
    t = min(512, s)
    nh = 2
    return pl.pallas_call(
        functools.partial(_mla_flash_kernel, t=t, nh=nh),
        out_shape=jax.ShapeDtypeStruct((s, MLA_HEADS * MLA_V), BF16),
        grid=(MLA_HEADS // nh, s // t),
        in_specs=[pl.BlockSpec((nh, t, MLA_QK), lambda h, i: (h, i, 0)),
                  pl.BlockSpec((nh, s, MLA_QK), lambda h, i: (h, 0, 0)),
                  pl.BlockSpec((nh, s, 2 * MLA_V), lambda h, i: (h, 0, 0))],
        out_specs=pl.BlockSpec((t, nh * MLA_V), lambda h, i: (i, h)),
        scratch_shapes=[pltpu.VMEM((nh, t, t), F32), pltpu.VMEM((nh, t, t), F32),
                        pltpu.VMEM((nh, t, LANE), F32), pltpu.VMEM((nh, t, LANE), F32),
                        pltpu.VMEM((nh, t, LANE), F32), pltpu.VMEM((nh, t, 2 * MLA_V), F32)],
        compiler_params=_params(("parallel", "arbitrary"), 56),
        name="mla_flash",
    )(q, k, v1)


def _gla_masks():
    i = np.arange(GLA_L)[:, None]
    j = np.arange(GLA_L)[None, :]
    same = (i // GLA_SUB) == (j // GLA_SUB)
    mats = [j <= i, same & (j <= i), same & (j > i), j > i]
    for sub in range(GLA_NSUB - 1):
        mats.append((j >= (sub + 1) * GLA_SUB) & (j <= i))
    return np.concatenate(mats, 0).astype(np.float32)


GLA_CHUNKS_PER_STEP = 4


def _gla_kernel(q_ref, k_ref, v_ref, lr_ref, r_ref, wg_ref, bg_ref, gain_ref, mask_ref, o_ref, state_ref):
    @pl.when(pl.program_id(0) == 0)
    def _():
        state_ref[...] = jnp.zeros_like(state_ref)

    L, hk = GLA_L, GLA_HEADS * GLA_DK
    chunks = [slice(c * L, (c + 1) * L) for c in range(q_ref.shape[0] // L)]
    heads = [(slice(h * GLA_DK, (h + 1) * GLA_DK), slice(h * GLA_DV, (h + 1) * GLA_DV)) for h in range(GLA_HEADS)]

    z = _dot(lr_ref[...], wg_ref[...]) + bg_ref[...]
    g = (jnp.minimum(z, 0.0) - jnp.log(1.0 + jnp.exp(-jnp.abs(z)))) * (1.0 / GLA_TAU)
    g_hi = g.astype(BF16)
    g_lo = (g - g_hi.astype(F32)).astype(BF16)

    cums = [_dot(mask_ref[...], jnp.concatenate([g_hi[rs], g_lo[rs]], 0)) for rs in chunks]

    sub_of_row = lax.broadcasted_iota(jnp.int32, (L, hk), 0) // GLA_SUB
    prep = []
    for rs, cum in zip(chunks, cums):
        b_all, b_loc, sfx_loc, sfx_all = cum[0:L], cum[L:2 * L], cum[2 * L:3 * L], cum[3 * L:4 * L]
        q = q_ref[rs, :].astype(F32) * (GLA_DK ** -0.5)
        k = k_ref[rs, :].astype(F32)
        k_end = k * jnp.exp(sfx_loc)
        prep.append(dict(
            decay=jnp.exp(b_all[L - 1:L, :]),
            q_inter=(q * jnp.exp(b_all)).astype(BF16),
            q_diag=(q * jnp.exp(b_loc)).astype(BF16),
            k_diag=(k * jnp.exp(-b_loc)).astype(BF16),
            k_state=(k * jnp.exp(sfx_all)).astype(BF16),
            q_off=[(q * jnp.exp(cum[(4 + sub) * L:(5 + sub) * L])).astype(BF16) for sub in range(GLA_NSUB - 1)],
            k_off=[jnp.where(sub_of_row == sub, k_end, 0.0).astype(BF16) for sub in range(GLA_NSUB - 1)]))

    row = lax.broadcasted_iota(jnp.int32, (L, L), 0)
    col = lax.broadcasted_iota(jnp.int32, (L, L), 1)
    diag_ok = ((row // GLA_SUB) == (col // GLA_SUB)) & (col <= row)
    off_ok = (row // GLA_SUB) > (col // GLA_SUB)
    local = []
    for rs, pr in zip(chunks, prep):
        per_head = []
        for ks, vs in heads:
            v_h = v_ref[rs, vs]
            a = jnp.where(diag_ok, _dot_nt(pr["q_diag"][:, ks], pr["k_diag"][:, ks]), 0.0)
            qo = jnp.concatenate([t[:, ks] for t in pr["q_off"]], 1)
            ko = jnp.concatenate([t[:, ks] for t in pr["k_off"]], 1)
            a = a + jnp.where(off_ok, _dot_nt(qo, ko), 0.0)
            per_head.append((_dot(a.astype(BF16), v_h), _dot_tn(v_h, pr["k_state"][:, ks])))
        local.append(per_head)

    states = [state_ref[h] for h in range(GLA_HEADS)]
    for rs, pr, per_head in zip(chunks, prep, local):
        r = r_ref[rs, :].astype(F32)
        gate = r * (1.0 / (1.0 + jnp.exp(-r)))
        for h, (ks, vs) in enumerate(heads):
            o_intra, vk = per_head[h]
            o = _dot_nt(pr["q_inter"][:, ks], states[h].astype(BF16)) + o_intra
            states[h] = states[h] * pr["decay"][:, ks] + vk
            o = o * lax.rsqrt(jnp.mean(o * o, -1, keepdims=True) + RMS_EPS) * gain_ref[:, vs]
            o_ref[rs, vs] = (o * gate[:, vs]).astype(o_ref.dtype)
    for h in range(GLA_HEADS):
        state_ref[h] = states[h]


def gla(h, wg_p, bg, gain):
    s = h.shape[0]
    rows = min(GLA_CHUNKS_PER_STEP * GLA_L, s)
    masks = jnp.asarray(np.tile(_gla_masks(), (1, 2)), BF16)
    row = lambda width, col: pl.BlockSpec((rows, width), lambda i: (i, col // width))
    full = lambda a: pl.BlockSpec(a.shape, lambda i: (0,) * a.ndim)
    hk, hv = GLA_HEADS * GLA_DK, GLA_HEADS * GLA_DV
    return pl.pallas_call(
        _gla_kernel,
        out_shape=jax.ShapeDtypeStruct((s, hv), BF16),
        grid=(s // rows,),
        in_specs=[row(hk, H_GQ), row(hk, H_GK), row(hv, H_GV), row(LANE, H_GLR), row(hv, H_GR),
                  full(wg_p), full(bg), full(gain), full(masks)],
        out_specs=pl.BlockSpec((rows, hv), lambda i: (i, 0)),
        scratch_shapes=[pltpu.VMEM((GLA_HEADS, GLA_DV, GLA_DK), F32)],
        compiler_params=_params(("arbitrary",), 32),
        name="gla",
    )(h, h, h, h, h, wg_p, bg, gain, masks)


def _t5_bucket_table():
    L = SWA_BLOCK
    dist = np.arange(L)[:, None] + L - np.arange(2 * L)[None, :]
    d = np.clip(dist, 0, None)
    max_exact = REL_BUCKETS // 2
    df = np.maximum(d, 1).astype(np.float32)
    large = max_exact + (np.log(df / np.float32(max_exact)) / np.float32(math.log(REL_MAX_DIST / max_exact))
                         * np.float32(REL_BUCKETS - max_exact)).astype(np.int32)
    large = np.minimum(large, REL_BUCKETS - 1)
    bucket = np.where(d < max_exact, d, large)
    in_window = (dist >= 0) & (dist < SWA_WINDOW)
    return np.where(in_window, bucket, -1).astype(np.int32)


def _swa_kernel(relb_ref, sink_ref, q_ref, kp_ref, kc_ref, vp_ref, vc_ref, bucket_ref, o_ref, bias_ref):
    i = pl.program_id(0)
    L = SWA_BLOCK

    @pl.when(i == 0)
    def _():
        bucket = bucket_ref[...]
        for h in range(SWA_HEADS):
            acc = jnp.full(bucket.shape, NEG_BIG, F32)
            for b in range(REL_BUCKETS):
                acc = jnp.where(bucket == b, relb_ref[b, h], acc)
            bias_ref[h] = acc

    kcat = jnp.concatenate([kp_ref[...], kc_ref[...]], 0)
    vcat = jnp.concatenate([vp_ref[...], vc_ref[...]], 0)
    scale = SWA_HD ** -0.5
    assert math.log2(scale).is_integer()
    q = q_ref[...] * scale
    col = lax.broadcasted_iota(jnp.int32, (L, 2 * L), 1)
    real_key = (col >= L) | (i > 0)
    g = SWA_HEADS // SWA_KV_HEADS
    ones = jnp.ones((2 * L, SWA_HD), BF16)
    v1 = [jnp.concatenate([vcat[:, kv * SWA_HD:(kv + 1) * SWA_HD], ones], 1) for kv in range(SWA_KV_HEADS)]
    for h in range(SWA_HEADS):
        kv = h // g
        hs = slice(h * SWA_HD, (h + 1) * SWA_HD)
        kvs = slice(kv * SWA_HD, (kv + 1) * SWA_HD)
        s = _dot_nt(q[:, hs], kcat[:, kvs]) + bias_ref[h]
        s = jnp.where(real_key, s, NEG_BIG)
        sink = sink_ref[h]
        m = jnp.maximum(jnp.max(s, -1, keepdims=True), sink)
        o2 = _dot(jnp.exp(s - m).astype(BF16), v1[kv])
        denom = o2[:, SWA_HD:] + jnp.exp(sink - m)
        o_ref[:, hs] = (o2[:, :SWA_HD] / denom).astype(o_ref.dtype)


def swa(h, sinks, rel_bias):
    s = h.shape[0]
    L = SWA_BLOCK
    bucket = jnp.asarray(_t5_bucket_table())
    kvw = SWA_KV_HEADS * SWA_HD
    hw = SWA_HEADS * SWA_HD
    cur = lambda width, col: pl.BlockSpec((L, width), lambda i, *_: (i, col // width))
    prev = lambda width, col: pl.BlockSpec((L, width), lambda i, *_: (jnp.maximum(i - 1, 0), col // width))
    return pl.pallas_call(
        _swa_kernel,
        out_shape=jax.ShapeDtypeStruct((s, hw), BF16),
        grid_spec=pltpu.PrefetchScalarGridSpec(
            num_scalar_prefetch=2,
            grid=(s // L,),
            in_specs=[cur(hw, H_SQ), prev(kvw, H_SK), cur(kvw, H_SK), prev(kvw, H_SV), cur(kvw, H_SV),
                      pl.BlockSpec(bucket.shape, lambda i, *_: (0, 0))],
            out_specs=pl.BlockSpec((L, hw), lambda i, *_: (i, 0)),
            scratch_shapes=[pltpu.VMEM((SWA_HEADS, L, 2 * L), F32)],
        ),
        compiler_params=_params(("arbitrary",), 32),
        name="swa",
    )(rel_bias, sinks, h, h, h, h, h, bucket)


def _layer_norm(y, g, b):
    mu = jnp.mean(y, -1, keepdims=True)
    yc = y - mu
    var = jnp.mean(yc * yc, -1, keepdims=True)
    return yc * lax.rsqrt(var + LN_EPS) * g + b


HALF = D_MODEL // 2
HIGH16 = 0xFFFF0000


SUBLANE = 8
TOK = HALF // LANE


def _store_token_tiles(ref, x):
    tm = x.shape[0]
    lo = lax.bitcast_convert_type(x[:, :HALF].astype(BF16).astype(F32), jnp.uint32) >> 16
    hi = lax.bitcast_convert_type(x[:, HALF:].astype(BF16).astype(F32), jnp.uint32) & jnp.uint32(HIGH16)
    packed = lo | hi
    for s in range(TOK):
        ref[pl.ds(s, tm, stride=TOK), :] = packed[:, s * LANE:(s + 1) * LANE]


def _load_token_tiles(ref, rows=None):
    first, tm = rows if rows is not None else (0, ref.shape[0] // TOK)
    lo, hi = [], []
    for s in range(TOK):
        w = ref[pl.ds(first * TOK + s, tm, stride=TOK), :]
        lo.append(lax.bitcast_convert_type(w << 16, F32))
        hi.append(lax.bitcast_convert_type(w & jnp.uint32(HIGH16), F32))
    return jnp.concatenate(lo, 1), jnp.concatenate(hi, 1)


ROW_SPLIT = 2


def _out_ln_kernel(a_ref, b_ref, c_ref, x_ref, w_ref, g_ref, beta_ref, o_ref, op_ref):
    na, nb = a_ref.shape[1], b_ref.shape[1]
    nr = x_ref.shape[0] // ROW_SPLIT
    for r0 in range(0, x_ref.shape[0], nr):
        rows = slice(r0, r0 + nr)
        m = _dot(a_ref[rows, :], w_ref[0:na, :])
        m = m + _dot(b_ref[rows, :], w_ref[na:na + nb, :])
        m = m + _dot(c_ref[rows, :], w_ref[na + nb:, :])
        y = _layer_norm(DEEPNORM_ALPHA * x_ref[rows, :] + m, g_ref[...], beta_ref[...])
        o_ref[rows, :] = y
        _store_token_tiles(op_ref.at[pl.ds(r0 * TOK, nr * TOK)], y)


def out_ln(a, b, c, x, w, g, beta, layer):
    s = x.shape[0]
    tm = min(512, s)
    row = lambda arr: pl.BlockSpec((tm, arr.shape[1]), lambda i: (i, 0))
    full = lambda arr: _layer_block(arr, layer)
    return pl.pallas_call(
        _out_ln_kernel,
        out_shape=[jax.ShapeDtypeStruct((s, D_MODEL), F32), jax.ShapeDtypeStruct((s * TOK, LANE), jnp.uint32)],
        grid=(s // tm,),
        in_specs=[row(a), row(b), row(c), row(x), full(w), full(g), full(beta)],
        out_specs=[pl.BlockSpec((tm, D_MODEL), lambda i: (i, 0)), pl.BlockSpec((tm * TOK, LANE), lambda i: (i, 0))],
        compiler_params=_params(("parallel",), 56),
        name="out_ln",
    )(a, b, c, x, w, g, beta)


FFN_ROWS = 1024
FFN_COLS = 512
FFN_OUT_CHUNK = 512
FFN_SUB = 256


def _ffn_kernel(te_ref, tv_ref, na_ref, xp_ref, wg_ref, wu_ref, wd_ref, o_ref, xb_ref, acc_ref):
    i, j = pl.program_id(0), pl.program_id(1)
    tm = xb_ref.shape[0]
    valid = tv_ref[i]

    @pl.when(j == 0)
    def _():
        acc_ref[...] = jnp.zeros_like(acc_ref)

    @pl.when((j == 0) & (valid > 0))
    def _():
        lo, hi = _load_token_tiles(xp_ref)
        xb_ref[:, :HALF] = lo.astype(BF16)
        xb_ref[:, HALF:] = hi.astype(BF16)

    def rows_step(r0, nr):
        xb = xb_ref[r0:r0 + nr, :]
        gate = _dot(xb, wg_ref[...].astype(BF16))
        up = _dot(xb, wu_ref[...].astype(BF16))
        hmid = (gate * (1.0 / (1.0 + jnp.exp(-gate))) * up).astype(BF16)
        for c in range(0, D_MODEL, FFN_OUT_CHUNK):
            cs = slice(c, c + FFN_OUT_CHUNK)
            acc_ref[r0:r0 + nr, cs] += _dot(hmid, wd_ref[:, cs].astype(BF16))

    nearly_full = valid > tm - FFN_SUB

    @pl.when(nearly_full)
    def _():
        rows_step(0, tm)

    for r0 in range(0, tm - FFN_SUB, FFN_SUB):
        @pl.when(jnp.logical_not(nearly_full) & (valid > r0))
        def _():
            rows_step(r0, FFN_SUB)

    @pl.when(j == pl.num_programs(1) - 1)
    def _():
        _store_token_tiles(o_ref, acc_ref[...])


def grouped_swiglu(xp, wg, wu, wd, tile_expert, tile_valid, n_active, tm, tf):
    n_tiles, nf = xp.shape[0] // (tm * TOK), D_FF // tf

    def tile(i, na):
        return jnp.minimum(i, na[0] - 1)

    def fcol(i, j, na):
        return jnp.where(i < na[0], j, nf - 1)

    return pl.pallas_call(
        _ffn_kernel,
        out_shape=jax.ShapeDtypeStruct(xp.shape, jnp.uint32),
        grid_spec=pltpu.PrefetchScalarGridSpec(
            num_scalar_prefetch=3,
            grid=(n_tiles, nf),
            in_specs=[pl.BlockSpec((tm * TOK, LANE), lambda i, j, te, tv, na: (tile(i, na), 0)),
                      pl.BlockSpec((None, D_MODEL, tf),
                                   lambda i, j, te, tv, na: (te[tile(i, na)], 0, fcol(i, j, na))),
                      pl.BlockSpec((None, D_MODEL, tf),
                                   lambda i, j, te, tv, na: (te[tile(i, na)], 0, fcol(i, j, na))),
                      pl.BlockSpec((None, tf, D_MODEL),
                                   lambda i, j, te, tv, na: (te[tile(i, na)], fcol(i, j, na), 0))],
            out_specs=pl.BlockSpec((tm * TOK, LANE), lambda i, j, te, tv, na: (i, 0)),
            scratch_shapes=[pltpu.VMEM((tm, D_MODEL), BF16), pltpu.VMEM((tm, D_MODEL), F32)],
        ),
        compiler_params=_params(("arbitrary", "arbitrary"), 60),
        name="grouped_swiglu",
    )(tile_expert, tile_valid, n_active, xp, wg, wu, wd)


def _router_kernel(x_ref, w_ref, e_ref, r_ref, wt_ref, cnt_ref, run_ref):
    i = pl.program_id(0)
    tm = x_ref.shape[0]
    ne = N_EXPERTS

    @pl.when(i == 0)
    def _():
        run_ref[...] = jnp.zeros_like(run_ref)

    x = x_ref[...]
    x_hi = x.astype(BF16)
    x_lo = (x - x_hi.astype(F32)).astype(BF16)
    w = w_ref[...]
    w_hi = w.astype(BF16).astype(F32)
    w_lo = w - w_hi
    both = _dot_nt(jnp.concatenate([w_hi, w_lo], 0).astype(BF16), x_hi)
    cross = _dot_nt(jnp.concatenate([w_hi, jnp.zeros_like(w_hi)], 0).astype(BF16), x_lo)
    logits = both[0:ne] + both[ne:2 * ne] + cross[0:ne]

    eidx = lax.broadcasted_iota(jnp.int32, logits.shape, 0).astype(F32)
    v1 = jnp.max(logits, 0, keepdims=True)
    i1 = jnp.min(jnp.where(logits == v1, eidx, float(ne)), 0, keepdims=True)
    rest = jnp.where(eidx == i1, -jnp.inf, logits)
    v2 = jnp.max(rest, 0, keepdims=True)
    i2 = jnp.min(jnp.where(rest == v2, eidx, float(ne)), 0, keepdims=True)
    t = jnp.exp(v2 - v1)
    w1 = 1.0 / (1.0 + t)
    wt_ref[0:1, :] = w1
    wt_ref[1:2, :] = t * w1
    e_ref[0:1, :] = i1.astype(jnp.int32)
    e_ref[1:2, :] = i2.astype(jnp.int32)

    sel1, sel2 = eidx == i1, eidx == i2
    sel = jnp.where(sel1, 1.0, 0.0) + jnp.where(sel2, 1.0, 0.0)
    before = (lax.broadcasted_iota(jnp.int32, (tm, tm), 0) < lax.broadcasted_iota(jnp.int32, (tm, tm), 1))
    sel16 = jnp.concatenate([sel, jnp.zeros_like(sel)], 0).astype(BF16)
    prefix = _dot(sel16, jnp.where(before, 1.0, 0.0).astype(BF16))[0:ne]
    rank = prefix + run_ref[:, 0:1]
    r_ref[0:1, :] = jnp.sum(jnp.where(sel1, rank, 0.0), 0, keepdims=True).astype(jnp.int32)
    r_ref[1:2, :] = jnp.sum(jnp.where(sel2, rank, 0.0), 0, keepdims=True).astype(jnp.int32)
    run_ref[...] = run_ref[...] + jnp.sum(sel, 1, keepdims=True)
    cnt_ref[...] = run_ref[...].astype(jnp.int32)


def route_tokens(x, w_router_t):
    s = x.shape[0]
    tm = min(512, s)
    pair = pl.BlockSpec((2, tm), lambda i: (0, i))
    return pl.pallas_call(
        _router_kernel,
        out_shape=[jax.ShapeDtypeStruct((2, s), jnp.int32), jax.ShapeDtypeStruct((2, s), jnp.int32),
                   jax.ShapeDtypeStruct((2, s), F32), jax.ShapeDtypeStruct((N_EXPERTS, LANE), jnp.int32)],
        grid=(s // tm,),
        in_specs=[pl.BlockSpec((tm, D_MODEL), lambda i: (i, 0)),
                  pl.BlockSpec((N_EXPERTS, D_MODEL), lambda i: (0, 0))],
        out_specs=[pair, pair, pair, pl.BlockSpec((N_EXPERTS, LANE), lambda i: (0, 0))],
        scratch_shapes=[pltpu.VMEM((N_EXPERTS, LANE), F32)],
        compiler_params=_params(("arbitrary",), 32),
        name="moe_router",
    )(x, w_router_t)


def _slot_kernel(base_ref, e_ref, r_ref, s_ref):
    e = e_ref[...]
    slot = r_ref[...]
    for k in range(N_EXPERTS):
        slot = slot + jnp.where(e == k, base_ref[k], 0)
    s_ref[...] = slot


def token_slots(e_idx, rank, base):
    whole = pl.BlockSpec(e_idx.shape, lambda i, *_: (0, 0))
    return pl.pallas_call(
        _slot_kernel,
        out_shape=jax.ShapeDtypeStruct(e_idx.shape, jnp.int32),
        grid_spec=pltpu.PrefetchScalarGridSpec(num_scalar_prefetch=1, grid=(1,), in_specs=[whole, whole],
                                               out_specs=whole),
        compiler_params=_params(("arbitrary",), 32),
        name="token_slots",
    )(base, e_idx, rank)


DMA_UNROLL = 8


def _tile_rows(index):
    return pl.ds(pl.multiple_of(index * TOK, TOK), TOK)


def _dispatch_kernel(fill_ref, slot_ref, x_ref, xs_ref, zero_ref, sem, zsem):
    i = pl.program_id(0)
    tm = x_ref.shape[0] // TOK

    def issue(b, c):
        for u in range(DMA_UNROLL):
            t = b * DMA_UNROLL + u
            for k in range(2):
                pltpu.make_async_copy(x_ref.at[_tile_rows(t)], xs_ref.at[_tile_rows(slot_ref[k, t])],
                                      sem).start(priority=k)
        return c

    lax.fori_loop(0, tm // DMA_UNROLL, issue, 0)
    for k in range(2):
        pltpu.make_async_copy(x_ref, xs_ref.at[pl.ds(0, tm * TOK)], sem).wait()

    @pl.when(i == pl.num_programs(0) - 1)
    def _():
        zero_ref[...] = jnp.zeros_like(zero_ref)

        def zero_copy(slot):
            return pltpu.make_async_copy(zero_ref.at[pl.ds(0, TOK)], xs_ref.at[_tile_rows(slot)], zsem)

        for e in range(N_EXPERTS):
            lo, hi = fill_ref[0, e], fill_ref[1, e]

            def zissue(slot, c):
                zero_copy(slot).start()
                return c

            def zdrain(slot, c):
                zero_copy(slot).wait()
                return c

            lax.fori_loop(lo, hi, zissue, 0)
            lax.fori_loop(lo, hi, zdrain, 0)

        zrows = zero_ref.shape[0]
        zb = zrows // TOK

        def block_copy(b):
            return pltpu.make_async_copy(zero_ref, xs_ref.at[pl.ds(pl.multiple_of(b * zrows, zrows), zrows)], zsem)

        def bissue(b, c):
            block_copy(b).start()
            return c

        def bdrain(b, c):
            block_copy(b).wait()
            return c

        first, last = fill_ref[2, 0] // zb, xs_ref.shape[0] // zrows
        lax.fori_loop(first, last, bissue, 0)
        lax.fori_loop(first, last, bdrain, 0)


def moe_dispatch(xp, slots, fill, rows):
    s = xp.shape[0] // TOK
    tm = min(512, s)
    zb = min(256, s)
    return pl.pallas_call(
        _dispatch_kernel,
        out_shape=jax.ShapeDtypeStruct((rows * TOK, LANE), xp.dtype),
        grid_spec=pltpu.PrefetchScalarGridSpec(
            num_scalar_prefetch=1,
            grid=(s // tm,),
            in_specs=[pl.BlockSpec((2, tm), lambda i, *_: (0, i), memory_space=pltpu.SMEM),
                      pl.BlockSpec((tm * TOK, LANE), lambda i, *_: (i, 0))],
            out_specs=pl.BlockSpec(memory_space=pl.ANY),
            scratch_shapes=[pltpu.VMEM((zb * TOK, LANE), xp.dtype), pltpu.SemaphoreType.DMA,
                            pltpu.SemaphoreType.DMA],
        ),
        compiler_params=_params(("arbitrary",), 32),
        name="moe_dispatch",
    )(fill, slots, xp)


def _ple(x, p, wup_ref, wgate_ref, bgate_ref):
    up = _dot(p.astype(BF16), wup_ref[...])
    zg = _dot(x.astype(BF16), wgate_ref[...]) + bgate_ref[...]
    return up * (1.0 / (1.0 + jnp.exp(-zg)))


def _ple_ln_dense_kernel(x_ref, f_ref, p_ref, wup_ref, wgate_ref, bgate_ref, g_ref, beta_ref, o_ref):
    nr = x_ref.shape[0] // ROW_SPLIT
    for r0 in range(0, x_ref.shape[0], nr):
        rows = slice(r0, r0 + nr)
        x = x_ref[rows, :]
        f = jnp.concatenate(_load_token_tiles(f_ref, (r0, nr)), 1)
        ple = _ple(x, p_ref[rows, :], wup_ref, wgate_ref, bgate_ref)
        o_ref[rows, :] = _layer_norm(DEEPNORM_ALPHA * x + f + ple, g_ref[...], beta_ref[...])


def _ple_ln_moe_kernel(slot_ref, x_ref, wt_ref, p_ref, wup_ref, wgate_ref, bgate_ref, g_ref, beta_ref, ys_ref,
                       o_ref, y1_ref, y2_ref, sem):
    tm = x_ref.shape[0]
    bufs = (y1_ref, y2_ref)

    def issue(b, c):
        for u in range(DMA_UNROLL):
            t = b * DMA_UNROLL + u
            for k in range(2):
                pltpu.make_async_copy(ys_ref.at[_tile_rows(slot_ref[k, t])], bufs[k].at[_tile_rows(t)],
                                      sem).start(priority=k)
        return c

    lax.fori_loop(0, tm // DMA_UNROLL, issue, 0)
    nr = tm // ROW_SPLIT
    ples = [_ple(x_ref[r0:r0 + nr, :], p_ref[r0:r0 + nr, :], wup_ref, wgate_ref, bgate_ref)
            for r0 in range(0, tm, nr)]
    for k in range(2):
        pltpu.make_async_copy(ys_ref.at[pl.ds(0, tm * TOK)], bufs[k], sem).wait()
    for i, r0 in enumerate(range(0, tm, nr)):
        rows = slice(r0, r0 + nr)
        wt = wt_ref[rows, :]
        f = (wt[:, 0:1] * jnp.concatenate(_load_token_tiles(y1_ref, (r0, nr)), 1)
             + wt[:, 1:2] * jnp.concatenate(_load_token_tiles(y2_ref, (r0, nr)), 1))
        o_ref[rows, :] = _layer_norm(DEEPNORM_ALPHA * x_ref[rows, :] + f + ples[i], g_ref[...], beta_ref[...])


def ple_ln(x, p, wup, wgate, bgate, g, beta, layer, f=None, moe=None):
    s = x.shape[0]
    tm = min(512, s)
    row = lambda arr: pl.BlockSpec((tm, arr.shape[1]), lambda i, *_: (i, 0))
    full = lambda arr: _layer_block(arr, layer)
    p_spec = pl.BlockSpec((None, None, tm, p.shape[-1]), lambda i, *_: (layer, 0, i, 0))
    tail = [p_spec, full(wup), full(wgate), full(bgate), full(g), full(beta)]
    out_spec = pl.BlockSpec((tm, D_MODEL), lambda i, *_: (i, 0))
    out_shape = jax.ShapeDtypeStruct((s, D_MODEL), F32)
    if moe is None:
        return pl.pallas_call(
            _ple_ln_dense_kernel, out_shape=out_shape, grid=(s // tm,),
            in_specs=[row(x), pl.BlockSpec((tm * TOK, LANE), lambda i: (i, 0))] + tail, out_specs=out_spec,
            compiler_params=_params(("parallel",), 56), name="ple_ln_dense",
        )(x, f, p, wup, wgate, bgate, g, beta)
    ys, slots, wts_t = moe
    return pl.pallas_call(
        _ple_ln_moe_kernel,
        out_shape=out_shape,
        grid_spec=pltpu.PrefetchScalarGridSpec(
            num_scalar_prefetch=0,
            grid=(s // tm,),
            in_specs=[pl.BlockSpec((2, tm), lambda i: (0, i), memory_space=pltpu.SMEM), row(x), row(wts_t)] + tail
            + [pl.BlockSpec(memory_space=pl.ANY)],
            out_specs=out_spec,
            scratch_shapes=[pltpu.VMEM((tm * TOK, LANE), jnp.uint32), pltpu.VMEM((tm * TOK, LANE), jnp.uint32),
                            pltpu.SemaphoreType.DMA],
        ),
        compiler_params=_params(("arbitrary",), 56),
        name="ple_ln_moe",
    )(slots, x, wts_t, p, wup, wgate, bgate, g, beta, ys)


IN_SIZES = (MLA_Q_RANK, MLA_KV_RANK, MLA_ROPE, GLA_HEADS * GLA_DK, GLA_HEADS * GLA_DK, GLA_HEADS * GLA_DV,
            GLA_GATE_RANK, GLA_HEADS * GLA_DV, SWA_HEADS * SWA_HD, SWA_KV_HEADS * SWA_HD, SWA_KV_HEADS * SWA_HD)
IN_COLS = sum(IN_SIZES)
IN_DEST = (H_QA, H_KVA, H_KR, H_GQ, H_GK, H_GV, H_GLR, H_GR, H_SQ, H_SK, H_SV)


def _pack_w_in_kernel(w_ref, o_ref):
    src = 0
    for width, dst in zip(IN_SIZES, IN_DEST):
        o_ref[:, dst:dst + width] = w_ref[:, src:src + width].astype(BF16)
        pad = -width % LANE
        if pad:
            o_ref[:, dst + width:dst + width + pad] = jnp.zeros((o_ref.shape[0], pad), BF16)
        src += width


def _pack_w_in(w):
    depth, d, _ = w.shape
    tr = 256
    return pl.pallas_call(
        _pack_w_in_kernel,
        out_shape=jax.ShapeDtypeStruct((depth, d, H_COLS), BF16),
        grid=(depth, d // tr),
        in_specs=[pl.BlockSpec((None, tr, IN_COLS), lambda l, i: (l, i, 0))],
        out_specs=pl.BlockSpec((None, tr, H_COLS), lambda l, i: (l, i, 0)),
        compiler_params=_params(("parallel", "parallel"), 32),
        name="pack_w_in",
    )(w)


def _pack_w_q_b(w):
    w = w.reshape(MLA_Q_RANK, MLA_HEADS, MLA_NOPE + MLA_ROPE)
    w = jnp.pad(w, ((0, 0), (0, 0), (0, MLA_QK - MLA_NOPE - MLA_ROPE)))
    return w.reshape(MLA_Q_RANK, MLA_HEADS * MLA_QK).astype(BF16)


def _token_mixer_ln(x, rope, layer, w_in_p, q_gain, w_q_b, kv_gain, w_kv_b, gla_w, gla_b, gla_gain, sinks,
                    rel_bias, w_out_b, ln_g, ln_b):
    h = proj_in(x, w_in_p, layer)
    q, k, v = mla_prep(h, *rope, q_gain.reshape(1, -1), kv_gain.reshape(1, -1), _pack_w_q_b(w_q_b),
                       w_kv_b.astype(BF16))
    a = mla_flash(q, k, v)
    gla_w_p = jnp.pad(gla_w, ((0, LANE - GLA_GATE_RANK), (0, 0))).astype(BF16)
    b = gla(h, gla_w_p, gla_b.reshape(1, -1), gla_gain.reshape(1, -1))
    c = swa(h, sinks, rel_bias)
    return out_ln(a, b, c, x, w_out_b, ln_g, ln_b, layer)


def _moe_plan(counts, tm, n_tiles):
    tiles = (counts + tm - 1) // tm
    ends = jnp.cumsum(tiles)
    base = (ends - tiles) * tm
    n_active = ends[-1:].astype(jnp.int32)
    tile_expert = jnp.searchsorted(ends, jnp.arange(n_tiles, dtype=jnp.int32), side="right")
    tile_expert = jnp.minimum(tile_expert, N_EXPERTS - 1).astype(jnp.int32)
    tile_start = jnp.arange(n_tiles, dtype=jnp.int32) * tm
    tile_valid = jnp.clip((base + counts)[tile_expert] - tile_start, 0, tm).astype(jnp.int32)
    used_rows = jnp.broadcast_to(ends[-1] * tm, counts.shape)
    fill = jnp.stack([base + counts, base + tiles * tm, used_rows]).astype(jnp.int32)
    return base.astype(jnp.int32), fill, tile_expert, tile_valid, n_active


def kernel(x, p, positions, w_in, mla_q_a_gain, mla_w_q_b, mla_kv_a_gain, mla_w_kv_b, gla_w_gate, gla_b_gate,
           gla_norm_gain, swa_sinks, rel_bias, w_out, ln1_g, ln1_b, ffn_w_gate, ffn_w_up, ffn_w_down,
           moe_router, moe_w_gate, moe_w_up, moe_w_down, ple_w_up, ple_w_gate, ple_b_gate, ln2_g, ln2_b):
    batch, s, _ = x.shape
    assert batch == 1
    xcur = x.reshape(s, D_MODEL)
    rope = rope_tables(positions)
    tm = min(FFN_ROWS, s)
    row_stack = lambda v: v.reshape(DEPTH, 1, -1)
    w_in_p = _pack_w_in(w_in)
    w_out_b = w_out.astype(BF16)
    tail = (p, ple_w_up.astype(BF16), ple_w_gate.astype(BF16), row_stack(ple_b_gate), row_stack(ln2_g),
            row_stack(ln2_b))
    for i in range(DEPTH):
        x1, x1p = _token_mixer_ln(xcur, rope, i, w_in_p, mla_q_a_gain[i], mla_w_q_b[i], mla_kv_a_gain[i],
                                  mla_w_kv_b[i], gla_w_gate[i], gla_b_gate[i], gla_norm_gain[i], swa_sinks[i],
                                  rel_bias, w_out_b, row_stack(ln1_g), row_stack(ln1_b))
        j = i // 2
        if i % 2 == 0:
            n_tiles = s // tm
            f = grouped_swiglu(x1p, ffn_w_gate[j][None], ffn_w_up[j][None], ffn_w_down[j][None],
                               jnp.zeros((n_tiles,), jnp.int32), jnp.full((n_tiles,), tm, jnp.int32),
                               jnp.full((1,), n_tiles, jnp.int32), tm, FFN_COLS)
            xcur = ple_ln(x1, *tail, i, f=f)
        else:
            n_tiles = (2 * s + N_EXPERTS * (tm - 1)) // tm
            e_idx, rank, wts, counts = route_tokens(x1, moe_router[j].T)
            base, fill, tile_expert, tile_valid, n_active = _moe_plan(counts[:, 0], tm, n_tiles)
            slots = token_slots(e_idx, rank, base)
            xs = moe_dispatch(x1p, slots, fill, n_tiles * tm)
            ys = grouped_swiglu(xs, moe_w_gate[j], moe_w_up[j], moe_w_down[j], tile_expert, tile_valid, n_active,
                                tm, FFN_COLS)
            xcur = ple_ln(x1, *tail, i, moe=(ys, slots, wts.T))
    return xcur.reshape(batch, s, D_MODEL)
```

```python
import functools
import math

import numpy as np
import jax
import jax.numpy as jnp
from jax import lax
from jax.experimental import pallas as pl
from jax.experimental.pallas import tpu as pltpu

F32 = jnp.float32
BF16 = jnp.bfloat16

D_MODEL = 2048
DEPTH = 2
MLA_HEADS = 8
MLA_Q_RANK = 512
MLA_KV_RANK = 256
MLA_NOPE = 128
MLA_ROPE = 64
MLA_V = 128
ROPE_THETA = 10000.0
GLA_HEADS = 4
GLA_DK = 64
GLA_DV = 128
GLA_GATE_RANK = 16
GLA_TAU = 16.0
SWA_HEADS = 8
SWA_KV_HEADS = 2
SWA_HD = 64
SWA_WINDOW = 128
SWA_BLOCK = 128
REL_BUCKETS = 32
REL_MAX_DIST = 128
D_FF = 5632
N_EXPERTS = 8
PLE_DIM = 256
LN_EPS = 1e-5
RMS_EPS = 1e-6
DEEPNORM_ALPHA = (2 * DEPTH) ** 0.25

LANE = 128
LOG2E = math.log2(math.e)
NEG_BIG = -1e30

H_QA, H_GV, H_GR, H_SQ = 0, 512, 1024, 1536
H_KVA, H_GQ, H_GK = 2048, 2304, 2560
H_KR, H_GLR, H_SK, H_SV = 2816, 2944, 3072, 3200
H_COLS = 3328

MLA_QK = 2 * LANE

GLA_L = 128
GLA_SUB = 32
GLA_NSUB = GLA_L // GLA_SUB


def _params(sem, vmem_mb):
    return pltpu.CompilerParams(dimension_semantics=sem, vmem_limit_bytes=vmem_mb * 2 ** 20)


def _dot(a, b):
    return jnp.dot(a, b, preferred_element_type=F32)


def _dot_nt(a, b):
    return lax.dot_general(a, b, (((1,), (1,)), ((), ())), preferred_element_type=F32)


def _dot_tn(a, b):
    return lax.dot_general(a, b, (((0,), (0,)), ((), ())), preferred_element_type=F32)


def _proj_in_kernel(x_ref, w_ref, o_ref, xb_ref):
    @pl.when(pl.program_id(1) == 0)
    def _():
        xb_ref[...] = x_ref[...].astype(BF16)

    o_ref[...] = _dot_nt(xb_ref[...], w_ref[...]).astype(o_ref.dtype)


def _layer_block(arr, layer):
    zeros = (0,) * (arr.ndim - 1)
    return pl.BlockSpec((None,) + arr.shape[1:], lambda i, *_: (layer,) + zeros, pipeline_mode=pl.Buffered(1))


def proj_in(x, w_p, layer):
    s = x.shape[0]
    tm = min(1024, s)
    tn = H_COLS // 2
    return pl.pallas_call(
        _proj_in_kernel,
        out_shape=jax.ShapeDtypeStruct((s, H_COLS), BF16),
        grid=(s // tm, H_COLS // tn),
        in_specs=[pl.BlockSpec((tm, D_MODEL), lambda i, j: (i, 0)),
                  pl.BlockSpec((None, tn, D_MODEL), lambda i, j: (layer, j, 0))],
        out_specs=pl.BlockSpec((tm, tn), lambda i, j: (i, j)),
        scratch_shapes=[pltpu.VMEM((tm, D_MODEL), BF16)],
        compiler_params=_params(("parallel", "arbitrary"), 56),
        name="proj_in",
    )(x, w_p)


def _rope_table_kernel(pos_ref, inv_ref, cos_ref, sa_ref, sb_ref):
    ang = pos_ref[...].astype(F32) * inv_ref[...]
    lane = lax.broadcasted_iota(jnp.int32, ang.shape, 1)
    half = MLA_ROPE // 2
    c, s = jnp.cos(ang), jnp.sin(ang)
    cos_ref[...] = c
    sa_ref[...] = jnp.where((lane >= half) & (lane < 2 * half), s, 0.0)
    sb_ref[...] = jnp.where(lane < half, -s, 0.0)


def rope_tables(positions):
    s = positions.shape[-1]
    half = MLA_ROPE // 2
    inv = ROPE_THETA ** (-jnp.arange(half, dtype=F32) / half)
    inv = jnp.concatenate([inv, inv, jnp.zeros((LANE - 2 * half,), F32)]).reshape(1, LANE)
    tm = min(1024, s)
    spec = pl.BlockSpec((tm, LANE), lambda i: (i, 0))
    return pl.pallas_call(
        _rope_table_kernel,
        out_shape=[jax.ShapeDtypeStruct((s, LANE), F32)] * 3,
        grid=(s // tm,),
        in_specs=[pl.BlockSpec((tm, 1), lambda i: (i, 0)), pl.BlockSpec((1, LANE), lambda i: (0, 0))],
        out_specs=[spec, spec, spec],
        compiler_params=_params(("parallel",), 32),
        name="rope_tables",
    )(positions.reshape(s, 1), inv)


def _rope(x, cos, sa, sb):
    return x * cos + pltpu.roll(x, MLA_ROPE // 2, 1) * sa + pltpu.roll(x, LANE - MLA_ROPE // 2, 1) * sb


def _mla_prep_kernel(qa_ref, kva_ref, kr_ref, cos_ref, sa_ref, sb_ref, gq_ref, gkv_ref, wq_ref, wkv_ref,
                     q_out, k_out, v_out):
    cos, sa, sb = cos_ref[...], sa_ref[...], sb_ref[...]
    qscale = (MLA_NOPE + MLA_ROPE) ** -0.5 * LOG2E

    qa = qa_ref[...].astype(F32)
    qn = qa * lax.rsqrt(jnp.mean(qa * qa, -1, keepdims=True) + RMS_EPS) * gq_ref[...]
    q = _dot(qn.astype(BF16), wq_ref[...])
    for h in range(MLA_HEADS):
        c0 = h * MLA_QK
        q_out[h, :, 0:LANE] = (q[:, c0:c0 + LANE] * qscale).astype(BF16)
        pe = _rope(q[:, c0 + LANE:c0 + 2 * LANE], cos, sa, sb)
        q_out[h, :, LANE:2 * LANE] = (pe * qscale).astype(BF16)

    kva = kva_ref[...].astype(F32)
    kvn = kva * lax.rsqrt(jnp.mean(kva * kva, -1, keepdims=True) + RMS_EPS) * gkv_ref[...]
    kv = _dot(kvn.astype(BF16), wkv_ref[...])
    kpe = _rope(kr_ref[...].astype(F32), cos, sa, sb).astype(BF16)
    for h in range(MLA_HEADS):
        c0 = h * (MLA_NOPE + MLA_V)
        k_out[h, :, 0:LANE] = kv[:, c0:c0 + MLA_NOPE].astype(BF16)
        k_out[h, :, LANE:2 * LANE] = kpe
        v_out[h, :, 0:MLA_V] = kv[:, c0 + MLA_NOPE:c0 + MLA_NOPE + MLA_V].astype(BF16)
        v_out[h, :, MLA_V:2 * MLA_V] = jnp.ones((kv.shape[0], MLA_V), BF16)


def mla_prep(h, cos, sa, sb, gq, gkv, wq_p, wkv):
    s = h.shape[0]
    tm = min(512, s)
    row = lambda width, col: pl.BlockSpec((tm, width), lambda i: (i, col // width))
    full = lambda a: pl.BlockSpec(a.shape, lambda i: (0,) * a.ndim)
    return pl.pallas_call(
        _mla_prep_kernel,
        out_shape=[jax.ShapeDtypeStruct((MLA_HEADS, s, MLA_QK), BF16),
                   jax.ShapeDtypeStruct((MLA_HEADS, s, MLA_QK), BF16),
                   jax.ShapeDtypeStruct((MLA_HEADS, s, 2 * MLA_V), BF16)],
        grid=(s // tm,),
        in_specs=[row(MLA_Q_RANK, H_QA), row(MLA_KV_RANK, H_KVA), row(LANE, H_KR),
                  row(LANE, 0), row(LANE, 0), row(LANE, 0),
                  full(gq), full(gkv), full(wq_p), full(wkv)],
        out_specs=[pl.BlockSpec((MLA_HEADS, tm, MLA_QK), lambda i: (0, i, 0)),
                   pl.BlockSpec((MLA_HEADS, tm, MLA_QK), lambda i: (0, i, 0)),
                   pl.BlockSpec((MLA_HEADS, tm, 2 * MLA_V), lambda i: (0, i, 0))],
        compiler_params=_params(("parallel",), 48),
        name="mla_prep",
    )(h, h, h, cos, sa, sb, gq, gkv, wq_p, wkv)


def _mla_flash_kernel(q_ref, k_ref, v_ref, o_ref, sa_ref, sb_ref, mxa_ref, mxb_ref, m_ref, acc_ref, *, t, nh):
    qi = pl.program_id(1)
    heads = range(nh)
    bufs = ((sa_ref, mxa_ref), (sb_ref, mxb_ref))

    def produce(b, parity, masked):
        s_ref, mx_ref = bufs[parity]
        start = pl.multiple_of(b * t, t)
        for h in heads:
            s = _dot_nt(q_ref[h], k_ref[h, pl.ds(start, t), :])
            if masked:
                rows = lax.broadcasted_iota(jnp.int32, s.shape, 0) + qi * t
                cols = lax.broadcasted_iota(jnp.int32, s.shape, 1) + b * t
                s = jnp.where(cols <= rows, s, NEG_BIG)
            s_ref[h] = s
            mx_ref[h] = jnp.broadcast_to(jnp.max(s, -1, keepdims=True), (t, LANE))

    def absorb(b, parity):
        s_ref, mx_ref = bufs[parity]
        start = pl.multiple_of(b * t, t)
        for h in heads:
            m_new = jnp.maximum(m_ref[h], mx_ref[h])
            alpha = jnp.exp2(m_ref[h] - m_new)
            p = jnp.exp2(s_ref[h] - jnp.concatenate([m_new] * (t // LANE), 1))
            acc_ref[h] = (jnp.concatenate([alpha, alpha], 1) * acc_ref[h]
                          + _dot(p.astype(BF16), v_ref[h, pl.ds(start, t), :]))
            m_ref[h] = m_new

    m_ref[...] = jnp.full(m_ref.shape, NEG_BIG, F32)
    acc_ref[...] = jnp.zeros_like(acc_ref)
    produce(0, 0, True)

    def pair(i, c):
        b = 2 * i
        produce(b + 1, 1, False)
        absorb(b, 0)
        produce(b + 2, 0, False)
        absorb(b + 1, 1)
        return c

    n_pairs = jnp.maximum(qi - 1, 0) // 2
    lax.fori_loop(0, n_pairs, pair, 0)
    done = 2 * n_pairs

    @pl.when(qi == 0)
    def _():
        absorb(0, 0)

    @pl.when((qi > 0) & (qi - done == 1))
    def _():
        produce(qi, 1, True)
        absorb(qi - 1, 0)
        absorb(qi, 1)

    @pl.when((qi > 0) & (qi - done == 2))
    def _():
        produce(qi - 1, 1, False)
        absorb(qi - 2, 0)
        produce(qi, 0, True)
        absorb(qi - 1, 1)
        absorb(qi, 0)

    for h in heads:
        acc = acc_ref[h]
        o_ref[:, h * MLA_V:(h + 1) * MLA_V] = (acc[:, :MLA_V] / acc[:, MLA_V:]).astype(o_ref.dtype)


def mla_flash(q, k, v1):
    _, s, _ = q.shape
    t = min(512, s)
    nh = 2
    return pl.pallas_call(
        functools.partial(_mla_flash_kernel, t=t, nh=nh),
        out_shape=jax.ShapeDtypeStruct((s, MLA_HEADS * MLA_V), BF16),
        grid=(MLA_HEADS // nh, s // t),
        in_specs=[pl.BlockSpec((nh, t, MLA_QK), lambda h, i: (h, i, 0)),
                  pl.BlockSpec((nh, s, MLA_QK), lambda h, i: (h, 0, 0)),
                  pl.BlockSpec((nh, s, 2 * MLA_V), lambda h, i: (h, 0, 0))],
        out_specs=pl.BlockSpec((t, nh * MLA_V), lambda h, i: (i, h)),
        scratch_shapes=[pltpu.VMEM((nh, t, t), F32), pltpu.VMEM((nh, t, t), F32),
                        pltpu.VMEM((nh, t, LANE), F32), pltpu.VMEM((nh, t, LANE), F32),
                        pltpu.VMEM((nh, t, LANE), F32), pltpu.VMEM((nh, t, 2 * MLA_V), F32)],
        compiler_params=_params(("parallel", "arbitrary"), 56),
        name="mla_flash",
    )(q, k, v1)


def _gla_masks():
    i = np.arange(GLA_L)[:, None]
    j = np.arange(GLA_L)[None, :]
    same = (i // GLA_SUB) == (j // GLA_SUB)
    mats = [j <= i, same & (j <= i), same & (j > i), j > i]
    for sub in range(GLA_NSUB - 1):
        mats.append((j >= (sub + 1) * GLA_SUB) & (j <= i))
    return np.concatenate(mats, 0).astype(np.float32)


GLA_CHUNKS_PER_STEP = 4


def _gla_kernel(q_ref, k_ref, v_ref, lr_ref, r_ref, wg_ref, bg_ref, gain_ref, mask_ref, o_ref, state_ref):
    @pl.when(pl.program_id(0) == 0)
    def _():
        state_ref[...] = jnp.zeros_like(state_ref)

    L, hk = GLA_L, GLA_HEADS * GLA_DK
    chunks = [slice(c * L, (c + 1) * L) for c in range(q_ref.shape[0] // L)]
    heads = [(slice(h * GLA_DK, (h + 1) * GLA_DK), slice(h * GLA_DV, (h + 1) * GLA_DV)) for h in range(GLA_HEADS)]

    z = _dot(lr_ref[...], wg_ref[...]) + bg_ref[...]
    g = (jnp.minimum(z, 0.0) - jnp.log(1.0 + jnp.exp(-jnp.abs(z)))) * (1.0 / GLA_TAU)
    g_hi = g.astype(BF16)
    g_lo = (g - g_hi.astype(F32)).astype(BF16)

    cums = [_dot(mask_ref[...], jnp.concatenate([g_hi[rs], g_lo[rs]], 0)) for rs in chunks]

    sub_of_row = lax.broadcasted_iota(jnp.int32, (L, hk), 0) // GLA_SUB
    prep = []
    for rs, cum in zip(chunks, cums):
        b_all, b_loc, sfx_loc, sfx_all = cum[0:L], cum[L:2 * L], cum[2 * L:3 * L], cum[3 * L:4 * L]
        q = q_ref[rs, :].astype(F32) * (GLA_DK ** -0.5)
        k = k_ref[rs, :].astype(F32)
        k_end = k * jnp.exp(sfx_loc)
        prep.append(dict(
            decay=jnp.exp(b_all[L - 1:L, :]),
            q_inter=(q * jnp.exp(b_all)).astype(BF16),
            q_diag=(q * jnp.exp(b_loc)).astype(BF16),
            k_diag=(k * jnp.exp(-b_loc)).astype(BF16),
            k_state=(k * jnp.exp(sfx_all)).astype(BF16),
            q_off=[(q * jnp.exp(cum[(4 + sub) * L:(5 + sub) * L])).astype(BF16) for sub in range(GLA_NSUB - 1)],
            k_off=[jnp.where(sub_of_row == sub, k_end, 0.0).astype(BF16) for sub in range(GLA_NSUB - 1)]))

    row = lax.broadcasted_iota(jnp.int32, (L, L), 0)
    col = lax.broadcasted_iota(jnp.int32, (L, L), 1)
    diag_ok = ((row // GLA_SUB) == (col // GLA_SUB)) & (col <= row)
    off_ok = (row // GLA_SUB) > (col // GLA_SUB)
    local = []
    for rs, pr in zip(chunks, prep):
        per_head = []
        for ks, vs in heads:
            v_h = v_ref[rs, vs]
            a = jnp.where(diag_ok, _dot_nt(pr["q_diag"][:, ks], pr["k_diag"][:, ks]), 0.0)
            qo = jnp.concatenate([t[:, ks] for t in pr["q_off"]], 1)
            ko = jnp.concatenate([t[:, ks] for t in pr["k_off"]], 1)
            a = a + jnp.where(off_ok, _dot_nt(qo, ko), 0.0)
            per_head.append((_dot(a.astype(BF16), v_h), _dot_tn(v_h, pr["k_state"][:, ks])))
        local.append(per_head)

    states = [state_ref[h] for h in range(GLA_HEADS)]
    for rs, pr, per_head in zip(chunks, prep, local):
        r = r_ref[rs, :].astype(F32)
        gate = r * (1.0 / (1.0 + jnp.exp(-r)))
        for h, (ks, vs) in enumerate(heads):
            o_intra, vk = per_head[h]
            o = _dot_nt(pr["q_inter"][:, ks], states[h].astype(BF16)) + o_intra
            states[h] = states[h] * pr["decay"][:, ks] + vk
            o = o * lax.rsqrt(jnp.mean(o * o, -1, keepdims=True) + RMS_EPS) * gain_ref[:, vs]
            o_ref[rs, vs] = (o * gate[:, vs]).astype(o_ref.dtype)
    for h in range(GLA_HEADS):
        state_ref[h] = states[h]


def gla(h, wg_p, bg, gain):
    s = h.shape[0]
    rows = min(GLA_CHUNKS_PER_STEP * GLA_L, s)
    masks = jnp.asarray(np.tile(_gla_masks(), (1, 2)), BF16)
    row = lambda width, col: pl.BlockSpec((rows, width), lambda i: (i, col // width))
    full = lambda a: pl.BlockSpec(a.shape, lambda i: (0,) * a.ndim)
    hk, hv = GLA_HEADS * GLA_DK, GLA_HEADS * GLA_DV
    return pl.pallas_call(
        _gla_kernel,
        out_shape=jax.ShapeDtypeStruct((s, hv), BF16),
        grid=(s // rows,),
        in_specs=[row(hk, H_GQ), row(hk, H_GK), row(hv, H_GV), row(LANE, H_GLR), row(hv, H_GR),
                  full(wg_p), full(bg), full(gain), full(masks)],
        out_specs=pl.BlockSpec((rows, hv), lambda i: (i, 0)),
        scratch_shapes=[pltpu.VMEM((GLA_HEADS, GLA_DV, GLA_DK), F32)],
        compiler_params=_params(("arbitrary",), 32),
        name="gla",
    )(h, h, h, h, h, wg_p, bg, gain, masks)


def _t5_bucket_table():
    L = SWA_BLOCK
    dist = np.arange(L)[:, None] + L - np.arange(2 * L)[None, :]
    d = np.clip(dist, 0, None)
    max_exact = REL_BUCKETS // 2
    df = np.maximum(d, 1).astype(np.float32)
    large = max_exact + (np.log(df / np.float32(max_exact)) / np.float32(math.log(REL_MAX_DIST / max_exact))
                         * np.float32(REL_BUCKETS - max_exact)).astype(np.int32)
    large = np.minimum(large, REL_BUCKETS - 1)
    bucket = np.where(d < max_exact, d, large)
    in_window = (dist >= 0) & (dist < SWA_WINDOW)
    return np.where(in_window, bucket, -1).astype(np.int32)


def _swa_kernel(relb_ref, sink_ref, q_ref, kp_ref, kc_ref, vp_ref, vc_ref, bucket_ref, o_ref, bias_ref):
    i = pl.program_id(0)
    L = SWA_BLOCK

    @pl.when(i == 0)
    def _():
        bucket = bucket_ref[...]
        for h in range(SWA_HEADS):
            acc = jnp.full(bucket.shape, NEG_BIG, F32)
            for b in range(REL_BUCKETS):
                acc = jnp.where(bucket == b, relb_ref[b, h], acc)
            bias_ref[h] = acc

    kcat = jnp.concatenate([kp_ref[...], kc_ref[...]], 0)
    vcat = jnp.concatenate([vp_ref[...], vc_ref[...]], 0)
    scale = SWA_HD ** -0.5
    assert math.log2(scale).is_integer()
    q = q_ref[...] * scale
    col = lax.broadcasted_iota(jnp.int32, (L, 2 * L), 1)
    real_key = (col >= L) | (i > 0)
    g = SWA_HEADS // SWA_KV_HEADS
    ones = jnp.ones((2 * L, SWA_HD), BF16)
    v1 = [jnp.concatenate([vcat[:, kv * SWA_HD:(kv + 1) * SWA_HD], ones], 1) for kv in range(SWA_KV_HEADS)]
    for h in range(SWA_HEADS):
        kv = h // g
        hs = slice(h * SWA_HD, (h + 1) * SWA_HD)
        kvs = slice(kv * SWA_HD, (kv + 1) * SWA_HD)
        s = _dot_nt(q[:, hs], kcat[:, kvs]) + bias_ref[h]
        s = jnp.where(real_key, s, NEG_BIG)
        sink = sink_ref[h]
        m = jnp.maximum(jnp.max(s, -1, keepdims=True), sink)
        o2 = _dot(jnp.exp(s - m).astype(BF16), v1[kv])
        denom = o2[:, SWA_HD:] + jnp.exp(sink - m)
        o_ref[:, hs] = (o2[:, :SWA_HD] / denom).astype(o_ref.dtype)


def swa(h, sinks, rel_bias):
    s = h.shape[0]
    L = SWA_BLOCK
    bucket = jnp.asarray(_t5_bucket_table())
    kvw = SWA_KV_HEADS * SWA_HD
    hw = SWA_HEADS * SWA_HD
    cur = lambda width, col: pl.BlockSpec((L, width), lambda i, *_: (i, col // width))
    prev = lambda width, col: pl.BlockSpec((L, width), lambda i, *_: (jnp.maximum(i - 1, 0), col // width))
    return pl.pallas_call(
        _swa_kernel,
        out_shape=jax.ShapeDtypeStruct((s, hw), BF16),
        grid_spec=pltpu.PrefetchScalarGridSpec(
            num_scalar_prefetch=2,
            grid=(s // L,),
            in_specs=[cur(hw, H_SQ), prev(kvw, H_SK), cur(kvw, H_SK), prev(kvw, H_SV), cur(kvw, H_SV),
                      pl.BlockSpec(bucket.shape, lambda i, *_: (0, 0))],
            out_specs=pl.BlockSpec((L, hw), lambda i, *_: (i, 0)),
            scratch_shapes=[pltpu.VMEM((SWA_HEADS, L, 2 * L), F32)],
        ),
        compiler_params=_params(("arbitrary",), 32),
        name="swa",
    )(rel_bias, sinks, h, h, h, h, h, bucket)


def _layer_norm(y, g, b):
    mu = jnp.mean(y, -1, keepdims=True)
    yc = y - mu
    var = jnp.mean(yc * yc, -1, keepdims=True)
    return yc * lax.rsqrt(var + LN_EPS) * g + b


HALF = D_MODEL // 2
HIGH16 = 0xFFFF0000


SUBLANE = 8
TOK = HALF // LANE


def _store_token_tiles(ref, x):
    tm = x.shape[0]
    lo = lax.bitcast_convert_type(x[:, :HALF].astype(BF16).astype(F32), jnp.uint32) >> 16
    hi = lax.bitcast_convert_type(x[:, HALF:].astype(BF16).astype(F32), jnp.uint32) & jnp.uint32(HIGH16)
    packed = lo | hi
    for s in range(TOK):
        ref[pl.ds(s, tm, stride=TOK), :] = packed[:, s * LANE:(s + 1) * LANE]


def _load_token_tiles(ref, rows=None):
    first, tm = rows if rows is not None else (0, ref.shape[0] // TOK)
    lo, hi = [], []
    for s in range(TOK):
        w = ref[pl.ds(first * TOK + s, tm, stride=TOK), :]
        lo.append(lax.bitcast_convert_type(w << 16, F32))
        hi.append(lax.bitcast_convert_type(w & jnp.uint32(HIGH16), F32))
    return jnp.concatenate(lo, 1), jnp.concatenate(hi, 1)


ROW_SPLIT = 2


def _out_ln_kernel(a_ref, b_ref, c_ref, x_ref, w_ref, g_ref, beta_ref, o_ref, op_ref):
    na, nb = a_ref.shape[1], b_ref.shape[1]
    nr = x_ref.shape[0] // ROW_SPLIT
    for r0 in range(0, x_ref.shape[0], nr):
        rows = slice(r0, r0 + nr)
        m = _dot(a_ref[rows, :], w_ref[0:na, :])
        m = m + _dot(b_ref[rows, :], w_ref[na:na + nb, :])
        m = m + _dot(c_ref[rows, :], w_ref[na + nb:, :])
        y = _layer_norm(DEEPNORM_ALPHA * x_ref[rows, :] + m, g_ref[...], beta_ref[...])
        o_ref[rows, :] = y
        _store_token_tiles(op_ref.at[pl.ds(r0 * TOK, nr * TOK)], y)


def out_ln(a, b, c, x, w, g, beta, layer):
    s = x.shape[0]
    tm = min(512, s)
    row = lambda arr: pl.BlockSpec((tm, arr.shape[1]), lambda i: (i, 0))
    full = lambda arr: _layer_block(arr, layer)
    return pl.pallas_call(
        _out_ln_kernel,
        out_shape=[jax.ShapeDtypeStruct((s, D_MODEL), F32), jax.ShapeDtypeStruct((s * TOK, LANE), jnp.uint32)],
        grid=(s // tm,),
        in_specs=[row(a), row(b), row(c), row(x), full(w), full(g), full(beta)],
        out_specs=[pl.BlockSpec((tm, D_MODEL), lambda i: (i, 0)), pl.BlockSpec((tm * TOK, LANE), lambda i: (i, 0))],
        compiler_params=_params(("parallel",), 56),
        name="out_ln",
    )(a, b, c, x, w, g, beta)


FFN_ROWS = 1024
FFN_COLS = 512
FFN_OUT_CHUNK = 512
FFN_SUB = 256


def _ffn_kernel(te_ref, tv_ref, na_ref, xp_ref, wg_ref, wu_ref, wd_ref, o_ref, xb_ref, acc_ref):
    i, j = pl.program_id(0), pl.program_id(1)
    tm = xb_ref.shape[0]
    valid = tv_ref[i]

    @pl.when(j == 0)
    def _():
        acc_ref[...] = jnp.zeros_like(acc_ref)

    @pl.when((j == 0) & (valid > 0))
    def _():
        lo, hi = _load_token_tiles(xp_ref)
        xb_ref[:, :HALF] = lo.astype(BF16)
        xb_ref[:, HALF:] = hi.astype(BF16)

    def rows_step(r0, nr):
        xb = xb_ref[r0:r0 + nr, :]
        gate = _dot(xb, wg_ref[...].astype(BF16))
        up = _dot(xb, wu_ref[...].astype(BF16))
        hmid = (gate * (1.0 / (1.0 + jnp.exp(-gate))) * up).astype(BF16)
        for c in range(0, D_MODEL, FFN_OUT_CHUNK):
            cs = slice(c, c + FFN_OUT_CHUNK)
            acc_ref[r0:r0 + nr, cs] += _dot(hmid, wd_ref[:, cs].astype(BF16))

    nearly_full = valid > tm - FFN_SUB

    @pl.when(nearly_full)
    def _():
        rows_step(0, tm)

    for r0 in range(0, tm - FFN_SUB, FFN_SUB):
        @pl.when(jnp.logical_not(nearly_full) & (valid > r0))
        def _():
            rows_step(r0, FFN_SUB)

    @pl.when(j == pl.num_programs(1) - 1)
    def _():
        _store_token_tiles(o_ref, acc_ref[...])


def grouped_swiglu(xp, wg, wu, wd, tile_expert, tile_valid, n_active, tm, tf):
    n_tiles, nf = xp.shape[0] // (tm * TOK), D_FF // tf

    def tile(i, na):
        return jnp.minimum(i, na[0] - 1)

    def fcol(i, j, na):
        return jnp.where(i < na[0], j, nf - 1)

    return pl.pallas_call(
        _ffn_kernel,
        out_shape=jax.ShapeDtypeStruct(xp.shape, jnp.uint32),
        grid_spec=pltpu.PrefetchScalarGridSpec(
            num_scalar_prefetch=3,
            grid=(n_tiles, nf),
            in_specs=[pl.BlockSpec((tm * TOK, LANE), lambda i, j, te, tv, na: (tile(i, na), 0)),
                      pl.BlockSpec((None, D_MODEL, tf),
                                   lambda i, j, te, tv, na: (te[tile(i, na)], 0, fcol(i, j, na))),
                      pl.BlockSpec((None, D_MODEL, tf),
                                   lambda i, j, te, tv, na: (te[tile(i, na)], 0, fcol(i, j, na))),
                      pl.BlockSpec((None, tf, D_MODEL),
                                   lambda i, j, te, tv, na: (te[tile(i, na)], fcol(i, j, na), 0))],
            out_specs=pl.BlockSpec((tm * TOK, LANE), lambda i, j, te, tv, na: (i, 0)),
            scratch_shapes=[pltpu.VMEM((tm, D_MODEL), BF16), pltpu.VMEM((tm, D_MODEL), F32)],
        ),
        compiler_params=_params(("arbitrary", "arbitrary"), 60),
        name="grouped_swiglu",
    )(tile_expert, tile_valid, n_active, xp, wg, wu, wd)


def _router_kernel(x_ref, w_ref, e_ref, r_ref, wt_ref, cnt_ref, run_ref):
    i = pl.program_id(0)
    tm = x_ref.shape[0]
    ne = N_EXPERTS

    @pl.when(i == 0)
    def _():
        run_ref[...] = jnp.zeros_like(run_ref)

    x = x_ref[...]
    x_hi = x.astype(BF16)
    x_lo = (x - x_hi.astype(F32)).astype(BF16)
    w = w_ref[...]
    w_hi = w.astype(BF16).astype(F32)
    w_lo = w - w_hi
    both = _dot_nt(jnp.concatenate([w_hi, w_lo], 0).astype(BF16), x_hi)
    cross = _dot_nt(jnp.concatenate([w_hi, jnp.zeros_like(w_hi)], 0).astype(BF16), x_lo)
    logits = both[0:ne] + both[ne:2 * ne] + cross[0:ne]

    eidx = lax.broadcasted_iota(jnp.int32, logits.shape, 0).astype(F32)
    v1 = jnp.max(logits, 0, keepdims=True)
    i1 = jnp.min(jnp.where(logits == v1, eidx, float(ne)), 0, keepdims=True)
    rest = jnp.where(eidx == i1, -jnp.inf, logits)
    v2 = jnp.max(rest, 0, keepdims=True)
    i2 = jnp.min(jnp.where(rest == v2, eidx, float(ne)), 0, keepdims=True)
    t = jnp.exp(v2 - v1)
    w1 = 1.0 / (1.0 + t)
    wt_ref[0:1, :] = w1
    wt_ref[1:2, :] = t * w1
    e_ref[0:1, :] = i1.astype(jnp.int32)
    e_ref[1:2, :] = i2.astype(jnp.int32)

    sel1, sel2 = eidx == i1, eidx == i2
    sel = jnp.where(sel1, 1.0, 0.0) + jnp.where(sel2, 1.0, 0.0)
    before = (lax.broadcasted_iota(jnp.int32, (tm, tm), 0) < lax.broadcasted_iota(jnp.int32, (tm, tm), 1))
    sel16 = jnp.concatenate([sel, jnp.zeros_like(sel)], 0).astype(BF16)
    prefix = _dot(sel16, jnp.where(before, 1.0, 0.0).astype(BF16))[0:ne]
    rank = prefix + run_ref[:, 0:1]
    r_ref[0:1, :] = jnp.sum(jnp.where(sel1, rank, 0.0), 0, keepdims=True).astype(jnp.int32)
    r_ref[1:2, :] = jnp.sum(jnp.where(sel2, rank, 0.0), 0, keepdims=True).astype(jnp.int32)
    run_ref[...] = run_ref[...] + jnp.sum(sel, 1, keepdims=True)
    cnt_ref[...] = run_ref[...].astype(jnp.int32)


def route_tokens(x, w_router_t):
    s = x.shape[0]
    tm = min(512, s)
    pair = pl.BlockSpec((2, tm), lambda i: (0, i))
    return pl.pallas_call(
        _router_kernel,
        out_shape=[jax.ShapeDtypeStruct((2, s), jnp.int32), jax.ShapeDtypeStruct((2, s), jnp.int32),
                   jax.ShapeDtypeStruct((2, s), F32), jax.ShapeDtypeStruct((N_EXPERTS, LANE), jnp.int32)],
        grid=(s // tm,),
        in_specs=[pl.BlockSpec((tm, D_MODEL), lambda i: (i, 0)),
                  pl.BlockSpec((N_EXPERTS, D_MODEL), lambda i: (0, 0))],
        out_specs=[pair, pair, pair, pl.BlockSpec((N_EXPERTS, LANE), lambda i: (0, 0))],
        scratch_shapes=[pltpu.VMEM((N_EXPERTS, LANE), F32)],
        compiler_params=_params(("arbitrary",), 32),
        name="moe_router",
    )(x, w_router_t)


def _slot_kernel(base_ref, e_ref, r_ref, s_ref):
    e = e_ref[...]
    slot = r_ref[...]
    for k in range(N_EXPERTS):
        slot = slot + jnp.where(e == k, base_ref[k], 0)
    s_ref[...] = slot


def token_slots(e_idx, rank, base):
    whole = pl.BlockSpec(e_idx.shape, lambda i, *_: (0, 0))
    return pl.pallas_call(
        _slot_kernel,
        out_shape=jax.ShapeDtypeStruct(e_idx.shape, jnp.int32),
        grid_spec=pltpu.PrefetchScalarGridSpec(num_scalar_prefetch=1, grid=(1,), in_specs=[whole, whole],
                                               out_specs=whole),
        compiler_params=_params(("arbitrary",), 32),
        name="token_slots",
    )(base, e_idx, rank)


DMA_UNROLL = 8


def _tile_rows(index):
    return pl.ds(pl.multiple_of(index * TOK, TOK), TOK)


def _dispatch_kernel(fill_ref, slot_ref, x_ref, xs_ref, zero_ref, sem, zsem):
    i = pl.program_id(0)
    tm = x_ref.shape[0] // TOK

    def issue(b, c):
        for u in range(DMA_UNROLL):
            t = b * DMA_UNROLL + u
            for k in range(2):
                pltpu.make_async_copy(x_ref.at[_tile_rows(t)], xs_ref.at[_tile_rows(slot_ref[k, t])],
                                      sem).start(priority=k)
        return c

    lax.fori_loop(0, tm // DMA_UNROLL, issue, 0)
    for k in range(2):
        pltpu.make_async_copy(x_ref, xs_ref.at[pl.ds(0, tm * TOK)], sem).wait()

    @pl.when(i == pl.num_programs(0) - 1)
    def _():
        zero_ref[...] = jnp.zeros_like(zero_ref)

        def zero_copy(slot):
            return pltpu.make_async_copy(zero_ref.at[pl.ds(0, TOK)], xs_ref.at[_tile_rows(slot)], zsem)

        for e in range(N_EXPERTS):
            lo, hi = fill_ref[0, e], fill_ref[1, e]

            def zissue(slot, c):
                zero_copy(slot).start()
                return c

            def zdrain(slot, c):
                zero_copy(slot).wait()
                return c

            lax.fori_loop(lo, hi, zissue, 0)
            lax.fori_loop(lo, hi, zdrain, 0)

        zrows = zero_ref.shape[0]
        zb = zrows // TOK

        def block_copy(b):
            return pltpu.make_async_copy(zero_ref, xs_ref.at[pl.ds(pl.multiple_of(b * zrows, zrows), zrows)], zsem)

        def bissue(b, c):
            block_copy(b).start()
            return c

        def bdrain(b, c):
            block_copy(b).wait()
            return c

        first, last = fill_ref[2, 0] // zb, xs_ref.shape[0] // zrows
        lax.fori_loop(first, last, bissue, 0)
        lax.fori_loop(first, last, bdrain, 0)


def moe_dispatch(xp, slots, fill, rows):
    s = xp.shape[0] // TOK
    tm = min(512, s)
    zb = min(256, s)
    return pl.pallas_call(
        _dispatch_kernel,
        out_shape=jax.ShapeDtypeStruct((rows * TOK, LANE), xp.dtype),
        grid_spec=pltpu.PrefetchScalarGridSpec(
            num_scalar_prefetch=1,
            grid=(s // tm,),
            in_specs=[pl.BlockSpec((2, tm), lambda i, *_: (0, i), memory_space=pltpu.SMEM),
                      pl.BlockSpec((tm * TOK, LANE), lambda i, *_: (i, 0))],
            out_specs=pl.BlockSpec(memory_space=pl.ANY),
            scratch_shapes=[pltpu.VMEM((zb * TOK, LANE), xp.dtype), pltpu.SemaphoreType.DMA,
                            pltpu.SemaphoreType.DMA],
        ),
        compiler_params=_params(("arbitrary",), 32),
        name="moe_dispatch",
    )(fill, slots, xp)


def _ple(x, p, wup_ref, wgate_ref, bgate_ref):
    up = _dot(p.astype(BF16), wup_ref[...])
    zg = _dot(x.astype(BF16), wgate_ref[...]) + bgate_ref[...]
    return up * (1.0 / (1.0 + jnp.exp(-zg)))


def _ple_ln_dense_kernel(x_ref, f_ref, p_ref, wup_ref, wgate_ref, bgate_ref, g_ref, beta_ref, o_ref):
    nr = x_ref.shape[0] // ROW_SPLIT
    for r0 in range(0, x_ref.shape[0], nr):
        rows = slice(r0, r0 + nr)
        x = x_ref[rows, :]
        f = jnp.concatenate(_load_token_tiles(f_ref, (r0, nr)), 1)
        ple = _ple(x, p_ref[rows, :], wup_ref, wgate_ref, bgate_ref)
        o_ref[rows, :] = _layer_norm(DEEPNORM_ALPHA * x + f + ple, g_ref[...], beta_ref[...])


def _ple_ln_moe_kernel(slot_ref, x_ref, wt_ref, p_ref, wup_ref, wgate_ref, bgate_ref, g_ref, beta_ref, ys_ref,
                       o_ref, y1_ref, y2_ref, sem):
    tm = x_ref.shape[0]
    bufs = (y1_ref, y2_ref)

    def issue(b, c):
        for u in range(DMA_UNROLL):
            t = b * DMA_UNROLL + u
            for k in range(2):
                pltpu.make_async_copy(ys_ref.at[_tile_rows(slot_ref[k, t])], bufs[k].at[_tile_rows(t)],
                                      sem).start(priority=k)
        return c

    lax.fori_loop(0, tm // DMA_UNROLL, issue, 0)
    nr = tm // ROW_SPLIT
    ples = [_ple(x_ref[r0:r0 + nr, :], p_ref[r0:r0 + nr, :], wup_ref, wgate_ref, bgate_ref)
            for r0 in range(0, tm, nr)]
    for k in range(2):
        pltpu.make_async_copy(ys_ref.at[pl.ds(0, tm * TOK)], bufs[k], sem).wait()
    for i, r0 in enumerate(range(0, tm, nr)):
        rows = slice(r0, r0 + nr)
        wt = wt_ref[rows, :]
        f = (wt[:, 0:1] * jnp.concatenate(_load_token_tiles(y1_ref, (r0, nr)), 1)
             + wt[:, 1:2] * jnp.concatenate(_load_token_tiles(y2_ref, (r0, nr)), 1))
        o_ref[rows, :] = _layer_norm(DEEPNORM_ALPHA * x_ref[rows, :] + f + ples[i], g_ref[...], beta_ref[...])


def ple_ln(x, p, wup, wgate, bgate, g, beta, layer, f=None, moe=None):
    s = x.shape[0]
    tm = min(512, s)
    row = lambda arr: pl.BlockSpec((tm, arr.shape[1]), lambda i, *_: (i, 0))
    full = lambda arr: _layer_block(arr, layer)
    p_spec = pl.BlockSpec((None, None, tm, p.shape[-1]), lambda i, *_: (layer, 0, i, 0))
    tail = [p_spec, full(wup), full(wgate), full(bgate), full(g), full(beta)]
    out_spec = pl.BlockSpec((tm, D_MODEL), lambda i, *_: (i, 0))
    out_shape = jax.ShapeDtypeStruct((s, D_MODEL), F32)
    if moe is None:
        return pl.pallas_call(
            _ple_ln_dense_kernel, out_shape=out_shape, grid=(s // tm,),
            in_specs=[row(x), pl.BlockSpec((tm * TOK, LANE), lambda i: (i, 0))] + tail, out_specs=out_spec,
            compiler_params=_params(("parallel",), 56), name="ple_ln_dense",
        )(x, f, p, wup, wgate, bgate, g, beta)
    ys, slots, wts_t = moe
    return pl.pallas_call(
        _ple_ln_moe_kernel,
        out_shape=out_shape,
        grid_spec=pltpu.PrefetchScalarGridSpec(
            num_scalar_prefetch=0,
            grid=(s // tm,),
            in_specs=[pl.BlockSpec((2, tm), lambda i: (0, i), memory_space=pltpu.SMEM), row(x), row(wts_t)] + tail
            + [pl.BlockSpec(memory_space=pl.ANY)],
            out_specs=out_spec,
            scratch_shapes=[pltpu.VMEM((tm * TOK, LANE), jnp.uint32), pltpu.VMEM((tm * TOK, LANE), jnp.uint32),
                            pltpu.SemaphoreType.DMA],
        ),
        compiler_params=_params(("arbitrary",), 56),
        name="ple_ln_moe",
    )(slots, x, wts_t, p, wup, wgate, bgate, g, beta, ys)


IN_SIZES = (MLA_Q_RANK, MLA_KV_RANK, MLA_ROPE, GLA_HEADS * GLA_DK, GLA_HEADS * GLA_DK, GLA_HEADS * GLA_DV,
            GLA_GATE_RANK, GLA_HEADS * GLA_DV, SWA_HEADS * SWA_HD, SWA_KV_HEADS * SWA_HD, SWA_KV_HEADS * SWA_HD)
IN_COLS = sum(IN_SIZES)
IN_DEST = (H_QA, H_KVA, H_KR, H_GQ, H_GK, H_GV, H_GLR, H_GR, H_SQ, H_SK, H_SV)


def _pack_w_in_kernel(w_ref, o_ref):
    src = 0
    for width, dst in zip(IN_SIZES, IN_DEST):
        o_ref[dst:dst + width, :] = w_ref[src:src + width, :].astype(BF16)
        pad = -width % LANE
        if pad:
            o_ref[dst + width:dst + width + pad, :] = jnp.zeros((pad, o_ref.shape[1]), BF16)
        src += width


def _pack_w_in(w_t):
    depth, _, d = w_t.shape
    tc = 512
    return pl.pallas_call(
        _pack_w_in_kernel,
        out_shape=jax.ShapeDtypeStruct((depth, H_COLS, d), BF16),
        grid=(depth, d // tc),
        in_specs=[pl.BlockSpec((None, IN_COLS, tc), lambda l, i: (l, 0, i))],
        out_specs=pl.BlockSpec((None, H_COLS, tc), lambda l, i: (l, 0, i)),
        compiler_params=_params(("parallel", "parallel"), 40),
        name="pack_w_in",
    )(w_t)


def _pack_w_q_b(w):
    w = w.reshape(MLA_Q_RANK, MLA_HEADS, MLA_NOPE + MLA_ROPE)
    w = jnp.pad(w, ((0, 0), (0, 0), (0, MLA_QK - MLA_NOPE - MLA_ROPE)))
    return w.reshape(MLA_Q_RANK, MLA_HEADS * MLA_QK).astype(BF16)


def _token_mixer_ln(x, rope, layer, w_in_p, q_gain, w_q_b, kv_gain, w_kv_b, gla_w, gla_b, gla_gain, sinks,
                    rel_bias, w_out_b, ln_g, ln_b):
    h = proj_in(x, w_in_p, layer)
    q, k, v = mla_prep(h, *rope, q_gain.reshape(1, -1), kv_gain.reshape(1, -1), _pack_w_q_b(w_q_b),
                       w_kv_b.astype(BF16))
    a = mla_flash(q, k, v)
    gla_w_p = jnp.pad(gla_w, ((0, LANE - GLA_GATE_RANK), (0, 0))).astype(BF16)
    b = gla(h, gla_w_p, gla_b.reshape(1, -1), gla_gain.reshape(1, -1))
    c = swa(h, sinks, rel_bias)
    return out_ln(a, b, c, x, w_out_b, ln_g, ln_b, layer)


def _moe_plan(counts, tm, n_tiles):
    tiles = (counts + tm - 1) // tm
    ends = jnp.cumsum(tiles)
    base = (ends - tiles) * tm
    n_active = ends[-1:].astype(jnp.int32)
    tile_expert = jnp.searchsorted(ends, jnp.arange(n_tiles, dtype=jnp.int32), side="right")
    tile_expert = jnp.minimum(tile_expert, N_EXPERTS - 1).astype(jnp.int32)
    tile_start = jnp.arange(n_tiles, dtype=jnp.int32) * tm
    tile_valid = jnp.clip((base + counts)[tile_expert] - tile_start, 0, tm).astype(jnp.int32)
    used_rows = jnp.broadcast_to(ends[-1] * tm, counts.shape)
    fill = jnp.stack([base + counts, base + tiles * tm, used_rows]).astype(jnp.int32)
    return base.astype(jnp.int32), fill, tile_expert, tile_valid, n_active


def kernel(x, p, positions, w_in, mla_q_a_gain, mla_w_q_b, mla_kv_a_gain, mla_w_kv_b, gla_w_gate, gla_b_gate,
           gla_norm_gain, swa_sinks, rel_bias, w_out, ln1_g, ln1_b, ffn_w_gate, ffn_w_up, ffn_w_down,
           moe_router, moe_w_gate, moe_w_up, moe_w_down, ple_w_up, ple_w_gate, ple_b_gate, ln2_g, ln2_b):
    batch, s, _ = x.shape
    assert batch == 1
    xcur = x.reshape(s, D_MODEL)
    rope = rope_tables(positions)
    tm = min(FFN_ROWS, s)
    row_stack = lambda v: v.reshape(DEPTH, 1, -1)
    w_in_p = _pack_w_in(jnp.swapaxes(w_in, 1, 2))
    w_out_b = w_out.astype(BF16)
    tail = (p, ple_w_up.astype(BF16), ple_w_gate.astype(BF16), row_stack(ple_b_gate), row_stack(ln2_g),
            row_stack(ln2_b))
    for i in range(DEPTH):
        x1, x1p = _token_mixer_ln(xcur, rope, i, w_in_p, mla_q_a_gain[i], mla_w_q_b[i], mla_kv_a_gain[i],
                                  mla_w_kv_b[i], gla_w_gate[i], gla_b_gate[i], gla_norm_gain[i], swa_sinks[i],
                                  rel_bias, w_out_b, row_stack(ln1_g), row_stack(ln1_b))
        j = i // 2
        if i % 2 == 0:
            n_tiles = s // tm
            f = grouped_swiglu(x1p, ffn_w_gate[j][None], ffn_w_up[j][None], ffn_w_down[j][None],
                               jnp.zeros((n_tiles,), jnp.int32), jnp.full((n_tiles,), tm, jnp.int32),
                               jnp.full((1,), n_tiles, jnp.int32), tm, FFN_COLS)
            xcur = ple_ln(x1, *tail, i, f=f)
        else:
            n_tiles = (2 * s + N_EXPERTS * (tm - 1)) // tm
            e_idx, rank, wts, counts = route_tokens(x1, moe_router[j].T)
            base, fill, tile_expert, tile_valid, n_active = _moe_plan(counts[:, 0], tm, n_tiles)
            slots = token_slots(e_idx, rank, base)
            xs = moe_dispatch(x1p, slots, fill, n_tiles * tm)
            ys = grouped_swiglu(xs, moe_w_gate[j], moe_w_up[j], moe_w_down[j], tile_expert, tile_valid, n_active,
                                tm, FFN_COLS)
            xcur = ple_ln(x1, *tail, i, moe=(ys, slots, wts.T))
    return xcur.reshape(batch, s, D_MODEL)
```

```python
import functools
import math

import numpy as np
import jax
import jax.numpy as jnp
from jax import lax
from jax.experimental import pallas as pl
from jax.experimental.pallas import tpu as pltpu

F32 = jnp.float32
BF16 = jnp.bfloat16

D_MODEL = 2048
DEPTH = 2
MLA_HEADS = 8
MLA_Q_RANK = 512
MLA_KV_RANK = 256
MLA_NOPE = 128
MLA_ROPE = 64
MLA_V = 128
ROPE_THETA = 10000.0
GLA_HEADS = 4
GLA_DK = 64
GLA_DV = 128
GLA_GATE_RANK = 16
GLA_TAU = 16.0
SWA_HEADS = 8
SWA_KV_HEADS = 2
SWA_HD = 64
SWA_WINDOW = 128
SWA_BLOCK = 128
REL_BUCKETS = 32
REL_MAX_DIST = 128
D_FF = 5632
N_EXPERTS = 8
PLE_DIM = 256
LN_EPS = 1e-5
RMS_EPS = 1e-6
DEEPNORM_ALPHA = (2 * DEPTH) ** 0.25

LANE = 128
LOG2E = math.log2(math.e)
NEG_BIG = -1e30

H_QA, H_GV, H_GR, H_SQ = 0, 512, 1024, 1536
H_KVA, H_GQ, H_GK = 2048, 2304, 2560
H_KR, H_GLR, H_SK, H_SV = 2816, 2944, 3072, 3200
H_COLS = 3328

MLA_QK = 2 * LANE

GLA_L = 128
GLA_SUB = 32
GLA_NSUB = GLA_L // GLA_SUB


def _params(sem, vmem_mb):
    return pltpu.CompilerParams(dimension_semantics=sem, vmem_limit_bytes=vmem_mb * 2 ** 20)


def _dot(a, b):
    return jnp.dot(a, b, preferred_element_type=F32)


def _dot_nt(a, b):
    return lax.dot_general(a, b, (((1,), (1,)), ((), ())), preferred_element_type=F32)


def _dot_tn(a, b):
    return lax.dot_general(a, b, (((0,), (0,)), ((), ())), preferred_element_type=F32)


def _proj_in_kernel(x_ref, w_ref, o_ref, xb_ref):
    @pl.when(pl.program_id(1) == 0)
    def _():
        xb_ref[...] = x_ref[...].astype(BF16)

    o_ref[...] = _dot_nt(xb_ref[...], w_ref[...]).astype(o_ref.dtype)


def _layer_block(arr, layer):
    zeros = (0,) * (arr.ndim - 1)
    return pl.BlockSpec((None,) + arr.shape[1:], lambda i, *_: (layer,) + zeros, pipeline_mode=pl.Buffered(1))


def proj_in(x, w_p, layer):
    s = x.shape[0]
    tm = min(1024, s)
    tn = H_COLS // 2
    return pl.pallas_call(
        _proj_in_kernel,
        out_shape=jax.ShapeDtypeStruct((s, H_COLS), BF16),
        grid=(s // tm, H_COLS // tn),
        in_specs=[pl.BlockSpec((tm, D_MODEL), lambda i, j: (i, 0)),
                  pl.BlockSpec((None, tn, D_MODEL), lambda i, j: (layer, j, 0))],
        out_specs=pl.BlockSpec((tm, tn), lambda i, j: (i, j)),
        scratch_shapes=[pltpu.VMEM((tm, D_MODEL), BF16)],
        compiler_params=_params(("parallel", "arbitrary"), 56),
        name="proj_in",
    )(x, w_p)


def _rope_table_kernel(pos_ref, inv_ref, cos_ref, sa_ref, sb_ref):
    ang = pos_ref[...].astype(F32) * inv_ref[...]
    lane = lax.broadcasted_iota(jnp.int32, ang.shape, 1)
    half = MLA_ROPE // 2
    c, s = jnp.cos(ang), jnp.sin(ang)
    cos_ref[...] = c
    sa_ref[...] = jnp.where((lane >= half) & (lane < 2 * half), s, 0.0)
    sb_ref[...] = jnp.where(lane < half, -s, 0.0)


def rope_tables(positions):
    s = positions.shape[-1]
    half = MLA_ROPE // 2
    inv = ROPE_THETA ** (-jnp.arange(half, dtype=F32) / half)
    inv = jnp.concatenate([inv, inv, jnp.zeros((LANE - 2 * half,), F32)]).reshape(1, LANE)
    tm = min(1024, s)
    spec = pl.BlockSpec((tm, LANE), lambda i: (i, 0))
    return pl.pallas_call(
        _rope_table_kernel,
        out_shape=[jax.ShapeDtypeStruct((s, LANE), F32)] * 3,
        grid=(s // tm,),
        in_specs=[pl.BlockSpec((tm, 1), lambda i: (i, 0)), pl.BlockSpec((1, LANE), lambda i: (0, 0))],
        out_specs=[spec, spec, spec],
        compiler_params=_params(("parallel",), 32),
        name="rope_tables",
    )(positions.reshape(s, 1), inv)


def _rope(x, cos, sa, sb):
    return x * cos + pltpu.roll(x, MLA_ROPE // 2, 1) * sa + pltpu.roll(x, LANE - MLA_ROPE // 2, 1) * sb


def _mla_prep_kernel(qa_ref, kva_ref, kr_ref, cos_ref, sa_ref, sb_ref, gq_ref, gkv_ref, wq_ref, wkv_ref,
                     q_out, k_out, v_out):
    cos, sa, sb = cos_ref[...], sa_ref[...], sb_ref[...]
    qscale = (MLA_NOPE + MLA_ROPE) ** -0.5 * LOG2E

    qa = qa_ref[...].astype(F32)
    qn = qa * lax.rsqrt(jnp.mean(qa * qa, -1, keepdims=True) + RMS_EPS) * gq_ref[...]
    q = _dot(qn.astype(BF16), wq_ref[...])
    for h in range(MLA_HEADS):
        c0 = h * MLA_QK
        q_out[h, :, 0:LANE] = (q[:, c0:c0 + LANE] * qscale).astype(BF16)
        pe = _rope(q[:, c0 + LANE:c0 + 2 * LANE], cos, sa, sb)
        q_out[h, :, LANE:2 * LANE] = (pe * qscale).astype(BF16)

    kva = kva_ref[...].astype(F32)
    kvn = kva * lax.rsqrt(jnp.mean(kva * kva, -1, keepdims=True) + RMS_EPS) * gkv_ref[...]
    kv = _dot(kvn.astype(BF16), wkv_ref[...])
    kpe = _rope(kr_ref[...].astype(F32), cos, sa, sb).astype(BF16)
    for h in range(MLA_HEADS):
        c0 = h * (MLA_NOPE + MLA_V)
        k_out[h, :, 0:LANE] = kv[:, c0:c0 + MLA_NOPE].astype(BF16)
        k_out[h, :, LANE:2 * LANE] = kpe
        v_out[h, :, 0:MLA_V] = kv[:, c0 + MLA_NOPE:c0 + MLA_NOPE + MLA_V].astype(BF16)
        v_out[h, :, MLA_V:2 * MLA_V] = jnp.ones((kv.shape[0], MLA_V), BF16)


def mla_prep(h, cos, sa, sb, gq, gkv, wq_p, wkv):
    s = h.shape[0]
    tm = min(512, s)
    row = lambda width, col: pl.BlockSpec((tm, width), lambda i: (i, col // width))
    full = lambda a: pl.BlockSpec(a.shape, lambda i: (0,) * a.ndim)
    return pl.pallas_call(
        _mla_prep_kernel,
        out_shape=[jax.ShapeDtypeStruct((MLA_HEADS, s, MLA_QK), BF16),
                   jax.ShapeDtypeStruct((MLA_HEADS, s, MLA_QK), BF16),
                   jax.ShapeDtypeStruct((MLA_HEADS, s, 2 * MLA_V), BF16)],
        grid=(s // tm,),
        in_specs=[row(MLA_Q_RANK, H_QA), row(MLA_KV_RANK, H_KVA), row(LANE, H_KR),
                  row(LANE, 0), row(LANE, 0), row(LANE, 0),
                  full(gq), full(gkv), full(wq_p), full(wkv)],
        out_specs=[pl.BlockSpec((MLA_HEADS, tm, MLA_QK), lambda i: (0, i, 0)),
                   pl.BlockSpec((MLA_HEADS, tm, MLA_QK), lambda i: (0, i, 0)),
                   pl.BlockSpec((MLA_HEADS, tm, 2 * MLA_V), lambda i: (0, i, 0))],
        compiler_params=_params(("parallel",), 48),
        name="mla_prep",
    )(h, h, h, cos, sa, sb, gq, gkv, wq_p, wkv)


def _mla_flash_kernel(q_ref, k_ref, v_ref, o_ref, sa_ref, sb_ref, mxa_ref, mxb_ref, m_ref, acc_ref, *, t, nh):
    qi = pl.program_id(1)
    heads = range(nh)
    bufs = ((sa_ref, mxa_ref), (sb_ref, mxb_ref))

    def produce(b, parity, masked):
        s_ref, mx_ref = bufs[parity]
        start = pl.multiple_of(b * t, t)
        for h in heads:
            s = _dot_nt(q_ref[h], k_ref[h, pl.ds(start, t), :])
            if masked:
                rows = lax.broadcasted_iota(jnp.int32, s.shape, 0) + qi * t
                cols = lax.broadcasted_iota(jnp.int32, s.shape, 1) + b * t
                s = jnp.where(cols <= rows, s, NEG_BIG)
            s_ref[h] = s
            mx_ref[h] = jnp.broadcast_to(jnp.max(s, -1, keepdims=True), (t, LANE))

    def absorb(b, parity):
        s_ref, mx_ref = bufs[parity]
        start = pl.multiple_of(b * t, t)
        for h in heads:
            m_new = jnp.maximum(m_ref[h], mx_ref[h])
            alpha = jnp.exp2(m_ref[h] - m_new)
            p = jnp.exp2(s_ref[h] - jnp.concatenate([m_new] * (t // LANE), 1))
            acc_ref[h] = (jnp.concatenate([alpha, alpha], 1) * acc_ref[h]
                          + _dot(p.astype(BF16), v_ref[h, pl.ds(start, t), :]))
            m_ref[h] = m_new

    m_ref[...] = jnp.full(m_ref.shape, NEG_BIG, F32)
    acc_ref[...] = jnp.zeros_like(acc_ref)
    produce(0, 0, True)

    def pair(i, c):
        b = 2 * i
        produce(b + 1, 1, False)
        absorb(b, 0)
        produce(b + 2, 0, False)
        absorb(b + 1, 1)
        return c

    n_pairs = jnp.maximum(qi - 1, 0) // 2
    lax.fori_loop(0, n_pairs, pair, 0)
    done = 2 * n_pairs

    @pl.when(qi == 0)
    def _():
        absorb(0, 0)

    @pl.when((qi > 0) & (qi - done == 1))
    def _():
        produce(qi, 1, True)
        absorb(qi - 1, 0)
        absorb(qi, 1)

    @pl.when((qi > 0) & (qi - done == 2))
    def _():
        produce(qi - 1, 1, False)
        absorb(qi - 2, 0)
        produce(qi, 0, True)
        absorb(qi - 1, 1)
        absorb(qi, 0)

    for h in heads:
        acc = acc_ref[h]
        o_ref[:, h * MLA_V:(h + 1) * MLA_V] = (acc[:, :MLA_V] / acc[:, MLA_V:]).astype(o_ref.dtype)


def mla_flash(q, k, v1):
    _, s, _ = q.shape
    t = min(512, s)
    nh = 2
    return pl.pallas_call(
        functools.partial(_mla_flash_kernel, t=t, nh=nh),
        out_shape=jax.ShapeDtypeStruct((s, MLA_HEADS * MLA_V), BF16),
        grid=(MLA_HEADS // nh, s // t),
        in_specs=[pl.BlockSpec((nh, t, MLA_QK), lambda h, i: (h, i, 0)),
                  pl.BlockSpec((nh, s, MLA_QK), lambda h, i: (h, 0, 0)),
                  pl.BlockSpec((nh, s, 2 * MLA_V), lambda h, i: (h, 0, 0))],
        out_specs=pl.BlockSpec((t, nh * MLA_V), lambda h, i: (i, h)),
        scratch_shapes=[pltpu.VMEM((nh, t, t), F32), pltpu.VMEM((nh, t, t), F32),
                        pltpu.VMEM((nh, t, LANE), F32), pltpu.VMEM((nh, t, LANE), F32),
                        pltpu.VMEM((nh, t, LANE), F32), pltpu.VMEM((nh, t, 2 * MLA_V), F32)],
        compiler_params=_params(("parallel", "arbitrary"), 56),
        name="mla_flash",
    )(q, k, v1)


def _gla_masks():
    i = np.arange(GLA_L)[:, None]
    j = np.arange(GLA_L)[None, :]
    same = (i // GLA_SUB) == (j // GLA_SUB)
    mats = [j <= i, same & (j <= i), same & (j > i), j > i]
    for sub in range(GLA_NSUB - 1):
        mats.append((j >= (sub + 1) * GLA_SUB) & (j <= i))
    return np.concatenate(mats, 0).astype(np.float32)


GLA_CHUNKS_PER_STEP = 4


def _gla_kernel(q_ref, k_ref, v_ref, lr_ref, r_ref, wg_ref, bg_ref, gain_ref, mask_ref, o_ref, state_ref):
    @pl.when(pl.program_id(0) == 0)
    def _():
        state_ref[...] = jnp.zeros_like(state_ref)

    L, hk = GLA_L, GLA_HEADS * GLA_DK
    chunks = [slice(c * L, (c + 1) * L) for c in range(q_ref.shape[0] // L)]
    heads = [(slice(h * GLA_DK, (h + 1) * GLA_DK), slice(h * GLA_DV, (h + 1) * GLA_DV)) for h in range(GLA_HEADS)]

    z = _dot(lr_ref[...], wg_ref[...]) + bg_ref[...]
    g = (jnp.minimum(z, 0.0) - jnp.log(1.0 + jnp.exp(-jnp.abs(z)))) * (1.0 / GLA_TAU)
    g_hi = g.astype(BF16)
    g_lo = (g - g_hi.astype(F32)).astype(BF16)

    cums = [_dot(mask_ref[...], jnp.concatenate([g_hi[rs], g_lo[rs]], 0)) for rs in chunks]

    sub_of_row = lax.broadcasted_iota(jnp.int32, (L, hk), 0) // GLA_SUB
    prep = []
    for rs, cum in zip(chunks, cums):
        b_all, b_loc, sfx_loc, sfx_all = cum[0:L], cum[L:2 * L], cum[2 * L:3 * L], cum[3 * L:4 * L]
        q = q_ref[rs, :].astype(F32) * (GLA_DK ** -0.5)
        k = k_ref[rs, :].astype(F32)
        k_end = k * jnp.exp(sfx_loc)
        prep.append(dict(
            decay=jnp.exp(b_all[L - 1:L, :]),
            q_inter=(q * jnp.exp(b_all)).astype(BF16),
            q_diag=(q * jnp.exp(b_loc)).astype(BF16),
            k_diag=(k * jnp.exp(-b_loc)).astype(BF16),
            k_state=(k * jnp.exp(sfx_all)).astype(BF16),
            q_off=[(q * jnp.exp(cum[(4 + sub) * L:(5 + sub) * L])).astype(BF16) for sub in range(GLA_NSUB - 1)],
            k_off=[jnp.where(sub_of_row == sub, k_end, 0.0).astype(BF16) for sub in range(GLA_NSUB - 1)]))

    row = lax.broadcasted_iota(jnp.int32, (L, L), 0)
    col = lax.broadcasted_iota(jnp.int32, (L, L), 1)
    diag_ok = ((row // GLA_SUB) == (col // GLA_SUB)) & (col <= row)
    off_ok = (row // GLA_SUB) > (col // GLA_SUB)
    local = []
    for rs, pr in zip(chunks, prep):
        per_head = []
        for ks, vs in heads:
            v_h = v_ref[rs, vs]
            a = jnp.where(diag_ok, _dot_nt(pr["q_diag"][:, ks], pr["k_diag"][:, ks]), 0.0)
            qo = jnp.concatenate([t[:, ks] for t in pr["q_off"]], 1)
            ko = jnp.concatenate([t[:, ks] for t in pr["k_off"]], 1)
            a = a + jnp.where(off_ok, _dot_nt(qo, ko), 0.0)
            per_head.append((_dot(a.astype(BF16), v_h), _dot_tn(v_h, pr["k_state"][:, ks])))
        local.append(per_head)

    states = [state_ref[h] for h in range(GLA_HEADS)]
    for rs, pr, per_head in zip(chunks, prep, local):
        r = r_ref[rs, :].astype(F32)
        gate = r * (1.0 / (1.0 + jnp.exp(-r)))
        for h, (ks, vs) in enumerate(heads):
            o_intra, vk = per_head[h]
            o = _dot_nt(pr["q_inter"][:, ks], states[h].astype(BF16)) + o_intra
            states[h] = states[h] * pr["decay"][:, ks] + vk
            o = o * lax.rsqrt(jnp.mean(o * o, -1, keepdims=True) + RMS_EPS) * gain_ref[:, vs]
            o_ref[rs, vs] = (o * gate[:, vs]).astype(o_ref.dtype)
    for h in range(GLA_HEADS):
        state_ref[h] = states[h]


def gla(h, wg_p, bg, gain):
    s = h.shape[0]
    rows = min(GLA_CHUNKS_PER_STEP * GLA_L, s)
    masks = jnp.asarray(np.tile(_gla_masks(), (1, 2)), BF16)
    row = lambda width, col: pl.BlockSpec((rows, width), lambda i: (i, col // width))
    full = lambda a: pl.BlockSpec(a.shape, lambda i: (0,) * a.ndim)
    hk, hv = GLA_HEADS * GLA_DK, GLA_HEADS * GLA_DV
    return pl.pallas_call(
        _gla_kernel,
        out_shape=jax.ShapeDtypeStruct((s, hv), BF16),
        grid=(s // rows,),
        in_specs=[row(hk, H_GQ), row(hk, H_GK), row(hv, H_GV), row(LANE, H_GLR), row(hv, H_GR),
                  full(wg_p), full(bg), full(gain), full(masks)],
        out_specs=pl.BlockSpec((rows, hv), lambda i: (i, 0)),
        scratch_shapes=[pltpu.VMEM((GLA_HEADS, GLA_DV, GLA_DK), F32)],
        compiler_params=_params(("arbitrary",), 32),
        name="gla",
    )(h, h, h, h, h, wg_p, bg, gain, masks)


def _t5_bucket_table():
    L = SWA_BLOCK
    dist = np.arange(L)[:, None] + L - np.arange(2 * L)[None, :]
    d = np.clip(dist, 0, None)
    max_exact = REL_BUCKETS // 2
    df = np.maximum(d, 1).astype(np.float32)
    large = max_exact + (np.log(df / np.float32(max_exact)) / np.float32(math.log(REL_MAX_DIST / max_exact))
                         * np.float32(REL_BUCKETS - max_exact)).astype(np.int32)
    large = np.minimum(large, REL_BUCKETS - 1)
    bucket = np.where(d < max_exact, d, large)
    in_window = (dist >= 0) & (dist < SWA_WINDOW)
    return np.where(in_window, bucket, -1).astype(np.int32)


def _swa_kernel(relb_ref, sink_ref, q_ref, kp_ref, kc_ref, vp_ref, vc_ref, bucket_ref, o_ref, bias_ref):
    i = pl.program_id(0)
    L = SWA_BLOCK

    @pl.when(i == 0)
    def _():
        bucket = bucket_ref[...]
        for h in range(SWA_HEADS):
            acc = jnp.full(bucket.shape, NEG_BIG, F32)
            for b in range(REL_BUCKETS):
                acc = jnp.where(bucket == b, relb_ref[b, h], acc)
            bias_ref[h] = acc

    kcat = jnp.concatenate([kp_ref[...], kc_ref[...]], 0)
    vcat = jnp.concatenate([vp_ref[...], vc_ref[...]], 0)
    scale = SWA_HD ** -0.5
    assert math.log2(scale).is_integer()
    q = q_ref[...] * scale
    col = lax.broadcasted_iota(jnp.int32, (L, 2 * L), 1)
    real_key = (col >= L) | (i > 0)
    g = SWA_HEADS // SWA_KV_HEADS
    ones = jnp.ones((2 * L, SWA_HD), BF16)
    v1 = [jnp.concatenate([vcat[:, kv * SWA_HD:(kv + 1) * SWA_HD], ones], 1) for kv in range(SWA_KV_HEADS)]
    for h in range(SWA_HEADS):
        kv = h // g
        hs = slice(h * SWA_HD, (h + 1) * SWA_HD)
        kvs = slice(kv * SWA_HD, (kv + 1) * SWA_HD)
        s = _dot_nt(q[:, hs], kcat[:, kvs]) + bias_ref[h]
        s = jnp.where(real_key, s, NEG_BIG)
        sink = sink_ref[h]
        m = jnp.maximum(jnp.max(s, -1, keepdims=True), sink)
        o2 = _dot(jnp.exp(s - m).astype(BF16), v1[kv])
        denom = o2[:, SWA_HD:] + jnp.exp(sink - m)
        o_ref[:, hs] = (o2[:, :SWA_HD] / denom).astype(o_ref.dtype)


def swa(h, sinks, rel_bias):
    s = h.shape[0]
    L = SWA_BLOCK
    bucket = jnp.asarray(_t5_bucket_table())
    kvw = SWA_KV_HEADS * SWA_HD
    hw = SWA_HEADS * SWA_HD
    cur = lambda width, col: pl.BlockSpec((L, width), lambda i, *_: (i, col // width))
    prev = lambda width, col: pl.BlockSpec((L, width), lambda i, *_: (jnp.maximum(i - 1, 0), col // width))
    return pl.pallas_call(
        _swa_kernel,
        out_shape=jax.ShapeDtypeStruct((s, hw), BF16),
        grid_spec=pltpu.PrefetchScalarGridSpec(
            num_scalar_prefetch=2,
            grid=(s // L,),
            in_specs=[cur(hw, H_SQ), prev(kvw, H_SK), cur(kvw, H_SK), prev(kvw, H_SV), cur(kvw, H_SV),
                      pl.BlockSpec(bucket.shape, lambda i, *_: (0, 0))],
            out_specs=pl.BlockSpec((L, hw), lambda i, *_: (i, 0)),
            scratch_shapes=[pltpu.VMEM((SWA_HEADS, L, 2 * L), F32)],
        ),
        compiler_params=_params(("arbitrary",), 32),
        name="swa",
    )(rel_bias, sinks, h, h, h, h, h, bucket)


def _layer_norm(y, g, b):
    mu = jnp.mean(y, -1, keepdims=True)
    yc = y - mu
    var = jnp.mean(yc * yc, -1, keepdims=True)
    return yc * lax.rsqrt(var + LN_EPS) * g + b


HALF = D_MODEL // 2
HIGH16 = 0xFFFF0000


SUBLANE = 8
TOK = HALF // LANE


def _store_token_tiles(ref, x):
    tm = x.shape[0]
    lo = lax.bitcast_convert_type(x[:, :HALF].astype(BF16).astype(F32), jnp.uint32) >> 16
    hi = lax.bitcast_convert_type(x[:, HALF:].astype(BF16).astype(F32), jnp.uint32) & jnp.uint32(HIGH16)
    packed = lo | hi
    for s in range(TOK):
        ref[pl.ds(s, tm, stride=TOK), :] = packed[:, s * LANE:(s + 1) * LANE]


def _load_token_tiles(ref, rows=None):
    first, tm = rows if rows is not None else (0, ref.shape[0] // TOK)
    lo, hi = [], []
    for s in range(TOK):
        w = ref[pl.ds(first * TOK + s, tm, stride=TOK), :]
        lo.append(lax.bitcast_convert_type(w << 16, F32))
        hi.append(lax.bitcast_convert_type(w & jnp.uint32(HIGH16), F32))
    return jnp.concatenate(lo, 1), jnp.concatenate(hi, 1)


ROW_SPLIT = 2


CAST_ROWS = 256


def _cast_once(w_ref, wb_ref):
    @pl.when(pl.program_id(0) == 0)
    def _():
        for r0 in range(0, w_ref.shape[0], CAST_ROWS):
            wb_ref[r0:r0 + CAST_ROWS, :] = w_ref[r0:r0 + CAST_ROWS, :].astype(BF16)


def _out_ln_kernel(a_ref, b_ref, c_ref, x_ref, wf_ref, g_ref, beta_ref, o_ref, op_ref, w_ref):
    _cast_once(wf_ref, w_ref)
    na, nb = a_ref.shape[1], b_ref.shape[1]
    nr = x_ref.shape[0] // ROW_SPLIT
    for r0 in range(0, x_ref.shape[0], nr):
        rows = slice(r0, r0 + nr)
        m = _dot(a_ref[rows, :], w_ref[0:na, :])
        m = m + _dot(b_ref[rows, :], w_ref[na:na + nb, :])
        m = m + _dot(c_ref[rows, :], w_ref[na + nb:, :])
        y = _layer_norm(DEEPNORM_ALPHA * x_ref[rows, :] + m, g_ref[...], beta_ref[...])
        o_ref[rows, :] = y
        _store_token_tiles(op_ref.at[pl.ds(r0 * TOK, nr * TOK)], y)


def out_ln(a, b, c, x, w, g, beta, layer):
    s = x.shape[0]
    tm = min(512, s)
    row = lambda arr: pl.BlockSpec((tm, arr.shape[1]), lambda i: (i, 0))
    full = lambda arr: _layer_block(arr, layer)
    return pl.pallas_call(
        _out_ln_kernel,
        out_shape=[jax.ShapeDtypeStruct((s, D_MODEL), F32), jax.ShapeDtypeStruct((s * TOK, LANE), jnp.uint32)],
        grid=(s // tm,),
        in_specs=[row(a), row(b), row(c), row(x), full(w), full(g), full(beta)],
        out_specs=[pl.BlockSpec((tm, D_MODEL), lambda i: (i, 0)), pl.BlockSpec((tm * TOK, LANE), lambda i: (i, 0))],
        scratch_shapes=[pltpu.VMEM(w.shape[1:], BF16)],
        compiler_params=_params(("arbitrary",), 60),
        name="out_ln",
    )(a, b, c, x, w, g, beta)


FFN_ROWS = 1024
FFN_COLS = 512
FFN_OUT_CHUNK = 512
FFN_SUB = 256


def _ffn_kernel(te_ref, tv_ref, na_ref, xp_ref, wg_ref, wu_ref, wd_ref, o_ref, xb_ref, acc_ref):
    i, j = pl.program_id(0), pl.program_id(1)
    tm = xb_ref.shape[0]
    valid = tv_ref[i]

    @pl.when(j == 0)
    def _():
        acc_ref[...] = jnp.zeros_like(acc_ref)

    @pl.when((j == 0) & (valid > 0))
    def _():
        lo, hi = _load_token_tiles(xp_ref)
        xb_ref[:, :HALF] = lo.astype(BF16)
        xb_ref[:, HALF:] = hi.astype(BF16)

    def rows_step(r0, nr):
        xb = xb_ref[r0:r0 + nr, :]
        gate = _dot(xb, wg_ref[...].astype(BF16))
        up = _dot(xb, wu_ref[...].astype(BF16))
        hmid = (gate * (1.0 / (1.0 + jnp.exp(-gate))) * up).astype(BF16)
        for c in range(0, D_MODEL, FFN_OUT_CHUNK):
            cs = slice(c, c + FFN_OUT_CHUNK)
            acc_ref[r0:r0 + nr, cs] += _dot(hmid, wd_ref[:, cs].astype(BF16))

    nearly_full = valid > tm - FFN_SUB

    @pl.when(nearly_full)
    def _():
        rows_step(0, tm)

    for r0 in range(0, tm - FFN_SUB, FFN_SUB):
        @pl.when(jnp.logical_not(nearly_full) & (valid > r0))
        def _():
            rows_step(r0, FFN_SUB)

    @pl.when(j == pl.num_programs(1) - 1)
    def _():
        _store_token_tiles(o_ref, acc_ref[...])


def grouped_swiglu(xp, wg, wu, wd, tile_expert, tile_valid, n_active, tm, tf):
    n_tiles, nf = xp.shape[0] // (tm * TOK), D_FF // tf

    def tile(i, na):
        return jnp.minimum(i, na[0] - 1)

    def fcol(i, j, na):
        return jnp.where(i < na[0], j, nf - 1)

    return pl.pallas_call(
        _ffn_kernel,
        out_shape=jax.ShapeDtypeStruct(xp.shape, jnp.uint32),
        grid_spec=pltpu.PrefetchScalarGridSpec(
            num_scalar_prefetch=3,
            grid=(n_tiles, nf),
            in_specs=[pl.BlockSpec((tm * TOK, LANE), lambda i, j, te, tv, na: (tile(i, na), 0)),
                      pl.BlockSpec((None, D_MODEL, tf),
                                   lambda i, j, te, tv, na: (te[tile(i, na)], 0, fcol(i, j, na))),
                      pl.BlockSpec((None, D_MODEL, tf),
                                   lambda i, j, te, tv, na: (te[tile(i, na)], 0, fcol(i, j, na))),
                      pl.BlockSpec((None, tf, D_MODEL),
                                   lambda i, j, te, tv, na: (te[tile(i, na)], fcol(i, j, na), 0))],
            out_specs=pl.BlockSpec((tm * TOK, LANE), lambda i, j, te, tv, na: (i, 0)),
            scratch_shapes=[pltpu.VMEM((tm, D_MODEL), BF16), pltpu.VMEM((tm, D_MODEL), F32)],
        ),
        compiler_params=_params(("arbitrary", "arbitrary"), 60),
        name="grouped_swiglu",
    )(tile_expert, tile_valid, n_active, xp, wg, wu, wd)


def _router_kernel(x_ref, w_ref, e_ref, r_ref, wt_ref, cnt_ref, run_ref):
    i = pl.program_id(0)
    tm = x_ref.shape[0]
    ne = N_EXPERTS

    @pl.when(i == 0)
    def _():
        run_ref[...] = jnp.zeros_like(run_ref)

    x = x_ref[...]
    x_hi = x.astype(BF16)
    x_lo = (x - x_hi.astype(F32)).astype(BF16)
    w = w_ref[...]
    w_hi = w.astype(BF16).astype(F32)
    w_lo = w - w_hi
    both = _dot_nt(jnp.concatenate([w_hi, w_lo], 0).astype(BF16), x_hi)
    cross = _dot_nt(jnp.concatenate([w_hi, jnp.zeros_like(w_hi)], 0).astype(BF16), x_lo)
    logits = both[0:ne] + both[ne:2 * ne] + cross[0:ne]

    eidx = lax.broadcasted_iota(jnp.int32, logits.shape, 0).astype(F32)
    v1 = jnp.max(logits, 0, keepdims=True)
    i1 = jnp.min(jnp.where(logits == v1, eidx, float(ne)), 0, keepdims=True)
    rest = jnp.where(eidx == i1, -jnp.inf, logits)
    v2 = jnp.max(rest, 0, keepdims=True)
    i2 = jnp.min(jnp.where(rest == v2, eidx, float(ne)), 0, keepdims=True)
    t = jnp.exp(v2 - v1)
    w1 = 1.0 / (1.0 + t)
    wt_ref[0:1, :] = w1
    wt_ref[1:2, :] = t * w1
    e_ref[0:1, :] = i1.astype(jnp.int32)
    e_ref[1:2, :] = i2.astype(jnp.int32)

    sel1, sel2 = eidx == i1, eidx == i2
    sel = jnp.where(sel1, 1.0, 0.0) + jnp.where(sel2, 1.0, 0.0)
    before = (lax.broadcasted_iota(jnp.int32, (tm, tm), 0) < lax.broadcasted_iota(jnp.int32, (tm, tm), 1))
    sel16 = jnp.concatenate([sel, jnp.zeros_like(sel)], 0).astype(BF16)
    prefix = _dot(sel16, jnp.where(before, 1.0, 0.0).astype(BF16))[0:ne]
    rank = prefix + run_ref[:, 0:1]
    r_ref[0:1, :] = jnp.sum(jnp.where(sel1, rank, 0.0), 0, keepdims=True).astype(jnp.int32)
    r_ref[1:2, :] = jnp.sum(jnp.where(sel2, rank, 0.0), 0, keepdims=True).astype(jnp.int32)
    run_ref[...] = run_ref[...] + jnp.sum(sel, 1, keepdims=True)
    cnt_ref[...] = run_ref[...].astype(jnp.int32)


def route_tokens(x, w_router_t):
    s = x.shape[0]
    tm = min(512, s)
    pair = pl.BlockSpec((2, tm), lambda i: (0, i))
    return pl.pallas_call(
        _router_kernel,
        out_shape=[jax.ShapeDtypeStruct((2, s), jnp.int32), jax.ShapeDtypeStruct((2, s), jnp.int32),
                   jax.ShapeDtypeStruct((2, s), F32), jax.ShapeDtypeStruct((N_EXPERTS, LANE), jnp.int32)],
        grid=(s // tm,),
        in_specs=[pl.BlockSpec((tm, D_MODEL), lambda i: (i, 0)),
                  pl.BlockSpec((N_EXPERTS, D_MODEL), lambda i: (0, 0))],
        out_specs=[pair, pair, pair, pl.BlockSpec((N_EXPERTS, LANE), lambda i: (0, 0))],
        scratch_shapes=[pltpu.VMEM((N_EXPERTS, LANE), F32)],
        compiler_params=_params(("arbitrary",), 32),
        name="moe_router",
    )(x, w_router_t)


def _slot_kernel(base_ref, e_ref, r_ref, s_ref):
    e = e_ref[...]
    slot = r_ref[...]
    for k in range(N_EXPERTS):
        slot = slot + jnp.where(e == k, base_ref[k], 0)
    s_ref[...] = slot


def token_slots(e_idx, rank, base):
    whole = pl.BlockSpec(e_idx.shape, lambda i, *_: (0, 0))
    return pl.pallas_call(
        _slot_kernel,
        out_shape=jax.ShapeDtypeStruct(e_idx.shape, jnp.int32),
        grid_spec=pltpu.PrefetchScalarGridSpec(num_scalar_prefetch=1, grid=(1,), in_specs=[whole, whole],
                                               out_specs=whole),
        compiler_params=_params(("arbitrary",), 32),
        name="token_slots",
    )(base, e_idx, rank)


DMA_UNROLL = 8


def _tile_rows(index):
    return pl.ds(pl.multiple_of(index * TOK, TOK), TOK)


def _dispatch_kernel(fill_ref, slot_ref, x_ref, xs_ref, zero_ref, sem, zsem):
    i = pl.program_id(0)
    tm = x_ref.shape[0] // TOK

    def issue(b, c):
        for u in range(DMA_UNROLL):
            t = b * DMA_UNROLL + u
            for k in range(2):
                pltpu.make_async_copy(x_ref.at[_tile_rows(t)], xs_ref.at[_tile_rows(slot_ref[k, t])],
                                      sem).start(priority=k)
        return c

    lax.fori_loop(0, tm // DMA_UNROLL, issue, 0)
    for k in range(2):
        pltpu.make_async_copy(x_ref, xs_ref.at[pl.ds(0, tm * TOK)], sem).wait()

    @pl.when(i == pl.num_programs(0) - 1)
    def _():
        zero_ref[...] = jnp.zeros_like(zero_ref)

        def zero_copy(slot):
            return pltpu.make_async_copy(zero_ref.at[pl.ds(0, TOK)], xs_ref.at[_tile_rows(slot)], zsem)

        for e in range(N_EXPERTS):
            lo, hi = fill_ref[0, e], fill_ref[1, e]

            def zissue(slot, c):
                zero_copy(slot).start()
                return c

            def zdrain(slot, c):
                zero_copy(slot).wait()
                return c

            lax.fori_loop(lo, hi, zissue, 0)
            lax.fori_loop(lo, hi, zdrain, 0)

        zrows = zero_ref.shape[0]
        zb = zrows // TOK

        def block_copy(b):
            return pltpu.make_async_copy(zero_ref, xs_ref.at[pl.ds(pl.multiple_of(b * zrows, zrows), zrows)], zsem)

        def bissue(b, c):
            block_copy(b).start()
            return c

        def bdrain(b, c):
            block_copy(b).wait()
            return c

        first, last = fill_ref[2, 0] // zb, xs_ref.shape[0] // zrows
        lax.fori_loop(first, last, bissue, 0)
        lax.fori_loop(first, last, bdrain, 0)


def moe_dispatch(xp, slots, fill, rows):
    s = xp.shape[0] // TOK
    tm = min(512, s)
    zb = min(256, s)
    return pl.pallas_call(
        _dispatch_kernel,
        out_shape=jax.ShapeDtypeStruct((rows * TOK, LANE), xp.dtype),
        grid_spec=pltpu.PrefetchScalarGridSpec(
            num_scalar_prefetch=1,
            grid=(s // tm,),
            in_specs=[pl.BlockSpec((2, tm), lambda i, *_: (0, i), memory_space=pltpu.SMEM),
                      pl.BlockSpec((tm * TOK, LANE), lambda i, *_: (i, 0))],
            out_specs=pl.BlockSpec(memory_space=pl.ANY),
            scratch_shapes=[pltpu.VMEM((zb * TOK, LANE), xp.dtype), pltpu.SemaphoreType.DMA,
                            pltpu.SemaphoreType.DMA],
        ),
        compiler_params=_params(("arbitrary",), 32),
        name="moe_dispatch",
    )(fill, slots, xp)


def _ple(x, p, wup_ref, wgate_ref, bgate_ref):
    up = _dot(p.astype(BF16), wup_ref[...])
    zg = _dot(x.astype(BF16), wgate_ref[...]) + bgate_ref[...]
    return up * (1.0 / (1.0 + jnp.exp(-zg)))


def _ple_ln_dense_kernel(x_ref, f_ref, p_ref, wupf_ref, wgatef_ref, bgate_ref, g_ref, beta_ref, o_ref,
                         wup_ref, wgate_ref):
    _cast_once(wupf_ref, wup_ref)
    _cast_once(wgatef_ref, wgate_ref)
    nr = x_ref.shape[0] // ROW_SPLIT
    for r0 in range(0, x_ref.shape[0], nr):
        rows = slice(r0, r0 + nr)
        x = x_ref[rows, :]
        f = jnp.concatenate(_load_token_tiles(f_ref, (r0, nr)), 1)
        ple = _ple(x, p_ref[rows, :], wup_ref, wgate_ref, bgate_ref)
        o_ref[rows, :] = _layer_norm(DEEPNORM_ALPHA * x + f + ple, g_ref[...], beta_ref[...])


def _ple_ln_moe_kernel(slot_ref, x_ref, wt_ref, p_ref, wupf_ref, wgatef_ref, bgate_ref, g_ref, beta_ref, ys_ref,
                       o_ref, wup_ref, wgate_ref, y1_ref, y2_ref, sem):
    _cast_once(wupf_ref, wup_ref)
    _cast_once(wgatef_ref, wgate_ref)
    tm = x_ref.shape[0]
    bufs = (y1_ref, y2_ref)

    def issue(b, c):
        for u in range(DMA_UNROLL):
            t = b * DMA_UNROLL + u
            for k in range(2):
                pltpu.make_async_copy(ys_ref.at[_tile_rows(slot_ref[k, t])], bufs[k].at[_tile_rows(t)],
                                      sem).start(priority=k)
        return c

    lax.fori_loop(0, tm // DMA_UNROLL, issue, 0)
    nr = tm // ROW_SPLIT
    ples = [_ple(x_ref[r0:r0 + nr, :], p_ref[r0:r0 + nr, :], wup_ref, wgate_ref, bgate_ref)
            for r0 in range(0, tm, nr)]
    for k in range(2):
        pltpu.make_async_copy(ys_ref.at[pl.ds(0, tm * TOK)], bufs[k], sem).wait()
    for i, r0 in enumerate(range(0, tm, nr)):
        rows = slice(r0, r0 + nr)
        wt = wt_ref[rows, :]
        f = (wt[:, 0:1] * jnp.concatenate(_load_token_tiles(y1_ref, (r0, nr)), 1)
             + wt[:, 1:2] * jnp.concatenate(_load_token_tiles(y2_ref, (r0, nr)), 1))
        o_ref[rows, :] = _layer_norm(DEEPNORM_ALPHA * x_ref[rows, :] + f + ples[i], g_ref[...], beta_ref[...])


def ple_ln(x, p, wup, wgate, bgate, g, beta, layer, f=None, moe=None):
    s = x.shape[0]
    tm = min(512 if moe is None else 256, s)
    row = lambda arr: pl.BlockSpec((tm, arr.shape[1]), lambda i, *_: (i, 0))
    full = lambda arr: _layer_block(arr, layer)
    p_spec = pl.BlockSpec((None, None, tm, p.shape[-1]), lambda i, *_: (layer, 0, i, 0))
    tail = [p_spec, full(wup), full(wgate), full(bgate), full(g), full(beta)]
    out_spec = pl.BlockSpec((tm, D_MODEL), lambda i, *_: (i, 0))
    out_shape = jax.ShapeDtypeStruct((s, D_MODEL), F32)
    casts = [pltpu.VMEM(wup.shape[1:], BF16), pltpu.VMEM(wgate.shape[1:], BF16)]
    if moe is None:
        return pl.pallas_call(
            _ple_ln_dense_kernel, out_shape=out_shape, grid=(s // tm,),
            in_specs=[row(x), pl.BlockSpec((tm * TOK, LANE), lambda i: (i, 0))] + tail, out_specs=out_spec,
            scratch_shapes=casts,
            compiler_params=_params(("arbitrary",), 60), name="ple_ln_dense",
        )(x, f, p, wup, wgate, bgate, g, beta)
    ys, slots, wts_t = moe
    return pl.pallas_call(
        _ple_ln_moe_kernel,
        out_shape=out_shape,
        grid_spec=pltpu.PrefetchScalarGridSpec(
            num_scalar_prefetch=0,
            grid=(s // tm,),
            in_specs=[pl.BlockSpec((2, tm), lambda i: (0, i), memory_space=pltpu.SMEM), row(x), row(wts_t)] + tail
            + [pl.BlockSpec(memory_space=pl.ANY)],
            out_specs=out_spec,
            scratch_shapes=casts + [pltpu.VMEM((tm * TOK, LANE), jnp.uint32),
                                    pltpu.VMEM((tm * TOK, LANE), jnp.uint32), pltpu.SemaphoreType.DMA],
        ),
        compiler_params=_params(("arbitrary",), 60),
        name="ple_ln_moe",
    )(slots, x, wts_t, p, wup, wgate, bgate, g, beta, ys)


IN_SIZES = (MLA_Q_RANK, MLA_KV_RANK, MLA_ROPE, GLA_HEADS * GLA_DK, GLA_HEADS * GLA_DK, GLA_HEADS * GLA_DV,
            GLA_GATE_RANK, GLA_HEADS * GLA_DV, SWA_HEADS * SWA_HD, SWA_KV_HEADS * SWA_HD, SWA_KV_HEADS * SWA_HD)
IN_COLS = sum(IN_SIZES)
IN_DEST = (H_QA, H_KVA, H_KR, H_GQ, H_GK, H_GV, H_GLR, H_GR, H_SQ, H_SK, H_SV)


def _pack_w_in_kernel(w_ref, o_ref):
    src = 0
    for width, dst in zip(IN_SIZES, IN_DEST):
        o_ref[dst:dst + width, :] = w_ref[src:src + width, :].astype(BF16)
        pad = -width % LANE
        if pad:
            o_ref[dst + width:dst + width + pad, :] = jnp.zeros((pad, o_ref.shape[1]), BF16)
        src += width


def _pack_w_in(w_t):
    depth, _, d = w_t.shape
    tc = 512
    return pl.pallas_call(
        _pack_w_in_kernel,
        out_shape=jax.ShapeDtypeStruct((depth, H_COLS, d), BF16),
        grid=(depth, d // tc),
        in_specs=[pl.BlockSpec((None, IN_COLS, tc), lambda l, i: (l, 0, i))],
        out_specs=pl.BlockSpec((None, H_COLS, tc), lambda l, i: (l, 0, i)),
        compiler_params=_params(("parallel", "parallel"), 40),
        name="pack_w_in",
    )(w_t)


def _pack_w_q_b(w):
    w = w.reshape(MLA_Q_RANK, MLA_HEADS, MLA_NOPE + MLA_ROPE)
    w = jnp.pad(w, ((0, 0), (0, 0), (0, MLA_QK - MLA_NOPE - MLA_ROPE)))
    return w.reshape(MLA_Q_RANK, MLA_HEADS * MLA_QK).astype(BF16)


def _token_mixer_ln(x, rope, layer, w_in_p, q_gain, w_q_b, kv_gain, w_kv_b, gla_w, gla_b, gla_gain, sinks,
                    rel_bias, w_out_b, ln_g, ln_b):
    h = proj_in(x, w_in_p, layer)
    q, k, v = mla_prep(h, *rope, q_gain.reshape(1, -1), kv_gain.reshape(1, -1), _pack_w_q_b(w_q_b),
                       w_kv_b.astype(BF16))
    a = mla_flash(q, k, v)
    gla_w_p = jnp.pad(gla_w, ((0, LANE - GLA_GATE_RANK), (0, 0))).astype(BF16)
    b = gla(h, gla_w_p, gla_b.reshape(1, -1), gla_gain.reshape(1, -1))
    c = swa(h, sinks, rel_bias)
    return out_ln(a, b, c, x, w_out_b, ln_g, ln_b, layer)


def _moe_plan(counts, tm, n_tiles):
    tiles = (counts + tm - 1) // tm
    ends = jnp.cumsum(tiles)
    base = (ends - tiles) * tm
    n_active = ends[-1:].astype(jnp.int32)
    tile_expert = jnp.searchsorted(ends, jnp.arange(n_tiles, dtype=jnp.int32), side="right")
    tile_expert = jnp.minimum(tile_expert, N_EXPERTS - 1).astype(jnp.int32)
    tile_start = jnp.arange(n_tiles, dtype=jnp.int32) * tm
    tile_valid = jnp.clip((base + counts)[tile_expert] - tile_start, 0, tm).astype(jnp.int32)
    used_rows = jnp.broadcast_to(ends[-1] * tm, counts.shape)
    fill = jnp.stack([base + counts, base + tiles * tm, used_rows]).astype(jnp.int32)
    return base.astype(jnp.int32), fill, tile_expert, tile_valid, n_active


def kernel(x, p, positions, w_in, mla_q_a_gain, mla_w_q_b, mla_kv_a_gain, mla_w_kv_b, gla_w_gate, gla_b_gate,
           gla_norm_gain, swa_sinks, rel_bias, w_out, ln1_g, ln1_b, ffn_w_gate, ffn_w_up, ffn_w_down,
           moe_router, moe_w_gate, moe_w_up, moe_w_down, ple_w_up, ple_w_gate, ple_b_gate, ln2_g, ln2_b):
    batch, s, _ = x.shape
    assert batch == 1
    xcur = x.reshape(s, D_MODEL)
    rope = rope_tables(positions)
    tm = min(FFN_ROWS, s)
    row_stack = lambda v: v.reshape(DEPTH, 1, -1)
    w_in_p = _pack_w_in(jnp.swapaxes(w_in, 1, 2))
    tail = (p, ple_w_up, ple_w_gate, row_stack(ple_b_gate), row_stack(ln2_g), row_stack(ln2_b))
    for i in range(DEPTH):
        x1, x1p = _token_mixer_ln(xcur, rope, i, w_in_p, mla_q_a_gain[i], mla_w_q_b[i], mla_kv_a_gain[i],
                                  mla_w_kv_b[i], gla_w_gate[i], gla_b_gate[i], gla_norm_gain[i], swa_sinks[i],
                                  rel_bias, w_out, row_stack(ln1_g), row_stack(ln1_b))
        j = i // 2
        if i % 2 == 0:
            n_tiles = s // tm
            f = grouped_swiglu(x1p, ffn_w_gate[j][None], ffn_w_up[j][None], ffn_w_down[j][None],
                               jnp.zeros((n_tiles,), jnp.int32), jnp.full((n_tiles,), tm, jnp.int32),
                               jnp.full((1,), n_tiles, jnp.int32), tm, FFN_COLS)
            xcur = ple_ln(x1, *tail, i, f=f)
        else:
            n_tiles = (2 * s + N_EXPERTS * (tm - 1)) // tm
            e_idx, rank, wts, counts = route_tokens(x1, moe_router[j].T)
            base, fill, tile_expert, tile_valid, n_active = _moe_plan(counts[:, 0], tm, n_tiles)
            slots = token_slots(e_idx, rank, base)
            xs = moe_dispatch(x1p, slots, fill, n_tiles * tm)
            ys = grouped_swiglu(xs, moe_w_gate[j], moe_w_up[j], moe_w_down[j], tile_expert, tile_valid, n_active,
                                tm, FFN_COLS)
            xcur = ple_ln(x1, *tail, i, moe=(ys, slots, wts.T))
    return xcur.reshape(batch, s, D_MODEL)
```

```python
import functools
import math
from typing import NamedTuple

import numpy as np
import jax
import jax.numpy as jnp
from jax import lax
from jax.experimental import pallas as pl
from jax.experimental.pallas import tpu as pltpu

F32 = jnp.float32
BF16 = jnp.bfloat16

D_MODEL = 2048
DEPTH = 2
MLA_HEADS = 8
MLA_Q_RANK = 512
MLA_KV_RANK = 256
MLA_NOPE = 128
MLA_ROPE = 64
MLA_V = 128
ROPE_THETA = 10000.0
GLA_HEADS = 4
GLA_DK = 64
GLA_DV = 128
GLA_GATE_RANK = 16
GLA_TAU = 16.0
SWA_HEADS = 8
SWA_KV_HEADS = 2
SWA_HD = 64
SWA_WINDOW = 128
SWA_BLOCK = 128
REL_BUCKETS = 32
REL_MAX_DIST = 128
D_FF = 5632
N_EXPERTS = 8
PLE_DIM = 256
LN_EPS = 1e-5
RMS_EPS = 1e-6
DEEPNORM_ALPHA = (2 * DEPTH) ** 0.25

LANE = 128
LOG2E = math.log2(math.e)
NEG_BIG = -1e30

H_QA, H_GV, H_GR, H_SQ = 0, 512, 1024, 1536
H_KVA, H_GQ, H_GK = 2048, 2304, 2560
H_KR, H_GLR, H_SK, H_SV = 2816, 2944, 3072, 3200
H_COLS = 3328

MLA_QK = 2 * LANE

GLA_L = 128
GLA_SUB = 32
GLA_NSUB = GLA_L // GLA_SUB
GLA_CHUNKS_PER_STEP = 4

V7X_VMEM_MIB = 64


class _Plan(NamedTuple):
    rows: int
    vmem_mib: int


PLANS = {
    "rope_tables": _Plan(1024, 32),
    "pack_w_in": _Plan(512, 40),
    "proj_in": _Plan(1024, 56),
    "mla_prep": _Plan(512, 48),
    "mla_flash": _Plan(512, 56),
    "gla": _Plan(GLA_CHUNKS_PER_STEP * GLA_L, 32),
    "swa": _Plan(SWA_BLOCK, 32),
    "out_ln": _Plan(512, 60),
    "grouped_swiglu": _Plan(1024, 60),
    "moe_router": _Plan(512, 32),
    "token_slots": _Plan(0, 32),
    "moe_dispatch": _Plan(512, 32),
    "ple_ln_dense": _Plan(512, 60),
    "ple_ln_moe": _Plan(256, 60),
}
assert all(plan.vmem_mib < V7X_VMEM_MIB for plan in PLANS.values())


def _rows(name, s):
    return min(PLANS[name].rows, s)


def _params(sem, name):
    return pltpu.CompilerParams(dimension_semantics=sem, vmem_limit_bytes=PLANS[name].vmem_mib * 2 ** 20)


def _dot(a, b):
    return jnp.dot(a, b, preferred_element_type=F32)


def _dot_nt(a, b):
    return lax.dot_general(a, b, (((1,), (1,)), ((), ())), preferred_element_type=F32)


def _dot_tn(a, b):
    return lax.dot_general(a, b, (((0,), (0,)), ((), ())), preferred_element_type=F32)


def _proj_in_kernel(x_ref, w_ref, o_ref, xb_ref):
    @pl.when(pl.program_id(1) == 0)
    def _():
        xb_ref[...] = x_ref[...].astype(BF16)

    o_ref[...] = _dot_nt(xb_ref[...], w_ref[...]).astype(o_ref.dtype)


def _layer_block(arr, layer):
    zeros = (0,) * (arr.ndim - 1)
    return pl.BlockSpec((None,) + arr.shape[1:], lambda i, *_: (layer,) + zeros, pipeline_mode=pl.Buffered(1))


def proj_in(x, w_p, layer):
    s = x.shape[0]
    tm = _rows("proj_in", s)
    tn = H_COLS // 2
    return pl.pallas_call(
        _proj_in_kernel,
        out_shape=jax.ShapeDtypeStruct((s, H_COLS), BF16),
        grid=(s // tm, H_COLS // tn),
        in_specs=[pl.BlockSpec((tm, D_MODEL), lambda i, j: (i, 0)),
                  pl.BlockSpec((None, tn, D_MODEL), lambda i, j: (layer, j, 0))],
        out_specs=pl.BlockSpec((tm, tn), lambda i, j: (i, j)),
        scratch_shapes=[pltpu.VMEM((tm, D_MODEL), BF16)],
        compiler_params=_params(("parallel", "arbitrary"), "proj_in"),
        name="proj_in",
    )(x, w_p)


def _rope_table_kernel(pos_ref, inv_ref, cos_ref, sa_ref, sb_ref):
    ang = pos_ref[...].astype(F32) * inv_ref[...]
    lane = lax.broadcasted_iota(jnp.int32, ang.shape, 1)
    half = MLA_ROPE // 2
    c, s = jnp.cos(ang), jnp.sin(ang)
    cos_ref[...] = c
    sa_ref[...] = jnp.where((lane >= half) & (lane < 2 * half), s, 0.0)
    sb_ref[...] = jnp.where(lane < half, -s, 0.0)


def rope_tables(positions):
    s = positions.shape[-1]
    half = MLA_ROPE // 2
    inv = ROPE_THETA ** (-jnp.arange(half, dtype=F32) / half)
    inv = jnp.concatenate([inv, inv, jnp.zeros((LANE - 2 * half,), F32)]).reshape(1, LANE)
    tm = _rows("rope_tables", s)
    spec = pl.BlockSpec((tm, LANE), lambda i: (i, 0))
    return pl.pallas_call(
        _rope_table_kernel,
        out_shape=[jax.ShapeDtypeStruct((s, LANE), F32)] * 3,
        grid=(s // tm,),
        in_specs=[pl.BlockSpec((tm, 1), lambda i: (i, 0)), pl.BlockSpec((1, LANE), lambda i: (0, 0))],
        out_specs=[spec, spec, spec],
        compiler_params=_params(("parallel",), "rope_tables"),
        name="rope_tables",
    )(positions.reshape(s, 1), inv)


def _rope(x, cos, sa, sb):
    return x * cos + pltpu.roll(x, MLA_ROPE // 2, 1) * sa + pltpu.roll(x, LANE - MLA_ROPE // 2, 1) * sb


def _mla_prep_kernel(qa_ref, kva_ref, kr_ref, cos_ref, sa_ref, sb_ref, gq_ref, gkv_ref, wq_ref, wkv_ref,
                     q_out, k_out, v_out):
    cos, sa, sb = cos_ref[...], sa_ref[...], sb_ref[...]
    qscale = (MLA_NOPE + MLA_ROPE) ** -0.5 * LOG2E

    qa = qa_ref[...].astype(F32)
    qn = qa * lax.rsqrt(jnp.mean(qa * qa, -1, keepdims=True) + RMS_EPS) * gq_ref[...]
    q = _dot(qn.astype(BF16), wq_ref[...])
    for h in range(MLA_HEADS):
        c0 = h * MLA_QK
        q_out[h, :, 0:LANE] = (q[:, c0:c0 + LANE] * qscale).astype(BF16)
        pe = _rope(q[:, c0 + LANE:c0 + 2 * LANE], cos, sa, sb)
        q_out[h, :, LANE:2 * LANE] = (pe * qscale).astype(BF16)

    kva = kva_ref[...].astype(F32)
    kvn = kva * lax.rsqrt(jnp.mean(kva * kva, -1, keepdims=True) + RMS_EPS) * gkv_ref[...]
    kv = _dot(kvn.astype(BF16), wkv_ref[...])
    kpe = _rope(kr_ref[...].astype(F32), cos, sa, sb).astype(BF16)
    for h in range(MLA_HEADS):
        c0 = h * (MLA_NOPE + MLA_V)
        k_out[h, :, 0:LANE] = kv[:, c0:c0 + MLA_NOPE].astype(BF16)
        k_out[h, :, LANE:2 * LANE] = kpe
        v_out[h, :, 0:MLA_V] = kv[:, c0 + MLA_NOPE:c0 + MLA_NOPE + MLA_V].astype(BF16)
        v_out[h, :, MLA_V:2 * MLA_V] = jnp.ones((kv.shape[0], MLA_V), BF16)


def mla_prep(h, cos, sa, sb, gq, gkv, wq_p, wkv):
    s = h.shape[0]
    tm = _rows("mla_prep", s)
    row = lambda width, col: pl.BlockSpec((tm, width), lambda i: (i, col // width))
    full = lambda a: pl.BlockSpec(a.shape, lambda i: (0,) * a.ndim)
    return pl.pallas_call(
        _mla_prep_kernel,
        out_shape=[jax.ShapeDtypeStruct((MLA_HEADS, s, MLA_QK), BF16),
                   jax.ShapeDtypeStruct((MLA_HEADS, s, MLA_QK), BF16),
                   jax.ShapeDtypeStruct((MLA_HEADS, s, 2 * MLA_V), BF16)],
        grid=(s // tm,),
        in_specs=[row(MLA_Q_RANK, H_QA), row(MLA_KV_RANK, H_KVA), row(LANE, H_KR),
                  row(LANE, 0), row(LANE, 0), row(LANE, 0),
                  full(gq), full(gkv), full(wq_p), full(wkv)],
        out_specs=[pl.BlockSpec((MLA_HEADS, tm, MLA_QK), lambda i: (0, i, 0)),
                   pl.BlockSpec((MLA_HEADS, tm, MLA_QK), lambda i: (0, i, 0)),
                   pl.BlockSpec((MLA_HEADS, tm, 2 * MLA_V), lambda i: (0, i, 0))],
        compiler_params=_params(("parallel",), "mla_prep"),
        name="mla_prep",
    )(h, h, h, cos, sa, sb, gq, gkv, wq_p, wkv)


def _mla_flash_kernel(q_ref, k_ref, v_ref, o_ref, sa_ref, sb_ref, mxa_ref, mxb_ref, m_ref, acc_ref, *, t, nh):
    qi = pl.program_id(1)
    heads = range(nh)
    bufs = ((sa_ref, mxa_ref), (sb_ref, mxb_ref))

    def produce(b, parity, masked):
        s_ref, mx_ref = bufs[parity]
        start = pl.multiple_of(b * t, t)
        for h in heads:
            s = _dot_nt(q_ref[h], k_ref[h, pl.ds(start, t), :])
            if masked:
                rows = lax.broadcasted_iota(jnp.int32, s.shape, 0) + qi * t
                cols = lax.broadcasted_iota(jnp.int32, s.shape, 1) + b * t
                s = jnp.where(cols <= rows, s, NEG_BIG)
            s_ref[h] = s
            mx_ref[h] = jnp.broadcast_to(jnp.max(s, -1, keepdims=True), (t, LANE))

    def absorb(b, parity):
        s_ref, mx_ref = bufs[parity]
        start = pl.multiple_of(b * t, t)
        for h in heads:
            m_new = jnp.maximum(m_ref[h], mx_ref[h])
            alpha = jnp.exp2(m_ref[h] - m_new)
            p = jnp.exp2(s_ref[h] - jnp.concatenate([m_new] * (t // LANE), 1))
            acc_ref[h] = (jnp.concatenate([alpha, alpha], 1) * acc_ref[h]
                          + _dot(p.astype(BF16), v_ref[h, pl.ds(start, t), :]))
            m_ref[h] = m_new

    m_ref[...] = jnp.full(m_ref.shape, NEG_BIG, F32)
    acc_ref[...] = jnp.zeros_like(acc_ref)
    produce(0, 0, True)

    def pair(i, c):
        b = 2 * i
        produce(b + 1, 1, False)
        absorb(b, 0)
        produce(b + 2, 0, False)
        absorb(b + 1, 1)
        return c

    n_pairs = jnp.maximum(qi - 1, 0) // 2
    lax.fori_loop(0, n_pairs, pair, 0)
    done = 2 * n_pairs

    @pl.when(qi == 0)
    def _():
        absorb(0, 0)

    @pl.when((qi > 0) & (qi - done == 1))
    def _():
        produce(qi, 1, True)
        absorb(qi - 1, 0)
        absorb(qi, 1)

    @pl.when((qi > 0) & (qi - done == 2))
    def _():
        produce(qi - 1, 1, False)
        absorb(qi - 2, 0)
        produce(qi, 0, True)
        absorb(qi - 1, 1)
        absorb(qi, 0)

    for h in heads:
        acc = acc_ref[h]
        o_ref[:, h * MLA_V:(h + 1) * MLA_V] = (acc[:, :MLA_V] / acc[:, MLA_V:]).astype(o_ref.dtype)


def mla_flash(q, k, v1):
    _, s, _ = q.shape
    t = _rows("mla_flash", s)
    nh = 2
    return pl.pallas_call(
        functools.partial(_mla_flash_kernel, t=t, nh=nh),
        out_shape=jax.ShapeDtypeStruct((s, MLA_HEADS * MLA_V), BF16),
        grid=(MLA_HEADS // nh, s // t),
        in_specs=[pl.BlockSpec((nh, t, MLA_QK), lambda h, i: (h, i, 0)),
                  pl.BlockSpec((nh, s, MLA_QK), lambda h, i: (h, 0, 0)),
                  pl.BlockSpec((nh, s, 2 * MLA_V), lambda h, i: (h, 0, 0))],
        out_specs=pl.BlockSpec((t, nh * MLA_V), lambda h, i: (i, h)),
        scratch_shapes=[pltpu.VMEM((nh, t, t), F32), pltpu.VMEM((nh, t, t), F32),
                        pltpu.VMEM((nh, t, LANE), F32), pltpu.VMEM((nh, t, LANE), F32),
                        pltpu.VMEM((nh, t, LANE), F32), pltpu.VMEM((nh, t, 2 * MLA_V), F32)],
        compiler_params=_params(("parallel", "arbitrary"), "mla_flash"),
        name="mla_flash",
    )(q, k, v1)


def _gla_masks():
    i = np.arange(GLA_L)[:, None]
    j = np.arange(GLA_L)[None, :]
    same = (i // GLA_SUB) == (j // GLA_SUB)
    mats = [j <= i, same & (j <= i), same & (j > i), j > i]
    for sub in range(GLA_NSUB - 1):
        mats.append((j >= (sub + 1) * GLA_SUB) & (j <= i))
    return np.concatenate(mats, 0).astype(np.float32)


def _gla_kernel(q_ref, k_ref, v_ref, lr_ref, r_ref, wg_ref, bg_ref, gain_ref, mask_ref, o_ref, state_ref):
    @pl.when(pl.program_id(0) == 0)
    def _():
        state_ref[...] = jnp.zeros_like(state_ref)

    L, hk = GLA_L, GLA_HEADS * GLA_DK
    chunks = [slice(c * L, (c + 1) * L) for c in range(q_ref.shape[0] // L)]
    heads = [(slice(h * GLA_DK, (h + 1) * GLA_DK), slice(h * GLA_DV, (h + 1) * GLA_DV)) for h in range(GLA_HEADS)]

    z = _dot(lr_ref[...], wg_ref[...]) + bg_ref[...]
    g = (jnp.minimum(z, 0.0) - jnp.log(1.0 + jnp.exp(-jnp.abs(z)))) * (1.0 / GLA_TAU)
    g_hi = g.astype(BF16)
    g_lo = (g - g_hi.astype(F32)).astype(BF16)

    cums = [_dot(mask_ref[...], jnp.concatenate([g_hi[rs], g_lo[rs]], 0)) for rs in chunks]

    sub_of_row = lax.broadcasted_iota(jnp.int32, (L, hk), 0) // GLA_SUB
    prep = []
    for rs, cum in zip(chunks, cums):
        b_all, b_loc, sfx_loc, sfx_all = cum[0:L], cum[L:2 * L], cum[2 * L:3 * L], cum[3 * L:4 * L]
        q = q_ref[rs, :].astype(F32) * (GLA_DK ** -0.5)
        k = k_ref[rs, :].astype(F32)
        k_end = k * jnp.exp(sfx_loc)
        prep.append(dict(
            decay=jnp.exp(b_all[L - 1:L, :]),
            q_inter=(q * jnp.exp(b_all)).astype(BF16),
            q_diag=(q * jnp.exp(b_loc)).astype(BF16),
            k_diag=(k * jnp.exp(-b_loc)).astype(BF16),
            k_state=(k * jnp.exp(sfx_all)).astype(BF16),
            q_off=[(q * jnp.exp(cum[(4 + sub) * L:(5 + sub) * L])).astype(BF16) for sub in range(GLA_NSUB - 1)],
            k_off=[jnp.where(sub_of_row == sub, k_end, 0.0).astype(BF16) for sub in range(GLA_NSUB - 1)]))

    row = lax.broadcasted_iota(jnp.int32, (L, L), 0)
    col = lax.broadcasted_iota(jnp.int32, (L, L), 1)
    diag_ok = ((row // GLA_SUB) == (col // GLA_SUB)) & (col <= row)
    off_ok = (row // GLA_SUB) > (col // GLA_SUB)
    local = []
    for rs, pr in zip(chunks, prep):
        per_head = []
        for ks, vs in heads:
            v_h = v_ref[rs, vs]
            a = jnp.where(diag_ok, _dot_nt(pr["q_diag"][:, ks], pr["k_diag"][:, ks]), 0.0)
            qo = jnp.concatenate([t[:, ks] for t in pr["q_off"]], 1)
            ko = jnp.concatenate([t[:, ks] for t in pr["k_off"]], 1)
            a = a + jnp.where(off_ok, _dot_nt(qo, ko), 0.0)
            per_head.append((_dot(a.astype(BF16), v_h), _dot_tn(v_h, pr["k_state"][:, ks])))
        local.append(per_head)

    states = [state_ref[h] for h in range(GLA_HEADS)]
    for rs, pr, per_head in zip(chunks, prep, local):
        r = r_ref[rs, :].astype(F32)
        gate = r * (1.0 / (1.0 + jnp.exp(-r)))
        for h, (ks, vs) in enumerate(heads):
            o_intra, vk = per_head[h]
            o = _dot_nt(pr["q_inter"][:, ks], states[h].astype(BF16)) + o_intra
            states[h] = states[h] * pr["decay"][:, ks] + vk
            o = o * lax.rsqrt(jnp.mean(o * o, -1, keepdims=True) + RMS_EPS) * gain_ref[:, vs]
            o_ref[rs, vs] = (o * gate[:, vs]).astype(o_ref.dtype)
    for h in range(GLA_HEADS):
        state_ref[h] = states[h]


def gla(h, wg_p, bg, gain):
    s = h.shape[0]
    rows = _rows("gla", s)
    masks = jnp.asarray(np.tile(_gla_masks(), (1, 2)), BF16)
    row = lambda width, col: pl.BlockSpec((rows, width), lambda i: (i, col // width))
    full = lambda a: pl.BlockSpec(a.shape, lambda i: (0,) * a.ndim)
    hk, hv = GLA_HEADS * GLA_DK, GLA_HEADS * GLA_DV
    return pl.pallas_call(
        _gla_kernel,
        out_shape=jax.ShapeDtypeStruct((s, hv), BF16),
        grid=(s // rows,),
        in_specs=[row(hk, H_GQ), row(hk, H_GK), row(hv, H_GV), row(LANE, H_GLR), row(hv, H_GR),
                  full(wg_p), full(bg), full(gain), full(masks)],
        out_specs=pl.BlockSpec((rows, hv), lambda i: (i, 0)),
        scratch_shapes=[pltpu.VMEM((GLA_HEADS, GLA_DV, GLA_DK), F32)],
        compiler_params=_params(("arbitrary",), "gla"),
        name="gla",
    )(h, h, h, h, h, wg_p, bg, gain, masks)


def _t5_bucket_table():
    L = SWA_BLOCK
    dist = np.arange(L)[:, None] + L - np.arange(2 * L)[None, :]
    d = np.clip(dist, 0, None)
    max_exact = REL_BUCKETS // 2
    df = np.maximum(d, 1).astype(np.float32)
    large = max_exact + (np.log(df / np.float32(max_exact)) / np.float32(math.log(REL_MAX_DIST / max_exact))
                         * np.float32(REL_BUCKETS - max_exact)).astype(np.int32)
    large = np.minimum(large, REL_BUCKETS - 1)
    bucket = np.where(d < max_exact, d, large)
    in_window = (dist >= 0) & (dist < SWA_WINDOW)
    return np.where(in_window, bucket, -1).astype(np.int32)


def _swa_kernel(relb_ref, sink_ref, q_ref, kp_ref, kc_ref, vp_ref, vc_ref, bucket_ref, o_ref, bias_ref):
    i = pl.program_id(0)
    L = SWA_BLOCK

    @pl.when(i == 0)
    def _():
        bucket = bucket_ref[...]
        for h in range(SWA_HEADS):
            acc = jnp.full(bucket.shape, NEG_BIG, F32)
            for b in range(REL_BUCKETS):
                acc = jnp.where(bucket == b, relb_ref[b, h], acc)
            bias_ref[h] = acc

    kcat = jnp.concatenate([kp_ref[...], kc_ref[...]], 0)
    vcat = jnp.concatenate([vp_ref[...], vc_ref[...]], 0)
    scale = SWA_HD ** -0.5
    assert math.log2(scale).is_integer()
    q = q_ref[...] * scale
    col = lax.broadcasted_iota(jnp.int32, (L, 2 * L), 1)
    real_key = (col >= L) | (i > 0)
    g = SWA_HEADS // SWA_KV_HEADS
    ones = jnp.ones((2 * L, SWA_HD), BF16)
    v1 = [jnp.concatenate([vcat[:, kv * SWA_HD:(kv + 1) * SWA_HD], ones], 1) for kv in range(SWA_KV_HEADS)]
    for h in range(SWA_HEADS):
        kv = h // g
        hs = slice(h * SWA_HD, (h + 1) * SWA_HD)
        kvs = slice(kv * SWA_HD, (kv + 1) * SWA_HD)
        s = _dot_nt(q[:, hs], kcat[:, kvs]) + bias_ref[h]
        s = jnp.where(real_key, s, NEG_BIG)
        sink = sink_ref[h]
        m = jnp.maximum(jnp.max(s, -1, keepdims=True), sink)
        o2 = _dot(jnp.exp(s - m).astype(BF16), v1[kv])
        denom = o2[:, SWA_HD:] + jnp.exp(sink - m)
        o_ref[:, hs] = (o2[:, :SWA_HD] / denom).astype(o_ref.dtype)


def swa(h, sinks, rel_bias):
    s = h.shape[0]
    L = SWA_BLOCK
    bucket = jnp.asarray(_t5_bucket_table())
    kvw = SWA_KV_HEADS * SWA_HD
    hw = SWA_HEADS * SWA_HD
    cur = lambda width, col: pl.BlockSpec((L, width), lambda i, *_: (i, col // width))
    prev = lambda width, col: pl.BlockSpec((L, width), lambda i, *_: (jnp.maximum(i - 1, 0), col // width))
    return pl.pallas_call(
        _swa_kernel,
        out_shape=jax.ShapeDtypeStruct((s, hw), BF16),
        grid_spec=pltpu.PrefetchScalarGridSpec(
            num_scalar_prefetch=2,
            grid=(s // L,),
            in_specs=[cur(hw, H_SQ), prev(kvw, H_SK), cur(kvw, H_SK), prev(kvw, H_SV), cur(kvw, H_SV),
                      pl.BlockSpec(bucket.shape, lambda i, *_: (0, 0))],
            out_specs=pl.BlockSpec((L, hw), lambda i, *_: (i, 0)),
            scratch_shapes=[pltpu.VMEM((SWA_HEADS, L, 2 * L), F32)],
        ),
        compiler_params=_params(("arbitrary",), "swa"),
        name="swa",
    )(rel_bias, sinks, h, h, h, h, h, bucket)


def _layer_norm(y, g, b):
    mu = jnp.mean(y, -1, keepdims=True)
    yc = y - mu
    var = jnp.mean(yc * yc, -1, keepdims=True)
    return yc * lax.rsqrt(var + LN_EPS) * g + b


HALF = D_MODEL // 2
HIGH16 = 0xFFFF0000


SUBLANE = 8
TOK = HALF // LANE
assert TOK == SUBLANE


def _store_token_tiles(ref, x):
    tm = x.shape[0]
    lo = lax.bitcast_convert_type(x[:, :HALF].astype(BF16).astype(F32), jnp.uint32) >> 16
    hi = lax.bitcast_convert_type(x[:, HALF:].astype(BF16).astype(F32), jnp.uint32) & jnp.uint32(HIGH16)
    packed = lo | hi
    for s in range(TOK):
        ref[pl.ds(s, tm, stride=TOK), :] = packed[:, s * LANE:(s + 1) * LANE]


def _load_token_tiles(ref, rows=None):
    first, tm = rows if rows is not None else (0, ref.shape[0] // TOK)
    lo, hi = [], []
    for s in range(TOK):
        w = ref[pl.ds(first * TOK + s, tm, stride=TOK), :]
        lo.append(lax.bitcast_convert_type(w << 16, F32))
        hi.append(lax.bitcast_convert_type(w & jnp.uint32(HIGH16), F32))
    return jnp.concatenate(lo, 1), jnp.concatenate(hi, 1)


ROW_SPLIT = 2


CAST_ROWS = 256


def _cast_once(w_ref, wb_ref):
    @pl.when(pl.program_id(0) == 0)
    def _():
        for r0 in range(0, w_ref.shape[0], CAST_ROWS):
            wb_ref[r0:r0 + CAST_ROWS, :] = w_ref[r0:r0 + CAST_ROWS, :].astype(BF16)


def _out_ln_kernel(a_ref, b_ref, c_ref, x_ref, wf_ref, g_ref, beta_ref, o_ref, op_ref, w_ref):
    _cast_once(wf_ref, w_ref)
    na, nb = a_ref.shape[1], b_ref.shape[1]
    nr = x_ref.shape[0] // ROW_SPLIT
    for r0 in range(0, x_ref.shape[0], nr):
        rows = slice(r0, r0 + nr)
        m = _dot(a_ref[rows, :], w_ref[0:na, :])
        m = m + _dot(b_ref[rows, :], w_ref[na:na + nb, :])
        m = m + _dot(c_ref[rows, :], w_ref[na + nb:, :])
        y = _layer_norm(DEEPNORM_ALPHA * x_ref[rows, :] + m, g_ref[...], beta_ref[...])
        o_ref[rows, :] = y
        _store_token_tiles(op_ref.at[pl.ds(r0 * TOK, nr * TOK)], y)


def out_ln(a, b, c, x, w, g, beta, layer):
    s = x.shape[0]
    tm = _rows("out_ln", s)
    row = lambda arr: pl.BlockSpec((tm, arr.shape[1]), lambda i: (i, 0))
    full = lambda arr: _layer_block(arr, layer)
    return pl.pallas_call(
        _out_ln_kernel,
        out_shape=[jax.ShapeDtypeStruct((s, D_MODEL), F32), jax.ShapeDtypeStruct((s * TOK, LANE), jnp.uint32)],
        grid=(s // tm,),
        in_specs=[row(a), row(b), row(c), row(x), full(w), full(g), full(beta)],
        out_specs=[pl.BlockSpec((tm, D_MODEL), lambda i: (i, 0)), pl.BlockSpec((tm * TOK, LANE), lambda i: (i, 0))],
        scratch_shapes=[pltpu.VMEM(w.shape[1:], BF16)],
        compiler_params=_params(("arbitrary",), "out_ln"),
        name="out_ln",
    )(a, b, c, x, w, g, beta)


FFN_COLS = 512
FFN_OUT_CHUNK = 512
FFN_SUB = 256


def _ffn_kernel(te_ref, tv_ref, na_ref, xp_ref, wg_ref, wu_ref, wd_ref, o_ref, xb_ref, acc_ref):
    i, j = pl.program_id(0), pl.program_id(1)
    tm = xb_ref.shape[0]
    valid = tv_ref[i]

    @pl.when(j == 0)
    def _():
        acc_ref[...] = jnp.zeros_like(acc_ref)

    @pl.when((j == 0) & (valid > 0))
    def _():
        lo, hi = _load_token_tiles(xp_ref)
        xb_ref[:, :HALF] = lo.astype(BF16)
        xb_ref[:, HALF:] = hi.astype(BF16)

    def rows_step(r0, nr):
        xb = xb_ref[r0:r0 + nr, :]
        gate = _dot(xb, wg_ref[...].astype(BF16))
        up = _dot(xb, wu_ref[...].astype(BF16))
        hmid = (gate * (1.0 / (1.0 + jnp.exp(-gate))) * up).astype(BF16)
        for c in range(0, D_MODEL, FFN_OUT_CHUNK):
            cs = slice(c, c + FFN_OUT_CHUNK)
            acc_ref[r0:r0 + nr, cs] += _dot(hmid, wd_ref[:, cs].astype(BF16))

    nearly_full = valid > tm - FFN_SUB

    @pl.when(nearly_full)
    def _():
        rows_step(0, tm)

    for r0 in range(0, tm - FFN_SUB, FFN_SUB):
        @pl.when(jnp.logical_not(nearly_full) & (valid > r0))
        def _():
            rows_step(r0, FFN_SUB)

    @pl.when(j == pl.num_programs(1) - 1)
    def _():
        _store_token_tiles(o_ref, acc_ref[...])


def grouped_swiglu(xp, wg, wu, wd, tile_expert, tile_valid, n_active, tm, tf):
    n_tiles, nf = xp.shape[0] // (tm * TOK), D_FF // tf

    def tile(i, na):
        return jnp.minimum(i, na[0] - 1)

    def fcol(i, j, na):
        return jnp.where(i < na[0], j, nf - 1)

    return pl.pallas_call(
        _ffn_kernel,
        out_shape=jax.ShapeDtypeStruct(xp.shape, jnp.uint32),
        grid_spec=pltpu.PrefetchScalarGridSpec(
            num_scalar_prefetch=3,
            grid=(n_tiles, nf),
            in_specs=[pl.BlockSpec((tm * TOK, LANE), lambda i, j, te, tv, na: (tile(i, na), 0)),
                      pl.BlockSpec((None, D_MODEL, tf),
                                   lambda i, j, te, tv, na: (te[tile(i, na)], 0, fcol(i, j, na))),
                      pl.BlockSpec((None, D_MODEL, tf),
                                   lambda i, j, te, tv, na: (te[tile(i, na)], 0, fcol(i, j, na))),
                      pl.BlockSpec((None, tf, D_MODEL),
                                   lambda i, j, te, tv, na: (te[tile(i, na)], fcol(i, j, na), 0))],
            out_specs=pl.BlockSpec((tm * TOK, LANE), lambda i, j, te, tv, na: (i, 0)),
            scratch_shapes=[pltpu.VMEM((tm, D_MODEL), BF16), pltpu.VMEM((tm, D_MODEL), F32)],
        ),
        compiler_params=_params(("arbitrary", "arbitrary"), "grouped_swiglu"),
        name="grouped_swiglu",
    )(tile_expert, tile_valid, n_active, xp, wg, wu, wd)


def _router_kernel(x_ref, w_ref, e_ref, r_ref, wt_ref, cnt_ref, run_ref):
    i = pl.program_id(0)
    tm = x_ref.shape[0]
    ne = N_EXPERTS

    @pl.when(i == 0)
    def _():
        run_ref[...] = jnp.zeros_like(run_ref)

    x = x_ref[...]
    x_hi = x.astype(BF16)
    x_lo = (x - x_hi.astype(F32)).astype(BF16)
    w = w_ref[...]
    w_hi = w.astype(BF16).astype(F32)
    w_lo = w - w_hi
    both = _dot_nt(jnp.concatenate([w_hi, w_lo], 0).astype(BF16), x_hi)
    cross = _dot_nt(jnp.concatenate([w_hi, jnp.zeros_like(w_hi)], 0).astype(BF16), x_lo)
    logits = both[0:ne] + both[ne:2 * ne] + cross[0:ne]

    eidx = lax.broadcasted_iota(jnp.int32, logits.shape, 0).astype(F32)
    v1 = jnp.max(logits, 0, keepdims=True)
    i1 = jnp.min(jnp.where(logits == v1, eidx, float(ne)), 0, keepdims=True)
    rest = jnp.where(eidx == i1, -jnp.inf, logits)
    v2 = jnp.max(rest, 0, keepdims=True)
    i2 = jnp.min(jnp.where(rest == v2, eidx, float(ne)), 0, keepdims=True)
    t = jnp.exp(v2 - v1)
    w1 = 1.0 / (1.0 + t)
    wt_ref[0:1, :] = w1
    wt_ref[1:2, :] = t * w1
    e_ref[0:1, :] = i1.astype(jnp.int32)
    e_ref[1:2, :] = i2.astype(jnp.int32)

    sel1, sel2 = eidx == i1, eidx == i2
    sel = jnp.where(sel1, 1.0, 0.0) + jnp.where(sel2, 1.0, 0.0)
    before = (lax.broadcasted_iota(jnp.int32, (tm, tm), 0) < lax.broadcasted_iota(jnp.int32, (tm, tm), 1))
    sel16 = jnp.concatenate([sel, jnp.zeros_like(sel)], 0).astype(BF16)
    prefix = _dot(sel16, jnp.where(before, 1.0, 0.0).astype(BF16))[0:ne]
    rank = prefix + run_ref[:, 0:1]
    r_ref[0:1, :] = jnp.sum(jnp.where(sel1, rank, 0.0), 0, keepdims=True).astype(jnp.int32)
    r_ref[1:2, :] = jnp.sum(jnp.where(sel2, rank, 0.0), 0, keepdims=True).astype(jnp.int32)
    run_ref[...] = run_ref[...] + jnp.sum(sel, 1, keepdims=True)
    cnt_ref[...] = run_ref[...].astype(jnp.int32)


def route_tokens(x, w_router_t):
    s = x.shape[0]
    tm = _rows("moe_router", s)
    pair = pl.BlockSpec((2, tm), lambda i: (0, i))
    return pl.pallas_call(
        _router_kernel,
        out_shape=[jax.ShapeDtypeStruct((2, s), jnp.int32), jax.ShapeDtypeStruct((2, s), jnp.int32),
                   jax.ShapeDtypeStruct((2, s), F32), jax.ShapeDtypeStruct((N_EXPERTS, LANE), jnp.int32)],
        grid=(s // tm,),
        in_specs=[pl.BlockSpec((tm, D_MODEL), lambda i: (i, 0)),
                  pl.BlockSpec((N_EXPERTS, D_MODEL), lambda i: (0, 0))],
        out_specs=[pair, pair, pair, pl.BlockSpec((N_EXPERTS, LANE), lambda i: (0, 0))],
        scratch_shapes=[pltpu.VMEM((N_EXPERTS, LANE), F32)],
        compiler_params=_params(("arbitrary",), "moe_router"),
        name="moe_router",
    )(x, w_router_t)


def _slot_kernel(base_ref, e_ref, r_ref, s_ref):
    e = e_ref[...]
    slot = r_ref[...]
    for k in range(N_EXPERTS):
        slot = slot + jnp.where(e == k, base_ref[k], 0)
    s_ref[...] = slot


def token_slots(e_idx, rank, base):
    whole = pl.BlockSpec(e_idx.shape, lambda i, *_: (0, 0))
    return pl.pallas_call(
        _slot_kernel,
        out_shape=jax.ShapeDtypeStruct(e_idx.shape, jnp.int32),
        grid_spec=pltpu.PrefetchScalarGridSpec(num_scalar_prefetch=1, grid=(1,), in_specs=[whole, whole],
                                               out_specs=whole),
        compiler_params=_params(("arbitrary",), "token_slots"),
        name="token_slots",
    )(base, e_idx, rank)


DMA_UNROLL = 8
ZERO_BLOCK = 256


def _tile_rows(index):
    return pl.ds(pl.multiple_of(index * TOK, TOK), TOK)


def _dispatch_kernel(fill_ref, slot_ref, x_ref, xs_ref, zero_ref, sem, zsem):
    i = pl.program_id(0)
    tm = x_ref.shape[0] // TOK

    def issue(b, c):
        for u in range(DMA_UNROLL):
            t = b * DMA_UNROLL + u
            for k in range(2):
                pltpu.make_async_copy(x_ref.at[_tile_rows(t)], xs_ref.at[_tile_rows(slot_ref[k, t])],
                                      sem).start(priority=k)
        return c

    lax.fori_loop(0, tm // DMA_UNROLL, issue, 0)
    for k in range(2):
        pltpu.make_async_copy(x_ref, xs_ref.at[pl.ds(0, tm * TOK)], sem).wait()

    @pl.when(i == pl.num_programs(0) - 1)
    def _():
        zero_ref[...] = jnp.zeros_like(zero_ref)

        def zero_copy(slot):
            return pltpu.make_async_copy(zero_ref.at[pl.ds(0, TOK)], xs_ref.at[_tile_rows(slot)], zsem)

        for e in range(N_EXPERTS):
            lo, hi = fill_ref[0, e], fill_ref[1, e]

            def zissue(slot, c):
                zero_copy(slot).start()
                return c

            def zdrain(slot, c):
                zero_copy(slot).wait()
                return c

            lax.fori_loop(lo, hi, zissue, 0)
            lax.fori_loop(lo, hi, zdrain, 0)

        zrows = zero_ref.shape[0]
        zb = zrows // TOK

        def block_copy(b):
            return pltpu.make_async_copy(zero_ref, xs_ref.at[pl.ds(pl.multiple_of(b * zrows, zrows), zrows)], zsem)

        def bissue(b, c):
            block_copy(b).start()
            return c

        def bdrain(b, c):
            block_copy(b).wait()
            return c

        first, last = fill_ref[2, 0] // zb, xs_ref.shape[0] // zrows
        lax.fori_loop(first, last, bissue, 0)
        lax.fori_loop(first, last, bdrain, 0)


def moe_dispatch(xp, slots, fill, rows):
    s = xp.shape[0] // TOK
    tm = _rows("moe_dispatch", s)
    zb = min(ZERO_BLOCK, s)
    return pl.pallas_call(
        _dispatch_kernel,
        out_shape=jax.ShapeDtypeStruct((rows * TOK, LANE), xp.dtype),
        grid_spec=pltpu.PrefetchScalarGridSpec(
            num_scalar_prefetch=1,
            grid=(s // tm,),
            in_specs=[pl.BlockSpec((2, tm), lambda i, *_: (0, i), memory_space=pltpu.SMEM),
                      pl.BlockSpec((tm * TOK, LANE), lambda i, *_: (i, 0))],
            out_specs=pl.BlockSpec(memory_space=pl.ANY),
            scratch_shapes=[pltpu.VMEM((zb * TOK, LANE), xp.dtype), pltpu.SemaphoreType.DMA,
                            pltpu.SemaphoreType.DMA],
        ),
        compiler_params=_params(("arbitrary",), "moe_dispatch"),
        name="moe_dispatch",
    )(fill, slots, xp)


def _ple(x, p, wup_ref, wgate_ref, bgate_ref):
    up = _dot(p.astype(BF16), wup_ref[...])
    zg = _dot(x.astype(BF16), wgate_ref[...]) + bgate_ref[...]
    return up * (1.0 / (1.0 + jnp.exp(-zg)))


def _ple_ln_dense_kernel(x_ref, f_ref, p_ref, wupf_ref, wgatef_ref, bgate_ref, g_ref, beta_ref, o_ref,
                         wup_ref, wgate_ref):
    _cast_once(wupf_ref, wup_ref)
    _cast_once(wgatef_ref, wgate_ref)
    nr = x_ref.shape[0] // ROW_SPLIT
    for r0 in range(0, x_ref.shape[0], nr):
        rows = slice(r0, r0 + nr)
        x = x_ref[rows, :]
        f = jnp.concatenate(_load_token_tiles(f_ref, (r0, nr)), 1)
        ple = _ple(x, p_ref[rows, :], wup_ref, wgate_ref, bgate_ref)
        o_ref[rows, :] = _layer_norm(DEEPNORM_ALPHA * x + f + ple, g_ref[...], beta_ref[...])


def _ple_ln_moe_kernel(slot_ref, x_ref, wt_ref, p_ref, wupf_ref, wgatef_ref, bgate_ref, g_ref, beta_ref, ys_ref,
                       o_ref, wup_ref, wgate_ref, y1_ref, y2_ref, sem):
    _cast_once(wupf_ref, wup_ref)
    _cast_once(wgatef_ref, wgate_ref)
    tm = x_ref.shape[0]
    bufs = (y1_ref, y2_ref)

    def issue(b, c):
        for u in range(DMA_UNROLL):
            t = b * DMA_UNROLL + u
            for k in range(2):
                pltpu.make_async_copy(ys_ref.at[_tile_rows(slot_ref[k, t])], bufs[k].at[_tile_rows(t)],
                                      sem).start(priority=k)
        return c

    lax.fori_loop(0, tm // DMA_UNROLL, issue, 0)
    nr = tm // ROW_SPLIT
    ples = [_ple(x_ref[r0:r0 + nr, :], p_ref[r0:r0 + nr, :], wup_ref, wgate_ref, bgate_ref)
            for r0 in range(0, tm, nr)]
    for k in range(2):
        pltpu.make_async_copy(ys_ref.at[pl.ds(0, tm * TOK)], bufs[k], sem).wait()
    for i, r0 in enumerate(range(0, tm, nr)):
        rows = slice(r0, r0 + nr)
        wt = wt_ref[rows, :]
        f = (wt[:, 0:1] * jnp.concatenate(_load_token_tiles(y1_ref, (r0, nr)), 1)
             + wt[:, 1:2] * jnp.concatenate(_load_token_tiles(y2_ref, (r0, nr)), 1))
        o_ref[rows, :] = _layer_norm(DEEPNORM_ALPHA * x_ref[rows, :] + f + ples[i], g_ref[...], beta_ref[...])


def ple_ln(x, p, wup, wgate, bgate, g, beta, layer, f=None, moe=None):
    s = x.shape[0]
    name = "ple_ln_dense" if moe is None else "ple_ln_moe"
    tm = _rows(name, s)
    row = lambda arr: pl.BlockSpec((tm, arr.shape[1]), lambda i, *_: (i, 0))
    full = lambda arr: _layer_block(arr, layer)
    p_spec = pl.BlockSpec((None, None, tm, p.shape[-1]), lambda i, *_: (layer, 0, i, 0))
    tail = [p_spec, full(wup), full(wgate), full(bgate), full(g), full(beta)]
    out_spec = pl.BlockSpec((tm, D_MODEL), lambda i, *_: (i, 0))
    out_shape = jax.ShapeDtypeStruct((s, D_MODEL), F32)
    casts = [pltpu.VMEM(wup.shape[1:], BF16), pltpu.VMEM(wgate.shape[1:], BF16)]
    if moe is None:
        return pl.pallas_call(
            _ple_ln_dense_kernel, out_shape=out_shape, grid=(s // tm,),
            in_specs=[row(x), pl.BlockSpec((tm * TOK, LANE), lambda i: (i, 0))] + tail, out_specs=out_spec,
            scratch_shapes=casts,
            compiler_params=_params(("arbitrary",), name), name=name,
        )(x, f, p, wup, wgate, bgate, g, beta)
    ys, slots, wts_t = moe
    return pl.pallas_call(
        _ple_ln_moe_kernel,
        out_shape=out_shape,
        grid_spec=pltpu.PrefetchScalarGridSpec(
            num_scalar_prefetch=0,
            grid=(s // tm,),
            in_specs=[pl.BlockSpec((2, tm), lambda i: (0, i), memory_space=pltpu.SMEM), row(x), row(wts_t)] + tail
            + [pl.BlockSpec(memory_space=pl.ANY)],
            out_specs=out_spec,
            scratch_shapes=casts + [pltpu.VMEM((tm * TOK, LANE), jnp.uint32),
                                    pltpu.VMEM((tm * TOK, LANE), jnp.uint32), pltpu.SemaphoreType.DMA],
        ),
        compiler_params=_params(("arbitrary",), name),
        name=name,
    )(slots, x, wts_t, p, wup, wgate, bgate, g, beta, ys)


IN_SIZES = (MLA_Q_RANK, MLA_KV_RANK, MLA_ROPE, GLA_HEADS * GLA_DK, GLA_HEADS * GLA_DK, GLA_HEADS * GLA_DV,
            GLA_GATE_RANK, GLA_HEADS * GLA_DV, SWA_HEADS * SWA_HD, SWA_KV_HEADS * SWA_HD, SWA_KV_HEADS * SWA_HD)
IN_COLS = sum(IN_SIZES)
IN_DEST = (H_QA, H_KVA, H_KR, H_GQ, H_GK, H_GV, H_GLR, H_GR, H_SQ, H_SK, H_SV)


def _pack_w_in_kernel(w_ref, o_ref):
    src = 0
    for width, dst in zip(IN_SIZES, IN_DEST):
        o_ref[dst:dst + width, :] = w_ref[src:src + width, :].astype(BF16)
        pad = -width % LANE
        if pad:
            o_ref[dst + width:dst + width + pad, :] = jnp.zeros((pad, o_ref.shape[1]), BF16)
        src += width


def _pack_w_in(w_t):
    depth, _, d = w_t.shape
    tc = _rows("pack_w_in", d)
    return pl.pallas_call(
        _pack_w_in_kernel,
        out_shape=jax.ShapeDtypeStruct((depth, H_COLS, d), BF16),
        grid=(depth, d // tc),
        in_specs=[pl.BlockSpec((None, IN_COLS, tc), lambda l, i: (l, 0, i))],
        out_specs=pl.BlockSpec((None, H_COLS, tc), lambda l, i: (l, 0, i)),
        compiler_params=_params(("parallel", "parallel"), "pack_w_in"),
        name="pack_w_in",
    )(w_t)


def _pack_w_q_b(w):
    w = w.reshape(MLA_Q_RANK, MLA_HEADS, MLA_NOPE + MLA_ROPE)
    w = jnp.pad(w, ((0, 0), (0, 0), (0, MLA_QK - MLA_NOPE - MLA_ROPE)))
    return w.reshape(MLA_Q_RANK, MLA_HEADS * MLA_QK).astype(BF16)


def _token_mixer_ln(x, rope, layer, w_in_p, q_gain, w_q_b, kv_gain, w_kv_b, gla_w, gla_b, gla_gain, sinks,
                    rel_bias, w_out_b, ln_g, ln_b):
    h = proj_in(x, w_in_p, layer)
    q, k, v = mla_prep(h, *rope, q_gain.reshape(1, -1), kv_gain.reshape(1, -1), _pack_w_q_b(w_q_b),
                       w_kv_b.astype(BF16))
    a = mla_flash(q, k, v)
    gla_w_p = jnp.pad(gla_w, ((0, LANE - GLA_GATE_RANK), (0, 0))).astype(BF16)
    b = gla(h, gla_w_p, gla_b.reshape(1, -1), gla_gain.reshape(1, -1))
    c = swa(h, sinks, rel_bias)
    return out_ln(a, b, c, x, w_out_b, ln_g, ln_b, layer)


def _moe_plan(counts, tm, n_tiles):
    tiles = (counts + tm - 1) // tm
    ends = jnp.cumsum(tiles)
    base = (ends - tiles) * tm
    n_active = ends[-1:].astype(jnp.int32)
    tile_expert = jnp.searchsorted(ends, jnp.arange(n_tiles, dtype=jnp.int32), side="right")
    tile_expert = jnp.minimum(tile_expert, N_EXPERTS - 1).astype(jnp.int32)
    tile_start = jnp.arange(n_tiles, dtype=jnp.int32) * tm
    tile_valid = jnp.clip((base + counts)[tile_expert] - tile_start, 0, tm).astype(jnp.int32)
    used_rows = jnp.broadcast_to(ends[-1] * tm, counts.shape)
    fill = jnp.stack([base + counts, base + tiles * tm, used_rows]).astype(jnp.int32)
    return base.astype(jnp.int32), fill, tile_expert, tile_valid, n_active


def kernel(x, p, positions, w_in, mla_q_a_gain, mla_w_q_b, mla_kv_a_gain, mla_w_kv_b, gla_w_gate, gla_b_gate,
           gla_norm_gain, swa_sinks, rel_bias, w_out, ln1_g, ln1_b, ffn_w_gate, ffn_w_up, ffn_w_down,
           moe_router, moe_w_gate, moe_w_up, moe_w_down, ple_w_up, ple_w_gate, ple_b_gate, ln2_g, ln2_b):
    batch, s, _ = x.shape
    assert batch == 1
    xcur = x.reshape(s, D_MODEL)
    rope = rope_tables(positions)
    tm = _rows("grouped_swiglu", s)
    row_stack = lambda v: v.reshape(DEPTH, 1, -1)
    w_in_p = _pack_w_in(jnp.swapaxes(w_in, 1, 2))
    tail = (p, ple_w_up, ple_w_gate, row_stack(ple_b_gate), row_stack(ln2_g), row_stack(ln2_b))
    for i in range(DEPTH):
        x1, x1p = _token_mixer_ln(xcur, rope, i, w_in_p, mla_q_a_gain[i], mla_w_q_b[i], mla_kv_a_gain[i],
                                  mla_w_kv_b[i], gla_w_gate[i], gla_b_gate[i], gla_norm_gain[i], swa_sinks[i],
                                  rel_bias, w_out, row_stack(ln1_g), row_stack(ln1_b))
        j = i // 2
        if i % 2 == 0:
            n_tiles = s // tm
            f = grouped_swiglu(x1p, ffn_w_gate[j][None], ffn_w_up[j][None], ffn_w_down[j][None],
                               jnp.zeros((n_tiles,), jnp.int32), jnp.full((n_tiles,), tm, jnp.int32),
                               jnp.full((1,), n_tiles, jnp.int32), tm, FFN_COLS)
            xcur = ple_ln(x1, *tail, i, f=f)
        else:
            n_tiles = (2 * s + N_EXPERTS * (tm - 1)) // tm
            e_idx, rank, wts, counts = route_tokens(x1, moe_router[j].T)
            base, fill, tile_expert, tile_valid, n_active = _moe_plan(counts[:, 0], tm, n_tiles)
            slots = token_slots(e_idx, rank, base)
            xs = moe_dispatch(x1p, slots, fill, n_tiles * tm)
            ys = grouped_swiglu(xs, moe_w_gate[j], moe_w_up[j], moe_w_down[j], tile_expert, tile_valid, n_active,
                                tm, FFN_COLS)
            xcur = ple_ln(x1, *tail, i, moe=(ys, slots, wts.T))
    return xcur.reshape(batch, s, D_MODEL)
```

```python
import functools
import math
from typing import NamedTuple

import numpy as np
import jax
import jax.numpy as jnp
from jax import lax
from jax.experimental import pallas as pl
from jax.experimental.pallas import tpu as pltpu

F32 = jnp.float32
BF16 = jnp.bfloat16

D_MODEL = 2048
DEPTH = 2
MLA_HEADS = 8
MLA_Q_RANK = 512
MLA_KV_RANK = 256
MLA_NOPE = 128
MLA_ROPE = 64
MLA_V = 128
ROPE_THETA = 10000.0
GLA_HEADS = 4
GLA_DK = 64
GLA_DV = 128
GLA_GATE_RANK = 16
GLA_TAU = 16.0
SWA_HEADS = 8
SWA_KV_HEADS = 2
SWA_HD = 64
SWA_WINDOW = 128
SWA_BLOCK = 128
REL_BUCKETS = 32
REL_MAX_DIST = 128
D_FF = 5632
N_EXPERTS = 8
PLE_DIM = 256
LN_EPS = 1e-5
RMS_EPS = 1e-6
DEEPNORM_ALPHA = (2 * DEPTH) ** 0.25

LANE = 128
LOG2E = math.log2(math.e)
NEG_BIG = -1e30

H_QA, H_GV, H_GR, H_SQ = 0, 512, 1024, 1536
H_KVA, H_GQ, H_GK = 2048, 2304, 2560
H_KR, H_GLR, H_SK, H_SV = 2816, 2944, 3072, 3200
H_COLS = 3328

MLA_QK = 2 * LANE

GLA_L = 128
GLA_SUB = 32
GLA_NSUB = GLA_L // GLA_SUB
GLA_CHUNKS_PER_STEP = 4

V7X_VMEM_MIB = 64


class _Plan(NamedTuple):
    rows: int
    vmem_mib: int


PLANS = {
    "rope_tables": _Plan(1024, 32),
    "pack_w_in": _Plan(512, 40),
    "proj_in": _Plan(1024, 56),
    "mla_prep": _Plan(512, 48),
    "mla_flash": _Plan(512, 56),
    "gla": _Plan(GLA_CHUNKS_PER_STEP * GLA_L, 32),
    "swa": _Plan(SWA_BLOCK, 32),
    "out_ln": _Plan(512, 60),
    "grouped_swiglu": _Plan(1024, 60),
    "moe_router": _Plan(512, 32),
    "token_slots": _Plan(0, 32),
    "moe_dispatch": _Plan(1024, 32),
    "ple_ln_dense": _Plan(512, 60),
    "ple_ln_moe": _Plan(256, 60),
}
assert all(plan.vmem_mib < V7X_VMEM_MIB for plan in PLANS.values())


def _rows(name, s):
    return min(PLANS[name].rows, s)


def _params(sem, name):
    return pltpu.CompilerParams(dimension_semantics=sem, vmem_limit_bytes=PLANS[name].vmem_mib * 2 ** 20)


def _dot(a, b):
    return jnp.dot(a, b, preferred_element_type=F32)


def _dot_nt(a, b):
    return lax.dot_general(a, b, (((1,), (1,)), ((), ())), preferred_element_type=F32)


def _dot_tn(a, b):
    return lax.dot_general(a, b, (((0,), (0,)), ((), ())), preferred_element_type=F32)


def _proj_in_kernel(x_ref, w_ref, o_ref, xb_ref):
    @pl.when(pl.program_id(1) == 0)
    def _():
        xb_ref[...] = x_ref[...].astype(BF16)

    o_ref[...] = _dot_nt(xb_ref[...], w_ref[...]).astype(o_ref.dtype)


def _layer_block(arr, layer):
    zeros = (0,) * (arr.ndim - 1)
    return pl.BlockSpec((None,) + arr.shape[1:], lambda i, *_: (layer,) + zeros, pipeline_mode=pl.Buffered(1))


def proj_in(x, w_p, layer):
    s = x.shape[0]
    tm = _rows("proj_in", s)
    tn = H_COLS // 2
    return pl.pallas_call(
        _proj_in_kernel,
        out_shape=jax.ShapeDtypeStruct((s, H_COLS), BF16),
        grid=(s // tm, H_COLS // tn),
        in_specs=[pl.BlockSpec((tm, D_MODEL), lambda i, j: (i, 0)),
                  pl.BlockSpec((None, tn, D_MODEL), lambda i, j: (layer, j, 0))],
        out_specs=pl.BlockSpec((tm, tn), lambda i, j: (i, j)),
        scratch_shapes=[pltpu.VMEM((tm, D_MODEL), BF16)],
        compiler_params=_params(("parallel", "arbitrary"), "proj_in"),
        name="proj_in",
    )(x, w_p)


def _rope_table_kernel(pos_ref, inv_ref, cos_ref, sa_ref, sb_ref):
    ang = pos_ref[...].astype(F32) * inv_ref[...]
    lane = lax.broadcasted_iota(jnp.int32, ang.shape, 1)
    half = MLA_ROPE // 2
    c, s = jnp.cos(ang), jnp.sin(ang)
    cos_ref[...] = c
    sa_ref[...] = jnp.where((lane >= half) & (lane < 2 * half), s, 0.0)
    sb_ref[...] = jnp.where(lane < half, -s, 0.0)


def rope_tables(positions):
    s = positions.shape[-1]
    half = MLA_ROPE // 2
    inv = ROPE_THETA ** (-jnp.arange(half, dtype=F32) / half)
    inv = jnp.concatenate([inv, inv, jnp.zeros((LANE - 2 * half,), F32)]).reshape(1, LANE)
    tm = _rows("rope_tables", s)
    spec = pl.BlockSpec((tm, LANE), lambda i: (i, 0))
    return pl.pallas_call(
        _rope_table_kernel,
        out_shape=[jax.ShapeDtypeStruct((s, LANE), F32)] * 3,
        grid=(s // tm,),
        in_specs=[pl.BlockSpec((tm, 1), lambda i: (i, 0)), pl.BlockSpec((1, LANE), lambda i: (0, 0))],
        out_specs=[spec, spec, spec],
        compiler_params=_params(("parallel",), "rope_tables"),
        name="rope_tables",
    )(positions.reshape(s, 1), inv)


def _rope(x, cos, sa, sb):
    return x * cos + pltpu.roll(x, MLA_ROPE // 2, 1) * sa + pltpu.roll(x, LANE - MLA_ROPE // 2, 1) * sb


def _mla_prep_kernel(qa_ref, kva_ref, kr_ref, cos_ref, sa_ref, sb_ref, gq_ref, gkv_ref, wq_ref, wkv_ref,
                     q_out, k_out, v_out):
    cos, sa, sb = cos_ref[...], sa_ref[...], sb_ref[...]
    qscale = (MLA_NOPE + MLA_ROPE) ** -0.5 * LOG2E

    qa = qa_ref[...].astype(F32)
    qn = qa * lax.rsqrt(jnp.mean(qa * qa, -1, keepdims=True) + RMS_EPS) * gq_ref[...]
    q = _dot(qn.astype(BF16), wq_ref[...])
    for h in range(MLA_HEADS):
        c0 = h * MLA_QK
        q_out[h, :, 0:LANE] = (q[:, c0:c0 + LANE] * qscale).astype(BF16)
        pe = _rope(q[:, c0 + LANE:c0 + 2 * LANE], cos, sa, sb)
        q_out[h, :, LANE:2 * LANE] = (pe * qscale).astype(BF16)

    kva = kva_ref[...].astype(F32)
    kvn = kva * lax.rsqrt(jnp.mean(kva * kva, -1, keepdims=True) + RMS_EPS) * gkv_ref[...]
    kv = _dot(kvn.astype(BF16), wkv_ref[...])
    kpe = _rope(kr_ref[...].astype(F32), cos, sa, sb).astype(BF16)
    for h in range(MLA_HEADS):
        c0 = h * (MLA_NOPE + MLA_V)
        k_out[h, :, 0:LANE] = kv[:, c0:c0 + MLA_NOPE].astype(BF16)
        k_out[h, :, LANE:2 * LANE] = kpe
        v_out[h, :, 0:MLA_V] = kv[:, c0 + MLA_NOPE:c0 + MLA_NOPE + MLA_V].astype(BF16)
        v_out[h, :, MLA_V:2 * MLA_V] = jnp.ones((kv.shape[0], MLA_V), BF16)


def mla_prep(h, cos, sa, sb, gq, gkv, wq_p, wkv):
    s = h.shape[0]
    tm = _rows("mla_prep", s)
    row = lambda width, col: pl.BlockSpec((tm, width), lambda i: (i, col // width))
    full = lambda a: pl.BlockSpec(a.shape, lambda i: (0,) * a.ndim)
    return pl.pallas_call(
        _mla_prep_kernel,
        out_shape=[jax.ShapeDtypeStruct((MLA_HEADS, s, MLA_QK), BF16),
                   jax.ShapeDtypeStruct((MLA_HEADS, s, MLA_QK), BF16),
                   jax.ShapeDtypeStruct((MLA_HEADS, s, 2 * MLA_V), BF16)],
        grid=(s // tm,),
        in_specs=[row(MLA_Q_RANK, H_QA), row(MLA_KV_RANK, H_KVA), row(LANE, H_KR),
                  row(LANE, 0), row(LANE, 0), row(LANE, 0),
                  full(gq), full(gkv), full(wq_p), full(wkv)],
        out_specs=[pl.BlockSpec((MLA_HEADS, tm, MLA_QK), lambda i: (0, i, 0)),
                   pl.BlockSpec((MLA_HEADS, tm, MLA_QK), lambda i: (0, i, 0)),
                   pl.BlockSpec((MLA_HEADS, tm, 2 * MLA_V), lambda i: (0, i, 0))],
        compiler_params=_params(("parallel",), "mla_prep"),
        name="mla_prep",
    )(h, h, h, cos, sa, sb, gq, gkv, wq_p, wkv)


def _mla_flash_kernel(q_ref, k_ref, v_ref, o_ref, sa_ref, sb_ref, mxa_ref, mxb_ref, m_ref, acc_ref, *, t, nh):
    qi = pl.program_id(1)
    heads = range(nh)
    bufs = ((sa_ref, mxa_ref), (sb_ref, mxb_ref))

    def produce(b, parity, masked):
        s_ref, mx_ref = bufs[parity]
        start = pl.multiple_of(b * t, t)
        for h in heads:
            s = _dot_nt(q_ref[h], k_ref[h, pl.ds(start, t), :])
            if masked:
                rows = lax.broadcasted_iota(jnp.int32, s.shape, 0) + qi * t
                cols = lax.broadcasted_iota(jnp.int32, s.shape, 1) + b * t
                s = jnp.where(cols <= rows, s, NEG_BIG)
            s_ref[h] = s
            mx_ref[h] = jnp.broadcast_to(jnp.max(s, -1, keepdims=True), (t, LANE))

    def absorb(b, parity):
        s_ref, mx_ref = bufs[parity]
        start = pl.multiple_of(b * t, t)
        for h in heads:
            m_new = jnp.maximum(m_ref[h], mx_ref[h])
            alpha = jnp.exp2(m_ref[h] - m_new)
            p = jnp.exp2(s_ref[h] - jnp.concatenate([m_new] * (t // LANE), 1))
            acc_ref[h] = (jnp.concatenate([alpha, alpha], 1) * acc_ref[h]
                          + _dot(p.astype(BF16), v_ref[h, pl.ds(start, t), :]))
            m_ref[h] = m_new

    m_ref[...] = jnp.full(m_ref.shape, NEG_BIG, F32)
    acc_ref[...] = jnp.zeros_like(acc_ref)
    produce(0, 0, True)

    def pair(i, c):
        b = 2 * i
        produce(b + 1, 1, False)
        absorb(b, 0)
        produce(b + 2, 0, False)
        absorb(b + 1, 1)
        return c

    n_pairs = jnp.maximum(qi - 1, 0) // 2
    lax.fori_loop(0, n_pairs, pair, 0)
    done = 2 * n_pairs

    @pl.when(qi == 0)
    def _():
        absorb(0, 0)

    @pl.when((qi > 0) & (qi - done == 1))
    def _():
        produce(qi, 1, True)
        absorb(qi - 1, 0)
        absorb(qi, 1)

    @pl.when((qi > 0) & (qi - done == 2))
    def _():
        produce(qi - 1, 1, False)
        absorb(qi - 2, 0)
        produce(qi, 0, True)
        absorb(qi - 1, 1)
        absorb(qi, 0)

    for h in heads:
        acc = acc_ref[h]
        o_ref[:, h * MLA_V:(h + 1) * MLA_V] = (acc[:, :MLA_V] / acc[:, MLA_V:]).astype(o_ref.dtype)


def mla_flash(q, k, v1):
    _, s, _ = q.shape
    t = _rows("mla_flash", s)
    nh = 2
    return pl.pallas_call(
        functools.partial(_mla_flash_kernel, t=t, nh=nh),
        out_shape=jax.ShapeDtypeStruct((s, MLA_HEADS * MLA_V), BF16),
        grid=(MLA_HEADS // nh, s // t),
        in_specs=[pl.BlockSpec((nh, t, MLA_QK), lambda h, i: (h, i, 0)),
                  pl.BlockSpec((nh, s, MLA_QK), lambda h, i: (h, 0, 0)),
                  pl.BlockSpec((nh, s, 2 * MLA_V), lambda h, i: (h, 0, 0))],
        out_specs=pl.BlockSpec((t, nh * MLA_V), lambda h, i: (i, h)),
        scratch_shapes=[pltpu.VMEM((nh, t, t), F32), pltpu.VMEM((nh, t, t), F32),
                        pltpu.VMEM((nh, t, LANE), F32), pltpu.VMEM((nh, t, LANE), F32),
                        pltpu.VMEM((nh, t, LANE), F32), pltpu.VMEM((nh, t, 2 * MLA_V), F32)],
        compiler_params=_params(("parallel", "arbitrary"), "mla_flash"),
        name="mla_flash",
    )(q, k, v1)


def _gla_masks():
    i = np.arange(GLA_L)[:, None]
    j = np.arange(GLA_L)[None, :]
    same = (i // GLA_SUB) == (j // GLA_SUB)
    mats = [j <= i, same & (j <= i), same & (j > i), j > i]
    for sub in range(GLA_NSUB - 1):
        mats.append((j >= (sub + 1) * GLA_SUB) & (j <= i))
    return np.concatenate(mats, 0).astype(np.float32)


def _gla_kernel(q_ref, k_ref, v_ref, lr_ref, r_ref, wg_ref, bg_ref, gain_ref, mask_ref, o_ref, state_ref):
    @pl.when(pl.program_id(0) == 0)
    def _():
        state_ref[...] = jnp.zeros_like(state_ref)

    L, hk = GLA_L, GLA_HEADS * GLA_DK
    chunks = [slice(c * L, (c + 1) * L) for c in range(q_ref.shape[0] // L)]
    heads = [(slice(h * GLA_DK, (h + 1) * GLA_DK), slice(h * GLA_DV, (h + 1) * GLA_DV)) for h in range(GLA_HEADS)]

    z = _dot(lr_ref[...], wg_ref[...]) + bg_ref[...]
    g = (jnp.minimum(z, 0.0) - jnp.log(1.0 + jnp.exp(-jnp.abs(z)))) * (1.0 / GLA_TAU)
    g_hi = g.astype(BF16)
    g_lo = (g - g_hi.astype(F32)).astype(BF16)

    cums = [_dot(mask_ref[...], jnp.concatenate([g_hi[rs], g_lo[rs]], 0)) for rs in chunks]

    sub_of_row = lax.broadcasted_iota(jnp.int32, (L, hk), 0) // GLA_SUB
    prep = []
    for rs, cum in zip(chunks, cums):
        b_all, b_loc, sfx_loc, sfx_all = cum[0:L], cum[L:2 * L], cum[2 * L:3 * L], cum[3 * L:4 * L]
        q = q_ref[rs, :].astype(F32) * (GLA_DK ** -0.5)
        k = k_ref[rs, :].astype(F32)
        k_end = k * jnp.exp(sfx_loc)
        prep.append(dict(
            decay=jnp.exp(b_all[L - 1:L, :]),
            q_inter=(q * jnp.exp(b_all)).astype(BF16),
            q_diag=(q * jnp.exp(b_loc)).astype(BF16),
            k_diag=(k * jnp.exp(-b_loc)).astype(BF16),
            k_state=(k * jnp.exp(sfx_all)).astype(BF16),
            q_off=[(q * jnp.exp(cum[(4 + sub) * L:(5 + sub) * L])).astype(BF16) for sub in range(GLA_NSUB - 1)],
            k_off=[jnp.where(sub_of_row == sub, k_end, 0.0).astype(BF16) for sub in range(GLA_NSUB - 1)]))

    row = lax.broadcasted_iota(jnp.int32, (L, L), 0)
    col = lax.broadcasted_iota(jnp.int32, (L, L), 1)
    diag_ok = ((row // GLA_SUB) == (col // GLA_SUB)) & (col <= row)
    off_ok = (row // GLA_SUB) > (col // GLA_SUB)
    local = []
    for rs, pr in zip(chunks, prep):
        per_head = []
        for ks, vs in heads:
            v_h = v_ref[rs, vs]
            a = jnp.where(diag_ok, _dot_nt(pr["q_diag"][:, ks], pr["k_diag"][:, ks]), 0.0)
            qo = jnp.concatenate([t[:, ks] for t in pr["q_off"]], 1)
            ko = jnp.concatenate([t[:, ks] for t in pr["k_off"]], 1)
            a = a + jnp.where(off_ok, _dot_nt(qo, ko), 0.0)
            per_head.append((_dot(a.astype(BF16), v_h), _dot_tn(v_h, pr["k_state"][:, ks])))
        local.append(per_head)

    states = [state_ref[h] for h in range(GLA_HEADS)]
    for rs, pr, per_head in zip(chunks, prep, local):
        r = r_ref[rs, :].astype(F32)
        gate = r * (1.0 / (1.0 + jnp.exp(-r)))
        for h, (ks, vs) in enumerate(heads):
            o_intra, vk = per_head[h]
            o = _dot_nt(pr["q_inter"][:, ks], states[h].astype(BF16)) + o_intra
            states[h] = states[h] * pr["decay"][:, ks] + vk
            o = o * lax.rsqrt(jnp.mean(o * o, -1, keepdims=True) + RMS_EPS) * gain_ref[:, vs]
            o_ref[rs, vs] = (o * gate[:, vs]).astype(o_ref.dtype)
    for h in range(GLA_HEADS):
        state_ref[h] = states[h]


def gla(h, wg_p, bg, gain):
    s = h.shape[0]
    rows = _rows("gla", s)
    masks = jnp.asarray(np.tile(_gla_masks(), (1, 2)), BF16)
    row = lambda width, col: pl.BlockSpec((rows, width), lambda i: (i, col // width))
    full = lambda a: pl.BlockSpec(a.shape, lambda i: (0,) * a.ndim)
    hk, hv = GLA_HEADS * GLA_DK, GLA_HEADS * GLA_DV
    return pl.pallas_call(
        _gla_kernel,
        out_shape=jax.ShapeDtypeStruct((s, hv), BF16),
        grid=(s // rows,),
        in_specs=[row(hk, H_GQ), row(hk, H_GK), row(hv, H_GV), row(LANE, H_GLR), row(hv, H_GR),
                  full(wg_p), full(bg), full(gain), full(masks)],
        out_specs=pl.BlockSpec((rows, hv), lambda i: (i, 0)),
        scratch_shapes=[pltpu.VMEM((GLA_HEADS, GLA_DV, GLA_DK), F32)],
        compiler_params=_params(("arbitrary",), "gla"),
        name="gla",
    )(h, h, h, h, h, wg_p, bg, gain, masks)


def _t5_bucket_table():
    L = SWA_BLOCK
    dist = np.arange(L)[:, None] + L - np.arange(2 * L)[None, :]
    d = np.clip(dist, 0, None)
    max_exact = REL_BUCKETS // 2
    df = np.maximum(d, 1).astype(np.float32)
    large = max_exact + (np.log(df / np.float32(max_exact)) / np.float32(math.log(REL_MAX_DIST / max_exact))
                         * np.float32(REL_BUCKETS - max_exact)).astype(np.int32)
    large = np.minimum(large, REL_BUCKETS - 1)
    bucket = np.where(d < max_exact, d, large)
    in_window = (dist >= 0) & (dist < SWA_WINDOW)
    return np.where(in_window, bucket, -1).astype(np.int32)


def _swa_kernel(relb_ref, sink_ref, q_ref, kp_ref, kc_ref, vp_ref, vc_ref, bucket_ref, o_ref, bias_ref):
    i = pl.program_id(0)
    L = SWA_BLOCK

    @pl.when(i == 0)
    def _():
        bucket = bucket_ref[...]
        for h in range(SWA_HEADS):
            acc = jnp.full(bucket.shape, NEG_BIG, F32)
            for b in range(REL_BUCKETS):
                acc = jnp.where(bucket == b, relb_ref[b, h], acc)
            bias_ref[h] = acc

    kcat = jnp.concatenate([kp_ref[...], kc_ref[...]], 0)
    vcat = jnp.concatenate([vp_ref[...], vc_ref[...]], 0)
    scale = SWA_HD ** -0.5
    assert math.log2(scale).is_integer()
    q = q_ref[...] * scale
    col = lax.broadcasted_iota(jnp.int32, (L, 2 * L), 1)
    real_key = (col >= L) | (i > 0)
    g = SWA_HEADS // SWA_KV_HEADS
    ones = jnp.ones((2 * L, SWA_HD), BF16)
    v1 = [jnp.concatenate([vcat[:, kv * SWA_HD:(kv + 1) * SWA_HD], ones], 1) for kv in range(SWA_KV_HEADS)]
    for h in range(SWA_HEADS):
        kv = h // g
        hs = slice(h * SWA_HD, (h + 1) * SWA_HD)
        kvs = slice(kv * SWA_HD, (kv + 1) * SWA_HD)
        s = _dot_nt(q[:, hs], kcat[:, kvs]) + bias_ref[h]
        s = jnp.where(real_key, s, NEG_BIG)
        sink = sink_ref[h]
        m = jnp.maximum(jnp.max(s, -1, keepdims=True), sink)
        o2 = _dot(jnp.exp(s - m).astype(BF16), v1[kv])
        denom = o2[:, SWA_HD:] + jnp.exp(sink - m)
        o_ref[:, hs] = (o2[:, :SWA_HD] / denom).astype(o_ref.dtype)


def swa(h, sinks, rel_bias):
    s = h.shape[0]
    L = SWA_BLOCK
    bucket = jnp.asarray(_t5_bucket_table())
    kvw = SWA_KV_HEADS * SWA_HD
    hw = SWA_HEADS * SWA_HD
    cur = lambda width, col: pl.BlockSpec((L, width), lambda i, *_: (i, col // width))
    prev = lambda width, col: pl.BlockSpec((L, width), lambda i, *_: (jnp.maximum(i - 1, 0), col // width))
    return pl.pallas_call(
        _swa_kernel,
        out_shape=jax.ShapeDtypeStruct((s, hw), BF16),
        grid_spec=pltpu.PrefetchScalarGridSpec(
            num_scalar_prefetch=2,
            grid=(s // L,),
            in_specs=[cur(hw, H_SQ), prev(kvw, H_SK), cur(kvw, H_SK), prev(kvw, H_SV), cur(kvw, H_SV),
                      pl.BlockSpec(bucket.shape, lambda i, *_: (0, 0))],
            out_specs=pl.BlockSpec((L, hw), lambda i, *_: (i, 0)),
            scratch_shapes=[pltpu.VMEM((SWA_HEADS, L, 2 * L), F32)],
        ),
        compiler_params=_params(("arbitrary",), "swa"),
        name="swa",
    )(rel_bias, sinks, h, h, h, h, h, bucket)


def _layer_norm(y, g, b):
    mu = jnp.mean(y, -1, keepdims=True)
    yc = y - mu
    var = jnp.mean(yc * yc, -1, keepdims=True)
    return yc * lax.rsqrt(var + LN_EPS) * g + b


HALF = D_MODEL // 2
HIGH16 = 0xFFFF0000


SUBLANE = 8
TOK = HALF // LANE
assert TOK == SUBLANE


def _store_token_tiles(ref, x):
    tm = x.shape[0]
    lo = lax.bitcast_convert_type(x[:, :HALF].astype(BF16).astype(F32), jnp.uint32) >> 16
    hi = lax.bitcast_convert_type(x[:, HALF:].astype(BF16).astype(F32), jnp.uint32) & jnp.uint32(HIGH16)
    packed = lo | hi
    for s in range(TOK):
        ref[pl.ds(s, tm, stride=TOK), :] = packed[:, s * LANE:(s + 1) * LANE]


def _load_token_tiles(ref, rows=None):
    first, tm = rows if rows is not None else (0, ref.shape[0] // TOK)
    lo, hi = [], []
    for s in range(TOK):
        w = ref[pl.ds(first * TOK + s, tm, stride=TOK), :]
        lo.append(lax.bitcast_convert_type(w << 16, F32))
        hi.append(lax.bitcast_convert_type(w & jnp.uint32(HIGH16), F32))
    return jnp.concatenate(lo, 1), jnp.concatenate(hi, 1)


ROW_SPLIT = 2


CAST_ROWS = 256


def _cast_once(w_ref, wb_ref):
    @pl.when(pl.program_id(0) == 0)
    def _():
        for r0 in range(0, w_ref.shape[0], CAST_ROWS):
            wb_ref[r0:r0 + CAST_ROWS, :] = w_ref[r0:r0 + CAST_ROWS, :].astype(BF16)


def _out_ln_kernel(a_ref, b_ref, c_ref, x_ref, wf_ref, g_ref, beta_ref, o_ref, op_ref, w_ref):
    _cast_once(wf_ref, w_ref)
    na, nb = a_ref.shape[1], b_ref.shape[1]
    nr = x_ref.shape[0] // ROW_SPLIT
    for r0 in range(0, x_ref.shape[0], nr):
        rows = slice(r0, r0 + nr)
        m = _dot(a_ref[rows, :], w_ref[0:na, :])
        m = m + _dot(b_ref[rows, :], w_ref[na:na + nb, :])
        m = m + _dot(c_ref[rows, :], w_ref[na + nb:, :])
        y = _layer_norm(DEEPNORM_ALPHA * x_ref[rows, :] + m, g_ref[...], beta_ref[...])
        o_ref[rows, :] = y
        _store_token_tiles(op_ref.at[pl.ds(r0 * TOK, nr * TOK)], y)


def out_ln(a, b, c, x, w, g, beta, layer):
    s = x.shape[0]
    tm = _rows("out_ln", s)
    row = lambda arr: pl.BlockSpec((tm, arr.shape[1]), lambda i: (i, 0))
    full = lambda arr: _layer_block(arr, layer)
    return pl.pallas_call(
        _out_ln_kernel,
        out_shape=[jax.ShapeDtypeStruct((s, D_MODEL), F32), jax.ShapeDtypeStruct((s * TOK, LANE), jnp.uint32)],
        grid=(s // tm,),
        in_specs=[row(a), row(b), row(c), row(x), full(w), full(g), full(beta)],
        out_specs=[pl.BlockSpec((tm, D_MODEL), lambda i: (i, 0)), pl.BlockSpec((tm * TOK, LANE), lambda i: (i, 0))],
        scratch_shapes=[pltpu.VMEM(w.shape[1:], BF16)],
        compiler_params=_params(("arbitrary",), "out_ln"),
        name="out_ln",
    )(a, b, c, x, w, g, beta)


FFN_COLS = 512
FFN_OUT_CHUNK = 512
FFN_SUB = 256


def _ffn_kernel(te_ref, tv_ref, na_ref, xp_ref, wg_ref, wu_ref, wd_ref, o_ref, xb_ref, acc_ref):
    i, j = pl.program_id(0), pl.program_id(1)
    tm = xb_ref.shape[0]
    valid = tv_ref[i]

    @pl.when(j == 0)
    def _():
        acc_ref[...] = jnp.zeros_like(acc_ref)

    @pl.when((j == 0) & (valid > 0))
    def _():
        lo, hi = _load_token_tiles(xp_ref)
        xb_ref[:, :HALF] = lo.astype(BF16)
        xb_ref[:, HALF:] = hi.astype(BF16)

    def rows_step(r0, nr):
        xb = xb_ref[r0:r0 + nr, :]
        gate = _dot(xb, wg_ref[...].astype(BF16))
        up = _dot(xb, wu_ref[...].astype(BF16))
        hmid = (gate * (1.0 / (1.0 + jnp.exp(-gate))) * up).astype(BF16)
        for c in range(0, D_MODEL, FFN_OUT_CHUNK):
            cs = slice(c, c + FFN_OUT_CHUNK)
            acc_ref[r0:r0 + nr, cs] += _dot(hmid, wd_ref[:, cs].astype(BF16))

    nearly_full = valid > tm - FFN_SUB

    @pl.when(nearly_full)
    def _():
        rows_step(0, tm)

    for r0 in range(0, tm - FFN_SUB, FFN_SUB):
        @pl.when(jnp.logical_not(nearly_full) & (valid > r0))
        def _():
            rows_step(r0, FFN_SUB)

    @pl.when(j == pl.num_programs(1) - 1)
    def _():
        _store_token_tiles(o_ref, acc_ref[...])


def grouped_swiglu(xp, wg, wu, wd, tile_expert, tile_valid, n_active, tm, tf):
    n_tiles, nf = xp.shape[0] // (tm * TOK), D_FF // tf

    def tile(i, na):
        return jnp.minimum(i, na[0] - 1)

    def fcol(i, j, na):
        return jnp.where(i < na[0], j, nf - 1)

    return pl.pallas_call(
        _ffn_kernel,
        out_shape=jax.ShapeDtypeStruct(xp.shape, jnp.uint32),
        grid_spec=pltpu.PrefetchScalarGridSpec(
            num_scalar_prefetch=3,
            grid=(n_tiles, nf),
            in_specs=[pl.BlockSpec((tm * TOK, LANE), lambda i, j, te, tv, na: (tile(i, na), 0)),
                      pl.BlockSpec((None, D_MODEL, tf),
                                   lambda i, j, te, tv, na: (te[tile(i, na)], 0, fcol(i, j, na))),
                      pl.BlockSpec((None, D_MODEL, tf),
                                   lambda i, j, te, tv, na: (te[tile(i, na)], 0, fcol(i, j, na))),
                      pl.BlockSpec((None, tf, D_MODEL),
                                   lambda i, j, te, tv, na: (te[tile(i, na)], fcol(i, j, na), 0))],
            out_specs=pl.BlockSpec((tm * TOK, LANE), lambda i, j, te, tv, na: (i, 0)),
            scratch_shapes=[pltpu.VMEM((tm, D_MODEL), BF16), pltpu.VMEM((tm, D_MODEL), F32)],
        ),
        compiler_params=_params(("arbitrary", "arbitrary"), "grouped_swiglu"),
        name="grouped_swiglu",
    )(tile_expert, tile_valid, n_active, xp, wg, wu, wd)


def _router_kernel(x_ref, w_ref, e_ref, r_ref, wt_ref, cnt_ref, run_ref):
    i = pl.program_id(0)
    tm = x_ref.shape[0]
    ne = N_EXPERTS

    @pl.when(i == 0)
    def _():
        run_ref[...] = jnp.zeros_like(run_ref)

    x = x_ref[...]
    x_hi = x.astype(BF16)
    x_lo = (x - x_hi.astype(F32)).astype(BF16)
    w = w_ref[...]
    w_hi = w.astype(BF16).astype(F32)
    w_lo = w - w_hi
    both = _dot_nt(jnp.concatenate([w_hi, w_lo], 0).astype(BF16), x_hi)
    cross = _dot_nt(jnp.concatenate([w_hi, jnp.zeros_like(w_hi)], 0).astype(BF16), x_lo)
    logits = both[0:ne] + both[ne:2 * ne] + cross[0:ne]

    eidx = lax.broadcasted_iota(jnp.int32, logits.shape, 0).astype(F32)
    v1 = jnp.max(logits, 0, keepdims=True)
    i1 = jnp.min(jnp.where(logits == v1, eidx, float(ne)), 0, keepdims=True)
    rest = jnp.where(eidx == i1, -jnp.inf, logits)
    v2 = jnp.max(rest, 0, keepdims=True)
    i2 = jnp.min(jnp.where(rest == v2, eidx, float(ne)), 0, keepdims=True)
    t = jnp.exp(v2 - v1)
    w1 = 1.0 / (1.0 + t)
    wt_ref[0:1, :] = w1
    wt_ref[1:2, :] = t * w1
    e_ref[0:1, :] = i1.astype(jnp.int32)
    e_ref[1:2, :] = i2.astype(jnp.int32)

    sel1, sel2 = eidx == i1, eidx == i2
    sel = jnp.where(sel1, 1.0, 0.0) + jnp.where(sel2, 1.0, 0.0)
    before = (lax.broadcasted_iota(jnp.int32, (tm, tm), 0) < lax.broadcasted_iota(jnp.int32, (tm, tm), 1))
    sel16 = jnp.concatenate([sel, jnp.zeros_like(sel)], 0).astype(BF16)
    prefix = _dot(sel16, jnp.where(before, 1.0, 0.0).astype(BF16))[0:ne]
    rank = prefix + run_ref[:, 0:1]
    r_ref[0:1, :] = jnp.sum(jnp.where(sel1, rank, 0.0), 0, keepdims=True).astype(jnp.int32)
    r_ref[1:2, :] = jnp.sum(jnp.where(sel2, rank, 0.0), 0, keepdims=True).astype(jnp.int32)
    run_ref[...] = run_ref[...] + jnp.sum(sel, 1, keepdims=True)
    cnt_ref[...] = run_ref[...].astype(jnp.int32)


def route_tokens(x, w_router_t):
    s = x.shape[0]
    tm = _rows("moe_router", s)
    pair = pl.BlockSpec((2, tm), lambda i: (0, i))
    return pl.pallas_call(
        _router_kernel,
        out_shape=[jax.ShapeDtypeStruct((2, s), jnp.int32), jax.ShapeDtypeStruct((2, s), jnp.int32),
                   jax.ShapeDtypeStruct((2, s), F32), jax.ShapeDtypeStruct((N_EXPERTS, LANE), jnp.int32)],
        grid=(s // tm,),
        in_specs=[pl.BlockSpec((tm, D_MODEL), lambda i: (i, 0)),
                  pl.BlockSpec((N_EXPERTS, D_MODEL), lambda i: (0, 0))],
        out_specs=[pair, pair, pair, pl.BlockSpec((N_EXPERTS, LANE), lambda i: (0, 0))],
        scratch_shapes=[pltpu.VMEM((N_EXPERTS, LANE), F32)],
        compiler_params=_params(("arbitrary",), "moe_router"),
        name="moe_router",
    )(x, w_router_t)


def _slot_kernel(base_ref, e_ref, r_ref, s_ref):
    e = e_ref[...]
    slot = r_ref[...]
    for k in range(N_EXPERTS):
        slot = slot + jnp.where(e == k, base_ref[k], 0)
    s_ref[...] = slot


def token_slots(e_idx, rank, base):
    whole = pl.BlockSpec(e_idx.shape, lambda i, *_: (0, 0))
    return pl.pallas_call(
        _slot_kernel,
        out_shape=jax.ShapeDtypeStruct(e_idx.shape, jnp.int32),
        grid_spec=pltpu.PrefetchScalarGridSpec(num_scalar_prefetch=1, grid=(1,), in_specs=[whole, whole],
                                               out_specs=whole),
        compiler_params=_params(("arbitrary",), "token_slots"),
        name="token_slots",
    )(base, e_idx, rank)


DMA_UNROLL = 8
ZERO_BLOCK = 256


def _tile_rows(index):
    return pl.ds(pl.multiple_of(index * TOK, TOK), TOK)


def _dispatch_kernel(fill_ref, slot_ref, x_ref, xs_ref, zero_ref, sem, zsem):
    i = pl.program_id(0)
    tm = x_ref.shape[0] // TOK

    def issue(b, c):
        for u in range(DMA_UNROLL):
            t = b * DMA_UNROLL + u
            for k in range(2):
                pltpu.make_async_copy(x_ref.at[_tile_rows(t)], xs_ref.at[_tile_rows(slot_ref[k, t])],
                                      sem).start(priority=k)
        return c

    lax.fori_loop(0, tm // DMA_UNROLL, issue, 0)
    for k in range(2):
        pltpu.make_async_copy(x_ref, xs_ref.at[pl.ds(0, tm * TOK)], sem).wait()

    @pl.when(i == pl.num_programs(0) - 1)
    def _():
        zero_ref[...] = jnp.zeros_like(zero_ref)

        def zero_copy(slot):
            return pltpu.make_async_copy(zero_ref.at[pl.ds(0, TOK)], xs_ref.at[_tile_rows(slot)], zsem)

        for e in range(N_EXPERTS):
            lo, hi = fill_ref[0, e], fill_ref[1, e]

            def zissue(slot, c):
                zero_copy(slot).start()
                return c

            def zdrain(slot, c):
                zero_copy(slot).wait()
                return c

            lax.fori_loop(lo, hi, zissue, 0)
            lax.fori_loop(lo, hi, zdrain, 0)

        zrows = zero_ref.shape[0]
        zb = zrows // TOK

        def block_copy(b):
            return pltpu.make_async_copy(zero_ref, xs_ref.at[pl.ds(pl.multiple_of(b * zrows, zrows), zrows)], zsem)

        def bissue(b, c):
            block_copy(b).start()
            return c

        def bdrain(b, c):
            block_copy(b).wait()
            return c

        first, last = fill_ref[2, 0] // zb, xs_ref.shape[0] // zrows
        lax.fori_loop(first, last, bissue, 0)
        lax.fori_loop(first, last, bdrain, 0)


def moe_dispatch(xp, slots, fill, rows):
    s = xp.shape[0] // TOK
    tm = _rows("moe_dispatch", s)
    zb = min(ZERO_BLOCK, s)
    return pl.pallas_call(
        _dispatch_kernel,
        out_shape=jax.ShapeDtypeStruct((rows * TOK, LANE), xp.dtype),
        grid_spec=pltpu.PrefetchScalarGridSpec(
            num_scalar_prefetch=1,
            grid=(s // tm,),
            in_specs=[pl.BlockSpec((2, tm), lambda i, *_: (0, i), memory_space=pltpu.SMEM),
                      pl.BlockSpec((tm * TOK, LANE), lambda i, *_: (i, 0))],
            out_specs=pl.BlockSpec(memory_space=pl.ANY),
            scratch_shapes=[pltpu.VMEM((zb * TOK, LANE), xp.dtype), pltpu.SemaphoreType.DMA,
                            pltpu.SemaphoreType.DMA],
        ),
        compiler_params=_params(("arbitrary",), "moe_dispatch"),
        name="moe_dispatch",
    )(fill, slots, xp)


def _ple(x, p, wup_ref, wgate_ref, bgate_ref):
    up = _dot(p.astype(BF16), wup_ref[...])
    zg = _dot(x.astype(BF16), wgate_ref[...]) + bgate_ref[...]
    return up * (1.0 / (1.0 + jnp.exp(-zg)))


def _ple_ln_dense_kernel(x_ref, f_ref, p_ref, wupf_ref, wgatef_ref, bgate_ref, g_ref, beta_ref, o_ref,
                         wup_ref, wgate_ref):
    _cast_once(wupf_ref, wup_ref)
    _cast_once(wgatef_ref, wgate_ref)
    nr = x_ref.shape[0] // ROW_SPLIT
    for r0 in range(0, x_ref.shape[0], nr):
        rows = slice(r0, r0 + nr)
        x = x_ref[rows, :]
        f = jnp.concatenate(_load_token_tiles(f_ref, (r0, nr)), 1)
        ple = _ple(x, p_ref[rows, :], wup_ref, wgate_ref, bgate_ref)
        o_ref[rows, :] = _layer_norm(DEEPNORM_ALPHA * x + f + ple, g_ref[...], beta_ref[...])


def _ple_ln_moe_kernel(slot_ref, x_ref, wt_ref, p_ref, wupf_ref, wgatef_ref, bgate_ref, g_ref, beta_ref, ys_ref,
                       o_ref, wup_ref, wgate_ref, y1_ref, y2_ref, sem):
    _cast_once(wupf_ref, wup_ref)
    _cast_once(wgatef_ref, wgate_ref)
    tm = x_ref.shape[0]
    bufs = (y1_ref, y2_ref)

    def issue(b, c):
        for u in range(DMA_UNROLL):
            t = b * DMA_UNROLL + u
            for k in range(2):
                pltpu.make_async_copy(ys_ref.at[_tile_rows(slot_ref[k, t])], bufs[k].at[_tile_rows(t)],
                                      sem).start(priority=k)
        return c

    lax.fori_loop(0, tm // DMA_UNROLL, issue, 0)
    nr = tm // ROW_SPLIT
    ples = [_ple(x_ref[r0:r0 + nr, :], p_ref[r0:r0 + nr, :], wup_ref, wgate_ref, bgate_ref)
            for r0 in range(0, tm, nr)]
    for k in range(2):
        pltpu.make_async_copy(ys_ref.at[pl.ds(0, tm * TOK)], bufs[k], sem).wait()
    for i, r0 in enumerate(range(0, tm, nr)):
        rows = slice(r0, r0 + nr)
        wt = wt_ref[rows, :]
        f = (wt[:, 0:1] * jnp.concatenate(_load_token_tiles(y1_ref, (r0, nr)), 1)
             + wt[:, 1:2] * jnp.concatenate(_load_token_tiles(y2_ref, (r0, nr)), 1))
        o_ref[rows, :] = _layer_norm(DEEPNORM_ALPHA * x_ref[rows, :] + f + ples[i], g_ref[...], beta_ref[...])


def ple_ln(x, p, wup, wgate, bgate, g, beta, layer, f=None, moe=None):
    s = x.shape[0]
    name = "ple_ln_dense" if moe is None else "ple_ln_moe"
    tm = _rows(name, s)
    row = lambda arr: pl.BlockSpec((tm, arr.shape[1]), lambda i, *_: (i, 0))
    full = lambda arr: _layer_block(arr, layer)
    p_spec = pl.BlockSpec((None, None, tm, p.shape[-1]), lambda i, *_: (layer, 0, i, 0))
    tail = [p_spec, full(wup), full(wgate), full(bgate), full(g), full(beta)]
    out_spec = pl.BlockSpec((tm, D_MODEL), lambda i, *_: (i, 0))
    out_shape = jax.ShapeDtypeStruct((s, D_MODEL), F32)
    casts = [pltpu.VMEM(wup.shape[1:], BF16), pltpu.VMEM(wgate.shape[1:], BF16)]
    if moe is None:
        return pl.pallas_call(
            _ple_ln_dense_kernel, out_shape=out_shape, grid=(s // tm,),
            in_specs=[row(x), pl.BlockSpec((tm * TOK, LANE), lambda i: (i, 0))] + tail, out_specs=out_spec,
            scratch_shapes=casts,
            compiler_params=_params(("arbitrary",), name), name=name,
        )(x, f, p, wup, wgate, bgate, g, beta)
    ys, slots, wts_t = moe
    return pl.pallas_call(
        _ple_ln_moe_kernel,
        out_shape=out_shape,
        grid_spec=pltpu.PrefetchScalarGridSpec(
            num_scalar_prefetch=0,
            grid=(s // tm,),
            in_specs=[pl.BlockSpec((2, tm), lambda i: (0, i), memory_space=pltpu.SMEM), row(x), row(wts_t)] + tail
            + [pl.BlockSpec(memory_space=pl.ANY)],
            out_specs=out_spec,
            scratch_shapes=casts + [pltpu.VMEM((tm * TOK, LANE), jnp.uint32),
                                    pltpu.VMEM((tm * TOK, LANE), jnp.uint32), pltpu.SemaphoreType.DMA],
        ),
        compiler_params=_params(("arbitrary",), name),
        name=name,
    )(slots, x, wts_t, p, wup, wgate, bgate, g, beta, ys)


IN_SIZES = (MLA_Q_RANK, MLA_KV_RANK, MLA_ROPE, GLA_HEADS * GLA_DK, GLA_HEADS * GLA_DK, GLA_HEADS * GLA_DV,
            GLA_GATE_RANK, GLA_HEADS * GLA_DV, SWA_HEADS * SWA_HD, SWA_KV_HEADS * SWA_HD, SWA_KV_HEADS * SWA_HD)
IN_COLS = sum(IN_SIZES)
IN_DEST = (H_QA, H_KVA, H_KR, H_GQ, H_GK, H_GV, H_GLR, H_GR, H_SQ, H_SK, H_SV)


def _pack_w_in_kernel(w_ref, o_ref):
    src = 0
    for width, dst in zip(IN_SIZES, IN_DEST):
        o_ref[dst:dst + width, :] = w_ref[src:src + width, :].astype(BF16)
        pad = -width % LANE
        if pad:
            o_ref[dst + width:dst + width + pad, :] = jnp.zeros((pad, o_ref.shape[1]), BF16)
        src += width


def _pack_w_in(w_t):
    depth, _, d = w_t.shape
    tc = _rows("pack_w_in", d)
    return pl.pallas_call(
        _pack_w_in_kernel,
        out_shape=jax.ShapeDtypeStruct((depth, H_COLS, d), BF16),
        grid=(depth, d // tc),
        in_specs=[pl.BlockSpec((None, IN_COLS, tc), lambda l, i: (l, 0, i))],
        out_specs=pl.BlockSpec((None, H_COLS, tc), lambda l, i: (l, 0, i)),
        compiler_params=_params(("parallel", "parallel"), "pack_w_in"),
        name="pack_w_in",
    )(w_t)


def _pack_w_q_b(w):
    w = w.reshape(MLA_Q_RANK, MLA_HEADS, MLA_NOPE + MLA_ROPE)
    w = jnp.pad(w, ((0, 0), (0, 0), (0, MLA_QK - MLA_NOPE - MLA_ROPE)))
    return w.reshape(MLA_Q_RANK, MLA_HEADS * MLA_QK).astype(BF16)


def _token_mixer_ln(x, rope, layer, w_in_p, q_gain, w_q_b, kv_gain, w_kv_b, gla_w, gla_b, gla_gain, sinks,
                    rel_bias, w_out_b, ln_g, ln_b):
    h = proj_in(x, w_in_p, layer)
    q, k, v = mla_prep(h, *rope, q_gain.reshape(1, -1), kv_gain.reshape(1, -1), _pack_w_q_b(w_q_b),
                       w_kv_b.astype(BF16))
    a = mla_flash(q, k, v)
    gla_w_p = jnp.pad(gla_w, ((0, LANE - GLA_GATE_RANK), (0, 0))).astype(BF16)
    b = gla(h, gla_w_p, gla_b.reshape(1, -1), gla_gain.reshape(1, -1))
    c = swa(h, sinks, rel_bias)
    return out_ln(a, b, c, x, w_out_b, ln_g, ln_b, layer)


def _moe_plan(counts, tm, n_tiles):
    tiles = (counts + tm - 1) // tm
    ends = jnp.cumsum(tiles)
    base = (ends - tiles) * tm
    n_active = ends[-1:].astype(jnp.int32)
    tile_expert = jnp.searchsorted(ends, jnp.arange(n_tiles, dtype=jnp.int32), side="right")
    tile_expert = jnp.minimum(tile_expert, N_EXPERTS - 1).astype(jnp.int32)
    tile_start = jnp.arange(n_tiles, dtype=jnp.int32) * tm
    tile_valid = jnp.clip((base + counts)[tile_expert] - tile_start, 0, tm).astype(jnp.int32)
    used_rows = jnp.broadcast_to(ends[-1] * tm, counts.shape)
    fill = jnp.stack([base + counts, base + tiles * tm, used_rows]).astype(jnp.int32)
    return base.astype(jnp.int32), fill, tile_expert, tile_valid, n_active


def kernel(x, p, positions, w_in, mla_q_a_gain, mla_w_q_b, mla_kv_a_gain, mla_w_kv_b, gla_w_gate, gla_b_gate,
           gla_norm_gain, swa_sinks, rel_bias, w_out, ln1_g, ln1_b, ffn_w_gate, ffn_w_up, ffn_w_down,
           moe_router, moe_w_gate, moe_w_up, moe_w_down, ple_w_up, ple_w_gate, ple_b_gate, ln2_g, ln2_b):
    batch, s, _ = x.shape
    assert batch == 1
    xcur = x.reshape(s, D_MODEL)
    rope = rope_tables(positions)
    tm = _rows("grouped_swiglu", s)
    row_stack = lambda v: v.reshape(DEPTH, 1, -1)
    w_in_p = _pack_w_in(jnp.swapaxes(w_in, 1, 2))
    tail = (p, ple_w_up, ple_w_gate, row_stack(ple_b_gate), row_stack(ln2_g), row_stack(ln2_b))
    for i in range(DEPTH):
        x1, x1p = _token_mixer_ln(xcur, rope, i, w_in_p, mla_q_a_gain[i], mla_w_q_b[i], mla_kv_a_gain[i],
                                  mla_w_kv_b[i], gla_w_gate[i], gla_b_gate[i], gla_norm_gain[i], swa_sinks[i],
                                  rel_bias, w_out, row_stack(ln1_g), row_stack(ln1_b))
        j = i // 2
        if i % 2 == 0:
            n_tiles = s // tm
            f = grouped_swiglu(x1p, ffn_w_gate[j][None], ffn_w_up[j][None], ffn_w_down[j][None],
                               jnp.zeros((n_tiles,), jnp.int32), jnp.full((n_tiles,), tm, jnp.int32),
                               jnp.full((1,), n_tiles, jnp.int32), tm, FFN_COLS)
            xcur = ple_ln(x1, *tail, i, f=f)
        else:
            n_tiles = (2 * s + N_EXPERTS * (tm - 1)) // tm
            e_idx, rank, wts, counts = route_tokens(x1, moe_router[j].T)
            base, fill, tile_expert, tile_valid, n_active = _moe_plan(counts[:, 0], tm, n_tiles)
            slots = token_slots(e_idx, rank, base)
            xs = moe_dispatch(x1p, slots, fill, n_tiles * tm)
            ys = grouped_swiglu(xs, moe_w_gate[j], moe_w_up[j], moe_w_down[j], tile_expert, tile_valid, n_active,
                                tm, FFN_COLS)
            xcur = ple_ln(x1, *tail, i, moe=(ys, slots, wts.T))
    return xcur.reshape(batch, s, D_MODEL)
```

```python
import functools
import math
from typing import NamedTuple

import numpy as np
import jax
import jax.numpy as jnp
from jax import lax
from jax.experimental import pallas as pl
from jax.experimental.pallas import tpu as pltpu

F32 = jnp.float32
BF16 = jnp.bfloat16

D_MODEL = 2048
DEPTH = 2
MLA_HEADS = 8
MLA_Q_RANK = 512
MLA_KV_RANK = 256
MLA_NOPE = 128
MLA_ROPE = 64
MLA_V = 128
ROPE_THETA = 10000.0
GLA_HEADS = 4
GLA_DK = 64
GLA_DV = 128
GLA_GATE_RANK = 16
GLA_TAU = 16.0
SWA_HEADS = 8
SWA_KV_HEADS = 2
SWA_HD = 64
SWA_WINDOW = 128
SWA_BLOCK = 128
REL_BUCKETS = 32
REL_MAX_DIST = 128
D_FF = 5632
N_EXPERTS = 8
PLE_DIM = 256
LN_EPS = 1e-5
RMS_EPS = 1e-6
DEEPNORM_ALPHA = (2 * DEPTH) ** 0.25

LANE = 128
LOG2E = math.log2(math.e)
NEG_BIG = -1e30

H_QA, H_GV, H_GR, H_SQ = 0, 512, 1024, 1536
H_KVA, H_GQ, H_GK = 2048, 2304, 2560
H_KR, H_GLR, H_SK, H_SV = 2816, 2944, 3072, 3200
H_COLS = 3328

MLA_QK = 2 * LANE

GLA_L = 128
GLA_SUB = 32
GLA_NSUB = GLA_L // GLA_SUB
GLA_CHUNKS_PER_STEP = 4

V7X_VMEM_MIB = 64


class _Plan(NamedTuple):
    rows: int
    vmem_mib: int


PLANS = {
    "rope_tables": _Plan(1024, 32),
    "pack_w_in": _Plan(512, 40),
    "proj_in": _Plan(512, 56),
    "mla_prep": _Plan(512, 48),
    "mla_flash": _Plan(512, 56),
    "gla": _Plan(GLA_CHUNKS_PER_STEP * GLA_L, 32),
    "swa": _Plan(SWA_BLOCK, 32),
    "out_ln": _Plan(512, 60),
    "grouped_swiglu": _Plan(1024, 60),
    "moe_router": _Plan(512, 32),
    "token_slots": _Plan(0, 32),
    "moe_dispatch": _Plan(512, 32),
    "ple_ln_dense": _Plan(512, 60),
    "ple_ln_moe": _Plan(256, 60),
}
assert all(plan.vmem_mib < V7X_VMEM_MIB for plan in PLANS.values())


def _rows(name, s):
    return min(PLANS[name].rows, s)


def _params(sem, name):
    return pltpu.CompilerParams(dimension_semantics=sem, vmem_limit_bytes=PLANS[name].vmem_mib * 2 ** 20)


def _dot(a, b):
    return jnp.dot(a, b, preferred_element_type=F32)


def _dot_nt(a, b):
    return lax.dot_general(a, b, (((1,), (1,)), ((), ())), preferred_element_type=F32)


def _dot_tn(a, b):
    return lax.dot_general(a, b, (((0,), (0,)), ((), ())), preferred_element_type=F32)


def _proj_in_kernel(x_ref, w_ref, o_ref, xb_ref):
    @pl.when(pl.program_id(1) == 0)
    def _():
        xb_ref[...] = x_ref[...].astype(BF16)

    o_ref[...] = _dot_nt(xb_ref[...], w_ref[...]).astype(o_ref.dtype)


def _layer_block(arr, layer):
    zeros = (0,) * (arr.ndim - 1)
    return pl.BlockSpec((None,) + arr.shape[1:], lambda i, *_: (layer,) + zeros, pipeline_mode=pl.Buffered(1))


def proj_in(x, w_p, layer):
    s = x.shape[0]
    tm = _rows("proj_in", s)
    tn = H_COLS
    return pl.pallas_call(
        _proj_in_kernel,
        out_shape=jax.ShapeDtypeStruct((s, H_COLS), BF16),
        grid=(s // tm, H_COLS // tn),
        in_specs=[pl.BlockSpec((tm, D_MODEL), lambda i, j: (i, 0)),
                  pl.BlockSpec((None, tn, D_MODEL), lambda i, j: (layer, j, 0), pipeline_mode=pl.Buffered(1))],
        out_specs=pl.BlockSpec((tm, tn), lambda i, j: (i, j)),
        scratch_shapes=[pltpu.VMEM((tm, D_MODEL), BF16)],
        compiler_params=_params(("parallel", "arbitrary"), "proj_in"),
        name="proj_in",
    )(x, w_p)


def _rope_table_kernel(pos_ref, inv_ref, cos_ref, sa_ref, sb_ref):
    ang = pos_ref[...].astype(F32) * inv_ref[...]
    lane = lax.broadcasted_iota(jnp.int32, ang.shape, 1)
    half = MLA_ROPE // 2
    c, s = jnp.cos(ang), jnp.sin(ang)
    cos_ref[...] = c
    sa_ref[...] = jnp.where((lane >= half) & (lane < 2 * half), s, 0.0)
    sb_ref[...] = jnp.where(lane < half, -s, 0.0)


def rope_tables(positions):
    s = positions.shape[-1]
    half = MLA_ROPE // 2
    inv = ROPE_THETA ** (-jnp.arange(half, dtype=F32) / half)
    inv = jnp.concatenate([inv, inv, jnp.zeros((LANE - 2 * half,), F32)]).reshape(1, LANE)
    tm = _rows("rope_tables", s)
    spec = pl.BlockSpec((tm, LANE), lambda i: (i, 0))
    return pl.pallas_call(
        _rope_table_kernel,
        out_shape=[jax.ShapeDtypeStruct((s, LANE), F32)] * 3,
        grid=(s // tm,),
        in_specs=[pl.BlockSpec((tm, 1), lambda i: (i, 0)), pl.BlockSpec((1, LANE), lambda i: (0, 0))],
        out_specs=[spec, spec, spec],
        compiler_params=_params(("parallel",), "rope_tables"),
        name="rope_tables",
    )(positions.reshape(s, 1), inv)


def _rope(x, cos, sa, sb):
    return x * cos + pltpu.roll(x, MLA_ROPE // 2, 1) * sa + pltpu.roll(x, LANE - MLA_ROPE // 2, 1) * sb


def _mla_prep_kernel(qa_ref, kva_ref, kr_ref, cos_ref, sa_ref, sb_ref, gq_ref, gkv_ref, wq_ref, wkv_ref,
                     q_out, k_out, v_out):
    cos, sa, sb = cos_ref[...], sa_ref[...], sb_ref[...]
    qscale = (MLA_NOPE + MLA_ROPE) ** -0.5 * LOG2E

    qa = qa_ref[...].astype(F32)
    qn = qa * lax.rsqrt(jnp.mean(qa * qa, -1, keepdims=True) + RMS_EPS) * gq_ref[...]
    q = _dot(qn.astype(BF16), wq_ref[...])
    for h in range(MLA_HEADS):
        c0 = h * MLA_QK
        q_out[h, :, 0:LANE] = (q[:, c0:c0 + LANE] * qscale).astype(BF16)
        pe = _rope(q[:, c0 + LANE:c0 + 2 * LANE], cos, sa, sb)
        q_out[h, :, LANE:2 * LANE] = (pe * qscale).astype(BF16)

    kva = kva_ref[...].astype(F32)
    kvn = kva * lax.rsqrt(jnp.mean(kva * kva, -1, keepdims=True) + RMS_EPS) * gkv_ref[...]
    kv = _dot(kvn.astype(BF16), wkv_ref[...])
    kpe = _rope(kr_ref[...].astype(F32), cos, sa, sb).astype(BF16)
    for h in range(MLA_HEADS):
        c0 = h * (MLA_NOPE + MLA_V)
        k_out[h, :, 0:LANE] = kv[:, c0:c0 + MLA_NOPE].astype(BF16)
        k_out[h, :, LANE:2 * LANE] = kpe
        v_out[h, :, 0:MLA_V] = kv[:, c0 + MLA_NOPE:c0 + MLA_NOPE + MLA_V].astype(BF16)
        v_out[h, :, MLA_V:2 * MLA_V] = jnp.ones((kv.shape[0], MLA_V), BF16)


def mla_prep(h, cos, sa, sb, gq, gkv, wq_p, wkv):
    s = h.shape[0]
    tm = _rows("mla_prep", s)
    row = lambda width, col: pl.BlockSpec((tm, width), lambda i: (i, col // width))
    full = lambda a: pl.BlockSpec(a.shape, lambda i: (0,) * a.ndim)
    return pl.pallas_call(
        _mla_prep_kernel,
        out_shape=[jax.ShapeDtypeStruct((MLA_HEADS, s, MLA_QK), BF16),
                   jax.ShapeDtypeStruct((MLA_HEADS, s, MLA_QK), BF16),
                   jax.ShapeDtypeStruct((MLA_HEADS, s, 2 * MLA_V), BF16)],
        grid=(s // tm,),
        in_specs=[row(MLA_Q_RANK, H_QA), row(MLA_KV_RANK, H_KVA), row(LANE, H_KR),
                  row(LANE, 0), row(LANE, 0), row(LANE, 0),
                  full(gq), full(gkv), full(wq_p), full(wkv)],
        out_specs=[pl.BlockSpec((MLA_HEADS, tm, MLA_QK), lambda i: (0, i, 0)),
                   pl.BlockSpec((MLA_HEADS, tm, MLA_QK), lambda i: (0, i, 0)),
                   pl.BlockSpec((MLA_HEADS, tm, 2 * MLA_V), lambda i: (0, i, 0))],
        compiler_params=_params(("parallel",), "mla_prep"),
        name="mla_prep",
    )(h, h, h, cos, sa, sb, gq, gkv, wq_p, wkv)


def _mla_flash_kernel(q_ref, k_ref, v_ref, o_ref, sa_ref, sb_ref, mxa_ref, mxb_ref, m_ref, acc_ref, *, t, nh):
    qi = pl.program_id(1)
    heads = range(nh)
    bufs = ((sa_ref, mxa_ref), (sb_ref, mxb_ref))

    def produce(b, parity, masked):
        s_ref, mx_ref = bufs[parity]
        start = pl.multiple_of(b * t, t)
        for h in heads:
            s = _dot_nt(q_ref[h], k_ref[h, pl.ds(start, t), :])
            if masked:
                rows = lax.broadcasted_iota(jnp.int32, s.shape, 0) + qi * t
                cols = lax.broadcasted_iota(jnp.int32, s.shape, 1) + b * t
                s = jnp.where(cols <= rows, s, NEG_BIG)
            s_ref[h] = s
            mx_ref[h] = jnp.broadcast_to(jnp.max(s, -1, keepdims=True), (t, LANE))

    def absorb(b, parity):
        s_ref, mx_ref = bufs[parity]
        start = pl.multiple_of(b * t, t)
        for h in heads:
            m_new = jnp.maximum(m_ref[h], mx_ref[h])
            alpha = jnp.exp2(m_ref[h] - m_new)
            p = jnp.exp2(s_ref[h] - jnp.concatenate([m_new] * (t // LANE), 1))
            acc_ref[h] = (jnp.concatenate([alpha, alpha], 1) * acc_ref[h]
                          + _dot(p.astype(BF16), v_ref[h, pl.ds(start, t), :]))
            m_ref[h] = m_new

    m_ref[...] = jnp.full(m_ref.shape, NEG_BIG, F32)
    acc_ref[...] = jnp.zeros_like(acc_ref)
    produce(0, 0, True)

    def pair(i, c):
        b = 2 * i
        produce(b + 1, 1, False)
        absorb(b, 0)
        produce(b + 2, 0, False)
        absorb(b + 1, 1)
        return c

    n_pairs = jnp.maximum(qi - 1, 0) // 2
    lax.fori_loop(0, n_pairs, pair, 0)
    done = 2 * n_pairs

    @pl.when(qi == 0)
    def _():
        absorb(0, 0)

    @pl.when((qi > 0) & (qi - done == 1))
    def _():
        produce(qi, 1, True)
        absorb(qi - 1, 0)
        absorb(qi, 1)

    @pl.when((qi > 0) & (qi - done == 2))
    def _():
        produce(qi - 1, 1, False)
        absorb(qi - 2, 0)
        produce(qi, 0, True)
        absorb(qi - 1, 1)
        absorb(qi, 0)

    for h in heads:
        acc = acc_ref[h]
        o_ref[:, h * MLA_V:(h + 1) * MLA_V] = (acc[:, :MLA_V] / acc[:, MLA_V:]).astype(o_ref.dtype)


def mla_flash(q, k, v1):
    _, s, _ = q.shape
    t = _rows("mla_flash", s)
    nh = 2
    return pl.pallas_call(
        functools.partial(_mla_flash_kernel, t=t, nh=nh),
        out_shape=jax.ShapeDtypeStruct((s, MLA_HEADS * MLA_V), BF16),
        grid=(MLA_HEADS // nh, s // t),
        in_specs=[pl.BlockSpec((nh, t, MLA_QK), lambda h, i: (h, i, 0)),
                  pl.BlockSpec((nh, s, MLA_QK), lambda h, i: (h, 0, 0)),
                  pl.BlockSpec((nh, s, 2 * MLA_V), lambda h, i: (h, 0, 0))],
        out_specs=pl.BlockSpec((t, nh * MLA_V), lambda h, i: (i, h)),
        scratch_shapes=[pltpu.VMEM((nh, t, t), F32), pltpu.VMEM((nh, t, t), F32),
                        pltpu.VMEM((nh, t, LANE), F32), pltpu.VMEM((nh, t, LANE), F32),
                        pltpu.VMEM((nh, t, LANE), F32), pltpu.VMEM((nh, t, 2 * MLA_V), F32)],
        compiler_params=_params(("parallel", "arbitrary"), "mla_flash"),
        name="mla_flash",
    )(q, k, v1)


def _gla_masks():
    i = np.arange(GLA_L)[:, None]
    j = np.arange(GLA_L)[None, :]
    same = (i // GLA_SUB) == (j // GLA_SUB)
    mats = [j <= i, same & (j <= i), same & (j > i), j > i]
    for sub in range(GLA_NSUB - 1):
        mats.append((j >= (sub + 1) * GLA_SUB) & (j <= i))
    return np.concatenate(mats, 0).astype(np.float32)


def _gla_kernel(q_ref, k_ref, v_ref, lr_ref, r_ref, wg_ref, bg_ref, gain_ref, mask_ref, o_ref, state_ref):
    @pl.when(pl.program_id(0) == 0)
    def _():
        state_ref[...] = jnp.zeros_like(state_ref)

    L, hk = GLA_L, GLA_HEADS * GLA_DK
    chunks = [slice(c * L, (c + 1) * L) for c in range(q_ref.shape[0] // L)]
    heads = [(slice(h * GLA_DK, (h + 1) * GLA_DK), slice(h * GLA_DV, (h + 1) * GLA_DV)) for h in range(GLA_HEADS)]

    z = _dot(lr_ref[...], wg_ref[...]) + bg_ref[...]
    g = (jnp.minimum(z, 0.0) - jnp.log(1.0 + jnp.exp(-jnp.abs(z)))) * (1.0 / GLA_TAU)
    g_hi = g.astype(BF16)
    g_lo = (g - g_hi.astype(F32)).astype(BF16)

    cums = [_dot(mask_ref[...], jnp.concatenate([g_hi[rs], g_lo[rs]], 0)) for rs in chunks]

    sub_of_row = lax.broadcasted_iota(jnp.int32, (L, hk), 0) // GLA_SUB
    prep = []
    for rs, cum in zip(chunks, cums):
        b_all, b_loc, sfx_loc, sfx_all = cum[0:L], cum[L:2 * L], cum[2 * L:3 * L], cum[3 * L:4 * L]
        q = q_ref[rs, :].astype(F32) * (GLA_DK ** -0.5)
        k = k_ref[rs, :].astype(F32)
        k_end = k * jnp.exp(sfx_loc)
        prep.append(dict(
            decay=jnp.exp(b_all[L - 1:L, :]),
            q_inter=(q * jnp.exp(b_all)).astype(BF16),
            q_diag=(q * jnp.exp(b_loc)).astype(BF16),
            k_diag=(k * jnp.exp(-b_loc)).astype(BF16),
            k_state=(k * jnp.exp(sfx_all)).astype(BF16),
            q_off=[(q * jnp.exp(cum[(4 + sub) * L:(5 + sub) * L])).astype(BF16) for sub in range(GLA_NSUB - 1)],
            k_off=[jnp.where(sub_of_row == sub, k_end, 0.0).astype(BF16) for sub in range(GLA_NSUB - 1)]))

    row = lax.broadcasted_iota(jnp.int32, (L, L), 0)
    col = lax.broadcasted_iota(jnp.int32, (L, L), 1)
    diag_ok = ((row // GLA_SUB) == (col // GLA_SUB)) & (col <= row)
    off_ok = (row // GLA_SUB) > (col // GLA_SUB)
    local = []
    for rs, pr in zip(chunks, prep):
        per_head = []
        for ks, vs in heads:
            v_h = v_ref[rs, vs]
            a = jnp.where(diag_ok, _dot_nt(pr["q_diag"][:, ks], pr["k_diag"][:, ks]), 0.0)
            qo = jnp.concatenate([t[:, ks] for t in pr["q_off"]], 1)
            ko = jnp.concatenate([t[:, ks] for t in pr["k_off"]], 1)
            a = a + jnp.where(off_ok, _dot_nt(qo, ko), 0.0)
            per_head.append((_dot(a.astype(BF16), v_h), _dot_tn(v_h, pr["k_state"][:, ks])))
        local.append(per_head)

    states = [state_ref[h] for h in range(GLA_HEADS)]
    for rs, pr, per_head in zip(chunks, prep, local):
        r = r_ref[rs, :].astype(F32)
        gate = r * (1.0 / (1.0 + jnp.exp(-r)))
        for h, (ks, vs) in enumerate(heads):
            o_intra, vk = per_head[h]
            o = _dot_nt(pr["q_inter"][:, ks], states[h].astype(BF16)) + o_intra
            states[h] = states[h] * pr["decay"][:, ks] + vk
            o = o * lax.rsqrt(jnp.mean(o * o, -1, keepdims=True) + RMS_EPS) * gain_ref[:, vs]
            o_ref[rs, vs] = (o * gate[:, vs]).astype(o_ref.dtype)
    for h in range(GLA_HEADS):
        state_ref[h] = states[h]


def gla(h, wg_p, bg, gain):
    s = h.shape[0]
    rows = _rows("gla", s)
    masks = jnp.asarray(np.tile(_gla_masks(), (1, 2)), BF16)
    row = lambda width, col: pl.BlockSpec((rows, width), lambda i: (i, col // width))
    full = lambda a: pl.BlockSpec(a.shape, lambda i: (0,) * a.ndim)
    hk, hv = GLA_HEADS * GLA_DK, GLA_HEADS * GLA_DV
    return pl.pallas_call(
        _gla_kernel,
        out_shape=jax.ShapeDtypeStruct((s, hv), BF16),
        grid=(s // rows,),
        in_specs=[row(hk, H_GQ), row(hk, H_GK), row(hv, H_GV), row(LANE, H_GLR), row(hv, H_GR),
                  full(wg_p), full(bg), full(gain), full(masks)],
        out_specs=pl.BlockSpec((rows, hv), lambda i: (i, 0)),
        scratch_shapes=[pltpu.VMEM((GLA_HEADS, GLA_DV, GLA_DK), F32)],
        compiler_params=_params(("arbitrary",), "gla"),
        name="gla",
    )(h, h, h, h, h, wg_p, bg, gain, masks)


def _t5_bucket_table():
    L = SWA_BLOCK
    dist = np.arange(L)[:, None] + L - np.arange(2 * L)[None, :]
    d = np.clip(dist, 0, None)
    max_exact = REL_BUCKETS // 2
    df = np.maximum(d, 1).astype(np.float32)
    large = max_exact + (np.log(df / np.float32(max_exact)) / np.float32(math.log(REL_MAX_DIST / max_exact))
                         * np.float32(REL_BUCKETS - max_exact)).astype(np.int32)
    large = np.minimum(large, REL_BUCKETS - 1)
    bucket = np.where(d < max_exact, d, large)
    in_window = (dist >= 0) & (dist < SWA_WINDOW)
    return np.where(in_window, bucket, -1).astype(np.int32)


def _swa_kernel(relb_ref, sink_ref, q_ref, kp_ref, kc_ref, vp_ref, vc_ref, bucket_ref, o_ref, bias_ref):
    i = pl.program_id(0)
    L = SWA_BLOCK

    @pl.when(i == 0)
    def _():
        bucket = bucket_ref[...]
        for h in range(SWA_HEADS):
            acc = jnp.full(bucket.shape, NEG_BIG, F32)
            for b in range(REL_BUCKETS):
                acc = jnp.where(bucket == b, relb_ref[b, h], acc)
            bias_ref[h] = acc

    kcat = jnp.concatenate([kp_ref[...], kc_ref[...]], 0)
    vcat = jnp.concatenate([vp_ref[...], vc_ref[...]], 0)
    scale = SWA_HD ** -0.5
    assert math.log2(scale).is_integer()
    q = q_ref[...] * scale
    col = lax.broadcasted_iota(jnp.int32, (L, 2 * L), 1)
    real_key = (col >= L) | (i > 0)
    g = SWA_HEADS // SWA_KV_HEADS
    ones = jnp.ones((2 * L, SWA_HD), BF16)
    v1 = [jnp.concatenate([vcat[:, kv * SWA_HD:(kv + 1) * SWA_HD], ones], 1) for kv in range(SWA_KV_HEADS)]
    for h in range(SWA_HEADS):
        kv = h // g
        hs = slice(h * SWA_HD, (h + 1) * SWA_HD)
        kvs = slice(kv * SWA_HD, (kv + 1) * SWA_HD)
        s = _dot_nt(q[:, hs], kcat[:, kvs]) + bias_ref[h]
        s = jnp.where(real_key, s, NEG_BIG)
        sink = sink_ref[h]
        m = jnp.maximum(jnp.max(s, -1, keepdims=True), sink)
        o2 = _dot(jnp.exp(s - m).astype(BF16), v1[kv])
        denom = o2[:, SWA_HD:] + jnp.exp(sink - m)
        o_ref[:, hs] = (o2[:, :SWA_HD] / denom).astype(o_ref.dtype)


def swa(h, sinks, rel_bias):
    s = h.shape[0]
    L = SWA_BLOCK
    bucket = jnp.asarray(_t5_bucket_table())
    kvw = SWA_KV_HEADS * SWA_HD
    hw = SWA_HEADS * SWA_HD
    cur = lambda width, col: pl.BlockSpec((L, width), lambda i, *_: (i, col // width))
    prev = lambda width, col: pl.BlockSpec((L, width), lambda i, *_: (jnp.maximum(i - 1, 0), col // width))
    return pl.pallas_call(
        _swa_kernel,
        out_shape=jax.ShapeDtypeStruct((s, hw), BF16),
        grid_spec=pltpu.PrefetchScalarGridSpec(
            num_scalar_prefetch=2,
            grid=(s // L,),
            in_specs=[cur(hw, H_SQ), prev(kvw, H_SK), cur(kvw, H_SK), prev(kvw, H_SV), cur(kvw, H_SV),
                      pl.BlockSpec(bucket.shape, lambda i, *_: (0, 0))],
            out_specs=pl.BlockSpec((L, hw), lambda i, *_: (i, 0)),
            scratch_shapes=[pltpu.VMEM((SWA_HEADS, L, 2 * L), F32)],
        ),
        compiler_params=_params(("arbitrary",), "swa"),
        name="swa",
    )(rel_bias, sinks, h, h, h, h, h, bucket)


def _layer_norm(y, g, b):
    mu = jnp.mean(y, -1, keepdims=True)
    yc = y - mu
    var = jnp.mean(yc * yc, -1, keepdims=True)
    return yc * lax.rsqrt(var + LN_EPS) * g + b


HALF = D_MODEL // 2
HIGH16 = 0xFFFF0000


SUBLANE = 8
TOK = HALF // LANE
assert TOK == SUBLANE


def _store_token_tiles(ref, x):
    tm = x.shape[0]
    lo = lax.bitcast_convert_type(x[:, :HALF].astype(BF16).astype(F32), jnp.uint32) >> 16
    hi = lax.bitcast_convert_type(x[:, HALF:].astype(BF16).astype(F32), jnp.uint32) & jnp.uint32(HIGH16)
    packed = lo | hi
    for s in range(TOK):
        ref[pl.ds(s, tm, stride=TOK), :] = packed[:, s * LANE:(s + 1) * LANE]


def _load_token_tiles(ref, rows=None):
    first, tm = rows if rows is not None else (0, ref.shape[0] // TOK)
    lo, hi = [], []
    for s in range(TOK):
        w = ref[pl.ds(first * TOK + s, tm, stride=TOK), :]
        lo.append(lax.bitcast_convert_type(w << 16, F32))
        hi.append(lax.bitcast_convert_type(w & jnp.uint32(HIGH16), F32))
    return jnp.concatenate(lo, 1), jnp.concatenate(hi, 1)


ROW_SPLIT = 2


CAST_ROWS = 256


def _cast_once(w_ref, wb_ref):
    @pl.when(pl.program_id(0) == 0)
    def _():
        for r0 in range(0, w_ref.shape[0], CAST_ROWS):
            wb_ref[r0:r0 + CAST_ROWS, :] = w_ref[r0:r0 + CAST_ROWS, :].astype(BF16)


def _out_ln_kernel(a_ref, b_ref, c_ref, x_ref, wf_ref, g_ref, beta_ref, o_ref, op_ref, w_ref):
    _cast_once(wf_ref, w_ref)
    na, nb = a_ref.shape[1], b_ref.shape[1]
    nr = x_ref.shape[0] // ROW_SPLIT
    for r0 in range(0, x_ref.shape[0], nr):
        rows = slice(r0, r0 + nr)
        m = _dot(a_ref[rows, :], w_ref[0:na, :])
        m = m + _dot(b_ref[rows, :], w_ref[na:na + nb, :])
        m = m + _dot(c_ref[rows, :], w_ref[na + nb:, :])
        y = _layer_norm(DEEPNORM_ALPHA * x_ref[rows, :] + m, g_ref[...], beta_ref[...])
        o_ref[rows, :] = y
        _store_token_tiles(op_ref.at[pl.ds(r0 * TOK, nr * TOK)], y)


def out_ln(a, b, c, x, w, g, beta, layer):
    s = x.shape[0]
    tm = _rows("out_ln", s)
    row = lambda arr: pl.BlockSpec((tm, arr.shape[1]), lambda i: (i, 0))
    full = lambda arr: _layer_block(arr, layer)
    return pl.pallas_call(
        _out_ln_kernel,
        out_shape=[jax.ShapeDtypeStruct((s, D_MODEL), F32), jax.ShapeDtypeStruct((s * TOK, LANE), jnp.uint32)],
        grid=(s // tm,),
        in_specs=[row(a), row(b), row(c), row(x), full(w), full(g), full(beta)],
        out_specs=[pl.BlockSpec((tm, D_MODEL), lambda i: (i, 0)), pl.BlockSpec((tm * TOK, LANE), lambda i: (i, 0))],
        scratch_shapes=[pltpu.VMEM(w.shape[1:], BF16)],
        compiler_params=_params(("arbitrary",), "out_ln"),
        name="out_ln",
    )(a, b, c, x, w, g, beta)


FFN_COLS = 512
FFN_OUT_CHUNK = 512
FFN_SUB = 256


def _ffn_kernel(te_ref, tv_ref, na_ref, xp_ref, wg_ref, wu_ref, wd_ref, o_ref, xb_ref, acc_ref):
    i, j = pl.program_id(0), pl.program_id(1)
    tm = xb_ref.shape[0]
    valid = tv_ref[i]

    @pl.when(j == 0)
    def _():
        acc_ref[...] = jnp.zeros_like(acc_ref)

    @pl.when((j == 0) & (valid > 0))
    def _():
        lo, hi = _load_token_tiles(xp_ref)
        xb_ref[:, :HALF] = lo.astype(BF16)
        xb_ref[:, HALF:] = hi.astype(BF16)

    def rows_step(r0, nr):
        xb = xb_ref[r0:r0 + nr, :]
        gate = _dot(xb, wg_ref[...].astype(BF16))
        up = _dot(xb, wu_ref[...].astype(BF16))
        hmid = (gate * (1.0 / (1.0 + jnp.exp(-gate))) * up).astype(BF16)
        for c in range(0, D_MODEL, FFN_OUT_CHUNK):
            cs = slice(c, c + FFN_OUT_CHUNK)
            acc_ref[r0:r0 + nr, cs] += _dot(hmid, wd_ref[:, cs].astype(BF16))

    nearly_full = valid > tm - FFN_SUB

    @pl.when(nearly_full)
    def _():
        rows_step(0, tm)

    for r0 in range(0, tm - FFN_SUB, FFN_SUB):
        @pl.when(jnp.logical_not(nearly_full) & (valid > r0))
        def _():
            rows_step(r0, FFN_SUB)

    @pl.when(j == pl.num_programs(1) - 1)
    def _():
        _store_token_tiles(o_ref, acc_ref[...])


def grouped_swiglu(xp, wg, wu, wd, tile_expert, tile_valid, n_active, tm, tf):
    n_tiles, nf = xp.shape[0] // (tm * TOK), D_FF // tf

    def tile(i, na):
        return jnp.minimum(i, na[0] - 1)

    def fcol(i, j, na):
        return jnp.where(i < na[0], j, nf - 1)

    return pl.pallas_call(
        _ffn_kernel,
        out_shape=jax.ShapeDtypeStruct(xp.shape, jnp.uint32),
        grid_spec=pltpu.PrefetchScalarGridSpec(
            num_scalar_prefetch=3,
            grid=(n_tiles, nf),
            in_specs=[pl.BlockSpec((tm * TOK, LANE), lambda i, j, te, tv, na: (tile(i, na), 0)),
                      pl.BlockSpec((None, D_MODEL, tf),
                                   lambda i, j, te, tv, na: (te[tile(i, na)], 0, fcol(i, j, na))),
                      pl.BlockSpec((None, D_MODEL, tf),
                                   lambda i, j, te, tv, na: (te[tile(i, na)], 0, fcol(i, j, na))),
                      pl.BlockSpec((None, tf, D_MODEL),
                                   lambda i, j, te, tv, na: (te[tile(i, na)], fcol(i, j, na), 0))],
            out_specs=pl.BlockSpec((tm * TOK, LANE), lambda i, j, te, tv, na: (i, 0)),
            scratch_shapes=[pltpu.VMEM((tm, D_MODEL), BF16), pltpu.VMEM((tm, D_MODEL), F32)],
        ),
        compiler_params=_params(("arbitrary", "arbitrary"), "grouped_swiglu"),
        name="grouped_swiglu",
    )(tile_expert, tile_valid, n_active, xp, wg, wu, wd)


def _router_kernel(x_ref, w_ref, e_ref, r_ref, wt_ref, cnt_ref, run_ref):
    i = pl.program_id(0)
    tm = x_ref.shape[0]
    ne = N_EXPERTS

    @pl.when(i == 0)
    def _():
        run_ref[...] = jnp.zeros_like(run_ref)

    x = x_ref[...]
    x_hi = x.astype(BF16)
    x_lo = (x - x_hi.astype(F32)).astype(BF16)
    w = w_ref[...]
    w_hi = w.astype(BF16).astype(F32)
    w_lo = w - w_hi
    both = _dot_nt(jnp.concatenate([w_hi, w_lo], 0).astype(BF16), x_hi)
    cross = _dot_nt(jnp.concatenate([w_hi, jnp.zeros_like(w_hi)], 0).astype(BF16), x_lo)
    logits = both[0:ne] + both[ne:2 * ne] + cross[0:ne]

    eidx = lax.broadcasted_iota(jnp.int32, logits.shape, 0).astype(F32)
    v1 = jnp.max(logits, 0, keepdims=True)
    i1 = jnp.min(jnp.where(logits == v1, eidx, float(ne)), 0, keepdims=True)
    rest = jnp.where(eidx == i1, -jnp.inf, logits)
    v2 = jnp.max(rest, 0, keepdims=True)
    i2 = jnp.min(jnp.where(rest == v2, eidx, float(ne)), 0, keepdims=True)
    t = jnp.exp(v2 - v1)
    w1 = 1.0 / (1.0 + t)
    wt_ref[0:1, :] = w1
    wt_ref[1:2, :] = t * w1
    e_ref[0:1, :] = i1.astype(jnp.int32)
    e_ref[1:2, :] = i2.astype(jnp.int32)

    sel1, sel2 = eidx == i1, eidx == i2
    sel = jnp.where(sel1, 1.0, 0.0) + jnp.where(sel2, 1.0, 0.0)
    before = (lax.broadcasted_iota(jnp.int32, (tm, tm), 0) < lax.broadcasted_iota(jnp.int32, (tm, tm), 1))
    sel16 = jnp.concatenate([sel, jnp.zeros_like(sel)], 0).astype(BF16)
    prefix = _dot(sel16, jnp.where(before, 1.0, 0.0).astype(BF16))[0:ne]
    rank = prefix + run_ref[:, 0:1]
    r_ref[0:1, :] = jnp.sum(jnp.where(sel1, rank, 0.0), 0, keepdims=True).astype(jnp.int32)
    r_ref[1:2, :] = jnp.sum(jnp.where(sel2, rank, 0.0), 0, keepdims=True).astype(jnp.int32)
    run_ref[...] = run_ref[...] + jnp.sum(sel, 1, keepdims=True)
    cnt_ref[...] = run_ref[...].astype(jnp.int32)


def route_tokens(x, w_router_t):
    s = x.shape[0]
    tm = _rows("moe_router", s)
    pair = pl.BlockSpec((2, tm), lambda i: (0, i))
    return pl.pallas_call(
        _router_kernel,
        out_shape=[jax.ShapeDtypeStruct((2, s), jnp.int32), jax.ShapeDtypeStruct((2, s), jnp.int32),
                   jax.ShapeDtypeStruct((2, s), F32), jax.ShapeDtypeStruct((N_EXPERTS, LANE), jnp.int32)],
        grid=(s // tm,),
        in_specs=[pl.BlockSpec((tm, D_MODEL), lambda i: (i, 0)),
                  pl.BlockSpec((N_EXPERTS, D_MODEL), lambda i: (0, 0))],
        out_specs=[pair, pair, pair, pl.BlockSpec((N_EXPERTS, LANE), lambda i: (0, 0))],
        scratch_shapes=[pltpu.VMEM((N_EXPERTS, LANE), F32)],
        compiler_params=_params(("arbitrary",), "moe_router"),
        name="moe_router",
    )(x, w_router_t)


def _slot_kernel(base_ref, e_ref, r_ref, s_ref):
    e = e_ref[...]
    slot = r_ref[...]
    for k in range(N_EXPERTS):
        slot = slot + jnp.where(e == k, base_ref[k], 0)
    s_ref[...] = slot


def token_slots(e_idx, rank, base):
    whole = pl.BlockSpec(e_idx.shape, lambda i, *_: (0, 0))
    return pl.pallas_call(
        _slot_kernel,
        out_shape=jax.ShapeDtypeStruct(e_idx.shape, jnp.int32),
        grid_spec=pltpu.PrefetchScalarGridSpec(num_scalar_prefetch=1, grid=(1,), in_specs=[whole, whole],
                                               out_specs=whole),
        compiler_params=_params(("arbitrary",), "token_slots"),
        name="token_slots",
    )(base, e_idx, rank)


DMA_UNROLL = 8
ZERO_BLOCK = 256


def _tile_rows(index):
    return pl.ds(pl.multiple_of(index * TOK, TOK), TOK)


def _dispatch_kernel(fill_ref, slot_ref, x_ref, xs_ref, zero_ref, sem, zsem):
    i = pl.program_id(0)
    tm = x_ref.shape[0] // TOK

    def issue(b, c):
        for u in range(DMA_UNROLL):
            t = b * DMA_UNROLL + u
            for k in range(2):
                pltpu.make_async_copy(x_ref.at[_tile_rows(t)], xs_ref.at[_tile_rows(slot_ref[k, t])],
                                      sem).start(priority=k)
        return c

    lax.fori_loop(0, tm // DMA_UNROLL, issue, 0)
    for k in range(2):
        pltpu.make_async_copy(x_ref, xs_ref.at[pl.ds(0, tm * TOK)], sem).wait()

    @pl.when(i == pl.num_programs(0) - 1)
    def _():
        zero_ref[...] = jnp.zeros_like(zero_ref)

        def zero_copy(slot):
            return pltpu.make_async_copy(zero_ref.at[pl.ds(0, TOK)], xs_ref.at[_tile_rows(slot)], zsem)

        for e in range(N_EXPERTS):
            lo, hi = fill_ref[0, e], fill_ref[1, e]

            def zissue(slot, c):
                zero_copy(slot).start()
                return c

            def zdrain(slot, c):
                zero_copy(slot).wait()
                return c

            lax.fori_loop(lo, hi, zissue, 0)
            lax.fori_loop(lo, hi, zdrain, 0)

        zrows = zero_ref.shape[0]
        zb = zrows // TOK

        def block_copy(b):
            return pltpu.make_async_copy(zero_ref, xs_ref.at[pl.ds(pl.multiple_of(b * zrows, zrows), zrows)], zsem)

        def bissue(b, c):
            block_copy(b).start()
            return c

        def bdrain(b, c):
            block_copy(b).wait()
            return c

        first, last = fill_ref[2, 0] // zb, xs_ref.shape[0] // zrows
        lax.fori_loop(first, last, bissue, 0)
        lax.fori_loop(first, last, bdrain, 0)


def moe_dispatch(xp, slots, fill, rows):
    s = xp.shape[0] // TOK
    tm = _rows("moe_dispatch", s)
    zb = min(ZERO_BLOCK, s)
    return pl.pallas_call(
        _dispatch_kernel,
        out_shape=jax.ShapeDtypeStruct((rows * TOK, LANE), xp.dtype),
        grid_spec=pltpu.PrefetchScalarGridSpec(
            num_scalar_prefetch=1,
            grid=(s // tm,),
            in_specs=[pl.BlockSpec((2, tm), lambda i, *_: (0, i), memory_space=pltpu.SMEM),
                      pl.BlockSpec((tm * TOK, LANE), lambda i, *_: (i, 0))],
            out_specs=pl.BlockSpec(memory_space=pl.ANY),
            scratch_shapes=[pltpu.VMEM((zb * TOK, LANE), xp.dtype), pltpu.SemaphoreType.DMA,
                            pltpu.SemaphoreType.DMA],
        ),
        compiler_params=_params(("arbitrary",), "moe_dispatch"),
        name="moe_dispatch",
    )(fill, slots, xp)


def _ple(x, p, wup_ref, wgate_ref, bgate_ref):
    up = _dot(p.astype(BF16), wup_ref[...])
    zg = _dot(x.astype(BF16), wgate_ref[...]) + bgate_ref[...]
    return up * (1.0 / (1.0 + jnp.exp(-zg)))


def _ple_ln_dense_kernel(x_ref, f_ref, p_ref, wupf_ref, wgatef_ref, bgate_ref, g_ref, beta_ref, o_ref,
                         wup_ref, wgate_ref):
    _cast_once(wupf_ref, wup_ref)
    _cast_once(wgatef_ref, wgate_ref)
    nr = x_ref.shape[0] // ROW_SPLIT
    for r0 in range(0, x_ref.shape[0], nr):
        rows = slice(r0, r0 + nr)
        x = x_ref[rows, :]
        f = jnp.concatenate(_load_token_tiles(f_ref, (r0, nr)), 1)
        ple = _ple(x, p_ref[rows, :], wup_ref, wgate_ref, bgate_ref)
        o_ref[rows, :] = _layer_norm(DEEPNORM_ALPHA * x + f + ple, g_ref[...], beta_ref[...])


def _ple_ln_moe_kernel(slot_ref, x_ref, wt_ref, p_ref, wupf_ref, wgatef_ref, bgate_ref, g_ref, beta_ref, ys_ref,
                       o_ref, wup_ref, wgate_ref, y1_ref, y2_ref, sem):
    _cast_once(wupf_ref, wup_ref)
    _cast_once(wgatef_ref, wgate_ref)
    tm = x_ref.shape[0]
    bufs = (y1_ref, y2_ref)

    def issue(b, c):
        for u in range(DMA_UNROLL):
            t = b * DMA_UNROLL + u
            for k in range(2):
                pltpu.make_async_copy(ys_ref.at[_tile_rows(slot_ref[k, t])], bufs[k].at[_tile_rows(t)],
                                      sem).start(priority=k)
        return c

    lax.fori_loop(0, tm // DMA_UNROLL, issue, 0)
    nr = tm // ROW_SPLIT
    ples = [_ple(x_ref[r0:r0 + nr, :], p_ref[r0:r0 + nr, :], wup_ref, wgate_ref, bgate_ref)
            for r0 in range(0, tm, nr)]
    for k in range(2):
        pltpu.make_async_copy(ys_ref.at[pl.ds(0, tm * TOK)], bufs[k], sem).wait()
    for i, r0 in enumerate(range(0, tm, nr)):
        rows = slice(r0, r0 + nr)
        wt = wt_ref[rows, :]
        f = (wt[:, 0:1] * jnp.concatenate(_load_token_tiles(y1_ref, (r0, nr)), 1)
             + wt[:, 1:2] * jnp.concatenate(_load_token_tiles(y2_ref, (r0, nr)), 1))
        o_ref[rows, :] = _layer_norm(DEEPNORM_ALPHA * x_ref[rows, :] + f + ples[i], g_ref[...], beta_ref[...])


def ple_ln(x, p, wup, wgate, bgate, g, beta, layer, f=None, moe=None):
    s = x.shape[0]
    name = "ple_ln_dense" if moe is None else "ple_ln_moe"
    tm = _rows(name, s)
    row = lambda arr: pl.BlockSpec((tm, arr.shape[1]), lambda i, *_: (i, 0))
    full = lambda arr: _layer_block(arr, layer)
    p_spec = pl.BlockSpec((None, None, tm, p.shape[-1]), lambda i, *_: (layer, 0, i, 0))
    tail = [p_spec, full(wup), full(wgate), full(bgate), full(g), full(beta)]
    out_spec = pl.BlockSpec((tm, D_MODEL), lambda i, *_: (i, 0))
    out_shape = jax.ShapeDtypeStruct((s, D_MODEL), F32)
    casts = [pltpu.VMEM(wup.shape[1:], BF16), pltpu.VMEM(wgate.shape[1:], BF16)]
    if moe is None:
        return pl.pallas_call(
            _ple_ln_dense_kernel, out_shape=out_shape, grid=(s // tm,),
            in_specs=[row(x), pl.BlockSpec((tm * TOK, LANE), lambda i: (i, 0))] + tail, out_specs=out_spec,
            scratch_shapes=casts,
            compiler_params=_params(("arbitrary",), name), name=name,
        )(x, f, p, wup, wgate, bgate, g, beta)
    ys, slots, wts_t = moe
    return pl.pallas_call(
        _ple_ln_moe_kernel,
        out_shape=out_shape,
        grid_spec=pltpu.PrefetchScalarGridSpec(
            num_scalar_prefetch=0,
            grid=(s // tm,),
            in_specs=[pl.BlockSpec((2, tm), lambda i: (0, i), memory_space=pltpu.SMEM), row(x), row(wts_t)] + tail
            + [pl.BlockSpec(memory_space=pl.ANY)],
            out_specs=out_spec,
            scratch_shapes=casts + [pltpu.VMEM((tm * TOK, LANE), jnp.uint32),
                                    pltpu.VMEM((tm * TOK, LANE), jnp.uint32), pltpu.SemaphoreType.DMA],
        ),
        compiler_params=_params(("arbitrary",), name),
        name=name,
    )(slots, x, wts_t, p, wup, wgate, bgate, g, beta, ys)


IN_SIZES = (MLA_Q_RANK, MLA_KV_RANK, MLA_ROPE, GLA_HEADS * GLA_DK, GLA_HEADS * GLA_DK, GLA_HEADS * GLA_DV,
            GLA_GATE_RANK, GLA_HEADS * GLA_DV, SWA_HEADS * SWA_HD, SWA_KV_HEADS * SWA_HD, SWA_KV_HEADS * SWA_HD)
IN_COLS = sum(IN_SIZES)
IN_DEST = (H_QA, H_KVA, H_KR, H_GQ, H_GK, H_GV, H_GLR, H_GR, H_SQ, H_SK, H_SV)


def _pack_w_in_kernel(w_ref, o_ref):
    src = 0
    for width, dst in zip(IN_SIZES, IN_DEST):
        o_ref[dst:dst + width, :] = w_ref[src:src + width, :].astype(BF16)
        pad = -width % LANE
        if pad:
            o_ref[dst + width:dst + width + pad, :] = jnp.zeros((pad, o_ref.shape[1]), BF16)
        src += width


def _pack_w_in(w_t):
    depth, _, d = w_t.shape
    tc = _rows("pack_w_in", d)
    return pl.pallas_call(
        _pack_w_in_kernel,
        out_shape=jax.ShapeDtypeStruct((depth, H_COLS, d), BF16),
        grid=(depth, d // tc),
        in_specs=[pl.BlockSpec((None, IN_COLS, tc), lambda l, i: (l, 0, i))],
        out_specs=pl.BlockSpec((None, H_COLS, tc), lambda l, i: (l, 0, i)),
        compiler_params=_params(("parallel", "parallel"), "pack_w_in"),
        name="pack_w_in",
    )(w_t)


def _pack_w_q_b(w):
    w = w.reshape(MLA_Q_RANK, MLA_HEADS, MLA_NOPE + MLA_ROPE)
    w = jnp.pad(w, ((0, 0), (0, 0), (0, MLA_QK - MLA_NOPE - MLA_ROPE)))
    return w.reshape(MLA_Q_RANK, MLA_HEADS * MLA_QK).astype(BF16)


def _token_mixer_ln(x, rope, layer, w_in_p, q_gain, w_q_b, kv_gain, w_kv_b, gla_w, gla_b, gla_gain, sinks,
                    rel_bias, w_out_b, ln_g, ln_b):
    h = proj_in(x, w_in_p, layer)
    q, k, v = mla_prep(h, *rope, q_gain.reshape(1, -1), kv_gain.reshape(1, -1), _pack_w_q_b(w_q_b),
                       w_kv_b.astype(BF16))
    a = mla_flash(q, k, v)
    gla_w_p = jnp.pad(gla_w, ((0, LANE - GLA_GATE_RANK), (0, 0))).astype(BF16)
    b = gla(h, gla_w_p, gla_b.reshape(1, -1), gla_gain.reshape(1, -1))
    c = swa(h, sinks, rel_bias)
    return out_ln(a, b, c, x, w_out_b, ln_g, ln_b, layer)


def _moe_plan(counts, tm, n_tiles):
    tiles = (counts + tm - 1) // tm
    ends = jnp.cumsum(tiles)
    base = (ends - tiles) * tm
    n_active = ends[-1:].astype(jnp.int32)
    tile_expert = jnp.searchsorted(ends, jnp.arange(n_tiles, dtype=jnp.int32), side="right")
    tile_expert = jnp.minimum(tile_expert, N_EXPERTS - 1).astype(jnp.int32)
    tile_start = jnp.arange(n_tiles, dtype=jnp.int32) * tm
    tile_valid = jnp.clip((base + counts)[tile_expert] - tile_start, 0, tm).astype(jnp.int32)
    used_rows = jnp.broadcast_to(ends[-1] * tm, counts.shape)
    fill = jnp.stack([base + counts, base + tiles * tm, used_rows]).astype(jnp.int32)
    return base.astype(jnp.int32), fill, tile_expert, tile_valid, n_active


def kernel(x, p, positions, w_in, mla_q_a_gain, mla_w_q_b, mla_kv_a_gain, mla_w_kv_b, gla_w_gate, gla_b_gate,
           gla_norm_gain, swa_sinks, rel_bias, w_out, ln1_g, ln1_b, ffn_w_gate, ffn_w_up, ffn_w_down,
           moe_router, moe_w_gate, moe_w_up, moe_w_down, ple_w_up, ple_w_gate, ple_b_gate, ln2_g, ln2_b):
    batch, s, _ = x.shape
    assert batch == 1
    xcur = x.reshape(s, D_MODEL)
    rope = rope_tables(positions)
    tm = _rows("grouped_swiglu", s)
    row_stack = lambda v: v.reshape(DEPTH, 1, -1)
    w_in_p = _pack_w_in(jnp.swapaxes(w_in, 1, 2))
    tail = (p, ple_w_up, ple_w_gate, row_stack(ple_b_gate), row_stack(ln2_g), row_stack(ln2_b))
    for i in range(DEPTH):
        x1, x1p = _token_mixer_ln(xcur, rope, i, w_in_p, mla_q_a_gain[i], mla_w_q_b[i], mla_kv_a_gain[i],
                                  mla_w_kv_b[i], gla_w_gate[i], gla_b_gate[i], gla_norm_gain[i], swa_sinks[i],
                                  rel_bias, w_out, row_stack(ln1_g), row_stack(ln1_b))
        j = i // 2
        if i % 2 == 0:
            n_tiles = s // tm
            f = grouped_swiglu(x1p, ffn_w_gate[j][None], ffn_w_up[j][None], ffn_w_down[j][None],
                               jnp.zeros((n_tiles,), jnp.int32), jnp.full((n_tiles,), tm, jnp.int32),
                               jnp.full((1,), n_tiles, jnp.int32), tm, FFN_COLS)
            xcur = ple_ln(x1, *tail, i, f=f)
        else:
            n_tiles = (2 * s + N_EXPERTS * (tm - 1)) // tm
            e_idx, rank, wts, counts = route_tokens(x1, moe_router[j].T)
            base, fill, tile_expert, tile_valid, n_active = _moe_plan(counts[:, 0], tm, n_tiles)
            slots = token_slots(e_idx, rank, base)
            xs = moe_dispatch(x1p, slots, fill, n_tiles * tm)
            ys = grouped_swiglu(xs, moe_w_gate[j], moe_w_up[j], moe_w_down[j], tile_expert, tile_valid, n_active,
                                tm, FFN_COLS)
            xcur = ple_ln(x1, *tail, i, moe=(ys, slots, wts.T))
    return xcur.reshape(batch, s, D_MODEL)
```
